```python
import math
import jax, jax.numpy as jnp
from jax import lax
import numpy as np

D_MODEL = 2048
BATCH = 8
SEQ = 2048
DEPTH = 2

MIX_WIDTH = D_MODEL
RET_WIDTH = MIX_WIDTH // 2
SSM_WIDTH = MIX_WIDTH - RET_WIDTH
RET_HEAD_DIM = 256
RET_HEADS = RET_WIDTH // RET_HEAD_DIM
RET_CHUNK = 128
ROPE_BASE = 10000.0
SSM_GROUP = 16
SSM_GROUPS = SSM_WIDTH // SSM_GROUP
SSM_STATE = 64
DT_MIN = 1e-3
DT_MAX = 1e-1
IN_PROJ_WIDTH = 4 * RET_WIDTH + SSM_WIDTH
N_EXPERTS = 32
N_EXPERT_GROUPS = 4
EXPERTS_PER_GROUP = N_EXPERTS // N_EXPERT_GROUPS
TOP_K = 2
D_EXPERT = D_MODEL // 2
EXPERT_BLOCK = 128
LN_EPS = 1e-5
DEEPNORM_ALPHA = (2.0 * DEPTH) ** 0.25
DEEPNORM_BETA = (8.0 * DEPTH) ** -0.25

kernel_name = "hybrid_retention_s5_grouped_moe_deepnorm"

F32 = jnp.float32


def _layer_norm(x, g, b):
    xf = x.astype(F32)
    mu = jnp.mean(xf, -1, keepdims=True)
    var = jnp.mean(jnp.square(xf - mu), -1, keepdims=True)
    return ((xf - mu) * lax.rsqrt(var + LN_EPS) * g.astype(F32) + b.astype(F32)).astype(x.dtype)


def _head_norm(o):
    mu = jnp.mean(o, -1, keepdims=True)
    var = jnp.mean(jnp.square(o - mu), -1, keepdims=True)
    return (o - mu) * lax.rsqrt(var + LN_EPS)


def _rotary(t):
    L = t.shape[1]
    inv = 1.0 / (ROPE_BASE ** (jnp.arange(0, RET_HEAD_DIM, 2, dtype=F32) / RET_HEAD_DIM))
    ang = jnp.arange(L, dtype=F32)[:, None] * inv[None, :]
    cos = jnp.cos(ang)[None, :, None, :]
    sin = jnp.sin(ang)[None, :, None, :]
    tf = t.astype(F32)
    t1, t2 = tf[..., 0::2], tf[..., 1::2]
    return jnp.stack([t1 * cos - t2 * sin, t1 * sin + t2 * cos], -1).reshape(tf.shape)


def _retention(q, k, v):
    Bn, L, H, Dh = q.shape
    C = RET_CHUNK
    N = L // C
    log_gamma = jnp.log(1.0 - 2.0 ** (-5.0 - jnp.arange(H, dtype=F32)))
    q = _rotary(q).reshape(Bn, N, C, H, Dh)
    k = (_rotary(k) * (Dh ** -0.5)).reshape(Bn, N, C, H, Dh)
    v = v.astype(F32).reshape(Bn, N, C, H, Dh)
    idx = jnp.arange(C, dtype=F32)
    diff = idx[:, None] - idx[None, :]
    mask = jnp.where(diff >= 0, jnp.exp(log_gamma[:, None, None] * jnp.maximum(diff, 0.0)), 0.0)
    scores = jnp.einsum('bnihd,bnjhd->bnhij', q, k) * mask
    inner = jnp.einsum('bnhij,bnjhd->bnihd', scores, v)
    k_decay = jnp.exp(log_gamma[None, :] * (C - 1.0 - idx)[:, None])
    kv = jnp.einsum('bnjhd,jh,bnjhe->nbhde', k, k_decay, v)
    chunk_decay = jnp.exp(log_gamma * C)[None, :, None, None]

    def step(R, kv_n):
        return chunk_decay * R + kv_n, R

    _, R_prev = lax.scan(step, jnp.zeros((Bn, H, Dh, Dh), F32), kv)
    q_decay = jnp.exp(log_gamma[None, :] * (idx + 1.0)[:, None])
    cross = jnp.einsum('bnihd,ih,nbhde->bnihe', q, q_decay, R_prev)
    return (inner + cross).reshape(Bn, L, H, Dh)


def _s5(u, lam_re, lam_im, b_re, b_im, c_re, c_im, d, log_dt, w_glu):
    Bn, L, _ = u.shape
    uf = u.astype(F32).reshape(Bn, L, SSM_GROUPS, SSM_GROUP)
    lam = lax.complex(lam_re.astype(F32), lam_im.astype(F32))
    dt = jnp.exp(log_dt.astype(F32))[:, None]
    lam_bar = jnp.exp(lam * dt)
    bmat = lax.complex(b_re.astype(F32), b_im.astype(F32))
    b_bar = ((lam_bar - 1.0) / lam)[..., None] * bmat
    bu = jnp.einsum('blgh,gph->blgp', uf.astype(jnp.complex64), b_bar)
    a_seq = jnp.broadcast_to(lam_bar, (L,) + lam_bar.shape)

    def binop(e1, e2):
        a1, b1 = e1
        a2, b2 = e2
        return a1 * a2, a2 * b1 + b2

    def scan_seq(bu_seq):
        return lax.associative_scan(binop, (a_seq, bu_seq))[1]

    h = jax.vmap(scan_seq)(bu)
    cmat = lax.complex(c_re.astype(F32), c_im.astype(F32))
    y = jnp.real(jnp.einsum('blgp,ghp->blgh', h, cmat)) + d.astype(F32) * uf
    y = jax.nn.gelu(y.reshape(Bn, L, SSM_WIDTH))
    return y * jax.nn.sigmoid(y @ w_glu.astype(F32))


def _mixer(x, w_in, w_out, lam_re, lam_im, b_re, b_im, c_re, c_im, d, log_dt, w_glu):
    Bn, L, _ = x.shape
    proj = x @ w_in
    q, k, v, g, u = jnp.split(proj, [RET_WIDTH, 2 * RET_WIDTH, 3 * RET_WIDTH, 4 * RET_WIDTH], axis=-1)
    heads = lambda t: t.reshape(Bn, L, RET_HEADS, RET_HEAD_DIM)
    o = _head_norm(_retention(heads(q), heads(k), heads(v))).reshape(Bn, L, RET_WIDTH)
    ret_out = jax.nn.silu(g.astype(F32)) * o
    ssm_out = _s5(u, lam_re, lam_im, b_re, b_im, c_re, c_im, d, log_dt, w_glu)
    mixed = jnp.concatenate([ret_out, ssm_out], -1).astype(x.dtype)
    return mixed @ w_out


def _moe(x, router_w, router_b, w_gate, w_up, w_down):
    Bn, L, D = x.shape
    T = Bn * L
    xt = x.reshape(T, D)
    logits = (xt @ router_w).astype(F32) + router_b.astype(F32)
    probs = jax.nn.softmax(logits, -1).reshape(T, N_EXPERT_GROUPS, EXPERTS_PER_GROUP)
    top_in_group, _ = lax.top_k(probs, TOP_K)
    group = jnp.argmax(jnp.sum(top_in_group, -1), -1)
    group_probs = jnp.take_along_axis(probs, group[:, None, None], 1)[:, 0]
    vals, local = lax.top_k(group_probs, TOP_K)
    expert = group[:, None] * EXPERTS_PER_GROUP + local
    weight = vals / jnp.sum(vals, -1, keepdims=True)
    A = T * TOP_K
    e_flat = expert.reshape(A)
    tok_flat = jnp.repeat(jnp.arange(T, dtype=jnp.int32), TOP_K)
    w_flat = weight.reshape(A)
    order = jnp.argsort(e_flat, stable=True)
    e_sorted = e_flat[order]
    counts = jnp.bincount(e_flat, length=N_EXPERTS)
    starts = jnp.cumsum(counts) - counts
    padded = (counts + EXPERT_BLOCK - 1) // EXPERT_BLOCK * EXPERT_BLOCK
    padded_end = jnp.cumsum(padded)
    padded_start = padded_end - padded
    dest = padded_start[e_sorted] + (jnp.arange(A) - starts[e_sorted])
    P = (A + EXPERT_BLOCK - 1) // EXPERT_BLOCK * EXPERT_BLOCK + N_EXPERTS * EXPERT_BLOCK
    n_blocks = P // EXPERT_BLOCK
    row_tok = jnp.full((P,), T, jnp.int32).at[dest].set(tok_flat[order])
    row_w = jnp.zeros((P,), F32).at[dest].set(w_flat[order])
    block_start = jnp.arange(n_blocks) * EXPERT_BLOCK
    block_expert = jnp.minimum(jnp.sum(block_start[:, None] >= padded_end[None, :], -1), N_EXPERTS - 1)
    x_pad = jnp.concatenate([xt, jnp.zeros((1, D), xt.dtype)], 0)
    xb = x_pad[row_tok].reshape(n_blocks, EXPERT_BLOCK, D)

    def expert_block(args):
        xblk, e = args
        hblk = jax.nn.silu(xblk @ w_gate[e]) * (xblk @ w_up[e])
        return hblk @ w_down[e]

    yb = lax.map(expert_block, (xb, block_expert)).reshape(P, D)
    out = jnp.zeros((T + 1, D), F32).at[row_tok].add(yb.astype(F32) * row_w[:, None])
    return out[:T].astype(x.dtype).reshape(Bn, L, D)


def setup_inputs(seed: int = 0) -> dict:
    key = jax.random.key(seed)
    ks = jax.random.split(key, 24)
    nrm = lambda k, shape, s: jax.random.normal(k, shape, F32) * s
    x = nrm(ks[0], (BATCH, SEQ, D_MODEL), 1.0)
    w_in = nrm(ks[1], (DEPTH, D_MODEL, IN_PROJ_WIDTH), D_MODEL ** -0.5)
    w_out = nrm(ks[2], (DEPTH, MIX_WIDTH, D_MODEL), MIX_WIDTH ** -0.5 * DEEPNORM_BETA)
    n = jnp.arange(SSM_STATE, dtype=F32)
    ssm_lambda_re = -0.5 + nrm(ks[3], (DEPTH, SSM_GROUPS, SSM_STATE), 0.01)
    ssm_lambda_im = math.pi * n[None, None, :] + nrm(ks[4], (DEPTH, SSM_GROUPS, SSM_STATE), 0.01)
    bs = (2.0 * SSM_GROUP) ** -0.5
    ssm_b_re = nrm(ks[5], (DEPTH, SSM_GROUPS, SSM_STATE, SSM_GROUP), bs)
    ssm_b_im = nrm(ks[6], (DEPTH, SSM_GROUPS, SSM_STATE, SSM_GROUP), bs)
    cs = (2.0 * SSM_STATE) ** -0.5
    ssm_c_re = nrm(ks[7], (DEPTH, SSM_GROUPS, SSM_GROUP, SSM_STATE), cs)
    ssm_c_im = nrm(ks[8], (DEPTH, SSM_GROUPS, SSM_GROUP, SSM_STATE), cs)
    ssm_d = nrm(ks[9], (DEPTH, SSM_GROUPS, SSM_GROUP), 1.0)
    ssm_log_dt = jax.random.uniform(ks[10], (DEPTH, SSM_GROUPS), F32, math.log(DT_MIN), math.log(DT_MAX))
    w_glu = nrm(ks[11], (DEPTH, SSM_WIDTH, SSM_WIDTH), SSM_WIDTH ** -0.5)
    ln1_g = 1.0 + nrm(ks[12], (DEPTH, D_MODEL), 0.02)
    ln1_b = nrm(ks[13], (DEPTH, D_MODEL), 0.02)
    ln2_g = 1.0 + nrm(ks[14], (DEPTH, D_MODEL), 0.02)
    ln2_b = nrm(ks[15], (DEPTH, D_MODEL), 0.02)
    router_w = nrm(ks[16], (D_MODEL, N_EXPERTS), D_MODEL ** -0.5)
    router_b = nrm(ks[17], (N_EXPERTS,), 0.01)
    w_gate = nrm(ks[18], (DEPTH, N_EXPERTS, D_MODEL, D_EXPERT), D_MODEL ** -0.5)
    w_up = nrm(ks[19], (DEPTH, N_EXPERTS, D_MODEL, D_EXPERT), D_MODEL ** -0.5)
    w_down = nrm(ks[20], (DEPTH, N_EXPERTS, D_EXPERT, D_MODEL), D_EXPERT ** -0.5 * DEEPNORM_BETA)
    return {"x": x, "w_in": w_in, "w_out": w_out,
            "ssm_lambda_re": ssm_lambda_re, "ssm_lambda_im": ssm_lambda_im,
            "ssm_b_re": ssm_b_re, "ssm_b_im": ssm_b_im,
            "ssm_c_re": ssm_c_re, "ssm_c_im": ssm_c_im,
            "ssm_d": ssm_d, "ssm_log_dt": ssm_log_dt, "w_glu": w_glu,
            "ln1_g": ln1_g, "ln1_b": ln1_b, "ln2_g": ln2_g, "ln2_b": ln2_b,
            "router_w": router_w, "router_b": router_b,
            "w_gate": w_gate, "w_up": w_up, "w_down": w_down}


def reference(x, w_in, w_out, ssm_lambda_re, ssm_lambda_im, ssm_b_re, ssm_b_im,
              ssm_c_re, ssm_c_im, ssm_d, ssm_log_dt, w_glu,
              ln1_g, ln1_b, ln2_g, ln2_b, router_w, router_b,
              w_gate, w_up, w_down):
    for l in range(DEPTH):
        h = _mixer(x, w_in[l], w_out[l], ssm_lambda_re[l], ssm_lambda_im[l],
                   ssm_b_re[l], ssm_b_im[l], ssm_c_re[l], ssm_c_im[l],
                   ssm_d[l], ssm_log_dt[l], w_glu[l])
        x = _layer_norm(DEEPNORM_ALPHA * x + h, ln1_g[l], ln1_b[l])
        h = _moe(x, router_w, router_b, w_gate[l], w_up[l], w_down[l])
        x = _layer_norm(DEEPNORM_ALPHA * x + h, ln2_g[l], ln2_b[l])
    return x
```

```python
import functools
import math

import numpy as np
import jax
import jax.numpy as jnp
from jax import lax
from jax.experimental import pallas as pl
from jax.experimental.pallas import tpu as pltpu

F32 = jnp.float32
BF16 = jnp.bfloat16
I32 = jnp.int32

D_MODEL = 2048
DEPTH = 2
RET_WIDTH = 1024
SSM_WIDTH = 1024
RET_HEAD_DIM = 256
RET_HEADS = RET_WIDTH // RET_HEAD_DIM
ROPE_BASE = 10000.0
SSM_GROUP = 16
SSM_GROUPS = SSM_WIDTH // SSM_GROUP
SSM_STATE = 64
IN_PROJ_WIDTH = 4 * RET_WIDTH + SSM_WIDTH
N_EXPERTS = 32
N_EXPERT_GROUPS = 4
EXPERTS_PER_GROUP = N_EXPERTS // N_EXPERT_GROUPS
TOP_K = 2
D_EXPERT = D_MODEL // 2
LN_EPS = 1e-5
DEEPNORM_ALPHA = (2.0 * DEPTH) ** 0.25

LANES = 128
SUBLANES = 8
VMEM_LIMIT = 56 * 1024 * 1024

TM_PROJ = 1024
TN_PROJ = 1024
RET_CHUNK = 256
S5_TL = 128
S5_LT = SSM_WIDTH // LANES
S5_NSTATE = (LANES // SSM_GROUP) * SSM_STATE
TM_POST = 256
TM_COMB = 256
ROW_TILE = 256
TILES_PER_SUPER = 8
SUPER = ROW_TILE * TILES_PER_SUPER
DE_CHUNK = 256
N_DE_CHUNKS = D_EXPERT // DE_CHUNK
DISPATCH_TOK = 1024
DISPATCH_GRP = 256


def _sigmoid(x):
    return 1.0 / (1.0 + jnp.exp(-x))


def _cparams(sem, vmem=VMEM_LIMIT):
    return pltpu.CompilerParams(dimension_semantics=sem, vmem_limit_bytes=vmem)


def _in_proj_kernel(x_ref, w_ref, cos_ref, sin_ref, o_ref):
    j = pl.program_id(1)
    acc = jnp.dot(x_ref[...].astype(BF16), w_ref[...], preferred_element_type=F32)

    @pl.when(j < 2)
    def _rot():
        scale = jnp.where(j == 1, RET_HEAD_DIM ** -0.5, 1.0).astype(F32)
        c = cos_ref[...] * scale
        s = sin_ref[...] * scale
        half = RET_HEAD_DIM // 2
        for h in range(RET_HEADS):
            lo = h * RET_HEAD_DIM
            t1 = acc[:, lo:lo + half]
            t2 = acc[:, lo + half:lo + RET_HEAD_DIM]
            o_ref[:, lo:lo + half] = (t1 * c - t2 * s).astype(BF16)
            o_ref[:, lo + half:lo + RET_HEAD_DIM] = (t1 * s + t2 * c).astype(BF16)

    @pl.when(j == 3)
    def _gate():
        o_ref[...] = (acc * _sigmoid(acc)).astype(BF16)

    @pl.when((j == 2) | (j == 4))
    def _plain():
        o_ref[...] = acc.astype(BF16)


def _in_proj(x2, w_bf, cos, sin, seq):
    t, d = x2.shape
    n = w_bf.shape[1]
    tiles_per_seq = seq // TM_PROJ
    return pl.pallas_call(
        _in_proj_kernel,
        grid=(t // TM_PROJ, n // TN_PROJ),
        in_specs=[
            pl.BlockSpec((TM_PROJ, d), lambda i, j: (i, 0)),
            pl.BlockSpec((d, TN_PROJ), lambda i, j: (0, j)),
            pl.BlockSpec((TM_PROJ, LANES), lambda i, j: (i % tiles_per_seq, 0)),
            pl.BlockSpec((TM_PROJ, LANES), lambda i, j: (i % tiles_per_seq, 0)),
        ],
        out_specs=pl.BlockSpec((TM_PROJ, TN_PROJ), lambda i, j: (i, j)),
        out_shape=jax.ShapeDtypeStruct((t, n), BF16),
        compiler_params=_cparams(("arbitrary", "arbitrary")),
        name="in_proj",
    )(x2, w_bf, cos, sin)


def _ret_kernel(q_ref, k_ref, v_ref, g_ref, mask_ref, qd_ref, kd_ref, o_ref, r_ref):
    n = pl.program_id(2)

    @pl.when(n == 0)
    def _init():
        r_ref[...] = jnp.zeros_like(r_ref)

    q = q_ref[...]
    k = k_ref[...]
    v = v_ref[...]
    s = lax.dot_general(q, k, (((1,), (1,)), ((), ())), preferred_element_type=F32)
    s = s * mask_ref[...]
    inner = jnp.dot(s.astype(BF16), v, preferred_element_type=F32)
    qd = qd_ref[...]
    r_prev = r_ref[...]
    cross = jnp.dot((q.astype(F32) * qd).astype(BF16), r_prev.astype(BF16), preferred_element_type=F32)
    o = inner + cross
    kdec = (k.astype(F32) * kd_ref[...]).astype(BF16)
    kv = lax.dot_general(kdec, v, (((0,), (0,)), ((), ())), preferred_element_type=F32)
    r_ref[...] = r_prev * qd[RET_CHUNK - 1:RET_CHUNK, :] + kv
    mu = jnp.mean(o, axis=-1, keepdims=True)
    oc = o - mu
    var = jnp.mean(oc * oc, axis=-1, keepdims=True)
    o_ref[...] = (g_ref[...].astype(F32) * (oc * lax.rsqrt(var + LN_EPS))).astype(BF16)


def _retention_tables():
    c = RET_CHUNK
    log_gamma = np.log(1.0 - 2.0 ** (-5.0 - np.arange(RET_HEADS, dtype=np.float64)))
    idx = np.arange(c, dtype=np.float64)
    diff = idx[:, None] - idx[None, :]
    mask = np.where(diff >= 0, np.exp(log_gamma[:, None, None] * np.maximum(diff, 0.0)), 0.0)
    qd = np.exp(log_gamma[:, None] * (idx + 1.0)[None, :])
    kd = np.exp(log_gamma[:, None] * (c - 1.0 - idx)[None, :])
    bc = lambda a: np.broadcast_to(a[:, :, None], (RET_HEADS, c, RET_HEAD_DIM)).astype(np.float32)
    return mask.astype(np.float32), bc(qd), bc(kd)


def _retention(proj, batch, seq):
    t = proj.shape[0]
    nch = seq // RET_CHUNK
    mask, qd, kd = _retention_tables()
    row = lambda b, h, n: b * nch + n
    blk = (RET_CHUNK, RET_HEAD_DIM)
    hb = RET_WIDTH // RET_HEAD_DIM
    return pl.pallas_call(
        _ret_kernel,
        grid=(batch, RET_HEADS, nch),
        in_specs=[
            pl.BlockSpec(blk, lambda b, h, n: (row(b, h, n), h)),
            pl.BlockSpec(blk, lambda b, h, n: (row(b, h, n), hb + h)),
            pl.BlockSpec(blk, lambda b, h, n: (row(b, h, n), 2 * hb + h)),
            pl.BlockSpec(blk, lambda b, h, n: (row(b, h, n), 3 * hb + h)),
            pl.BlockSpec((None, RET_CHUNK, RET_CHUNK), lambda b, h, n: (h, 0, 0)),
            pl.BlockSpec((None, RET_CHUNK, RET_HEAD_DIM), lambda b, h, n: (h, 0, 0)),
            pl.BlockSpec((None, RET_CHUNK, RET_HEAD_DIM), lambda b, h, n: (h, 0, 0)),
        ],
        out_specs=pl.BlockSpec(blk, lambda b, h, n: (row(b, h, n), h)),
        out_shape=jax.ShapeDtypeStruct((t, RET_WIDTH), BF16),
        scratch_shapes=[pltpu.VMEM((RET_HEAD_DIM, RET_HEAD_DIM), F32)],
        compiler_params=_cparams(("arbitrary", "arbitrary", "arbitrary")),
        name="retention",
    )(proj, proj, proj, proj, jnp.asarray(mask), jnp.asarray(qd), jnp.asarray(kd))


def _s5_kernel(u_ref, bm_ref, cm_ref, lam_ref, d_ref, y_ref, us_ref, ut_ref, h_ref, st_ref):
    n = pl.program_id(1)
    tl = S5_TL
    nb = SUBLANES
    ns = S5_NSTATE

    @pl.when(n == 0)
    def _init():
        st_ref[...] = jnp.zeros_like(st_ref)

    for b in range(nb):
        us_ref[b * tl:(b + 1) * tl, :] = u_ref[b].astype(F32)

    def relayout(t, c):
        ut_ref[pl.ds(pl.multiple_of(t * nb, nb), nb), :] = us_ref[pl.ds(t, nb, stride=tl), :]
        return c

    lax.fori_loop(0, tl, relayout, 0, unroll=8)
    ut = ut_ref[...]
    h_ref[...] = jnp.dot(ut.astype(BF16), bm_ref[...], preferred_element_type=F32)

    ar = jnp.broadcast_to(lam_ref[0:1, :], (nb, ns))
    ai = jnp.broadcast_to(lam_ref[1:2, :], (nb, ns))

    def step(t, carry):
        sr, si = carry
        r0 = pl.multiple_of(t * nb, nb)
        br = h_ref[pl.ds(r0, nb), 0:ns]
        bi = h_ref[pl.ds(r0, nb), ns:2 * ns]
        nr = ar * sr - ai * si + br
        ni = ar * si + ai * sr + bi
        h_ref[pl.ds(r0, nb), 0:ns] = nr
        h_ref[pl.ds(r0, nb), ns:2 * ns] = ni
        return nr, ni

    sr, si = lax.fori_loop(0, tl, step, (st_ref[0:nb, :], st_ref[nb:2 * nb, :]), unroll=4)
    st_ref[0:nb, :] = sr
    st_ref[nb:2 * nb, :] = si

    y = jnp.dot(h_ref[...].astype(BF16), cm_ref[...], preferred_element_type=F32) + ut * d_ref[...]
    ut_ref[...] = y
    for b in range(nb):
        y_ref[b] = ut_ref[pl.ds(b, tl, stride=nb), :].astype(BF16)


def _s5_tables(lam_re, lam_im, b_re, b_im, c_re, c_im, d, log_dt):
    lam = lax.complex(lam_re.astype(F32), lam_im.astype(F32))
    dt = jnp.exp(log_dt.astype(F32))[:, None]
    lam_bar = jnp.exp(lam * dt)
    b_bar = ((lam_bar - 1.0) / lam)[..., None] * lax.complex(b_re.astype(F32), b_im.astype(F32))
    gpt = LANES // SSM_GROUP
    eye = jnp.eye(gpt, dtype=F32)

    def bdiag_in(m):
        m = m.reshape(S5_LT, gpt, SSM_STATE, SSM_GROUP)
        return jnp.einsum('jgpi,gh->jgihp', m, eye).reshape(S5_LT, LANES, gpt * SSM_STATE)

    def bdiag_out(m):
        m = m.reshape(S5_LT, gpt, SSM_GROUP, SSM_STATE)
        return jnp.einsum('jgop,gh->jgpho', m, eye).reshape(S5_LT, gpt * SSM_STATE, LANES)

    bmat = jnp.concatenate([bdiag_in(jnp.real(b_bar)), bdiag_in(jnp.imag(b_bar))], axis=-1).astype(BF16)
    cmat = jnp.concatenate([bdiag_out(c_re.astype(F32)), -bdiag_out(c_im.astype(F32))], axis=1).astype(BF16)
    lam_t = jnp.stack([jnp.real(lam_bar).reshape(S5_LT, S5_NSTATE),
                       jnp.imag(lam_bar).reshape(S5_LT, S5_NSTATE)], axis=1)
    d_t = d.astype(F32).reshape(S5_LT, 1, LANES)
    return bmat, cmat, lam_t, d_t


def _s5(proj, tables, batch, seq):
    bmat, cmat, lam_t, d_t = tables
    assert batch == SUBLANES
    proj3 = proj.reshape(batch, seq, IN_PROJ_WIDTH)
    ucol = (4 * RET_WIDTH) // LANES
    tl = S5_TL
    return pl.pallas_call(
        _s5_kernel,
        grid=(S5_LT, seq // tl),
        in_specs=[
            pl.BlockSpec((batch, tl, LANES), lambda j, n: (0, n, ucol + j)),
            pl.BlockSpec((None, LANES, 2 * S5_NSTATE), lambda j, n: (j, 0, 0)),
            pl.BlockSpec((None, 2 * S5_NSTATE, LANES), lambda j, n: (j, 0, 0)),
            pl.BlockSpec((None, 2, S5_NSTATE), lambda j, n: (j, 0, 0)),
            pl.BlockSpec((None, 1, LANES), lambda j, n: (j, 0, 0)),
        ],
        out_specs=pl.BlockSpec((batch, tl, LANES), lambda j, n: (0, n, j)),
        out_shape=jax.ShapeDtypeStruct((batch, seq, SSM_WIDTH), BF16),
        scratch_shapes=[
            pltpu.VMEM((batch * tl, LANES), F32),
            pltpu.VMEM((batch * tl, LANES), F32),
            pltpu.VMEM((batch * tl, 2 * S5_NSTATE), F32),
            pltpu.VMEM((2 * SUBLANES, S5_NSTATE), F32),
        ],
        compiler_params=_cparams(("arbitrary", "arbitrary")),
        name="s5",
    )(proj3, bmat, cmat, lam_t, d_t)


def _layer_norm_rows(r, g, b):
    mu = jnp.mean(r, axis=-1, keepdims=True)
    rc = r - mu
    var = jnp.mean(rc * rc, axis=-1, keepdims=True)
    return rc * lax.rsqrt(var + LN_EPS) * g + b


def _post_mix_kernel(x_ref, ret_ref, y_ref, wglu_ref, wout_ref, lng_ref, lnb_ref,
                     rcat_ref, rhi_ref, rb_ref, tri_ref,
                     x1_ref, e_ref, w_ref, rank_ref, cnt_ref, carry_ref):
    i = pl.program_id(0)
    tm = TM_POST

    @pl.when(i == 0)
    def _init():
        carry_ref[...] = jnp.zeros_like(carry_ref)

    ya = jax.nn.gelu(y_ref[...].astype(F32))
    z = jnp.dot(ya.astype(BF16), wglu_ref[...], preferred_element_type=F32)
    ssm = (ya * _sigmoid(z)).astype(BF16)
    h = jnp.dot(ret_ref[...], wout_ref[0:RET_WIDTH, :], preferred_element_type=F32)
    h = h + jnp.dot(ssm, wout_ref[RET_WIDTH:, :], preferred_element_type=F32)
    x1 = _layer_norm_rows(DEEPNORM_ALPHA * x_ref[...] + h, lng_ref[...], lnb_ref[...])
    x1_ref[...] = x1

    xh = x1.astype(BF16)
    xl = (x1 - xh.astype(F32)).astype(BF16)
    nt = (((1,), (1,)), ((), ()))
    l1 = lax.dot_general(rcat_ref[...], xh, nt, preferred_element_type=F32)
    l2 = lax.dot_general(rhi_ref[...], xl, nt, preferred_element_type=F32)
    logits = l1[0:N_EXPERTS] + l1[N_EXPERTS:] + l2 + rb_ref[...]
    m = jnp.max(logits, axis=0, keepdims=True)
    ex = jnp.exp(logits - m)
    p = ex / jnp.sum(ex, axis=0, keepdims=True)

    eg = EXPERTS_PER_GROUP
    iota_g = lax.broadcasted_iota(I32, (eg, tm), 0)
    best = None
    for g in range(N_EXPERT_GROUPS):
        pg = p[g * eg:(g + 1) * eg]
        m1 = jnp.max(pg, axis=0, keepdims=True)
        i1 = jnp.min(jnp.where(pg == m1, iota_g, eg), axis=0, keepdims=True)
        pg2 = jnp.where(iota_g == i1, -1.0, pg)
        m2 = jnp.max(pg2, axis=0, keepdims=True)
        i2 = jnp.min(jnp.where(pg2 == m2, iota_g, eg), axis=0, keepdims=True)
        sg = m1 + m2
        if best is None:
            best = (sg, m1, m2, i1, i2)
        else:
            better = sg > best[0]
            cand = (sg, m1, m2, i1 + g * eg, i2 + g * eg)
            best = tuple(jnp.where(better, c, o) for c, o in zip(cand, best))
    _, v1, v2, e1, e2 = best
    tot = v1 + v2
    e_ref[0:1, :] = e1
    e_ref[1:2, :] = e2
    w_ref[0:1, :] = v1 / tot
    w_ref[1:2, :] = v2 / tot

    iota_e = lax.broadcasted_iota(I32, (N_EXPERTS, tm), 0)
    oh1 = iota_e == e1
    oh2 = iota_e == e2
    oh = jnp.where(oh1, 1.0, jnp.where(oh2, 1.0, 0.0))
    before = jnp.dot(oh.astype(BF16), tri_ref[...], preferred_element_type=F32) + carry_ref[:, 0:1]
    rank_ref[0:1, :] = jnp.sum(jnp.where(oh1, before, 0.0), axis=0, keepdims=True).astype(I32)
    rank_ref[1:2, :] = jnp.sum(jnp.where(oh2, before, 0.0), axis=0, keepdims=True).astype(I32)
    carry = carry_ref[...] + jnp.sum(oh, axis=1, keepdims=True)
    carry_ref[...] = carry
    cnt_ref[...] = carry.astype(I32)


def _post_mix(x2, ret, y, wglu_bf, wout_bf, lng, lnb, rcat, rhi, rb):
    t, d = x2.shape
    tm = TM_POST
    tri = jnp.asarray(np.triu(np.ones((tm, tm), np.float32), 1), BF16)
    const = lambda i: (0, 0)
    tok = lambda i: (i, 0)
    lane = lambda i: (0, i)
    return pl.pallas_call(
        _post_mix_kernel,
        grid=(t // tm,),
        in_specs=[
            pl.BlockSpec((tm, d), tok),
            pl.BlockSpec((tm, RET_WIDTH), tok),
            pl.BlockSpec((tm, SSM_WIDTH), tok),
            pl.BlockSpec((SSM_WIDTH, SSM_WIDTH), const),
            pl.BlockSpec((RET_WIDTH + SSM_WIDTH, d), const),
            pl.BlockSpec((1, d), const),
            pl.BlockSpec((1, d), const),
            pl.BlockSpec((2 * N_EXPERTS, d), const),
            pl.BlockSpec((N_EXPERTS, d), const),
            pl.BlockSpec((N_EXPERTS, 1), const),
            pl.BlockSpec((tm, tm), const),
        ],
        out_specs=[
            pl.BlockSpec((tm, d), tok),
            pl.BlockSpec((TOP_K, tm), lane),
            pl.BlockSpec((TOP_K, tm), lane),
            pl.BlockSpec((TOP_K, tm), lane),
            pl.BlockSpec((N_EXPERTS, LANES), const),
        ],
        out_shape=[
            jax.ShapeDtypeStruct((t, d), F32),
            jax.ShapeDtypeStruct((TOP_K, t), I32),
            jax.ShapeDtypeStruct((TOP_K, t), F32),
            jax.ShapeDtypeStruct((TOP_K, t), I32),
            jax.ShapeDtypeStruct((N_EXPERTS, LANES), I32),
        ],
        scratch_shapes=[pltpu.VMEM((N_EXPERTS, LANES), F32)],
        compiler_params=_cparams(("arbitrary",)),
        name="post_mix",
    )(x2, ret, y, wglu_bf, wout_bf, lng, lnb, rcat, rhi, rb, tri)


def _dispatch_kernel(d0_ref, d1_ref, x_hbm, xb_in, xb_hbm, sem):
    del xb_in
    base = pl.program_id(0) * DISPATCH_TOK
    ngrp = DISPATCH_TOK // DISPATCH_GRP

    def wait_group(g):
        n = 2 * DISPATCH_GRP
        pltpu.make_async_copy(x_hbm.at[pl.ds(0, n)], xb_hbm.at[pl.ds(0, n)], sem.at[g % 2]).wait()

    for g in range(ngrp):
        def issue(t, c, g=g):
            tt = g * DISPATCH_GRP + t
            src = x_hbm.at[base + tt]
            pltpu.make_async_copy(src, xb_hbm.at[d0_ref[0, 0, tt]], sem.at[g % 2]).start()
            pltpu.make_async_copy(src, xb_hbm.at[d1_ref[0, 0, tt]], sem.at[g % 2]).start()
            return c

        lax.fori_loop(0, DISPATCH_GRP, issue, 0)
        if g >= 1:
            wait_group(g - 1)
    wait_group(ngrp - 1)


def _dispatch(x1, dest, xb_zero):
    t, d = x1.shape
    nstep = t // DISPATCH_TOK
    d0 = dest[0].reshape(nstep, 1, DISPATCH_TOK)
    d1 = dest[1].reshape(nstep, 1, DISPATCH_TOK)
    smem_blk = pl.BlockSpec((1, 1, DISPATCH_TOK), lambda i: (i, 0, 0), memory_space=pltpu.SMEM)
    return pl.pallas_call(
        _dispatch_kernel,
        grid=(nstep,),
        in_specs=[smem_blk, smem_blk,
                  pl.BlockSpec(memory_space=pl.ANY), pl.BlockSpec(memory_space=pl.ANY)],
        out_specs=pl.BlockSpec(memory_space=pl.ANY),
        out_shape=jax.ShapeDtypeStruct(xb_zero.shape, xb_zero.dtype),
        scratch_shapes=[pltpu.SemaphoreType.DMA((2,))],
        input_output_aliases={3: 0},
        compiler_params=_cparams(("arbitrary",)),
        name="dispatch",
    )(d0, d1, x1, xb_zero)


def _experts_kernel(se_ref, row_ref, nt_ref, nz_ref, xb_hbm, wg_ref, wu_ref, wd_ref, yb_hbm,
                    xs_ref, acc_ref, wgb_ref, wub_ref, wdb_ref, stg_ref, sem_in, sem_out):
    del se_ref
    s = pl.program_id(0)
    j = pl.program_id(1)
    nt = nt_ref[s]
    nz = nz_ref[s]
    row0 = row_ref[s]
    rt = ROW_TILE

    def rows(i):
        return pl.ds(pl.multiple_of(row0 + i * rt, rt), rt)

    def in_copy(i, slot):
        return pltpu.make_async_copy(xb_hbm.at[rows(i)], stg_ref.at[slot], sem_in.at[slot])

    def out_copy(i, slot):
        return pltpu.make_async_copy(stg_ref.at[slot], yb_hbm.at[rows(i)], sem_out.at[slot])

    def partial_out(i):
        r = pl.multiple_of(i * rt, rt)
        xi = xs_ref[pl.ds(r, rt), :]
        g = jnp.dot(xi, wgb_ref[...], preferred_element_type=F32)
        u = jnp.dot(xi, wub_ref[...], preferred_element_type=F32)
        hj = (g * _sigmoid(g) * u).astype(BF16)
        return r, jnp.dot(hj, wdb_ref[...], preferred_element_type=F32)

    @pl.when((nz > 0) & (j == 0))
    def _zero_tail():
        stg_ref[0] = jnp.zeros((rt, stg_ref.shape[2]), F32)

        def start(i, c):
            out_copy(i, 0).start()
            return c

        def wait(i, c):
            out_copy(i, 0).wait()
            return c

        lax.fori_loop(0, nz, start, 0)
        lax.fori_loop(0, nz, wait, 0)

    @pl.when(nt > 0)
    def _work():
        wgb_ref[...] = wg_ref[...].astype(BF16)
        wub_ref[...] = wu_ref[...].astype(BF16)
        wdb_ref[...] = wd_ref[...].astype(BF16)

        @pl.when(j == 0)
        def _first():
            in_copy(0, 0).start()

            def load(i, c):
                slot = i % 2

                @pl.when(i + 1 < nt)
                def _():
                    in_copy(i + 1, 1 - slot).start()

                in_copy(i, slot).wait()
                xs_ref[pl.ds(pl.multiple_of(i * rt, rt), rt), :] = stg_ref[slot].astype(BF16)
                return c

            lax.fori_loop(0, nt, load, 0)

            def tile(i, c):
                r, part = partial_out(i)
                acc_ref[pl.ds(r, rt), :] = part
                return c

            lax.fori_loop(0, nt, tile, 0)

        @pl.when((j > 0) & (j < N_DE_CHUNKS - 1))
        def _mid():
            def tile(i, c):
                r, part = partial_out(i)
                acc_ref[pl.ds(r, rt), :] += part
                return c

            lax.fori_loop(0, nt, tile, 0)

        @pl.when(j == N_DE_CHUNKS - 1)
        def _last():
            def tile(i, c):
                slot = i % 2

                @pl.when(i >= 2)
                def _():
                    out_copy(i - 2, slot).wait()

                r, part = partial_out(i)
                stg_ref[slot] = acc_ref[pl.ds(r, rt), :] + part
                out_copy(i, slot).start()
                return c

            lax.fori_loop(0, nt, tile, 0)

            @pl.when(nt >= 2)
            def _():
                out_copy(nt - 2, nt % 2).wait()

            out_copy(nt - 1, (nt - 1) % 2).wait()


def _experts(xb, se, row, ntl, nzl, w_gate, w_up, w_down):
    p, d = xb.shape
    nsuper = se.shape[0]
    last = N_DE_CHUNKS - 1

    def jj(s, j, nt_ref):
        return jnp.where(nt_ref[s] > 0, j, last)

    grid_spec = pltpu.PrefetchScalarGridSpec(
        num_scalar_prefetch=4,
        grid=(nsuper, N_DE_CHUNKS),
        in_specs=[
            pl.BlockSpec(memory_space=pl.ANY),
            pl.BlockSpec((None, d, DE_CHUNK), lambda s, j, se_r, row_r, nt_r, nz_r: (se_r[s], 0, jj(s, j, nt_r))),
            pl.BlockSpec((None, d, DE_CHUNK), lambda s, j, se_r, row_r, nt_r, nz_r: (se_r[s], 0, jj(s, j, nt_r))),
            pl.BlockSpec((None, DE_CHUNK, d), lambda s, j, se_r, row_r, nt_r, nz_r: (se_r[s], jj(s, j, nt_r), 0)),
        ],
        out_specs=pl.BlockSpec(memory_space=pl.ANY),
        scratch_shapes=[
            pltpu.VMEM((SUPER, d), BF16),
            pltpu.VMEM((SUPER, d), F32),
            pltpu.VMEM((d, DE_CHUNK), BF16),
            pltpu.VMEM((d, DE_CHUNK), BF16),
            pltpu.VMEM((DE_CHUNK, d), BF16),
            pltpu.VMEM((2, ROW_TILE, d), F32),
            pltpu.SemaphoreType.DMA((2,)),
            pltpu.SemaphoreType.DMA((2,)),
        ],
    )
    return pl.pallas_call(
        _experts_kernel,
        grid_spec=grid_spec,
        out_shape=jax.ShapeDtypeStruct((p, d), F32),
        compiler_params=_cparams(("arbitrary", "arbitrary")),
        name="experts",
    )(se, row, ntl, nzl, xb, w_gate, w_up, w_down)


def _combine_kernel(d0_ref, d1_ref, x1_ref, w_ref, lng_ref, lnb_ref, yb_hbm, o_ref, ybuf_ref, sem):
    tm = TM_COMB

    def issue(t, c):
        pltpu.make_async_copy(yb_hbm.at[d0_ref[0, 0, t]], ybuf_ref.at[0, t], sem.at[0]).start()
        pltpu.make_async_copy(yb_hbm.at[d1_ref[0, 0, t]], ybuf_ref.at[1, t], sem.at[0]).start()
        return c

    lax.fori_loop(0, tm, issue, 0)
    wpad = jnp.concatenate([w_ref[...], jnp.zeros((LANES - TOP_K, tm), F32)], axis=0)
    wt = wpad.T
    x1 = x1_ref[...]
    pltpu.make_async_copy(yb_hbm.at[pl.ds(0, tm)], ybuf_ref.at[0], sem.at[0]).wait()
    pltpu.make_async_copy(yb_hbm.at[pl.ds(0, tm)], ybuf_ref.at[1], sem.at[0]).wait()
    moe = ybuf_ref[0] * wt[:, 0:1] + ybuf_ref[1] * wt[:, 1:2]
    o_ref[...] = _layer_norm_rows(DEEPNORM_ALPHA * x1 + moe, lng_ref[...], lnb_ref[...])


def _combine(x1, dest, wts, lng, lnb, yb):
    t, d = x1.shape
    tm = TM_COMB
    nstep = t // tm
    d0 = dest[0].reshape(nstep, 1, tm)
    d1 = dest[1].reshape(nstep, 1, tm)
    smem_blk = pl.BlockSpec((1, 1, tm), lambda i: (i, 0, 0), memory_space=pltpu.SMEM)
    const = lambda i: (0, 0)
    return pl.pallas_call(
        _combine_kernel,
        grid=(nstep,),
        in_specs=[smem_blk, smem_blk,
                  pl.BlockSpec((tm, d), lambda i: (i, 0)),
                  pl.BlockSpec((TOP_K, tm), lambda i: (0, i)),
                  pl.BlockSpec((1, d), const),
                  pl.BlockSpec((1, d), const),
                  pl.BlockSpec(memory_space=pl.ANY)],
        out_specs=pl.BlockSpec((tm, d), lambda i: (i, 0)),
        out_shape=jax.ShapeDtypeStruct((t, d), F32),
        scratch_shapes=[pltpu.VMEM((TOP_K, tm, d), F32), pltpu.SemaphoreType.DMA((1,))],
        compiler_params=_cparams(("arbitrary",)),
        name="combine",
    )(d0, d1, x1, wts, lng, lnb, yb)


def _rotary_tables(seq):
    inv = 1.0 / (ROPE_BASE ** (jnp.arange(0, RET_HEAD_DIM, 2, dtype=F32) / RET_HEAD_DIM))
    ang = jnp.arange(seq, dtype=F32)[:, None] * inv[None, :]
    return jnp.cos(ang), jnp.sin(ang)


def _qk_column_perm():
    half = RET_HEAD_DIM // 2
    per_head = np.concatenate([np.arange(half) * 2, np.arange(half) * 2 + 1])
    qk = np.concatenate([h * RET_HEAD_DIM + per_head for h in range(2 * RET_HEADS)])
    return np.concatenate([qk, np.arange(2 * RET_WIDTH, IN_PROJ_WIDTH)])


def _routing_tables(counts, n_rows):
    padded = (counts + ROW_TILE - 1) // ROW_TILE * ROW_TILE
    pend = jnp.cumsum(padded)
    pstart = pend - padded
    ntile = padded // ROW_TILE
    nsup = (ntile + TILES_PER_SUPER - 1) // TILES_PER_SUPER
    send = jnp.cumsum(nsup)
    sstart = send - nsup
    nsuper = n_rows // SUPER + N_EXPERTS
    s = jnp.arange(nsuper, dtype=I32)
    total = send[-1]
    which = lambda q: jnp.clip(jnp.searchsorted(send, q, side='right'), 0, N_EXPERTS - 1).astype(I32)
    se = which(s)
    valid = s < total
    k = s - sstart[se]
    row = pstart[se] + k * SUPER
    nt = jnp.clip(ntile[se] - k * TILES_PER_SUPER, 0, TILES_PER_SUPER)
    idle = s - total
    tail_row = pend[-1] + idle * SUPER
    nz = jnp.clip((n_rows - tail_row) // ROW_TILE, 0, TILES_PER_SUPER)
    se = jnp.where(valid, se, which(total - 1))
    row = jnp.where(valid, row, jnp.minimum(tail_row, n_rows - ROW_TILE))
    nt = jnp.where(valid, nt, 0)
    nz = jnp.where(valid, 0, nz)
    return pstart.astype(I32), se.astype(I32), row.astype(I32), nt.astype(I32), nz.astype(I32)


def kernel(x, w_in, w_out, ssm_lambda_re, ssm_lambda_im, ssm_b_re, ssm_b_im, ssm_c_re, ssm_c_im,
           ssm_d, ssm_log_dt, w_glu, ln1_g, ln1_b, ln2_g, ln2_b, router_w, router_b,
           w_gate, w_up, w_down):
    batch, seq, d = x.shape
    t = batch * seq
    n_assign = t * TOP_K
    n_rows = n_assign + N_EXPERTS * ROW_TILE
    assert seq % TM_PROJ == 0 and seq % RET_CHUNK == 0 and seq % S5_TL == 0
    assert t % TM_POST == 0 and t % TM_COMB == 0 and t % DISPATCH_TOK == 0 and n_rows % SUPER == 0

    cos, sin = _rotary_tables(seq)
    perm = _qk_column_perm()
    rw_t = router_w.astype(F32).T
    rhi = rw_t.astype(BF16)
    rlo = (rw_t - rhi.astype(F32)).astype(BF16)
    rcat = jnp.concatenate([rhi, rlo], axis=0)
    rb = router_b.astype(F32).reshape(N_EXPERTS, 1)

    x2 = x.reshape(t, d)
    for l in range(DEPTH):
        w_bf = w_in[l][:, perm].astype(BF16)
        proj = _in_proj(x2, w_bf, cos, sin, seq)
        ret = _retention(proj, batch, seq)
        tables = _s5_tables(ssm_lambda_re[l], ssm_lambda_im[l], ssm_b_re[l], ssm_b_im[l],
                            ssm_c_re[l], ssm_c_im[l], ssm_d[l], ssm_log_dt[l])
        y = _s5(proj, tables, batch, seq).reshape(t, SSM_WIDTH)
        x1, e, wts, rank, cnt = _post_mix(
            x2, ret, y, w_glu[l].astype(BF16), w_out[l].astype(BF16),
            ln1_g[l].reshape(1, d), ln1_b[l].reshape(1, d), rcat, rhi, rb)
        pstart, se, row, ntl, nzl = _routing_tables(cnt[:, 0], n_rows)
        dest = pstart[e] + rank
        xb = _dispatch(x1, dest, jnp.zeros((n_rows, d), F32))
        yb = _experts(xb, se, row, ntl, nzl, w_gate[l], w_up[l], w_down[l])
        x2 = _combine(x1, dest, wts, ln2_g[l].reshape(1, d), ln2_b[l].reshape(1, d), yb)
    return x2.reshape(batch, seq, d)
```

```python
import functools
import math

import numpy as np
import jax
import jax.numpy as jnp
from jax import lax
from jax.experimental import pallas as pl
from jax.experimental.pallas import tpu as pltpu

F32 = jnp.float32
BF16 = jnp.bfloat16
I32 = jnp.int32

D_MODEL = 2048
DEPTH = 2
RET_WIDTH = 1024
SSM_WIDTH = 1024
RET_HEAD_DIM = 256
RET_HEADS = RET_WIDTH // RET_HEAD_DIM
ROPE_BASE = 10000.0
SSM_GROUP = 16
SSM_GROUPS = SSM_WIDTH // SSM_GROUP
SSM_STATE = 64
IN_PROJ_WIDTH = 4 * RET_WIDTH + SSM_WIDTH
N_EXPERTS = 32
N_EXPERT_GROUPS = 4
EXPERTS_PER_GROUP = N_EXPERTS // N_EXPERT_GROUPS
TOP_K = 2
D_EXPERT = D_MODEL // 2
LN_EPS = 1e-5
DEEPNORM_ALPHA = (2.0 * DEPTH) ** 0.25

LANES = 128
SUBLANES = 8
VMEM_LIMIT = 56 * 1024 * 1024

TM_PROJ = 1024
TN_PROJ = 1024
RET_CHUNK = 256
S5_TL = 128
S5_LT = SSM_WIDTH // LANES
S5_NSTATE = (LANES // SSM_GROUP) * SSM_STATE
TM_POST = 256
TM_COMB = 256
ROW_TILE = 256
TILES_PER_SUPER = 8
SUPER = ROW_TILE * TILES_PER_SUPER
DE_CHUNK = 256
N_DE_CHUNKS = D_EXPERT // DE_CHUNK
TM_DISP = 512


def _sigmoid(x):
    return 1.0 / (1.0 + jnp.exp(-x))


def _cparams(sem, vmem=VMEM_LIMIT):
    return pltpu.CompilerParams(dimension_semantics=sem, vmem_limit_bytes=vmem)


def _in_proj_kernel(x_ref, w_ref, cos_ref, sin_ref, o_ref):
    j = pl.program_id(1)
    acc = jnp.dot(x_ref[...].astype(BF16), w_ref[...], preferred_element_type=F32)

    @pl.when(j < 2)
    def _rot():
        scale = jnp.where(j == 1, RET_HEAD_DIM ** -0.5, 1.0).astype(F32)
        c = cos_ref[...] * scale
        s = sin_ref[...] * scale
        half = RET_HEAD_DIM // 2
        for h in range(RET_HEADS):
            lo = h * RET_HEAD_DIM
            t1 = acc[:, lo:lo + half]
            t2 = acc[:, lo + half:lo + RET_HEAD_DIM]
            o_ref[:, lo:lo + half] = (t1 * c - t2 * s).astype(BF16)
            o_ref[:, lo + half:lo + RET_HEAD_DIM] = (t1 * s + t2 * c).astype(BF16)

    @pl.when(j == 3)
    def _gate():
        o_ref[...] = (acc * _sigmoid(acc)).astype(BF16)

    @pl.when((j == 2) | (j == 4))
    def _plain():
        o_ref[...] = acc.astype(BF16)


def _in_proj(x2, w_bf, cos, sin, seq):
    t, d = x2.shape
    n = w_bf.shape[1]
    tiles_per_seq = seq // TM_PROJ
    return pl.pallas_call(
        _in_proj_kernel,
        grid=(t // TM_PROJ, n // TN_PROJ),
        in_specs=[
            pl.BlockSpec((TM_PROJ, d), lambda i, j: (i, 0)),
            pl.BlockSpec((d, TN_PROJ), lambda i, j: (0, j)),
            pl.BlockSpec((TM_PROJ, LANES), lambda i, j: (i % tiles_per_seq, 0)),
            pl.BlockSpec((TM_PROJ, LANES), lambda i, j: (i % tiles_per_seq, 0)),
        ],
        out_specs=pl.BlockSpec((TM_PROJ, TN_PROJ), lambda i, j: (i, j)),
        out_shape=jax.ShapeDtypeStruct((t, n), BF16),
        compiler_params=_cparams(("arbitrary", "arbitrary")),
        name="in_proj",
    )(x2, w_bf, cos, sin)


def _ret_kernel(q_ref, k_ref, v_ref, g_ref, mask_ref, qd_ref, kd_ref, o_ref, r_ref):
    n = pl.program_id(2)

    @pl.when(n == 0)
    def _init():
        r_ref[...] = jnp.zeros_like(r_ref)

    q = q_ref[...]
    k = k_ref[...]
    v = v_ref[...]
    s = lax.dot_general(q, k, (((1,), (1,)), ((), ())), preferred_element_type=F32)
    s = s * mask_ref[...]
    inner = jnp.dot(s.astype(BF16), v, preferred_element_type=F32)
    qd = qd_ref[...]
    r_prev = r_ref[...]
    cross = jnp.dot((q.astype(F32) * qd).astype(BF16), r_prev.astype(BF16), preferred_element_type=F32)
    o = inner + cross
    kdec = (k.astype(F32) * kd_ref[...]).astype(BF16)
    kv = lax.dot_general(kdec, v, (((0,), (0,)), ((), ())), preferred_element_type=F32)
    r_ref[...] = r_prev * qd[RET_CHUNK - 1:RET_CHUNK, :] + kv
    mu = jnp.mean(o, axis=-1, keepdims=True)
    oc = o - mu
    var = jnp.mean(oc * oc, axis=-1, keepdims=True)
    o_ref[...] = (g_ref[...].astype(F32) * (oc * lax.rsqrt(var + LN_EPS))).astype(BF16)


def _retention_tables():
    c = RET_CHUNK
    log_gamma = np.log(1.0 - 2.0 ** (-5.0 - np.arange(RET_HEADS, dtype=np.float64)))
    idx = np.arange(c, dtype=np.float64)
    diff = idx[:, None] - idx[None, :]
    mask = np.where(diff >= 0, np.exp(log_gamma[:, None, None] * np.maximum(diff, 0.0)), 0.0)
    qd = np.exp(log_gamma[:, None] * (idx + 1.0)[None, :])
    kd = np.exp(log_gamma[:, None] * (c - 1.0 - idx)[None, :])
    bc = lambda a: np.broadcast_to(a[:, :, None], (RET_HEADS, c, RET_HEAD_DIM)).astype(np.float32)
    return mask.astype(np.float32), bc(qd), bc(kd)


def _retention(proj, batch, seq):
    t = proj.shape[0]
    nch = seq // RET_CHUNK
    mask, qd, kd = _retention_tables()
    row = lambda b, h, n: b * nch + n
    blk = (RET_CHUNK, RET_HEAD_DIM)
    hb = RET_WIDTH // RET_HEAD_DIM
    return pl.pallas_call(
        _ret_kernel,
        grid=(batch, RET_HEADS, nch),
        in_specs=[
            pl.BlockSpec(blk, lambda b, h, n: (row(b, h, n), h)),
            pl.BlockSpec(blk, lambda b, h, n: (row(b, h, n), hb + h)),
            pl.BlockSpec(blk, lambda b, h, n: (row(b, h, n), 2 * hb + h)),
            pl.BlockSpec(blk, lambda b, h, n: (row(b, h, n), 3 * hb + h)),
            pl.BlockSpec((None, RET_CHUNK, RET_CHUNK), lambda b, h, n: (h, 0, 0)),
            pl.BlockSpec((None, RET_CHUNK, RET_HEAD_DIM), lambda b, h, n: (h, 0, 0)),
            pl.BlockSpec((None, RET_CHUNK, RET_HEAD_DIM), lambda b, h, n: (h, 0, 0)),
        ],
        out_specs=pl.BlockSpec(blk, lambda b, h, n: (row(b, h, n), h)),
        out_shape=jax.ShapeDtypeStruct((t, RET_WIDTH), BF16),
        scratch_shapes=[pltpu.VMEM((RET_HEAD_DIM, RET_HEAD_DIM), F32)],
        compiler_params=_cparams(("arbitrary", "arbitrary", "arbitrary")),
        name="retention",
    )(proj, proj, proj, proj, jnp.asarray(mask), jnp.asarray(qd), jnp.asarray(kd))


def _s5_kernel(u_ref, bm_ref, cm_ref, lam_ref, d_ref, y_ref, us_ref, ut_ref, h_ref, st_ref):
    n = pl.program_id(1)
    tl = S5_TL
    nb = SUBLANES
    ns = S5_NSTATE

    @pl.when(n == 0)
    def _init():
        st_ref[...] = jnp.zeros_like(st_ref)

    for b in range(nb):
        us_ref[b * tl:(b + 1) * tl, :] = u_ref[b].astype(F32)

    def relayout(t, c):
        ut_ref[pl.ds(pl.multiple_of(t * nb, nb), nb), :] = us_ref[pl.ds(t, nb, stride=tl), :]
        return c

    lax.fori_loop(0, tl, relayout, 0, unroll=8)
    ut = ut_ref[...]
    h_ref[...] = jnp.dot(ut.astype(BF16), bm_ref[...], preferred_element_type=F32)

    ar = jnp.broadcast_to(lam_ref[0:1, :], (nb, ns))
    ai = jnp.broadcast_to(lam_ref[1:2, :], (nb, ns))

    def step(t, carry):
        sr, si = carry
        r0 = pl.multiple_of(t * nb, nb)
        br = h_ref[pl.ds(r0, nb), 0:ns]
        bi = h_ref[pl.ds(r0, nb), ns:2 * ns]
        nr = ar * sr - ai * si + br
        ni = ar * si + ai * sr + bi
        h_ref[pl.ds(r0, nb), 0:ns] = nr
        h_ref[pl.ds(r0, nb), ns:2 * ns] = ni
        return nr, ni

    sr, si = lax.fori_loop(0, tl, step, (st_ref[0:nb, :], st_ref[nb:2 * nb, :]), unroll=4)
    st_ref[0:nb, :] = sr
    st_ref[nb:2 * nb, :] = si

    y = jnp.dot(h_ref[...].astype(BF16), cm_ref[...], preferred_element_type=F32) + ut * d_ref[...]
    ut_ref[...] = y
    for b in range(nb):
        y_ref[b] = ut_ref[pl.ds(b, tl, stride=nb), :].astype(BF16)


def _s5_tables(lam_re, lam_im, b_re, b_im, c_re, c_im, d, log_dt):
    lam = lax.complex(lam_re.astype(F32), lam_im.astype(F32))
    dt = jnp.exp(log_dt.astype(F32))[:, None]
    lam_bar = jnp.exp(lam * dt)
    b_bar = ((lam_bar - 1.0) / lam)[..., None] * lax.complex(b_re.astype(F32), b_im.astype(F32))
    gpt = LANES // SSM_GROUP
    eye = jnp.eye(gpt, dtype=F32)

    def bdiag_in(m):
        m = m.reshape(S5_LT, gpt, SSM_STATE, SSM_GROUP)
        return jnp.einsum('jgpi,gh->jgihp', m, eye).reshape(S5_LT, LANES, gpt * SSM_STATE)

    def bdiag_out(m):
        m = m.reshape(S5_LT, gpt, SSM_GROUP, SSM_STATE)
        return jnp.einsum('jgop,gh->jgpho', m, eye).reshape(S5_LT, gpt * SSM_STATE, LANES)

    bmat = jnp.concatenate([bdiag_in(jnp.real(b_bar)), bdiag_in(jnp.imag(b_bar))], axis=-1).astype(BF16)
    cmat = jnp.concatenate([bdiag_out(c_re.astype(F32)), -bdiag_out(c_im.astype(F32))], axis=1).astype(BF16)
    lam_t = jnp.stack([jnp.real(lam_bar).reshape(S5_LT, S5_NSTATE),
                       jnp.imag(lam_bar).reshape(S5_LT, S5_NSTATE)], axis=1)
    d_t = d.astype(F32).reshape(S5_LT, 1, LANES)
    return bmat, cmat, lam_t, d_t


def _s5(proj, tables, batch, seq):
    bmat, cmat, lam_t, d_t = tables
    assert batch == SUBLANES
    proj3 = proj.reshape(batch, seq, IN_PROJ_WIDTH)
    ucol = (4 * RET_WIDTH) // LANES
    tl = S5_TL
    return pl.pallas_call(
        _s5_kernel,
        grid=(S5_LT, seq // tl),
        in_specs=[
            pl.BlockSpec((batch, tl, LANES), lambda j, n: (0, n, ucol + j)),
            pl.BlockSpec((None, LANES, 2 * S5_NSTATE), lambda j, n: (j, 0, 0)),
            pl.BlockSpec((None, 2 * S5_NSTATE, LANES), lambda j, n: (j, 0, 0)),
            pl.BlockSpec((None, 2, S5_NSTATE), lambda j, n: (j, 0, 0)),
            pl.BlockSpec((None, 1, LANES), lambda j, n: (j, 0, 0)),
        ],
        out_specs=pl.BlockSpec((batch, tl, LANES), lambda j, n: (0, n, j)),
        out_shape=jax.ShapeDtypeStruct((batch, seq, SSM_WIDTH), BF16),
        scratch_shapes=[
            pltpu.VMEM((batch * tl, LANES), F32),
            pltpu.VMEM((batch * tl, LANES), F32),
            pltpu.VMEM((batch * tl, 2 * S5_NSTATE), F32),
            pltpu.VMEM((2 * SUBLANES, S5_NSTATE), F32),
        ],
        compiler_params=_cparams(("arbitrary", "arbitrary")),
        name="s5",
    )(proj3, bmat, cmat, lam_t, d_t)


def _layer_norm_rows(r, g, b):
    mu = jnp.mean(r, axis=-1, keepdims=True)
    rc = r - mu
    var = jnp.mean(rc * rc, axis=-1, keepdims=True)
    return rc * lax.rsqrt(var + LN_EPS) * g + b


def _post_mix_kernel(x_ref, ret_ref, y_ref, wglu_ref, wout_ref, lng_ref, lnb_ref,
                     rcat_ref, rhi_ref, rb_ref, tri_ref,
                     x1_ref, e_ref, w_ref, rank_ref, cnt_ref, carry_ref):
    i = pl.program_id(0)
    tm = TM_POST

    @pl.when(i == 0)
    def _init():
        carry_ref[...] = jnp.zeros_like(carry_ref)

    ya = jax.nn.gelu(y_ref[...].astype(F32))
    z = jnp.dot(ya.astype(BF16), wglu_ref[...], preferred_element_type=F32)
    ssm = (ya * _sigmoid(z)).astype(BF16)
    h = jnp.dot(ret_ref[...], wout_ref[0:RET_WIDTH, :], preferred_element_type=F32)
    h = h + jnp.dot(ssm, wout_ref[RET_WIDTH:, :], preferred_element_type=F32)
    x1 = _layer_norm_rows(DEEPNORM_ALPHA * x_ref[...] + h, lng_ref[...], lnb_ref[...])
    x1_ref[...] = x1

    xh = x1.astype(BF16)
    xl = (x1 - xh.astype(F32)).astype(BF16)
    nt = (((1,), (1,)), ((), ()))
    l1 = lax.dot_general(rcat_ref[...], xh, nt, preferred_element_type=F32)
    l2 = lax.dot_general(rhi_ref[...], xl, nt, preferred_element_type=F32)
    logits = l1[0:N_EXPERTS] + l1[N_EXPERTS:] + l2 + rb_ref[...]
    m = jnp.max(logits, axis=0, keepdims=True)
    ex = jnp.exp(logits - m)
    p = ex / jnp.sum(ex, axis=0, keepdims=True)

    eg = EXPERTS_PER_GROUP
    iota_g = lax.broadcasted_iota(I32, (eg, tm), 0)
    best = None
    for g in range(N_EXPERT_GROUPS):
        pg = p[g * eg:(g + 1) * eg]
        m1 = jnp.max(pg, axis=0, keepdims=True)
        i1 = jnp.min(jnp.where(pg == m1, iota_g, eg), axis=0, keepdims=True)
        pg2 = jnp.where(iota_g == i1, -1.0, pg)
        m2 = jnp.max(pg2, axis=0, keepdims=True)
        i2 = jnp.min(jnp.where(pg2 == m2, iota_g, eg), axis=0, keepdims=True)
        sg = m1 + m2
        if best is None:
            best = (sg, m1, m2, i1, i2)
        else:
            better = sg > best[0]
            cand = (sg, m1, m2, i1 + g * eg, i2 + g * eg)
            best = tuple(jnp.where(better, c, o) for c, o in zip(cand, best))
    _, v1, v2, e1, e2 = best
    tot = v1 + v2
    e_ref[0:1, :] = e1
    e_ref[1:2, :] = e2
    w_ref[0:1, :] = v1 / tot
    w_ref[1:2, :] = v2 / tot

    iota_e = lax.broadcasted_iota(I32, (N_EXPERTS, tm), 0)
    oh1 = iota_e == e1
    oh2 = iota_e == e2
    oh = jnp.where(oh1, 1.0, jnp.where(oh2, 1.0, 0.0))
    before = jnp.dot(oh.astype(BF16), tri_ref[...], preferred_element_type=F32) + carry_ref[:, 0:1]
    rank_ref[0:1, :] = jnp.sum(jnp.where(oh1, before, 0.0), axis=0, keepdims=True).astype(I32)
    rank_ref[1:2, :] = jnp.sum(jnp.where(oh2, before, 0.0), axis=0, keepdims=True).astype(I32)
    carry = carry_ref[...] + jnp.sum(oh, axis=1, keepdims=True)
    carry_ref[...] = carry
    cnt_ref[...] = carry.astype(I32)


def _post_mix(x2, ret, y, wglu_bf, wout_bf, lng, lnb, rcat, rhi, rb):
    t, d = x2.shape
    tm = TM_POST
    tri = jnp.asarray(np.triu(np.ones((tm, tm), np.float32), 1), BF16)
    const = lambda i: (0, 0)
    tok = lambda i: (i, 0)
    lane = lambda i: (0, i)
    return pl.pallas_call(
        _post_mix_kernel,
        grid=(t // tm,),
        in_specs=[
            pl.BlockSpec((tm, d), tok),
            pl.BlockSpec((tm, RET_WIDTH), tok),
            pl.BlockSpec((tm, SSM_WIDTH), tok),
            pl.BlockSpec((SSM_WIDTH, SSM_WIDTH), const),
            pl.BlockSpec((RET_WIDTH + SSM_WIDTH, d), const),
            pl.BlockSpec((1, d), const),
            pl.BlockSpec((1, d), const),
            pl.BlockSpec((2 * N_EXPERTS, d), const),
            pl.BlockSpec((N_EXPERTS, d), const),
            pl.BlockSpec((N_EXPERTS, 1), const),
            pl.BlockSpec((tm, tm), const),
        ],
        out_specs=[
            pl.BlockSpec((tm, d), tok),
            pl.BlockSpec((TOP_K, tm), lane),
            pl.BlockSpec((TOP_K, tm), lane),
            pl.BlockSpec((TOP_K, tm), lane),
            pl.BlockSpec((N_EXPERTS, LANES), const),
        ],
        out_shape=[
            jax.ShapeDtypeStruct((t, d), F32),
            jax.ShapeDtypeStruct((TOP_K, t), I32),
            jax.ShapeDtypeStruct((TOP_K, t), F32),
            jax.ShapeDtypeStruct((TOP_K, t), I32),
            jax.ShapeDtypeStruct((N_EXPERTS, LANES), I32),
        ],
        scratch_shapes=[pltpu.VMEM((N_EXPERTS, LANES), F32)],
        compiler_params=_cparams(("arbitrary",)),
        name="post_mix",
    )(x2, ret, y, wglu_bf, wout_bf, lng, lnb, rcat, rhi, rb, tri)


def _dispatch_kernel(zrow_ref, d0_ref, d1_ref, x_ref, xb_hbm, zbuf_ref, sem, zsem):
    tm = TM_DISP

    @pl.when(pl.program_id(0) == 0)
    def _zero_fill():
        zbuf_ref[...] = jnp.zeros_like(zbuf_ref)

        def zcopy(k):
            rows = pl.ds(pl.multiple_of(zrow_ref[k], ROW_TILE), ROW_TILE)
            return pltpu.make_async_copy(zbuf_ref, xb_hbm.at[rows], zsem.at[0])

        def start(k, c):
            @pl.when(zrow_ref[k] >= 0)
            def _():
                zcopy(k).start()
            return c

        def wait(k, c):
            @pl.when(zrow_ref[k] >= 0)
            def _():
                zcopy(k).wait()
            return c

        lax.fori_loop(0, 2 * N_EXPERTS, start, 0)
        lax.fori_loop(0, 2 * N_EXPERTS, wait, 0)

    def issue(t, c):
        src = x_ref.at[t]
        pltpu.make_async_copy(src, xb_hbm.at[d0_ref[0, 0, t]], sem.at[0]).start()
        pltpu.make_async_copy(src, xb_hbm.at[d1_ref[0, 0, t]], sem.at[0]).start()
        return c

    lax.fori_loop(0, tm, issue, 0)
    for _ in range(TOP_K):
        pltpu.make_async_copy(x_ref, xb_hbm.at[pl.ds(0, tm)], sem.at[0]).wait()


def _dispatch(x1, dest, zrow, n_rows):
    t, d = x1.shape
    tm = TM_DISP
    nstep = t // tm
    d0 = dest[0].reshape(nstep, 1, tm)
    d1 = dest[1].reshape(nstep, 1, tm)
    smem_blk = pl.BlockSpec((1, 1, tm), lambda i, z: (i, 0, 0), memory_space=pltpu.SMEM)
    grid_spec = pltpu.PrefetchScalarGridSpec(
        num_scalar_prefetch=1,
        grid=(nstep,),
        in_specs=[smem_blk, smem_blk, pl.BlockSpec((tm, d), lambda i, z: (i, 0))],
        out_specs=pl.BlockSpec(memory_space=pl.ANY),
        scratch_shapes=[pltpu.VMEM((ROW_TILE, d), F32),
                        pltpu.SemaphoreType.DMA((1,)), pltpu.SemaphoreType.DMA((1,))],
    )
    return pl.pallas_call(
        _dispatch_kernel,
        grid_spec=grid_spec,
        out_shape=jax.ShapeDtypeStruct((n_rows, d), F32),
        compiler_params=_cparams(("arbitrary",)),
        name="dispatch",
    )(zrow, d0, d1, x1)


def _experts_kernel(se_ref, row_ref, nt_ref, nz_ref, xb_hbm, wg_ref, wu_ref, wd_ref, yb_hbm,
                    xs_ref, acc_ref, wgb_ref, wub_ref, wdb_ref, stg_ref, sem_in, sem_out):
    del se_ref
    s = pl.program_id(0)
    j = pl.program_id(1)
    nt = nt_ref[s]
    nz = nz_ref[s]
    row0 = row_ref[s]
    rt = ROW_TILE

    def rows(i):
        return pl.ds(pl.multiple_of(row0 + i * rt, rt), rt)

    def in_copy(i, slot):
        return pltpu.make_async_copy(xb_hbm.at[rows(i)], stg_ref.at[slot], sem_in.at[slot])

    def out_copy(i, slot):
        return pltpu.make_async_copy(stg_ref.at[slot], yb_hbm.at[rows(i)], sem_out.at[slot])

    def partial_out(i):
        r = pl.multiple_of(i * rt, rt)
        xi = xs_ref[pl.ds(r, rt), :]
        g = jnp.dot(xi, wgb_ref[...], preferred_element_type=F32)
        u = jnp.dot(xi, wub_ref[...], preferred_element_type=F32)
        hj = (g * _sigmoid(g) * u).astype(BF16)
        return r, jnp.dot(hj, wdb_ref[...], preferred_element_type=F32)

    @pl.when((nz > 0) & (j == 0))
    def _zero_tail():
        stg_ref[0] = jnp.zeros((rt, stg_ref.shape[2]), F32)

        def start(i, c):
            out_copy(i, 0).start()
            return c

        def wait(i, c):
            out_copy(i, 0).wait()
            return c

        lax.fori_loop(0, nz, start, 0)
        lax.fori_loop(0, nz, wait, 0)

    @pl.when(nt > 0)
    def _work():
        wgb_ref[...] = wg_ref[...].astype(BF16)
        wub_ref[...] = wu_ref[...].astype(BF16)
        wdb_ref[...] = wd_ref[...].astype(BF16)

        @pl.when(j == 0)
        def _first():
            in_copy(0, 0).start()

            def load(i, c):
                slot = i % 2

                @pl.when(i + 1 < nt)
                def _():
                    in_copy(i + 1, 1 - slot).start()

                in_copy(i, slot).wait()
                xs_ref[pl.ds(pl.multiple_of(i * rt, rt), rt), :] = stg_ref[slot].astype(BF16)
                return c

            lax.fori_loop(0, nt, load, 0)

            def tile(i, c):
                r, part = partial_out(i)
                acc_ref[pl.ds(r, rt), :] = part
                return c

            lax.fori_loop(0, nt, tile, 0)

        @pl.when((j > 0) & (j < N_DE_CHUNKS - 1))
        def _mid():
            def tile(i, c):
                r, part = partial_out(i)
                acc_ref[pl.ds(r, rt), :] += part
                return c

            lax.fori_loop(0, nt, tile, 0)

        @pl.when(j == N_DE_CHUNKS - 1)
        def _last():
            def tile(i, c):
                slot = i % 2

                @pl.when(i >= 2)
                def _():
                    out_copy(i - 2, slot).wait()

                r, part = partial_out(i)
                stg_ref[slot] = acc_ref[pl.ds(r, rt), :] + part
                out_copy(i, slot).start()
                return c

            lax.fori_loop(0, nt, tile, 0)

            @pl.when(nt >= 2)
            def _():
                out_copy(nt - 2, nt % 2).wait()

            out_copy(nt - 1, (nt - 1) % 2).wait()


def _experts(xb, se, row, ntl, nzl, w_gate, w_up, w_down, layer):
    p, d = xb.shape
    nsuper = se.shape[0]
    last = N_DE_CHUNKS - 1

    def jj(s, j, nt_ref):
        return jnp.where(nt_ref[s] > 0, j, last)

    grid_spec = pltpu.PrefetchScalarGridSpec(
        num_scalar_prefetch=4,
        grid=(nsuper, N_DE_CHUNKS),
        in_specs=[
            pl.BlockSpec(memory_space=pl.ANY),
            pl.BlockSpec((None, None, d, DE_CHUNK),
                         lambda s, j, se_r, row_r, nt_r, nz_r: (layer, se_r[s], 0, jj(s, j, nt_r))),
            pl.BlockSpec((None, None, d, DE_CHUNK),
                         lambda s, j, se_r, row_r, nt_r, nz_r: (layer, se_r[s], 0, jj(s, j, nt_r))),
            pl.BlockSpec((None, None, DE_CHUNK, d),
                         lambda s, j, se_r, row_r, nt_r, nz_r: (layer, se_r[s], jj(s, j, nt_r), 0)),
        ],
        out_specs=pl.BlockSpec(memory_space=pl.ANY),
        scratch_shapes=[
            pltpu.VMEM((SUPER, d), BF16),
            pltpu.VMEM((SUPER, d), F32),
            pltpu.VMEM((d, DE_CHUNK), BF16),
            pltpu.VMEM((d, DE_CHUNK), BF16),
            pltpu.VMEM((DE_CHUNK, d), BF16),
            pltpu.VMEM((2, ROW_TILE, d), F32),
            pltpu.SemaphoreType.DMA((2,)),
            pltpu.SemaphoreType.DMA((2,)),
        ],
    )
    return pl.pallas_call(
        _experts_kernel,
        grid_spec=grid_spec,
        out_shape=jax.ShapeDtypeStruct((p, d), F32),
        compiler_params=_cparams(("arbitrary", "arbitrary")),
        name="experts",
    )(se, row, ntl, nzl, xb, w_gate, w_up, w_down)


def _combine_kernel(d0_ref, d1_ref, x1_ref, w_ref, lng_ref, lnb_ref, yb_hbm, o_ref, ybuf_ref, sem):
    tm = TM_COMB

    def issue(t, c):
        pltpu.make_async_copy(yb_hbm.at[d0_ref[0, 0, t]], ybuf_ref.at[0, t], sem.at[0]).start()
        pltpu.make_async_copy(yb_hbm.at[d1_ref[0, 0, t]], ybuf_ref.at[1, t], sem.at[0]).start()
        return c

    lax.fori_loop(0, tm, issue, 0)
    wpad = jnp.concatenate([w_ref[...], jnp.zeros((LANES - TOP_K, tm), F32)], axis=0)
    wt = wpad.T
    x1 = x1_ref[...]
    pltpu.make_async_copy(yb_hbm.at[pl.ds(0, tm)], ybuf_ref.at[0], sem.at[0]).wait()
    pltpu.make_async_copy(yb_hbm.at[pl.ds(0, tm)], ybuf_ref.at[1], sem.at[0]).wait()
    moe = ybuf_ref[0] * wt[:, 0:1] + ybuf_ref[1] * wt[:, 1:2]
    o_ref[...] = _layer_norm_rows(DEEPNORM_ALPHA * x1 + moe, lng_ref[...], lnb_ref[...])


def _combine(x1, dest, wts, lng, lnb, yb):
    t, d = x1.shape
    tm = TM_COMB
    nstep = t // tm
    d0 = dest[0].reshape(nstep, 1, tm)
    d1 = dest[1].reshape(nstep, 1, tm)
    smem_blk = pl.BlockSpec((1, 1, tm), lambda i: (i, 0, 0), memory_space=pltpu.SMEM)
    const = lambda i: (0, 0)
    return pl.pallas_call(
        _combine_kernel,
        grid=(nstep,),
        in_specs=[smem_blk, smem_blk,
                  pl.BlockSpec((tm, d), lambda i: (i, 0)),
                  pl.BlockSpec((TOP_K, tm), lambda i: (0, i)),
                  pl.BlockSpec((1, d), const),
                  pl.BlockSpec((1, d), const),
                  pl.BlockSpec(memory_space=pl.ANY)],
        out_specs=pl.BlockSpec((tm, d), lambda i: (i, 0)),
        out_shape=jax.ShapeDtypeStruct((t, d), F32),
        scratch_shapes=[pltpu.VMEM((TOP_K, tm, d), F32), pltpu.SemaphoreType.DMA((1,))],
        compiler_params=_cparams(("arbitrary",)),
        name="combine",
    )(d0, d1, x1, wts, lng, lnb, yb)


def _rotary_tables(seq):
    inv = 1.0 / (ROPE_BASE ** (jnp.arange(0, RET_HEAD_DIM, 2, dtype=F32) / RET_HEAD_DIM))
    ang = jnp.arange(seq, dtype=F32)[:, None] * inv[None, :]
    return jnp.cos(ang), jnp.sin(ang)


def _in_proj_weights(w):
    d = w.shape[0]
    half = RET_HEAD_DIM // 2
    qk = w[:, :2 * RET_WIDTH].astype(BF16).reshape(d, 2 * RET_HEADS, half, 2)
    qk = jnp.swapaxes(qk, 2, 3).reshape(d, 2 * RET_WIDTH)
    return jnp.concatenate([qk, w[:, 2 * RET_WIDTH:].astype(BF16)], axis=1)


def _routing_tables(counts, n_rows):
    padded = (counts + ROW_TILE - 1) // ROW_TILE * ROW_TILE
    pend = jnp.cumsum(padded)
    pstart = pend - padded
    ntile = padded // ROW_TILE
    nsup = (ntile + TILES_PER_SUPER - 1) // TILES_PER_SUPER
    send = jnp.cumsum(nsup)
    sstart = send - nsup
    nsuper = n_rows // SUPER + N_EXPERTS
    s = jnp.arange(nsuper, dtype=I32)
    total = send[-1]
    which = lambda q: jnp.clip(jnp.searchsorted(send, q, side='right'), 0, N_EXPERTS - 1).astype(I32)
    se = which(s)
    valid = s < total
    k = s - sstart[se]
    row = pstart[se] + k * SUPER
    nt = jnp.clip(ntile[se] - k * TILES_PER_SUPER, 0, TILES_PER_SUPER)
    idle = s - total
    tail_row = pend[-1] + idle * SUPER
    nz = jnp.clip((n_rows - tail_row) // ROW_TILE, 0, TILES_PER_SUPER)
    se = jnp.where(valid, se, which(total - 1))
    row = jnp.where(valid, row, jnp.minimum(tail_row, n_rows - ROW_TILE))
    nt = jnp.where(valid, nt, 0)
    nz = jnp.where(valid, 0, nz)
    seg_last = jnp.where(padded > 0, pend - ROW_TILE, -1)
    tail = pend[-1] + jnp.arange(N_EXPERTS, dtype=I32) * ROW_TILE
    zrow = jnp.concatenate([seg_last, jnp.where(tail < n_rows, tail, -1)])
    return (pstart.astype(I32), se.astype(I32), row.astype(I32), nt.astype(I32), nz.astype(I32),
            zrow.astype(I32))


def kernel(x, w_in, w_out, ssm_lambda_re, ssm_lambda_im, ssm_b_re, ssm_b_im, ssm_c_re, ssm_c_im,
           ssm_d, ssm_log_dt, w_glu, ln1_g, ln1_b, ln2_g, ln2_b, router_w, router_b,
           w_gate, w_up, w_down):
    batch, seq, d = x.shape
    t = batch * seq
    n_assign = t * TOP_K
    n_rows = n_assign + N_EXPERTS * ROW_TILE
    assert seq % TM_PROJ == 0 and seq % RET_CHUNK == 0 and seq % S5_TL == 0
    assert t % TM_POST == 0 and t % TM_COMB == 0 and t % TM_DISP == 0 and n_rows % SUPER == 0

    cos, sin = _rotary_tables(seq)
    expert_ids = jnp.arange(N_EXPERTS, dtype=I32)
    rw_t = router_w.astype(F32).T
    rhi = rw_t.astype(BF16)
    rlo = (rw_t - rhi.astype(F32)).astype(BF16)
    rcat = jnp.concatenate([rhi, rlo], axis=0)
    rb = router_b.astype(F32).reshape(N_EXPERTS, 1)

    x2 = x.reshape(t, d)
    for l in range(DEPTH):
        proj = _in_proj(x2, _in_proj_weights(w_in[l]), cos, sin, seq)
        ret = _retention(proj, batch, seq)
        tables = _s5_tables(ssm_lambda_re[l], ssm_lambda_im[l], ssm_b_re[l], ssm_b_im[l],
                            ssm_c_re[l], ssm_c_im[l], ssm_d[l], ssm_log_dt[l])
        y = _s5(proj, tables, batch, seq).reshape(t, SSM_WIDTH)
        x1, e, wts, rank, cnt = _post_mix(
            x2, ret, y, w_glu[l].astype(BF16), w_out[l].astype(BF16),
            ln1_g[l].reshape(1, d), ln1_b[l].reshape(1, d), rcat, rhi, rb)
        pstart, se, row, ntl, nzl, zrow = _routing_tables(cnt[:, 0], n_rows)
        dest = rank + jnp.sum(jnp.where(e[..., None] == expert_ids, pstart, 0), axis=-1)
        xb = _dispatch(x1, dest, zrow, n_rows)
        yb = _experts(xb, se, row, ntl, nzl, w_gate, w_up, w_down, l)
        x2 = _combine(x1, dest, wts, ln2_g[l].reshape(1, d), ln2_b[l].reshape(1, d), yb)
    return x2.reshape(batch, seq, d)
```

```python
import functools
import math

import numpy as np
import jax
import jax.numpy as jnp
from jax import lax
from jax.experimental import pallas as pl
from jax.experimental.pallas import tpu as pltpu

F32 = jnp.float32
BF16 = jnp.bfloat16
I32 = jnp.int32

D_MODEL = 2048
DEPTH = 2
RET_WIDTH = 1024
SSM_WIDTH = 1024
RET_HEAD_DIM = 256
RET_HEADS = RET_WIDTH // RET_HEAD_DIM
ROPE_BASE = 10000.0
SSM_GROUP = 16
SSM_GROUPS = SSM_WIDTH // SSM_GROUP
SSM_STATE = 64
IN_PROJ_WIDTH = 4 * RET_WIDTH + SSM_WIDTH
N_EXPERTS = 32
N_EXPERT_GROUPS = 4
EXPERTS_PER_GROUP = N_EXPERTS // N_EXPERT_GROUPS
TOP_K = 2
D_EXPERT = D_MODEL // 2
LN_EPS = 1e-5
DEEPNORM_ALPHA = (2.0 * DEPTH) ** 0.25

LANES = 128
SUBLANES = 8
VMEM_LIMIT = 56 * 1024 * 1024

TM_PROJ = 1024
TN_PROJ = 1024
RET_CHUNK = 256
S5_TL = 128
S5_LT = SSM_WIDTH // LANES
S5_NSTATE = (LANES // SSM_GROUP) * SSM_STATE
TM_POST = 512
TSUB_POST = 256
TM_COMB = 256
ROW_TILE = 256
TILES_PER_SUPER = 8
SUPER = ROW_TILE * TILES_PER_SUPER
DE_CHUNK = 256
N_DE_CHUNKS = D_EXPERT // DE_CHUNK
TM_DISP = 512


def _sigmoid(x):
    return 1.0 / (1.0 + jnp.exp(-x))


def _cparams(sem, vmem=VMEM_LIMIT):
    return pltpu.CompilerParams(dimension_semantics=sem, vmem_limit_bytes=vmem)


def _in_proj_kernel(x_ref, w_ref, cos_ref, sin_ref, o_ref):
    j = pl.program_id(1)
    acc = jnp.dot(x_ref[...].astype(BF16), w_ref[...], preferred_element_type=F32)
    is_rot = j < 2
    is_gate = j == 3
    scale = jnp.where(j == 1, RET_HEAD_DIM ** -0.5, 1.0).astype(F32)
    c = jnp.where(is_rot, cos_ref[...] * scale, 1.0)
    s = jnp.where(is_rot, sin_ref[...] * scale, 0.0)
    half = RET_HEAD_DIM // 2
    for h in range(RET_HEADS):
        lo = h * RET_HEAD_DIM
        t1 = acc[:, lo:lo + half]
        t2 = acc[:, lo + half:lo + RET_HEAD_DIM]
        r1 = t1 * c - t2 * s
        r2 = t1 * s + t2 * c
        o_ref[:, lo:lo + half] = (r1 * jnp.where(is_gate, _sigmoid(r1), 1.0)).astype(BF16)
        o_ref[:, lo + half:lo + RET_HEAD_DIM] = (r2 * jnp.where(is_gate, _sigmoid(r2), 1.0)).astype(BF16)


def _in_proj(x2, w_bf, cos, sin, seq):
    t, d = x2.shape
    n = w_bf.shape[1]
    tiles_per_seq = seq // TM_PROJ
    return pl.pallas_call(
        _in_proj_kernel,
        grid=(t // TM_PROJ, n // TN_PROJ),
        in_specs=[
            pl.BlockSpec((TM_PROJ, d), lambda i, j: (i, 0)),
            pl.BlockSpec((d, TN_PROJ), lambda i, j: (0, j)),
            pl.BlockSpec((TM_PROJ, LANES), lambda i, j: (i % tiles_per_seq, 0)),
            pl.BlockSpec((TM_PROJ, LANES), lambda i, j: (i % tiles_per_seq, 0)),
        ],
        out_specs=pl.BlockSpec((TM_PROJ, TN_PROJ), lambda i, j: (i, j)),
        out_shape=jax.ShapeDtypeStruct((t, n), BF16),
        compiler_params=_cparams(("arbitrary", "arbitrary")),
        name="in_proj",
    )(x2, w_bf, cos, sin)


def _ret_kernel(q_ref, k_ref, v_ref, g_ref, mask_ref, qd_ref, kd_ref, o_ref, r_ref):
    n = pl.program_id(1)

    @pl.when(n == 0)
    def _init():
        r_ref[...] = jnp.zeros_like(r_ref)

    for h in range(RET_HEADS):
        cols = slice(h * RET_HEAD_DIM, (h + 1) * RET_HEAD_DIM)
        q = q_ref[:, cols]
        k = k_ref[:, cols]
        v = v_ref[:, cols]
        s = lax.dot_general(q, k, (((1,), (1,)), ((), ())), preferred_element_type=F32)
        s = s * mask_ref[h]
        inner = jnp.dot(s.astype(BF16), v, preferred_element_type=F32)
        qd = qd_ref[h]
        r_prev = r_ref[h]
        cross = jnp.dot((q.astype(F32) * qd).astype(BF16), r_prev.astype(BF16), preferred_element_type=F32)
        o = inner + cross
        kdec = (k.astype(F32) * kd_ref[h]).astype(BF16)
        kv = lax.dot_general(kdec, v, (((0,), (0,)), ((), ())), preferred_element_type=F32)
        r_ref[h] = r_prev * qd[RET_CHUNK - 1:RET_CHUNK, :] + kv
        mu = jnp.mean(o, axis=-1, keepdims=True)
        oc = o - mu
        var = jnp.mean(oc * oc, axis=-1, keepdims=True)
        o_ref[:, cols] = (g_ref[:, cols].astype(F32) * (oc * lax.rsqrt(var + LN_EPS))).astype(BF16)


def _retention_tables():
    c = RET_CHUNK
    log_gamma = np.log(1.0 - 2.0 ** (-5.0 - np.arange(RET_HEADS, dtype=np.float64)))
    idx = np.arange(c, dtype=np.float64)
    diff = idx[:, None] - idx[None, :]
    mask = np.where(diff >= 0, np.exp(log_gamma[:, None, None] * np.maximum(diff, 0.0)), 0.0)
    qd = np.exp(log_gamma[:, None] * (idx + 1.0)[None, :])
    kd = np.exp(log_gamma[:, None] * (c - 1.0 - idx)[None, :])
    bc = lambda a: np.broadcast_to(a[:, :, None], (RET_HEADS, c, RET_HEAD_DIM)).astype(np.float32)
    return mask.astype(np.float32), bc(qd), bc(kd)


def _retention(proj, batch, seq):
    t = proj.shape[0]
    nch = seq // RET_CHUNK
    mask, qd, kd = _retention_tables()
    blk = (RET_CHUNK, RET_WIDTH)
    seg = lambda c: pl.BlockSpec(blk, lambda b, n: (b * nch + n, c))
    whole = lambda a: pl.BlockSpec(a.shape, lambda b, n: (0, 0, 0))
    return pl.pallas_call(
        _ret_kernel,
        grid=(batch, nch),
        in_specs=[seg(0), seg(1), seg(2), seg(3), whole(mask), whole(qd), whole(kd)],
        out_specs=seg(0),
        out_shape=jax.ShapeDtypeStruct((t, RET_WIDTH), BF16),
        scratch_shapes=[pltpu.VMEM((RET_HEADS, RET_HEAD_DIM, RET_HEAD_DIM), F32)],
        compiler_params=_cparams(("arbitrary", "arbitrary")),
        name="retention",
    )(proj, proj, proj, proj, jnp.asarray(mask), jnp.asarray(qd), jnp.asarray(kd))


def _s5_kernel(u_ref, bm_ref, cm_ref, lam_ref, d_ref, y_ref, us_ref, ut_ref, h_ref, st_ref):
    n = pl.program_id(1)
    tl = S5_TL
    nb = SUBLANES
    ns = S5_NSTATE

    @pl.when(n == 0)
    def _init():
        st_ref[...] = jnp.zeros_like(st_ref)

    for b in range(nb):
        us_ref[b * tl:(b + 1) * tl, :] = u_ref[b].astype(F32)

    def relayout(t, c):
        ut_ref[pl.ds(pl.multiple_of(t * nb, nb), nb), :] = us_ref[pl.ds(t, nb, stride=tl), :]
        return c

    lax.fori_loop(0, tl, relayout, 0, unroll=8)
    ut = ut_ref[...]
    h_ref[...] = jnp.dot(ut.astype(BF16), bm_ref[...], preferred_element_type=F32)

    ar = jnp.broadcast_to(lam_ref[0:1, :], (nb, ns))
    ai = jnp.broadcast_to(lam_ref[1:2, :], (nb, ns))

    def step(t, carry):
        sr, si = carry
        r0 = pl.multiple_of(t * nb, nb)
        br = h_ref[pl.ds(r0, nb), 0:ns]
        bi = h_ref[pl.ds(r0, nb), ns:2 * ns]
        nr = ar * sr - ai * si + br
        ni = ar * si + ai * sr + bi
        h_ref[pl.ds(r0, nb), 0:ns] = nr
        h_ref[pl.ds(r0, nb), ns:2 * ns] = ni
        return nr, ni

    sr, si = lax.fori_loop(0, tl, step, (st_ref[0:nb, :], st_ref[nb:2 * nb, :]), unroll=4)
    st_ref[0:nb, :] = sr
    st_ref[nb:2 * nb, :] = si

    y = jnp.dot(h_ref[...].astype(BF16), cm_ref[...], preferred_element_type=F32) + ut * d_ref[...]
    ut_ref[...] = y
    for b in range(nb):
        y_ref[b] = ut_ref[pl.ds(b, tl, stride=nb), :].astype(BF16)


def _s5_tables(lam_re, lam_im, b_re, b_im, c_re, c_im, d, log_dt):
    lam = lax.complex(lam_re.astype(F32), lam_im.astype(F32))
    dt = jnp.exp(log_dt.astype(F32))[:, None]
    lam_bar = jnp.exp(lam * dt)
    b_bar = ((lam_bar - 1.0) / lam)[..., None] * lax.complex(b_re.astype(F32), b_im.astype(F32))
    gpt = LANES // SSM_GROUP
    eye = jnp.eye(gpt, dtype=F32)

    def bdiag_in(m):
        m = m.reshape(S5_LT, gpt, SSM_STATE, SSM_GROUP)
        return jnp.einsum('jgpi,gh->jgihp', m, eye).reshape(S5_LT, LANES, gpt * SSM_STATE)

    def bdiag_out(m):
        m = m.reshape(S5_LT, gpt, SSM_GROUP, SSM_STATE)
        return jnp.einsum('jgop,gh->jgpho', m, eye).reshape(S5_LT, gpt * SSM_STATE, LANES)

    bmat = jnp.concatenate([bdiag_in(jnp.real(b_bar)), bdiag_in(jnp.imag(b_bar))], axis=-1).astype(BF16)
    cmat = jnp.concatenate([bdiag_out(c_re.astype(F32)), -bdiag_out(c_im.astype(F32))], axis=1).astype(BF16)
    lam_t = jnp.stack([jnp.real(lam_bar).reshape(S5_LT, S5_NSTATE),
                       jnp.imag(lam_bar).reshape(S5_LT, S5_NSTATE)], axis=1)
    d_t = d.astype(F32).reshape(S5_LT, 1, LANES)
    return bmat, cmat, lam_t, d_t


def _s5(proj, tables, batch, seq):
    bmat, cmat, lam_t, d_t = tables
    assert batch == SUBLANES
    proj3 = proj.reshape(batch, seq, IN_PROJ_WIDTH)
    ucol = (4 * RET_WIDTH) // LANES
    tl = S5_TL
    return pl.pallas_call(
        _s5_kernel,
        grid=(S5_LT, seq // tl),
        in_specs=[
            pl.BlockSpec((batch, tl, LANES), lambda j, n: (0, n, ucol + j)),
            pl.BlockSpec((None, LANES, 2 * S5_NSTATE), lambda j, n: (j, 0, 0)),
            pl.BlockSpec((None, 2 * S5_NSTATE, LANES), lambda j, n: (j, 0, 0)),
            pl.BlockSpec((None, 2, S5_NSTATE), lambda j, n: (j, 0, 0)),
            pl.BlockSpec((None, 1, LANES), lambda j, n: (j, 0, 0)),
        ],
        out_specs=pl.BlockSpec((batch, tl, LANES), lambda j, n: (0, n, j)),
        out_shape=jax.ShapeDtypeStruct((batch, seq, SSM_WIDTH), BF16),
        scratch_shapes=[
            pltpu.VMEM((batch * tl, LANES), F32),
            pltpu.VMEM((batch * tl, LANES), F32),
            pltpu.VMEM((batch * tl, 2 * S5_NSTATE), F32),
            pltpu.VMEM((2 * SUBLANES, S5_NSTATE), F32),
        ],
        compiler_params=_cparams(("arbitrary", "arbitrary")),
        name="s5",
    )(proj3, bmat, cmat, lam_t, d_t)


def _layer_norm_rows(r, g, b):
    mu = jnp.mean(r, axis=-1, keepdims=True)
    rc = r - mu
    var = jnp.mean(rc * rc, axis=-1, keepdims=True)
    return rc * lax.rsqrt(var + LN_EPS) * g + b


def _post_mix_kernel(x_ref, ret_ref, y_ref, wglu_ref, wout_ref, lng_ref, lnb_ref,
                     rcat_ref, rhi_ref, rb_ref, tri_ref,
                     x1_ref, e_ref, w_ref, rank_ref, cnt_ref, carry_ref):
    i = pl.program_id(0)

    @pl.when(i == 0)
    def _init():
        carry_ref[...] = jnp.zeros_like(carry_ref)

    for sub in range(TM_POST // TSUB_POST):
        _post_mix_subtile(sub, x_ref, ret_ref, y_ref, wglu_ref, wout_ref, lng_ref, lnb_ref,
                          rcat_ref, rhi_ref, rb_ref, tri_ref, x1_ref, e_ref, w_ref, rank_ref, carry_ref)
    cnt_ref[...] = carry_ref[...].astype(I32)


def _post_mix_subtile(sub, x_ref, ret_ref, y_ref, wglu_ref, wout_ref, lng_ref, lnb_ref,
                      rcat_ref, rhi_ref, rb_ref, tri_ref, x1_ref, e_ref, w_ref, rank_ref, carry_ref):
    tm = TSUB_POST
    rows = slice(sub * tm, (sub + 1) * tm)
    ya = jax.nn.gelu(y_ref[rows, :].astype(F32))
    z = jnp.dot(ya.astype(BF16), wglu_ref[...], preferred_element_type=F32)
    ssm = (ya * _sigmoid(z)).astype(BF16)
    h = jnp.dot(ret_ref[rows, :], wout_ref[0:RET_WIDTH, :], preferred_element_type=F32)
    h = h + jnp.dot(ssm, wout_ref[RET_WIDTH:, :], preferred_element_type=F32)
    x1 = _layer_norm_rows(DEEPNORM_ALPHA * x_ref[rows, :] + h, lng_ref[...], lnb_ref[...])
    x1_ref[rows, :] = x1

    xh = x1.astype(BF16)
    xl = (x1 - xh.astype(F32)).astype(BF16)
    nt = (((1,), (1,)), ((), ()))
    l1 = lax.dot_general(rcat_ref[...], xh, nt, preferred_element_type=F32)
    l2 = lax.dot_general(rhi_ref[...], xl, nt, preferred_element_type=F32)
    logits = l1[0:N_EXPERTS] + l1[N_EXPERTS:] + l2 + rb_ref[...]
    m = jnp.max(logits, axis=0, keepdims=True)
    ex = jnp.exp(logits - m)
    p = ex / jnp.sum(ex, axis=0, keepdims=True)

    eg = EXPERTS_PER_GROUP
    iota_g = lax.broadcasted_iota(I32, (eg, tm), 0)
    best = None
    for g in range(N_EXPERT_GROUPS):
        pg = p[g * eg:(g + 1) * eg]
        m1 = jnp.max(pg, axis=0, keepdims=True)
        i1 = jnp.min(jnp.where(pg == m1, iota_g, eg), axis=0, keepdims=True)
        pg2 = jnp.where(iota_g == i1, -1.0, pg)
        m2 = jnp.max(pg2, axis=0, keepdims=True)
        i2 = jnp.min(jnp.where(pg2 == m2, iota_g, eg), axis=0, keepdims=True)
        sg = m1 + m2
        if best is None:
            best = (sg, m1, m2, i1, i2)
        else:
            better = sg > best[0]
            cand = (sg, m1, m2, i1 + g * eg, i2 + g * eg)
            best = tuple(jnp.where(better, c, o) for c, o in zip(cand, best))
    _, v1, v2, e1, e2 = best
    tot = v1 + v2
    e_ref[0:1, rows] = e1
    e_ref[1:2, rows] = e2
    w_ref[0:1, rows] = v1 / tot
    w_ref[1:2, rows] = v2 / tot

    iota_e = lax.broadcasted_iota(I32, (N_EXPERTS, tm), 0)
    oh1 = iota_e == e1
    oh2 = iota_e == e2
    oh = jnp.where(oh1, 1.0, jnp.where(oh2, 1.0, 0.0))
    before = jnp.dot(oh.astype(BF16), tri_ref[...], preferred_element_type=F32) + carry_ref[:, 0:1]
    rank_ref[0:1, rows] = jnp.sum(jnp.where(oh1, before, 0.0), axis=0, keepdims=True).astype(I32)
    rank_ref[1:2, rows] = jnp.sum(jnp.where(oh2, before, 0.0), axis=0, keepdims=True).astype(I32)
    carry_ref[...] = carry_ref[...] + jnp.sum(oh, axis=1, keepdims=True)


def _post_mix(x2, ret, y, wglu_bf, wout_bf, lng, lnb, rcat, rhi, rb):
    t, d = x2.shape
    tm = TM_POST
    ts = TSUB_POST
    tri = jnp.asarray(np.triu(np.ones((ts, ts), np.float32), 1), BF16)
    const = lambda i: (0, 0)
    tok = lambda i: (i, 0)
    lane = lambda i: (0, i)
    return pl.pallas_call(
        _post_mix_kernel,
        grid=(t // tm,),
        in_specs=[
            pl.BlockSpec((tm, d), tok),
            pl.BlockSpec((tm, RET_WIDTH), tok),
            pl.BlockSpec((tm, SSM_WIDTH), tok),
            pl.BlockSpec((SSM_WIDTH, SSM_WIDTH), const),
            pl.BlockSpec((RET_WIDTH + SSM_WIDTH, d), const),
            pl.BlockSpec((1, d), const),
            pl.BlockSpec((1, d), const),
            pl.BlockSpec((2 * N_EXPERTS, d), const),
            pl.BlockSpec((N_EXPERTS, d), const),
            pl.BlockSpec((N_EXPERTS, 1), const),
            pl.BlockSpec((ts, ts), const),
        ],
        out_specs=[
            pl.BlockSpec((tm, d), tok),
            pl.BlockSpec((TOP_K, tm), lane),
            pl.BlockSpec((TOP_K, tm), lane),
            pl.BlockSpec((TOP_K, tm), lane),
            pl.BlockSpec((N_EXPERTS, LANES), const),
        ],
        out_shape=[
            jax.ShapeDtypeStruct((t, d), F32),
            jax.ShapeDtypeStruct((TOP_K, t), I32),
            jax.ShapeDtypeStruct((TOP_K, t), F32),
            jax.ShapeDtypeStruct((TOP_K, t), I32),
            jax.ShapeDtypeStruct((N_EXPERTS, LANES), I32),
        ],
        scratch_shapes=[pltpu.VMEM((N_EXPERTS, LANES), F32)],
        compiler_params=_cparams(("arbitrary",)),
        name="post_mix",
    )(x2, ret, y, wglu_bf, wout_bf, lng, lnb, rcat, rhi, rb, tri)


def _dispatch_kernel(zrow_ref, d0_ref, d1_ref, x_ref, xb_hbm, zbuf_ref, sem, zsem):
    tm = TM_DISP

    @pl.when(pl.program_id(0) == 0)
    def _zero_fill():
        zbuf_ref[...] = jnp.zeros_like(zbuf_ref)

        def zcopy(k):
            rows = pl.ds(pl.multiple_of(zrow_ref[k], ROW_TILE), ROW_TILE)
            return pltpu.make_async_copy(zbuf_ref, xb_hbm.at[rows], zsem.at[0])

        def start(k, c):
            @pl.when(zrow_ref[k] >= 0)
            def _():
                zcopy(k).start()
            return c

        def wait(k, c):
            @pl.when(zrow_ref[k] >= 0)
            def _():
                zcopy(k).wait()
            return c

        lax.fori_loop(0, 2 * N_EXPERTS, start, 0)
        lax.fori_loop(0, 2 * N_EXPERTS, wait, 0)

    def issue(t, c):
        src = x_ref.at[t]
        pltpu.make_async_copy(src, xb_hbm.at[d0_ref[0, 0, t]], sem.at[0]).start()
        pltpu.make_async_copy(src, xb_hbm.at[d1_ref[0, 0, t]], sem.at[0]).start()
        return c

    lax.fori_loop(0, tm, issue, 0)
    for _ in range(TOP_K):
        pltpu.make_async_copy(x_ref, xb_hbm.at[pl.ds(0, tm)], sem.at[0]).wait()


def _dispatch(x1, dest, zrow, n_rows):
    t, d = x1.shape
    tm = TM_DISP
    nstep = t // tm
    d0 = dest[0].reshape(nstep, 1, tm)
    d1 = dest[1].reshape(nstep, 1, tm)
    smem_blk = pl.BlockSpec((1, 1, tm), lambda i, z: (i, 0, 0), memory_space=pltpu.SMEM)
    grid_spec = pltpu.PrefetchScalarGridSpec(
        num_scalar_prefetch=1,
        grid=(nstep,),
        in_specs=[smem_blk, smem_blk, pl.BlockSpec((tm, d), lambda i, z: (i, 0))],
        out_specs=pl.BlockSpec(memory_space=pl.ANY),
        scratch_shapes=[pltpu.VMEM((ROW_TILE, d), F32),
                        pltpu.SemaphoreType.DMA((1,)), pltpu.SemaphoreType.DMA((1,))],
    )
    return pl.pallas_call(
        _dispatch_kernel,
        grid_spec=grid_spec,
        out_shape=jax.ShapeDtypeStruct((n_rows, d), F32),
        compiler_params=_cparams(("arbitrary",)),
        name="dispatch",
    )(zrow, d0, d1, x1)


def _experts_kernel(se_ref, row_ref, nt_ref, nz_ref, xb_hbm, wg_ref, wu_ref, wd_ref, yb_hbm,
                    xs_ref, acc_ref, wgb_ref, wub_ref, wdb_ref, stg_ref, sem_in, sem_out):
    del se_ref
    s = pl.program_id(0)
    j = pl.program_id(1)
    nt = nt_ref[s]
    nz = nz_ref[s]
    row0 = row_ref[s]
    rt = ROW_TILE

    def rows(i):
        return pl.ds(pl.multiple_of(row0 + i * rt, rt), rt)

    def in_copy(i, slot):
        return pltpu.make_async_copy(xb_hbm.at[rows(i)], stg_ref.at[slot], sem_in.at[slot])

    def out_copy(i, slot):
        return pltpu.make_async_copy(stg_ref.at[slot], yb_hbm.at[rows(i)], sem_out.at[slot])

    def partial_out(i):
        r = pl.multiple_of(i * rt, rt)
        xi = xs_ref[pl.ds(r, rt), :]
        g = jnp.dot(xi, wgb_ref[...], preferred_element_type=F32)
        u = jnp.dot(xi, wub_ref[...], preferred_element_type=F32)
        hj = (g * _sigmoid(g) * u).astype(BF16)
        return r, jnp.dot(hj, wdb_ref[...], preferred_element_type=F32)

    @pl.when((nz > 0) & (j == 0))
    def _zero_tail():
        stg_ref[0] = jnp.zeros((rt, stg_ref.shape[2]), F32)

        def start(i, c):
            out_copy(i, 0).start()
            return c

        def wait(i, c):
            out_copy(i, 0).wait()
            return c

        lax.fori_loop(0, nz, start, 0)
        lax.fori_loop(0, nz, wait, 0)

    @pl.when(nt > 0)
    def _work():
        wgb_ref[...] = wg_ref[...].astype(BF16)
        wub_ref[...] = wu_ref[...].astype(BF16)
        wdb_ref[...] = wd_ref[...].astype(BF16)

        @pl.when(j == 0)
        def _first():
            in_copy(0, 0).start()

            def tile(i, c):
                slot = i % 2

                @pl.when(i + 1 < nt)
                def _():
                    in_copy(i + 1, 1 - slot).start()

                in_copy(i, slot).wait()
                xs_ref[pl.ds(pl.multiple_of(i * rt, rt), rt), :] = stg_ref[slot].astype(BF16)
                r, part = partial_out(i)
                acc_ref[pl.ds(r, rt), :] = part
                return c

            lax.fori_loop(0, nt, tile, 0)

        @pl.when((j > 0) & (j < N_DE_CHUNKS - 1))
        def _mid():
            def tile(i, c):
                r, part = partial_out(i)
                acc_ref[pl.ds(r, rt), :] += part
                return c

            lax.fori_loop(0, nt, tile, 0)

        @pl.when(j == N_DE_CHUNKS - 1)
        def _last():
            def tile(i, c):
                slot = i % 2

                @pl.when(i >= 2)
                def _():
                    out_copy(i - 2, slot).wait()

                r, part = partial_out(i)
                stg_ref[slot] = acc_ref[pl.ds(r, rt), :] + part
                out_copy(i, slot).start()
                return c

            lax.fori_loop(0, nt, tile, 0)

            @pl.when(nt >= 2)
            def _():
                out_copy(nt - 2, nt % 2).wait()

            out_copy(nt - 1, (nt - 1) % 2).wait()


def _experts(xb, se, row, ntl, nzl, w_gate, w_up, w_down, layer):
    p, d = xb.shape
    nsuper = se.shape[0]
    last = N_DE_CHUNKS - 1

    def jj(s, j, nt_ref):
        return jnp.where(nt_ref[s] > 0, j, last)

    grid_spec = pltpu.PrefetchScalarGridSpec(
        num_scalar_prefetch=4,
        grid=(nsuper, N_DE_CHUNKS),
        in_specs=[
            pl.BlockSpec(memory_space=pl.ANY),
            pl.BlockSpec((None, None, d, DE_CHUNK),
                         lambda s, j, se_r, row_r, nt_r, nz_r: (layer, se_r[s], 0, jj(s, j, nt_r))),
            pl.BlockSpec((None, None, d, DE_CHUNK),
                         lambda s, j, se_r, row_r, nt_r, nz_r: (layer, se_r[s], 0, jj(s, j, nt_r))),
            pl.BlockSpec((None, None, DE_CHUNK, d),
                         lambda s, j, se_r, row_r, nt_r, nz_r: (layer, se_r[s], jj(s, j, nt_r), 0)),
        ],
        out_specs=pl.BlockSpec(memory_space=pl.ANY),
        scratch_shapes=[
            pltpu.VMEM((SUPER, d), BF16),
            pltpu.VMEM((SUPER, d), F32),
            pltpu.VMEM((d, DE_CHUNK), BF16),
            pltpu.VMEM((d, DE_CHUNK), BF16),
            pltpu.VMEM((DE_CHUNK, d), BF16),
            pltpu.VMEM((2, ROW_TILE, d), F32),
            pltpu.SemaphoreType.DMA((2,)),
            pltpu.SemaphoreType.DMA((2,)),
        ],
    )
    return pl.pallas_call(
        _experts_kernel,
        grid_spec=grid_spec,
        out_shape=jax.ShapeDtypeStruct((p, d), F32),
        compiler_params=_cparams(("arbitrary", "arbitrary")),
        name="experts",
    )(se, row, ntl, nzl, xb, w_gate, w_up, w_down)


def _combine_kernel(d0_ref, d1_ref, x1_ref, w_ref, lng_ref, lnb_ref, yb_hbm, o_ref, ybuf_ref, sem):
    tm = TM_COMB

    def issue(t, c):
        pltpu.make_async_copy(yb_hbm.at[d0_ref[0, 0, t]], ybuf_ref.at[0, t], sem.at[0]).start()
        pltpu.make_async_copy(yb_hbm.at[d1_ref[0, 0, t]], ybuf_ref.at[1, t], sem.at[0]).start()
        return c

    lax.fori_loop(0, tm, issue, 0)
    wpad = jnp.concatenate([w_ref[...], jnp.zeros((LANES - TOP_K, tm), F32)], axis=0)
    wt = wpad.T
    x1 = x1_ref[...]
    pltpu.make_async_copy(yb_hbm.at[pl.ds(0, tm)], ybuf_ref.at[0], sem.at[0]).wait()
    pltpu.make_async_copy(yb_hbm.at[pl.ds(0, tm)], ybuf_ref.at[1], sem.at[0]).wait()
    moe = ybuf_ref[0] * wt[:, 0:1] + ybuf_ref[1] * wt[:, 1:2]
    o_ref[...] = _layer_norm_rows(DEEPNORM_ALPHA * x1 + moe, lng_ref[...], lnb_ref[...])


def _combine(x1, dest, wts, lng, lnb, yb):
    t, d = x1.shape
    tm = TM_COMB
    nstep = t // tm
    d0 = dest[0].reshape(nstep, 1, tm)
    d1 = dest[1].reshape(nstep, 1, tm)
    smem_blk = pl.BlockSpec((1, 1, tm), lambda i: (i, 0, 0), memory_space=pltpu.SMEM)
    const = lambda i: (0, 0)
    return pl.pallas_call(
        _combine_kernel,
        grid=(nstep,),
        in_specs=[smem_blk, smem_blk,
                  pl.BlockSpec((tm, d), lambda i: (i, 0)),
                  pl.BlockSpec((TOP_K, tm), lambda i: (0, i)),
                  pl.BlockSpec((1, d), const),
                  pl.BlockSpec((1, d), const),
                  pl.BlockSpec(memory_space=pl.ANY)],
        out_specs=pl.BlockSpec((tm, d), lambda i: (i, 0)),
        out_shape=jax.ShapeDtypeStruct((t, d), F32),
        scratch_shapes=[pltpu.VMEM((TOP_K, tm, d), F32), pltpu.SemaphoreType.DMA((1,))],
        compiler_params=_cparams(("arbitrary",)),
        name="combine",
    )(d0, d1, x1, wts, lng, lnb, yb)


def _rotary_tables(seq):
    inv = 1.0 / (ROPE_BASE ** (jnp.arange(0, RET_HEAD_DIM, 2, dtype=F32) / RET_HEAD_DIM))
    ang = jnp.arange(seq, dtype=F32)[:, None] * inv[None, :]
    return jnp.cos(ang), jnp.sin(ang)


def _in_proj_weights(w):
    d = w.shape[0]
    half = RET_HEAD_DIM // 2
    qk = w[:, :2 * RET_WIDTH].astype(BF16).reshape(d, 2 * RET_HEADS, half, 2)
    qk = jnp.swapaxes(qk, 2, 3).reshape(d, 2 * RET_WIDTH)
    return jnp.concatenate([qk, w[:, 2 * RET_WIDTH:].astype(BF16)], axis=1)


def _routing_tables(counts, n_rows):
    padded = (counts + ROW_TILE - 1) // ROW_TILE * ROW_TILE
    pend = jnp.cumsum(padded)
    pstart = pend - padded
    ntile = padded // ROW_TILE
    nsup = (ntile + TILES_PER_SUPER - 1) // TILES_PER_SUPER
    send = jnp.cumsum(nsup)
    sstart = send - nsup
    nsuper = n_rows // SUPER + N_EXPERTS
    s = jnp.arange(nsuper, dtype=I32)
    total = send[-1]
    which = lambda q: jnp.clip(jnp.searchsorted(send, q, side='right'), 0, N_EXPERTS - 1).astype(I32)
    se = which(s)
    valid = s < total
    k = s - sstart[se]
    row = pstart[se] + k * SUPER
    nt = jnp.clip(ntile[se] - k * TILES_PER_SUPER, 0, TILES_PER_SUPER)
    idle = s - total
    tail_row = pend[-1] + idle * SUPER
    nz = jnp.clip((n_rows - tail_row) // ROW_TILE, 0, TILES_PER_SUPER)
    se = jnp.where(valid, se, which(total - 1))
    row = jnp.where(valid, row, jnp.minimum(tail_row, n_rows - ROW_TILE))
    nt = jnp.where(valid, nt, 0)
    nz = jnp.where(valid, 0, nz)
    seg_last = jnp.where(padded > 0, pend - ROW_TILE, -1)
    tail = pend[-1] + jnp.arange(N_EXPERTS, dtype=I32) * ROW_TILE
    zrow = jnp.concatenate([seg_last, jnp.where(tail < n_rows, tail, -1)])
    return (pstart.astype(I32), se.astype(I32), row.astype(I32), nt.astype(I32), nz.astype(I32),
            zrow.astype(I32))


def kernel(x, w_in, w_out, ssm_lambda_re, ssm_lambda_im, ssm_b_re, ssm_b_im, ssm_c_re, ssm_c_im,
           ssm_d, ssm_log_dt, w_glu, ln1_g, ln1_b, ln2_g, ln2_b, router_w, router_b,
           w_gate, w_up, w_down):
    batch, seq, d = x.shape
    t = batch * seq
    n_assign = t * TOP_K
    n_rows = n_assign + N_EXPERTS * ROW_TILE
    assert seq % TM_PROJ == 0 and seq % RET_CHUNK == 0 and seq % S5_TL == 0
    assert t % TM_POST == 0 and t % TM_COMB == 0 and t % TM_DISP == 0 and n_rows % SUPER == 0

    cos, sin = _rotary_tables(seq)
    expert_ids = jnp.arange(N_EXPERTS, dtype=I32)
    rw_t = router_w.astype(F32).T
    rhi = rw_t.astype(BF16)
    rlo = (rw_t - rhi.astype(F32)).astype(BF16)
    rcat = jnp.concatenate([rhi, rlo], axis=0)
    rb = router_b.astype(F32).reshape(N_EXPERTS, 1)

    x2 = x.reshape(t, d)
    for l in range(DEPTH):
        proj = _in_proj(x2, _in_proj_weights(w_in[l]), cos, sin, seq)
        ret = _retention(proj, batch, seq)
        tables = _s5_tables(ssm_lambda_re[l], ssm_lambda_im[l], ssm_b_re[l], ssm_b_im[l],
                            ssm_c_re[l], ssm_c_im[l], ssm_d[l], ssm_log_dt[l])
        y = _s5(proj, tables, batch, seq).reshape(t, SSM_WIDTH)
        x1, e, wts, rank, cnt = _post_mix(
            x2, ret, y, w_glu[l].astype(BF16), w_out[l].astype(BF16),
            ln1_g[l].reshape(1, d), ln1_b[l].reshape(1, d), rcat, rhi, rb)
        pstart, se, row, ntl, nzl, zrow = _routing_tables(cnt[:, 0], n_rows)
        dest = rank + jnp.sum(jnp.where(e[..., None] == expert_ids, pstart, 0), axis=-1)
        xb = _dispatch(x1, dest, zrow, n_rows)
        yb = _experts(xb, se, row, ntl, nzl, w_gate, w_up, w_down, l)
        x2 = _combine(x1, dest, wts, ln2_g[l].reshape(1, d), ln2_b[l].reshape(1, d), yb)
    return x2.reshape(batch, seq, d)
```

```python
import functools
import math

import numpy as np
import jax
import jax.numpy as jnp
from jax import lax
from jax.experimental import pallas as pl
from jax.experimental.pallas import tpu as pltpu

F32 = jnp.float32
BF16 = jnp.bfloat16
I32 = jnp.int32

D_MODEL = 2048
DEPTH = 2
RET_WIDTH = 1024
SSM_WIDTH = 1024
RET_HEAD_DIM = 256
RET_HEADS = RET_WIDTH // RET_HEAD_DIM
ROPE_BASE = 10000.0
SSM_GROUP = 16
SSM_GROUPS = SSM_WIDTH // SSM_GROUP
SSM_STATE = 64
IN_PROJ_WIDTH = 4 * RET_WIDTH + SSM_WIDTH
N_EXPERTS = 32
N_EXPERT_GROUPS = 4
EXPERTS_PER_GROUP = N_EXPERTS // N_EXPERT_GROUPS
TOP_K = 2
D_EXPERT = D_MODEL // 2
LN_EPS = 1e-5
DEEPNORM_ALPHA = (2.0 * DEPTH) ** 0.25

LANES = 128
SUBLANES = 8
VMEM_LIMIT = 56 * 1024 * 1024

TM_PROJ = 1024
TN_PROJ = 1024
RET_CHUNK = 256
S5_TL = 128
S5_LT = SSM_WIDTH // LANES
S5_NSTATE = (LANES // SSM_GROUP) * SSM_STATE
TM_POST = 512
TSUB_POST = 256
TM_COMB = 256
ROW_TILE = 256
TILES_PER_SUPER = 8
SUPER = ROW_TILE * TILES_PER_SUPER
DE_CHUNK = 256
N_DE_CHUNKS = D_EXPERT // DE_CHUNK
TM_DISP = 512
ISSUE_UNROLL = 8


def _sigmoid(x):
    return 1.0 / (1.0 + jnp.exp(-x))


def _cparams(sem, vmem=VMEM_LIMIT):
    return pltpu.CompilerParams(dimension_semantics=sem, vmem_limit_bytes=vmem)


PACK_WORDS = D_MODEL // 2
PACK_SUB = PACK_WORDS // LANES
U32 = jnp.uint32


def _pack_rows(x):
    bits = lambda v: lax.bitcast_convert_type(v.astype(BF16).astype(F32), U32)
    return (bits(x[:, :PACK_WORDS]) >> 16) | (bits(x[:, PACK_WORDS:]) & jnp.uint32(0xFFFF0000))


def _unpack_lo(w):
    return lax.bitcast_convert_type(w << 16, F32)


def _unpack_hi(w):
    return lax.bitcast_convert_type(w & jnp.uint32(0xFFFF0000), F32)


def _store_token_tiles(ref, row0, n, packed):
    for c in range(PACK_SUB):
        ref[pl.ds(row0 * PACK_SUB + c, n, stride=PACK_SUB), :] = packed[:, c * LANES:(c + 1) * LANES]


def _load_token_tiles(ref, row0, n):
    return [ref[pl.ds(row0 * PACK_SUB + c, n, stride=PACK_SUB), :] for c in range(PACK_SUB)]


def _in_proj_kernel(x_ref, w_ref, cos_ref, sin_ref, o_ref):
    j = pl.program_id(1)
    acc = jnp.dot(x_ref[...].astype(BF16), w_ref[...], preferred_element_type=F32)
    is_rot = j < 2
    is_gate = j == 3
    scale = jnp.where(j == 1, RET_HEAD_DIM ** -0.5, 1.0).astype(F32)
    c = jnp.where(is_rot, cos_ref[...] * scale, 1.0)
    s = jnp.where(is_rot, sin_ref[...] * scale, 0.0)
    half = RET_HEAD_DIM // 2
    for h in range(RET_HEADS):
        lo = h * RET_HEAD_DIM
        t1 = acc[:, lo:lo + half]
        t2 = acc[:, lo + half:lo + RET_HEAD_DIM]
        r1 = t1 * c - t2 * s
        r2 = t1 * s + t2 * c
        o_ref[:, lo:lo + half] = (r1 * jnp.where(is_gate, _sigmoid(r1), 1.0)).astype(BF16)
        o_ref[:, lo + half:lo + RET_HEAD_DIM] = (r2 * jnp.where(is_gate, _sigmoid(r2), 1.0)).astype(BF16)


def _in_proj(x2, w_bf, cos, sin, seq):
    t, d = x2.shape
    n = w_bf.shape[1]
    tiles_per_seq = seq // TM_PROJ
    return pl.pallas_call(
        _in_proj_kernel,
        grid=(t // TM_PROJ, n // TN_PROJ),
        in_specs=[
            pl.BlockSpec((TM_PROJ, d), lambda i, j: (i, 0)),
            pl.BlockSpec((d, TN_PROJ), lambda i, j: (0, j)),
            pl.BlockSpec((TM_PROJ, LANES), lambda i, j: (i % tiles_per_seq, 0)),
            pl.BlockSpec((TM_PROJ, LANES), lambda i, j: (i % tiles_per_seq, 0)),
        ],
        out_specs=pl.BlockSpec((TM_PROJ, TN_PROJ), lambda i, j: (i, j)),
        out_shape=jax.ShapeDtypeStruct((t, n), BF16),
        compiler_params=_cparams(("arbitrary", "arbitrary")),
        name="in_proj",
    )(x2, w_bf, cos, sin)


def _ret_kernel(q_ref, k_ref, v_ref, g_ref, mask_ref, qd_ref, kd_ref, o_ref, r_ref):
    n = pl.program_id(1)

    @pl.when(n == 0)
    def _init():
        r_ref[...] = jnp.zeros_like(r_ref)

    for h in range(RET_HEADS):
        cols = slice(h * RET_HEAD_DIM, (h + 1) * RET_HEAD_DIM)
        q = q_ref[:, cols]
        k = k_ref[:, cols]
        v = v_ref[:, cols]
        s = lax.dot_general(q, k, (((1,), (1,)), ((), ())), preferred_element_type=F32)
        s = s * mask_ref[h]
        inner = jnp.dot(s.astype(BF16), v, preferred_element_type=F32)
        qd = qd_ref[h]
        r_prev = r_ref[h]
        cross = jnp.dot((q.astype(F32) * qd).astype(BF16), r_prev.astype(BF16), preferred_element_type=F32)
        o = inner + cross
        kdec = (k.astype(F32) * kd_ref[h]).astype(BF16)
        kv = lax.dot_general(kdec, v, (((0,), (0,)), ((), ())), preferred_element_type=F32)
        r_ref[h] = r_prev * qd[RET_CHUNK - 1:RET_CHUNK, :] + kv
        mu = jnp.mean(o, axis=-1, keepdims=True)
        oc = o - mu
        var = jnp.mean(oc * oc, axis=-1, keepdims=True)
        o_ref[:, cols] = (g_ref[:, cols].astype(F32) * (oc * lax.rsqrt(var + LN_EPS))).astype(BF16)


def _retention_tables():
    c = RET_CHUNK
    log_gamma = np.log(1.0 - 2.0 ** (-5.0 - np.arange(RET_HEADS, dtype=np.float64)))
    idx = np.arange(c, dtype=np.float64)
    diff = idx[:, None] - idx[None, :]
    mask = np.where(diff >= 0, np.exp(log_gamma[:, None, None] * np.maximum(diff, 0.0)), 0.0)
    qd = np.exp(log_gamma[:, None] * (idx + 1.0)[None, :])
    kd = np.exp(log_gamma[:, None] * (c - 1.0 - idx)[None, :])
    bc = lambda a: np.broadcast_to(a[:, :, None], (RET_HEADS, c, RET_HEAD_DIM)).astype(np.float32)
    return mask.astype(np.float32), bc(qd), bc(kd)


def _retention(proj, batch, seq):
    t = proj.shape[0]
    nch = seq // RET_CHUNK
    mask, qd, kd = _retention_tables()
    blk = (RET_CHUNK, RET_WIDTH)
    seg = lambda c: pl.BlockSpec(blk, lambda b, n: (b * nch + n, c))
    whole = lambda a: pl.BlockSpec(a.shape, lambda b, n: (0, 0, 0))
    return pl.pallas_call(
        _ret_kernel,
        grid=(batch, nch),
        in_specs=[seg(0), seg(1), seg(2), seg(3), whole(mask), whole(qd), whole(kd)],
        out_specs=seg(0),
        out_shape=jax.ShapeDtypeStruct((t, RET_WIDTH), BF16),
        scratch_shapes=[pltpu.VMEM((RET_HEADS, RET_HEAD_DIM, RET_HEAD_DIM), F32)],
        compiler_params=_cparams(("arbitrary", "arbitrary")),
        name="retention",
    )(proj, proj, proj, proj, jnp.asarray(mask), jnp.asarray(qd), jnp.asarray(kd))


def _s5_kernel(u_ref, bm_ref, cm_ref, lam_ref, d_ref, y_ref, us_ref, ut_ref, h_ref, st_ref):
    n = pl.program_id(1)
    tl = S5_TL
    nb = SUBLANES
    ns = S5_NSTATE

    @pl.when(n == 0)
    def _init():
        st_ref[...] = jnp.zeros_like(st_ref)

    for b in range(nb):
        us_ref[b * tl:(b + 1) * tl, :] = u_ref[b].astype(F32)

    def relayout(t, c):
        ut_ref[pl.ds(pl.multiple_of(t * nb, nb), nb), :] = us_ref[pl.ds(t, nb, stride=tl), :]
        return c

    lax.fori_loop(0, tl, relayout, 0, unroll=8)
    ut = ut_ref[...]
    h_ref[...] = jnp.dot(ut.astype(BF16), bm_ref[...], preferred_element_type=F32)

    ar = jnp.broadcast_to(lam_ref[0:1, :], (nb, ns))
    ai = jnp.broadcast_to(lam_ref[1:2, :], (nb, ns))

    def step(t, carry):
        sr, si = carry
        r0 = pl.multiple_of(t * nb, nb)
        br = h_ref[pl.ds(r0, nb), 0:ns]
        bi = h_ref[pl.ds(r0, nb), ns:2 * ns]
        nr = ar * sr - ai * si + br
        ni = ar * si + ai * sr + bi
        h_ref[pl.ds(r0, nb), 0:ns] = nr
        h_ref[pl.ds(r0, nb), ns:2 * ns] = ni
        return nr, ni

    sr, si = lax.fori_loop(0, tl, step, (st_ref[0:nb, :], st_ref[nb:2 * nb, :]), unroll=4)
    st_ref[0:nb, :] = sr
    st_ref[nb:2 * nb, :] = si

    y = jnp.dot(h_ref[...].astype(BF16), cm_ref[...], preferred_element_type=F32) + ut * d_ref[...]
    ut_ref[...] = y
    for b in range(nb):
        y_ref[b] = ut_ref[pl.ds(b, tl, stride=nb), :].astype(BF16)


def _s5_tables(lam_re, lam_im, b_re, b_im, c_re, c_im, d, log_dt):
    lam = lax.complex(lam_re.astype(F32), lam_im.astype(F32))
    dt = jnp.exp(log_dt.astype(F32))[:, None]
    lam_bar = jnp.exp(lam * dt)
    b_bar = ((lam_bar - 1.0) / lam)[..., None] * lax.complex(b_re.astype(F32), b_im.astype(F32))
    gpt = LANES // SSM_GROUP
    eye = jnp.eye(gpt, dtype=F32)

    def bdiag_in(m):
        m = m.reshape(S5_LT, gpt, SSM_STATE, SSM_GROUP)
        return jnp.einsum('jgpi,gh->jgihp', m, eye).reshape(S5_LT, LANES, gpt * SSM_STATE)

    def bdiag_out(m):
        m = m.reshape(S5_LT, gpt, SSM_GROUP, SSM_STATE)
        return jnp.einsum('jgop,gh->jgpho', m, eye).reshape(S5_LT, gpt * SSM_STATE, LANES)

    bmat = jnp.concatenate([bdiag_in(jnp.real(b_bar)), bdiag_in(jnp.imag(b_bar))], axis=-1).astype(BF16)
    cmat = jnp.concatenate([bdiag_out(c_re.astype(F32)), -bdiag_out(c_im.astype(F32))], axis=1).astype(BF16)
    lam_t = jnp.stack([jnp.real(lam_bar).reshape(S5_LT, S5_NSTATE),
                       jnp.imag(lam_bar).reshape(S5_LT, S5_NSTATE)], axis=1)
    d_t = d.astype(F32).reshape(S5_LT, 1, LANES)
    return bmat, cmat, lam_t, d_t


def _s5(proj, tables, batch, seq):
    bmat, cmat, lam_t, d_t = tables
    assert batch == SUBLANES
    proj3 = proj.reshape(batch, seq, IN_PROJ_WIDTH)
    ucol = (4 * RET_WIDTH) // LANES
    tl = S5_TL
    return pl.pallas_call(
        _s5_kernel,
        grid=(S5_LT, seq // tl),
        in_specs=[
            pl.BlockSpec((batch, tl, LANES), lambda j, n: (0, n, ucol + j)),
            pl.BlockSpec((None, LANES, 2 * S5_NSTATE), lambda j, n: (j, 0, 0)),
            pl.BlockSpec((None, 2 * S5_NSTATE, LANES), lambda j, n: (j, 0, 0)),
            pl.BlockSpec((None, 2, S5_NSTATE), lambda j, n: (j, 0, 0)),
            pl.BlockSpec((None, 1, LANES), lambda j, n: (j, 0, 0)),
        ],
        out_specs=pl.BlockSpec((batch, tl, LANES), lambda j, n: (0, n, j)),
        out_shape=jax.ShapeDtypeStruct((batch, seq, SSM_WIDTH), BF16),
        scratch_shapes=[
            pltpu.VMEM((batch * tl, LANES), F32),
            pltpu.VMEM((batch * tl, LANES), F32),
            pltpu.VMEM((batch * tl, 2 * S5_NSTATE), F32),
            pltpu.VMEM((2 * SUBLANES, S5_NSTATE), F32),
        ],
        compiler_params=_cparams(("arbitrary", "arbitrary")),
        name="s5",
    )(proj3, bmat, cmat, lam_t, d_t)


def _layer_norm_rows(r, g, b):
    mu = jnp.mean(r, axis=-1, keepdims=True)
    rc = r - mu
    var = jnp.mean(rc * rc, axis=-1, keepdims=True)
    return rc * lax.rsqrt(var + LN_EPS) * g + b


def _post_mix_kernel(x_ref, ret_ref, y_ref, wglu_ref, wout_ref, lng_ref, lnb_ref,
                     rcat_ref, rhi_ref, rb_ref, tri_ref,
                     x1_ref, x1p_ref, e_ref, w_ref, rank_ref, cnt_ref, carry_ref):
    i = pl.program_id(0)

    @pl.when(i == 0)
    def _init():
        carry_ref[...] = jnp.zeros_like(carry_ref)

    for sub in range(TM_POST // TSUB_POST):
        _post_mix_subtile(sub, x_ref, ret_ref, y_ref, wglu_ref, wout_ref, lng_ref, lnb_ref,
                          rcat_ref, rhi_ref, rb_ref, tri_ref, x1_ref, x1p_ref, e_ref, w_ref, rank_ref, carry_ref)
    cnt_ref[...] = carry_ref[...].astype(I32)


def _post_mix_subtile(sub, x_ref, ret_ref, y_ref, wglu_ref, wout_ref, lng_ref, lnb_ref,
                      rcat_ref, rhi_ref, rb_ref, tri_ref, x1_ref, x1p_ref, e_ref, w_ref, rank_ref, carry_ref):
    tm = TSUB_POST
    rows = slice(sub * tm, (sub + 1) * tm)
    ya = jax.nn.gelu(y_ref[rows, :].astype(F32))
    z = jnp.dot(ya.astype(BF16), wglu_ref[...], preferred_element_type=F32)
    ssm = (ya * _sigmoid(z)).astype(BF16)
    h = jnp.dot(ret_ref[rows, :], wout_ref[0:RET_WIDTH, :], preferred_element_type=F32)
    h = h + jnp.dot(ssm, wout_ref[RET_WIDTH:, :], preferred_element_type=F32)
    x1 = _layer_norm_rows(DEEPNORM_ALPHA * x_ref[rows, :] + h, lng_ref[...], lnb_ref[...])
    x1_ref[rows, :] = x1
    _store_token_tiles(x1p_ref, sub * tm, tm, _pack_rows(x1))

    xh = x1.astype(BF16)
    xl = (x1 - xh.astype(F32)).astype(BF16)
    nt = (((1,), (1,)), ((), ()))
    l1 = lax.dot_general(rcat_ref[...], xh, nt, preferred_element_type=F32)
    l2 = lax.dot_general(rhi_ref[...], xl, nt, preferred_element_type=F32)
    logits = l1[0:N_EXPERTS] + l1[N_EXPERTS:] + l2 + rb_ref[...]
    m = jnp.max(logits, axis=0, keepdims=True)
    ex = jnp.exp(logits - m)
    p = ex / jnp.sum(ex, axis=0, keepdims=True)

    eg = EXPERTS_PER_GROUP
    iota_g = lax.broadcasted_iota(I32, (eg, tm), 0)
    best = None
    for g in range(N_EXPERT_GROUPS):
        pg = p[g * eg:(g + 1) * eg]
        m1 = jnp.max(pg, axis=0, keepdims=True)
        i1 = jnp.min(jnp.where(pg == m1, iota_g, eg), axis=0, keepdims=True)
        pg2 = jnp.where(iota_g == i1, -1.0, pg)
        m2 = jnp.max(pg2, axis=0, keepdims=True)
        i2 = jnp.min(jnp.where(pg2 == m2, iota_g, eg), axis=0, keepdims=True)
        sg = m1 + m2
        if best is None:
            best = (sg, m1, m2, i1, i2)
        else:
            better = sg > best[0]
            cand = (sg, m1, m2, i1 + g * eg, i2 + g * eg)
            best = tuple(jnp.where(better, c, o) for c, o in zip(cand, best))
    _, v1, v2, e1, e2 = best
    tot = v1 + v2
    e_ref[0:1, rows] = e1
    e_ref[1:2, rows] = e2
    w_ref[0:1, rows] = v1 / tot
    w_ref[1:2, rows] = v2 / tot

    iota_e = lax.broadcasted_iota(I32, (N_EXPERTS, tm), 0)
    oh1 = iota_e == e1
    oh2 = iota_e == e2
    oh = jnp.where(oh1, 1.0, jnp.where(oh2, 1.0, 0.0))
    before = jnp.dot(oh.astype(BF16), tri_ref[...], preferred_element_type=F32) + carry_ref[:, 0:1]
    rank_ref[0:1, rows] = jnp.sum(jnp.where(oh1, before, 0.0), axis=0, keepdims=True).astype(I32)
    rank_ref[1:2, rows] = jnp.sum(jnp.where(oh2, before, 0.0), axis=0, keepdims=True).astype(I32)
    carry_ref[...] = carry_ref[...] + jnp.sum(oh, axis=1, keepdims=True)


def _post_mix(x2, ret, y, wglu_bf, wout_bf, lng, lnb, rcat, rhi, rb):
    t, d = x2.shape
    tm = TM_POST
    ts = TSUB_POST
    tri = jnp.asarray(np.triu(np.ones((ts, ts), np.float32), 1), BF16)
    const = lambda i: (0, 0)
    tok = lambda i: (i, 0)
    lane = lambda i: (0, i)
    return pl.pallas_call(
        _post_mix_kernel,
        grid=(t // tm,),
        in_specs=[
            pl.BlockSpec((tm, d), tok),
            pl.BlockSpec((tm, RET_WIDTH), tok),
            pl.BlockSpec((tm, SSM_WIDTH), tok),
            pl.BlockSpec((SSM_WIDTH, SSM_WIDTH), const),
            pl.BlockSpec((RET_WIDTH + SSM_WIDTH, d), const),
            pl.BlockSpec((1, d), const),
            pl.BlockSpec((1, d), const),
            pl.BlockSpec((2 * N_EXPERTS, d), const),
            pl.BlockSpec((N_EXPERTS, d), const),
            pl.BlockSpec((N_EXPERTS, 1), const),
            pl.BlockSpec((ts, ts), const),
        ],
        out_specs=[
            pl.BlockSpec((tm, d), tok),
            pl.BlockSpec((tm * PACK_SUB, LANES), tok),
            pl.BlockSpec((TOP_K, tm), lane),
            pl.BlockSpec((TOP_K, tm), lane),
            pl.BlockSpec((TOP_K, tm), lane),
            pl.BlockSpec((N_EXPERTS, LANES), const),
        ],
        out_shape=[
            jax.ShapeDtypeStruct((t, d), F32),
            jax.ShapeDtypeStruct((t * PACK_SUB, LANES), U32),
            jax.ShapeDtypeStruct((TOP_K, t), I32),
            jax.ShapeDtypeStruct((TOP_K, t), F32),
            jax.ShapeDtypeStruct((TOP_K, t), I32),
            jax.ShapeDtypeStruct((N_EXPERTS, LANES), I32),
        ],
        scratch_shapes=[pltpu.VMEM((N_EXPERTS, LANES), F32)],
        compiler_params=_cparams(("arbitrary",)),
        name="post_mix",
    )(x2, ret, y, wglu_bf, wout_bf, lng, lnb, rcat, rhi, rb, tri)


def _dispatch_kernel(zrow_ref, d0_ref, d1_ref, x_ref, xb_hbm, zbuf_ref, sem, zsem):
    tm = TM_DISP

    @pl.when(pl.program_id(0) == 0)
    def _zero_fill():
        zbuf_ref[...] = jnp.zeros_like(zbuf_ref)

        def zcopy(k):
            rows = pl.ds(pl.multiple_of(zrow_ref[k], ROW_TILE), ROW_TILE)
            return pltpu.make_async_copy(zbuf_ref, xb_hbm.at[rows], zsem.at[0])

        def start(k, c):
            @pl.when(zrow_ref[k] >= 0)
            def _():
                zcopy(k).start()
            return c

        def wait(k, c):
            @pl.when(zrow_ref[k] >= 0)
            def _():
                zcopy(k).wait()
            return c

        lax.fori_loop(0, 2 * N_EXPERTS, start, 0)
        lax.fori_loop(0, 2 * N_EXPERTS, wait, 0)

    def issue(tb, c):
        for u in range(ISSUE_UNROLL):
            t = tb * ISSUE_UNROLL + u
            src = x_ref.at[t]
            pltpu.make_async_copy(src, xb_hbm.at[d0_ref[0, 0, t]], sem.at[0]).start()
            pltpu.make_async_copy(src, xb_hbm.at[d1_ref[0, 0, t]], sem.at[0]).start()
        return c

    lax.fori_loop(0, tm // ISSUE_UNROLL, issue, 0)
    for _ in range(TOP_K):
        pltpu.make_async_copy(x_ref, xb_hbm.at[pl.ds(0, tm)], sem.at[0]).wait()


def _dispatch(x1p, dest, zrow, n_rows):
    t = x1p.shape[0]
    tm = TM_DISP
    nstep = t // tm
    d0 = dest[0].reshape(nstep, 1, tm)
    d1 = dest[1].reshape(nstep, 1, tm)
    smem_blk = pl.BlockSpec((1, 1, tm), lambda i, z: (i, 0, 0), memory_space=pltpu.SMEM)
    grid_spec = pltpu.PrefetchScalarGridSpec(
        num_scalar_prefetch=1,
        grid=(nstep,),
        in_specs=[smem_blk, smem_blk, pl.BlockSpec((tm, PACK_SUB, LANES), lambda i, z: (i, 0, 0))],
        out_specs=pl.BlockSpec(memory_space=pl.ANY),
        scratch_shapes=[pltpu.VMEM((ROW_TILE, PACK_SUB, LANES), U32),
                        pltpu.SemaphoreType.DMA((1,)), pltpu.SemaphoreType.DMA((1,))],
    )
    return pl.pallas_call(
        _dispatch_kernel,
        grid_spec=grid_spec,
        out_shape=jax.ShapeDtypeStruct((n_rows, PACK_SUB, LANES), U32),
        compiler_params=_cparams(("arbitrary",)),
        name="dispatch",
    )(zrow, d0, d1, x1p)


def _experts_kernel(se_ref, row_ref, nt_ref, nz_ref, xb_hbm, wg_ref, wu_ref, wd_ref, yb_hbm,
                    xs_ref, acc_ref, wgb_ref, wub_ref, wdb_ref, stg_in, stg_out, sem_in, sem_out):
    del se_ref
    s = pl.program_id(0)
    j = pl.program_id(1)
    nt = nt_ref[s]
    nz = nz_ref[s]
    row0 = row_ref[s]
    rt = ROW_TILE

    def rows(i):
        n = rt * PACK_SUB
        return pl.ds(pl.multiple_of((row0 + i * rt) * PACK_SUB, n), n)

    def in_copy(i, slot):
        return pltpu.make_async_copy(xb_hbm.at[rows(i)], stg_in.at[slot], sem_in.at[slot])

    def out_copy(i, slot):
        return pltpu.make_async_copy(stg_out.at[slot], yb_hbm.at[rows(i)], sem_out.at[slot])

    def partial_out(i):
        r = pl.multiple_of(i * rt, rt)
        xi = xs_ref[pl.ds(r, rt), :]
        g = jnp.dot(xi, wgb_ref[...], preferred_element_type=F32)
        u = jnp.dot(xi, wub_ref[...], preferred_element_type=F32)
        hj = (g * _sigmoid(g) * u).astype(BF16)
        return r, jnp.dot(hj, wdb_ref[...], preferred_element_type=F32)

    @pl.when((nz > 0) & (j == 0))
    def _zero_tail():
        stg_out[0] = jnp.zeros(stg_out.shape[1:], U32)

        def start(i, c):
            out_copy(i, 0).start()
            return c

        def wait(i, c):
            out_copy(i, 0).wait()
            return c

        lax.fori_loop(0, nz, start, 0)
        lax.fori_loop(0, nz, wait, 0)

    @pl.when(nt > 0)
    def _work():
        wgb_ref[...] = wg_ref[...].astype(BF16)
        wub_ref[...] = wu_ref[...].astype(BF16)
        wdb_ref[...] = wd_ref[...].astype(BF16)

        @pl.when(j == 0)
        def _first():
            in_copy(0, 0).start()

            def tile(i, c):
                slot = i % 2

                @pl.when(i + 1 < nt)
                def _():
                    in_copy(i + 1, 1 - slot).start()

                in_copy(i, slot).wait()
                xrows = pl.ds(pl.multiple_of(i * rt, rt), rt)
                for c, words in enumerate(_load_token_tiles(stg_in.at[slot], 0, rt)):
                    xs_ref[xrows, c * LANES:(c + 1) * LANES] = _unpack_lo(words).astype(BF16)
                    xs_ref[xrows, PACK_WORDS + c * LANES:PACK_WORDS + (c + 1) * LANES] = (
                        _unpack_hi(words).astype(BF16))
                r, part = partial_out(i)
                acc_ref[pl.ds(r, rt), :] = part
                return c

            lax.fori_loop(0, nt, tile, 0)

        @pl.when((j > 0) & (j < N_DE_CHUNKS - 1))
        def _mid():
            def tile(i, c):
                r, part = partial_out(i)
                acc_ref[pl.ds(r, rt), :] += part
                return c

            lax.fori_loop(0, nt, tile, 0)

        @pl.when(j == N_DE_CHUNKS - 1)
        def _last():
            def tile(i, c):
                slot = i % 2

                @pl.when(i >= 2)
                def _():
                    out_copy(i - 2, slot).wait()

                r, part = partial_out(i)
                _store_token_tiles(stg_out.at[slot], 0, rt, _pack_rows(acc_ref[pl.ds(r, rt), :] + part))
                out_copy(i, slot).start()
                return c

            lax.fori_loop(0, nt, tile, 0)

            @pl.when(nt >= 2)
            def _():
                out_copy(nt - 2, nt % 2).wait()

            out_copy(nt - 1, (nt - 1) % 2).wait()


def _experts(xb, se, row, ntl, nzl, w_gate, w_up, w_down, layer):
    p = xb.shape[0] // PACK_SUB
    d = D_MODEL
    nsuper = se.shape[0]
    last = N_DE_CHUNKS - 1

    def jj(s, j, nt_ref):
        return jnp.where(nt_ref[s] > 0, j, last)

    grid_spec = pltpu.PrefetchScalarGridSpec(
        num_scalar_prefetch=4,
        grid=(nsuper, N_DE_CHUNKS),
        in_specs=[
            pl.BlockSpec(memory_space=pl.ANY),
            pl.BlockSpec((None, None, d, DE_CHUNK),
                         lambda s, j, se_r, row_r, nt_r, nz_r: (layer, se_r[s], 0, jj(s, j, nt_r))),
            pl.BlockSpec((None, None, d, DE_CHUNK),
                         lambda s, j, se_r, row_r, nt_r, nz_r: (layer, se_r[s], 0, jj(s, j, nt_r))),
            pl.BlockSpec((None, None, DE_CHUNK, d),
                         lambda s, j, se_r, row_r, nt_r, nz_r: (layer, se_r[s], jj(s, j, nt_r), 0)),
        ],
        out_specs=pl.BlockSpec(memory_space=pl.ANY),
        scratch_shapes=[
            pltpu.VMEM((SUPER, d), BF16),
            pltpu.VMEM((SUPER, d), F32),
            pltpu.VMEM((d, DE_CHUNK), BF16),
            pltpu.VMEM((d, DE_CHUNK), BF16),
            pltpu.VMEM((DE_CHUNK, d), BF16),
            pltpu.VMEM((2, ROW_TILE * PACK_SUB, LANES), U32),
            pltpu.VMEM((2, ROW_TILE * PACK_SUB, LANES), U32),
            pltpu.SemaphoreType.DMA((2,)),
            pltpu.SemaphoreType.DMA((2,)),
        ],
    )
    return pl.pallas_call(
        _experts_kernel,
        grid_spec=grid_spec,
        out_shape=jax.ShapeDtypeStruct((p * PACK_SUB, LANES), U32),
        compiler_params=_cparams(("arbitrary", "arbitrary")),
        name="experts",
    )(se, row, ntl, nzl, xb, w_gate, w_up, w_down)


def _combine_kernel(d0_ref, d1_ref, x1_ref, w_ref, lng_ref, lnb_ref, yb_hbm, o_ref, ybuf_ref, sem):
    tm = TM_COMB

    def issue(tb, c):
        for u in range(ISSUE_UNROLL):
            t = tb * ISSUE_UNROLL + u
            dst = pl.ds(pl.multiple_of(t * PACK_SUB, PACK_SUB), PACK_SUB)
            pltpu.make_async_copy(yb_hbm.at[d0_ref[0, 0, t]], ybuf_ref.at[0, dst], sem.at[0]).start()
            pltpu.make_async_copy(yb_hbm.at[d1_ref[0, 0, t]], ybuf_ref.at[1, dst], sem.at[0]).start()
        return c

    lax.fori_loop(0, tm // ISSUE_UNROLL, issue, 0)
    wpad = jnp.concatenate([w_ref[...], jnp.zeros((LANES - TOP_K, tm), F32)], axis=0)
    wt = wpad.T
    w0 = wt[:, 0:1]
    w1 = wt[:, 1:2]
    x1 = x1_ref[...]
    for k in range(TOP_K):
        pltpu.make_async_copy(ybuf_ref.at[k], ybuf_ref.at[k], sem.at[0]).wait()
    y0 = _load_token_tiles(ybuf_ref.at[0], 0, tm)
    y1 = _load_token_tiles(ybuf_ref.at[1], 0, tm)
    lo = [_unpack_lo(a) * w0 + _unpack_lo(b) * w1 for a, b in zip(y0, y1)]
    hi = [_unpack_hi(a) * w0 + _unpack_hi(b) * w1 for a, b in zip(y0, y1)]
    moe = jnp.concatenate(lo + hi, axis=1)
    o_ref[...] = _layer_norm_rows(DEEPNORM_ALPHA * x1 + moe, lng_ref[...], lnb_ref[...])


def _combine(x1, dest, wts, lng, lnb, yb):
    t, d = x1.shape
    tm = TM_COMB
    nstep = t // tm
    d0 = dest[0].reshape(nstep, 1, tm)
    d1 = dest[1].reshape(nstep, 1, tm)
    smem_blk = pl.BlockSpec((1, 1, tm), lambda i: (i, 0, 0), memory_space=pltpu.SMEM)
    const = lambda i: (0, 0)
    return pl.pallas_call(
        _combine_kernel,
        grid=(nstep,),
        in_specs=[smem_blk, smem_blk,
                  pl.BlockSpec((tm, d), lambda i: (i, 0)),
                  pl.BlockSpec((TOP_K, tm), lambda i: (0, i)),
                  pl.BlockSpec((1, d), const),
                  pl.BlockSpec((1, d), const),
                  pl.BlockSpec(memory_space=pl.ANY)],
        out_specs=pl.BlockSpec((tm, d), lambda i: (i, 0)),
        out_shape=jax.ShapeDtypeStruct((t, d), F32),
        scratch_shapes=[pltpu.VMEM((TOP_K, tm * PACK_SUB, LANES), U32), pltpu.SemaphoreType.DMA((1,))],
        compiler_params=_cparams(("arbitrary",)),
        name="combine",
    )(d0, d1, x1, wts, lng, lnb, yb)


def _rotary_tables(seq):
    inv = 1.0 / (ROPE_BASE ** (jnp.arange(0, RET_HEAD_DIM, 2, dtype=F32) / RET_HEAD_DIM))
    ang = jnp.arange(seq, dtype=F32)[:, None] * inv[None, :]
    return jnp.cos(ang), jnp.sin(ang)


def _in_proj_weights(w):
    d = w.shape[0]
    half = RET_HEAD_DIM // 2
    qk = w[:, :2 * RET_WIDTH].astype(BF16).reshape(d, 2 * RET_HEADS, half, 2)
    qk = jnp.swapaxes(qk, 2, 3).reshape(d, 2 * RET_WIDTH)
    return jnp.concatenate([qk, w[:, 2 * RET_WIDTH:].astype(BF16)], axis=1)


def _routing_tables(counts, n_rows):
    padded = (counts + ROW_TILE - 1) // ROW_TILE * ROW_TILE
    pend = jnp.cumsum(padded)
    pstart = pend - padded
    ntile = padded // ROW_TILE
    nsup = (ntile + TILES_PER_SUPER - 1) // TILES_PER_SUPER
    send = jnp.cumsum(nsup)
    sstart = send - nsup
    nsuper = n_rows // SUPER + N_EXPERTS
    s = jnp.arange(nsuper, dtype=I32)
    total = send[-1]
    which = lambda q: jnp.clip(jnp.searchsorted(send, q, side='right'), 0, N_EXPERTS - 1).astype(I32)
    se = which(s)
    valid = s < total
    k = s - sstart[se]
    row = pstart[se] + k * SUPER
    nt = jnp.clip(ntile[se] - k * TILES_PER_SUPER, 0, TILES_PER_SUPER)
    idle = s - total
    tail_row = pend[-1] + idle * SUPER
    nz = jnp.clip((n_rows - tail_row) // ROW_TILE, 0, TILES_PER_SUPER)
    se = jnp.where(valid, se, which(total - 1))
    row = jnp.where(valid, row, jnp.minimum(tail_row, n_rows - ROW_TILE))
    nt = jnp.where(valid, nt, 0)
    nz = jnp.where(valid, 0, nz)
    seg_last = jnp.where(padded > 0, pend - ROW_TILE, -1)
    tail = pend[-1] + jnp.arange(N_EXPERTS, dtype=I32) * ROW_TILE
    zrow = jnp.concatenate([seg_last, jnp.where(tail < n_rows, tail, -1)])
    return (pstart.astype(I32), se.astype(I32), row.astype(I32), nt.astype(I32), nz.astype(I32),
            zrow.astype(I32))


def kernel(x, w_in, w_out, ssm_lambda_re, ssm_lambda_im, ssm_b_re, ssm_b_im, ssm_c_re, ssm_c_im,
           ssm_d, ssm_log_dt, w_glu, ln1_g, ln1_b, ln2_g, ln2_b, router_w, router_b,
           w_gate, w_up, w_down):
    batch, seq, d = x.shape
    t = batch * seq
    n_assign = t * TOP_K
    n_rows = n_assign + N_EXPERTS * ROW_TILE
    assert seq % TM_PROJ == 0 and seq % RET_CHUNK == 0 and seq % S5_TL == 0
    assert t % TM_POST == 0 and t % TM_COMB == 0 and t % TM_DISP == 0 and n_rows % SUPER == 0

    cos, sin = _rotary_tables(seq)
    expert_ids = jnp.arange(N_EXPERTS, dtype=I32)
    rw_t = router_w.astype(F32).T
    rhi = rw_t.astype(BF16)
    rlo = (rw_t - rhi.astype(F32)).astype(BF16)
    rcat = jnp.concatenate([rhi, rlo], axis=0)
    rb = router_b.astype(F32).reshape(N_EXPERTS, 1)

    x2 = x.reshape(t, d)
    for l in range(DEPTH):
        proj = _in_proj(x2, _in_proj_weights(w_in[l]), cos, sin, seq)
        ret = _retention(proj, batch, seq)
        tables = _s5_tables(ssm_lambda_re[l], ssm_lambda_im[l], ssm_b_re[l], ssm_b_im[l],
                            ssm_c_re[l], ssm_c_im[l], ssm_d[l], ssm_log_dt[l])
        y = _s5(proj, tables, batch, seq).reshape(t, SSM_WIDTH)
        x1, x1p, e, wts, rank, cnt = _post_mix(
            x2, ret, y, w_glu[l].astype(BF16), w_out[l].astype(BF16),
            ln1_g[l].reshape(1, d), ln1_b[l].reshape(1, d), rcat, rhi, rb)
        pstart, se, row, ntl, nzl, zrow = _routing_tables(cnt[:, 0], n_rows)
        dest = rank + jnp.sum(jnp.where(e[..., None] == expert_ids, pstart, 0), axis=-1)
        xb = _dispatch(x1p.reshape(t, PACK_SUB, LANES), dest, zrow, n_rows)
        yb = _experts(xb.reshape(n_rows * PACK_SUB, LANES), se, row, ntl, nzl, w_gate, w_up, w_down, l)
        x2 = _combine(x1, dest, wts, ln2_g[l].reshape(1, d), ln2_b[l].reshape(1, d),
                      yb.reshape(n_rows, PACK_SUB, LANES))
    return x2.reshape(batch, seq, d)
```

```python
import functools
import math

import numpy as np
import jax
import jax.numpy as jnp
from jax import lax
from jax.experimental import pallas as pl
from jax.experimental.pallas import tpu as pltpu

F32 = jnp.float32
BF16 = jnp.bfloat16
I32 = jnp.int32

D_MODEL = 2048
DEPTH = 2
RET_WIDTH = 1024
SSM_WIDTH = 1024
RET_HEAD_DIM = 256
RET_HEADS = RET_WIDTH // RET_HEAD_DIM
ROPE_BASE = 10000.0
SSM_GROUP = 16
SSM_GROUPS = SSM_WIDTH // SSM_GROUP
SSM_STATE = 64
IN_PROJ_WIDTH = 4 * RET_WIDTH + SSM_WIDTH
N_EXPERTS = 32
N_EXPERT_GROUPS = 4
EXPERTS_PER_GROUP = N_EXPERTS // N_EXPERT_GROUPS
TOP_K = 2
D_EXPERT = D_MODEL // 2
LN_EPS = 1e-5
DEEPNORM_ALPHA = (2.0 * DEPTH) ** 0.25

LANES = 128
SUBLANES = 8
VMEM_LIMIT = 56 * 1024 * 1024

TM_PROJ = 1024
TN_PROJ = 1024
RET_CHUNK = 256
S5_TL = 128
S5_GROUP = 16
S5_LT = SSM_WIDTH // LANES
S5_NSTATE = (LANES // SSM_GROUP) * SSM_STATE
TM_POST = 512
TSUB_POST = 256
TM_COMB = 256
ROW_TILE = 256
TILES_PER_SUPER = 8
SUPER = ROW_TILE * TILES_PER_SUPER
DE_CHUNK = 256
N_DE_CHUNKS = D_EXPERT // DE_CHUNK
TM_DISP = 512
ISSUE_UNROLL = 8


def _sigmoid(x):
    return 1.0 / (1.0 + jnp.exp(-x))


def _cparams(sem, vmem=VMEM_LIMIT):
    return pltpu.CompilerParams(dimension_semantics=sem, vmem_limit_bytes=vmem)


PACK_WORDS = D_MODEL // 2
PACK_SUB = PACK_WORDS // LANES
U32 = jnp.uint32


def _pack_rows(x):
    bits = lambda v: lax.bitcast_convert_type(v.astype(BF16).astype(F32), U32)
    return (bits(x[:, :PACK_WORDS]) >> 16) | (bits(x[:, PACK_WORDS:]) & jnp.uint32(0xFFFF0000))


def _unpack_lo(w):
    return lax.bitcast_convert_type(w << 16, F32)


def _unpack_hi(w):
    return lax.bitcast_convert_type(w & jnp.uint32(0xFFFF0000), F32)


def _store_token_tiles(ref, row0, n, packed):
    for c in range(PACK_SUB):
        ref[pl.ds(row0 * PACK_SUB + c, n, stride=PACK_SUB), :] = packed[:, c * LANES:(c + 1) * LANES]


def _load_token_tiles(ref, row0, n):
    return [ref[pl.ds(row0 * PACK_SUB + c, n, stride=PACK_SUB), :] for c in range(PACK_SUB)]


def _in_proj_kernel(x_ref, w_ref, cos_ref, sin_ref, o_ref):
    j = pl.program_id(1)
    acc = jnp.dot(x_ref[...].astype(BF16), w_ref[...], preferred_element_type=F32)
    is_rot = j < 2
    is_gate = j == 3
    scale = jnp.where(j == 1, RET_HEAD_DIM ** -0.5, 1.0).astype(F32)
    c = jnp.where(is_rot, cos_ref[...] * scale, 1.0)
    s = jnp.where(is_rot, sin_ref[...] * scale, 0.0)
    half = RET_HEAD_DIM // 2
    for h in range(RET_HEADS):
        lo = h * RET_HEAD_DIM
        t1 = acc[:, lo:lo + half]
        t2 = acc[:, lo + half:lo + RET_HEAD_DIM]
        r1 = t1 * c - t2 * s
        r2 = t1 * s + t2 * c
        o_ref[:, lo:lo + half] = (r1 * jnp.where(is_gate, _sigmoid(r1), 1.0)).astype(BF16)
        o_ref[:, lo + half:lo + RET_HEAD_DIM] = (r2 * jnp.where(is_gate, _sigmoid(r2), 1.0)).astype(BF16)


def _in_proj(x2, w_bf, cos, sin, seq):
    t, d = x2.shape
    n = w_bf.shape[1]
    tiles_per_seq = seq // TM_PROJ
    return pl.pallas_call(
        _in_proj_kernel,
        grid=(t // TM_PROJ, n // TN_PROJ),
        in_specs=[
            pl.BlockSpec((TM_PROJ, d), lambda i, j: (i, 0)),
            pl.BlockSpec((d, TN_PROJ), lambda i, j: (0, j)),
            pl.BlockSpec((TM_PROJ, LANES), lambda i, j: (i % tiles_per_seq, 0)),
            pl.BlockSpec((TM_PROJ, LANES), lambda i, j: (i % tiles_per_seq, 0)),
        ],
        out_specs=pl.BlockSpec((TM_PROJ, TN_PROJ), lambda i, j: (i, j)),
        out_shape=jax.ShapeDtypeStruct((t, n), BF16),
        compiler_params=_cparams(("arbitrary", "arbitrary")),
        name="in_proj",
    )(x2, w_bf, cos, sin)


def _ret_kernel(q_ref, k_ref, v_ref, g_ref, mask_ref, qd_ref, kd_ref, o_ref, r_ref):
    n = pl.program_id(1)

    @pl.when(n == 0)
    def _init():
        r_ref[...] = jnp.zeros_like(r_ref)

    for h in range(RET_HEADS):
        cols = slice(h * RET_HEAD_DIM, (h + 1) * RET_HEAD_DIM)
        q = q_ref[:, cols]
        k = k_ref[:, cols]
        v = v_ref[:, cols]
        s = lax.dot_general(q, k, (((1,), (1,)), ((), ())), preferred_element_type=F32)
        s = s * mask_ref[h]
        inner = jnp.dot(s.astype(BF16), v, preferred_element_type=F32)
        qd = qd_ref[h]
        r_prev = r_ref[h]
        cross = jnp.dot((q.astype(F32) * qd).astype(BF16), r_prev.astype(BF16), preferred_element_type=F32)
        o = inner + cross
        kdec = (k.astype(F32) * kd_ref[h]).astype(BF16)
        kv = lax.dot_general(kdec, v, (((0,), (0,)), ((), ())), preferred_element_type=F32)
        r_ref[h] = r_prev * qd[RET_CHUNK - 1:RET_CHUNK, :] + kv
        mu = jnp.mean(o, axis=-1, keepdims=True)
        oc = o - mu
        var = jnp.mean(oc * oc, axis=-1, keepdims=True)
        o_ref[:, cols] = (g_ref[:, cols].astype(F32) * (oc * lax.rsqrt(var + LN_EPS))).astype(BF16)


def _retention_tables():
    c = RET_CHUNK
    log_gamma = np.log(1.0 - 2.0 ** (-5.0 - np.arange(RET_HEADS, dtype=np.float64)))
    idx = np.arange(c, dtype=np.float64)
    diff = idx[:, None] - idx[None, :]
    mask = np.where(diff >= 0, np.exp(log_gamma[:, None, None] * np.maximum(diff, 0.0)), 0.0)
    qd = np.exp(log_gamma[:, None] * (idx + 1.0)[None, :])
    kd = np.exp(log_gamma[:, None] * (c - 1.0 - idx)[None, :])
    bc = lambda a: np.broadcast_to(a[:, :, None], (RET_HEADS, c, RET_HEAD_DIM)).astype(np.float32)
    return mask.astype(np.float32), bc(qd), bc(kd)


def _retention(proj, batch, seq):
    t = proj.shape[0]
    nch = seq // RET_CHUNK
    mask, qd, kd = _retention_tables()
    blk = (RET_CHUNK, RET_WIDTH)
    seg = lambda c: pl.BlockSpec(blk, lambda b, n: (b * nch + n, c))
    whole = lambda a: pl.BlockSpec(a.shape, lambda b, n: (0, 0, 0))
    return pl.pallas_call(
        _ret_kernel,
        grid=(batch, nch),
        in_specs=[seg(0), seg(1), seg(2), seg(3), whole(mask), whole(qd), whole(kd)],
        out_specs=seg(0),
        out_shape=jax.ShapeDtypeStruct((t, RET_WIDTH), BF16),
        scratch_shapes=[pltpu.VMEM((RET_HEADS, RET_HEAD_DIM, RET_HEAD_DIM), F32)],
        compiler_params=_cparams(("arbitrary", "arbitrary")),
        name="retention",
    )(proj, proj, proj, proj, jnp.asarray(mask), jnp.asarray(qd), jnp.asarray(kd))


def _s5_kernel(u_ref, bm_ref, cm_ref, lam_ref, d_ref, y_ref, us_ref, ut_ref, bu_ref, hb_ref, yt_ref, st_ref):
    n = pl.program_id(1)
    tl = S5_TL
    nb = SUBLANES
    ns = S5_NSTATE

    @pl.when(n == 0)
    def _init():
        st_ref[...] = jnp.zeros_like(st_ref)

    for b in range(nb):
        us_ref[b * tl:(b + 1) * tl, :] = u_ref[b].astype(F32)

    for t in range(tl):
        ut_ref[t * nb:(t + 1) * nb, :] = us_ref[pl.ds(t, nb, stride=tl), :]

    ar = jnp.broadcast_to(lam_ref[0:1, :], (nb, ns))
    ai = jnp.broadcast_to(lam_ref[1:2, :], (nb, ns))
    sr = st_ref[0:nb, :]
    si = st_ref[nb:2 * nb, :]

    grows = S5_GROUP * nb

    def project_in(g):
        rows = slice(g * grows, (g + 1) * grows)
        bu_ref[rows, :] = jnp.dot(ut_ref[rows, :].astype(BF16), bm_ref[...], preferred_element_type=F32)

    project_in(0)
    for g in range(tl // S5_GROUP):
        if g + 1 < tl // S5_GROUP:
            project_in(g + 1)
        for tt in range(0, S5_GROUP, 2):
            r0 = g * grows + tt * nb
            pair_r = []
            pair_i = []
            for r in (r0, r0 + nb):
                br = bu_ref[r:r + nb, 0:ns]
                bi = bu_ref[r:r + nb, ns:2 * ns]
                sr, si = ar * sr - ai * si + br, ar * si + ai * sr + bi
                pair_r.append(sr)
                pair_i.append(si)
            hb_ref[r0:r0 + 2 * nb, 0:ns] = jnp.concatenate(pair_r, axis=0).astype(BF16)
            hb_ref[r0:r0 + 2 * nb, ns:2 * ns] = jnp.concatenate(pair_i, axis=0).astype(BF16)
        rows = slice(g * grows, (g + 1) * grows)
        yt_ref[rows, :] = (jnp.dot(hb_ref[rows, :], cm_ref[...], preferred_element_type=F32)
                           + ut_ref[rows, :] * d_ref[...])
    st_ref[0:nb, :] = sr
    st_ref[nb:2 * nb, :] = si
    for b in range(nb):
        y_ref[b] = yt_ref[pl.ds(b, tl, stride=nb), :].astype(BF16)


def _s5_tables(lam_re, lam_im, b_re, b_im, c_re, c_im, d, log_dt):
    lam = lax.complex(lam_re.astype(F32), lam_im.astype(F32))
    dt = jnp.exp(log_dt.astype(F32))[:, None]
    lam_bar = jnp.exp(lam * dt)
    b_bar = ((lam_bar - 1.0) / lam)[..., None] * lax.complex(b_re.astype(F32), b_im.astype(F32))
    gpt = LANES // SSM_GROUP
    eye = jnp.eye(gpt, dtype=F32)

    def bdiag_in(m):
        m = m.reshape(S5_LT, gpt, SSM_STATE, SSM_GROUP)
        return jnp.einsum('jgpi,gh->jgihp', m, eye).reshape(S5_LT, LANES, gpt * SSM_STATE)

    def bdiag_out(m):
        m = m.reshape(S5_LT, gpt, SSM_GROUP, SSM_STATE)
        return jnp.einsum('jgop,gh->jgpho', m, eye).reshape(S5_LT, gpt * SSM_STATE, LANES)

    bmat = jnp.concatenate([bdiag_in(jnp.real(b_bar)), bdiag_in(jnp.imag(b_bar))], axis=-1).astype(BF16)
    cmat = jnp.concatenate([bdiag_out(c_re.astype(F32)), -bdiag_out(c_im.astype(F32))], axis=1).astype(BF16)
    lam_t = jnp.stack([jnp.real(lam_bar).reshape(S5_LT, S5_NSTATE),
                       jnp.imag(lam_bar).reshape(S5_LT, S5_NSTATE)], axis=1)
    d_t = d.astype(F32).reshape(S5_LT, 1, LANES)
    return bmat, cmat, lam_t, d_t


def _s5(proj, tables, batch, seq):
    bmat, cmat, lam_t, d_t = tables
    assert batch == SUBLANES
    proj3 = proj.reshape(batch, seq, IN_PROJ_WIDTH)
    ucol = (4 * RET_WIDTH) // LANES
    tl = S5_TL
    return pl.pallas_call(
        _s5_kernel,
        grid=(S5_LT, seq // tl),
        in_specs=[
            pl.BlockSpec((batch, tl, LANES), lambda j, n: (0, n, ucol + j)),
            pl.BlockSpec((None, LANES, 2 * S5_NSTATE), lambda j, n: (j, 0, 0)),
            pl.BlockSpec((None, 2 * S5_NSTATE, LANES), lambda j, n: (j, 0, 0)),
            pl.BlockSpec((None, 2, S5_NSTATE), lambda j, n: (j, 0, 0)),
            pl.BlockSpec((None, 1, LANES), lambda j, n: (j, 0, 0)),
        ],
        out_specs=pl.BlockSpec((batch, tl, LANES), lambda j, n: (0, n, j)),
        out_shape=jax.ShapeDtypeStruct((batch, seq, SSM_WIDTH), BF16),
        scratch_shapes=[
            pltpu.VMEM((batch * tl, LANES), F32),
            pltpu.VMEM((batch * tl, LANES), F32),
            pltpu.VMEM((batch * tl, 2 * S5_NSTATE), F32),
            pltpu.VMEM((batch * tl, 2 * S5_NSTATE), BF16),
            pltpu.VMEM((batch * tl, LANES), F32),
            pltpu.VMEM((2 * SUBLANES, S5_NSTATE), F32),
        ],
        compiler_params=_cparams(("arbitrary", "arbitrary")),
        name="s5",
    )(proj3, bmat, cmat, lam_t, d_t)


def _layer_norm_rows(r, g, b):
    mu = jnp.mean(r, axis=-1, keepdims=True)
    rc = r - mu
    var = jnp.mean(rc * rc, axis=-1, keepdims=True)
    return rc * lax.rsqrt(var + LN_EPS) * g + b


def _post_mix_kernel(x_ref, ret_ref, y_ref, wglu_ref, wout_ref, lng_ref, lnb_ref,
                     rcat_ref, rhi_ref, rb_ref, tri_ref,
                     x1_ref, x1p_ref, e_ref, w_ref, rank_ref, cnt_ref, carry_ref):
    i = pl.program_id(0)

    @pl.when(i == 0)
    def _init():
        carry_ref[...] = jnp.zeros_like(carry_ref)

    for sub in range(TM_POST // TSUB_POST):
        _post_mix_subtile(sub, x_ref, ret_ref, y_ref, wglu_ref, wout_ref, lng_ref, lnb_ref,
                          rcat_ref, rhi_ref, rb_ref, tri_ref, x1_ref, x1p_ref, e_ref, w_ref, rank_ref, carry_ref)
    cnt_ref[...] = carry_ref[...].astype(I32)


def _post_mix_subtile(sub, x_ref, ret_ref, y_ref, wglu_ref, wout_ref, lng_ref, lnb_ref,
                      rcat_ref, rhi_ref, rb_ref, tri_ref, x1_ref, x1p_ref, e_ref, w_ref, rank_ref, carry_ref):
    tm = TSUB_POST
    rows = slice(sub * tm, (sub + 1) * tm)
    ya = jax.nn.gelu(y_ref[rows, :].astype(F32))
    z = jnp.dot(ya.astype(BF16), wglu_ref[...], preferred_element_type=F32)
    ssm = (ya * _sigmoid(z)).astype(BF16)
    h = jnp.dot(ret_ref[rows, :], wout_ref[0:RET_WIDTH, :], preferred_element_type=F32)
    h = h + jnp.dot(ssm, wout_ref[RET_WIDTH:, :], preferred_element_type=F32)
    x1 = _layer_norm_rows(DEEPNORM_ALPHA * x_ref[rows, :] + h, lng_ref[...], lnb_ref[...])
    x1_ref[rows, :] = x1
    _store_token_tiles(x1p_ref, sub * tm, tm, _pack_rows(x1))

    xh = x1.astype(BF16)
    xl = (x1 - xh.astype(F32)).astype(BF16)
    nt = (((1,), (1,)), ((), ()))
    l1 = lax.dot_general(rcat_ref[...], xh, nt, preferred_element_type=F32)
    l2 = lax.dot_general(rhi_ref[...], xl, nt, preferred_element_type=F32)
    logits = l1[0:N_EXPERTS] + l1[N_EXPERTS:] + l2 + rb_ref[...]
    m = jnp.max(logits, axis=0, keepdims=True)
    ex = jnp.exp(logits - m)
    p = ex / jnp.sum(ex, axis=0, keepdims=True)

    eg = EXPERTS_PER_GROUP
    iota_g = lax.broadcasted_iota(I32, (eg, tm), 0)
    best = None
    for g in range(N_EXPERT_GROUPS):
        pg = p[g * eg:(g + 1) * eg]
        m1 = jnp.max(pg, axis=0, keepdims=True)
        i1 = jnp.min(jnp.where(pg == m1, iota_g, eg), axis=0, keepdims=True)
        pg2 = jnp.where(iota_g == i1, -1.0, pg)
        m2 = jnp.max(pg2, axis=0, keepdims=True)
        i2 = jnp.min(jnp.where(pg2 == m2, iota_g, eg), axis=0, keepdims=True)
        sg = m1 + m2
        if best is None:
            best = (sg, m1, m2, i1, i2)
        else:
            better = sg > best[0]
            cand = (sg, m1, m2, i1 + g * eg, i2 + g * eg)
            best = tuple(jnp.where(better, c, o) for c, o in zip(cand, best))
    _, v1, v2, e1, e2 = best
    tot = v1 + v2
    e_ref[0:1, rows] = e1
    e_ref[1:2, rows] = e2
    w_ref[0:1, rows] = v1 / tot
    w_ref[1:2, rows] = v2 / tot

    iota_e = lax.broadcasted_iota(I32, (N_EXPERTS, tm), 0)
    oh1 = iota_e == e1
    oh2 = iota_e == e2
    oh = jnp.where(oh1, 1.0, jnp.where(oh2, 1.0, 0.0))
    before = jnp.dot(oh.astype(BF16), tri_ref[...], preferred_element_type=F32) + carry_ref[:, 0:1]
    rank_ref[0:1, rows] = jnp.sum(jnp.where(oh1, before, 0.0), axis=0, keepdims=True).astype(I32)
    rank_ref[1:2, rows] = jnp.sum(jnp.where(oh2, before, 0.0), axis=0, keepdims=True).astype(I32)
    carry_ref[...] = carry_ref[...] + jnp.sum(oh, axis=1, keepdims=True)


def _post_mix(x2, ret, y, wglu_bf, wout_bf, lng, lnb, rcat, rhi, rb):
    t, d = x2.shape
    tm = TM_POST
    ts = TSUB_POST
    tri = jnp.asarray(np.triu(np.ones((ts, ts), np.float32), 1), BF16)
    const = lambda i: (0, 0)
    tok = lambda i: (i, 0)
    lane = lambda i: (0, i)
    return pl.pallas_call(
        _post_mix_kernel,
        grid=(t // tm,),
        in_specs=[
            pl.BlockSpec((tm, d), tok),
            pl.BlockSpec((tm, RET_WIDTH), tok),
            pl.BlockSpec((tm, SSM_WIDTH), tok),
            pl.BlockSpec((SSM_WIDTH, SSM_WIDTH), const),
            pl.BlockSpec((RET_WIDTH + SSM_WIDTH, d), const),
            pl.BlockSpec((1, d), const),
            pl.BlockSpec((1, d), const),
            pl.BlockSpec((2 * N_EXPERTS, d), const),
            pl.BlockSpec((N_EXPERTS, d), const),
            pl.BlockSpec((N_EXPERTS, 1), const),
            pl.BlockSpec((ts, ts), const),
        ],
        out_specs=[
            pl.BlockSpec((tm, d), tok),
            pl.BlockSpec((tm * PACK_SUB, LANES), tok),
            pl.BlockSpec((TOP_K, tm), lane),
            pl.BlockSpec((TOP_K, tm), lane),
            pl.BlockSpec((TOP_K, tm), lane),
            pl.BlockSpec((N_EXPERTS, LANES), const),
        ],
        out_shape=[
            jax.ShapeDtypeStruct((t, d), F32),
            jax.ShapeDtypeStruct((t * PACK_SUB, LANES), U32),
            jax.ShapeDtypeStruct((TOP_K, t), I32),
            jax.ShapeDtypeStruct((TOP_K, t), F32),
            jax.ShapeDtypeStruct((TOP_K, t), I32),
            jax.ShapeDtypeStruct((N_EXPERTS, LANES), I32),
        ],
        scratch_shapes=[pltpu.VMEM((N_EXPERTS, LANES), F32)],
        compiler_params=_cparams(("arbitrary",)),
        name="post_mix",
    )(x2, ret, y, wglu_bf, wout_bf, lng, lnb, rcat, rhi, rb, tri)


def _dispatch_kernel(zrow_ref, d0_ref, d1_ref, x_ref, xb_hbm, zbuf_ref, sem, zsem):
    tm = TM_DISP

    @pl.when(pl.program_id(0) == 0)
    def _zero_fill():
        zbuf_ref[...] = jnp.zeros_like(zbuf_ref)

        def zcopy(k):
            n = ROW_TILE * PACK_SUB
            rows = pl.ds(pl.multiple_of(zrow_ref[k] * PACK_SUB, n), n)
            return pltpu.make_async_copy(zbuf_ref, xb_hbm.at[rows], zsem.at[0])

        def start(k, c):
            @pl.when(zrow_ref[k] >= 0)
            def _():
                zcopy(k).start()
            return c

        def wait(k, c):
            @pl.when(zrow_ref[k] >= 0)
            def _():
                zcopy(k).wait()
            return c

        lax.fori_loop(0, 2 * N_EXPERTS, start, 0)
        lax.fori_loop(0, 2 * N_EXPERTS, wait, 0)

    def issue(tb, c):
        for u in range(ISSUE_UNROLL):
            t = tb * ISSUE_UNROLL + u
            src = x_ref.at[pl.ds(t * PACK_SUB, PACK_SUB)]
            tile = lambda r: pl.ds(pl.multiple_of(r, PACK_SUB), PACK_SUB)
            pltpu.make_async_copy(src, xb_hbm.at[tile(d0_ref[0, 0, t])], sem.at[0]).start(priority=0)
            pltpu.make_async_copy(src, xb_hbm.at[tile(d1_ref[0, 0, t])], sem.at[0]).start(priority=1)
        return c

    lax.fori_loop(0, tm // ISSUE_UNROLL, issue, 0)
    for _ in range(TOP_K):
        pltpu.make_async_copy(x_ref, xb_hbm.at[pl.ds(0, tm * PACK_SUB)], sem.at[0]).wait()


def _dispatch(x1p, dest_tile, zrow, n_rows):
    t = x1p.shape[0] // PACK_SUB
    tm = TM_DISP
    nstep = t // tm
    d0 = dest_tile[0].reshape(nstep, 1, tm)
    d1 = dest_tile[1].reshape(nstep, 1, tm)
    smem_blk = pl.BlockSpec((1, 1, tm), lambda i, z: (i, 0, 0), memory_space=pltpu.SMEM)
    grid_spec = pltpu.PrefetchScalarGridSpec(
        num_scalar_prefetch=1,
        grid=(nstep,),
        in_specs=[smem_blk, smem_blk, pl.BlockSpec((tm * PACK_SUB, LANES), lambda i, z: (i, 0))],
        out_specs=pl.BlockSpec(memory_space=pl.ANY),
        scratch_shapes=[pltpu.VMEM((ROW_TILE * PACK_SUB, LANES), U32),
                        pltpu.SemaphoreType.DMA((1,)), pltpu.SemaphoreType.DMA((1,))],
    )
    return pl.pallas_call(
        _dispatch_kernel,
        grid_spec=grid_spec,
        out_shape=jax.ShapeDtypeStruct((n_rows * PACK_SUB, LANES), U32),
        compiler_params=_cparams(("arbitrary",)),
        name="dispatch",
    )(zrow, d0, d1, x1p)


def _experts_kernel(se_ref, row_ref, nt_ref, nz_ref, xb_hbm, wg_ref, wu_ref, wd_ref, yb_hbm,
                    xs_ref, acc_ref, wgb_ref, wub_ref, wdb_ref, stg_in, stg_out, sem_in, sem_out):
    del se_ref
    s = pl.program_id(0)
    j = pl.program_id(1)
    nt = nt_ref[s]
    nz = nz_ref[s]
    row0 = row_ref[s]
    rt = ROW_TILE

    def rows(i):
        n = rt * PACK_SUB
        return pl.ds(pl.multiple_of((row0 + i * rt) * PACK_SUB, n), n)

    def in_copy(i, slot):
        return pltpu.make_async_copy(xb_hbm.at[rows(i)], stg_in.at[slot], sem_in.at[slot])

    def out_copy(i, slot):
        return pltpu.make_async_copy(stg_out.at[slot], yb_hbm.at[rows(i)], sem_out.at[slot])

    def partial_out(i):
        r = pl.multiple_of(i * rt, rt)
        xi = xs_ref[pl.ds(r, rt), :]
        g = jnp.dot(xi, wgb_ref[...], preferred_element_type=F32)
        u = jnp.dot(xi, wub_ref[...], preferred_element_type=F32)
        hj = (g * _sigmoid(g) * u).astype(BF16)
        return r, jnp.dot(hj, wdb_ref[...], preferred_element_type=F32)

    @pl.when((nz > 0) & (j == 0))
    def _zero_tail():
        stg_out[0] = jnp.zeros(stg_out.shape[1:], U32)

        def start(i, c):
            out_copy(i, 0).start()
            return c

        def wait(i, c):
            out_copy(i, 0).wait()
            return c

        lax.fori_loop(0, nz, start, 0)
        lax.fori_loop(0, nz, wait, 0)

    @pl.when(nt > 0)
    def _work():
        wgb_ref[...] = wg_ref[...].astype(BF16)
        wub_ref[...] = wu_ref[...].astype(BF16)
        wdb_ref[...] = wd_ref[...].astype(BF16)

        @pl.when(j == 0)
        def _first():
            in_copy(0, 0).start()

            def tile(i, c):
                slot = i % 2

                @pl.when(i + 1 < nt)
                def _():
                    in_copy(i + 1, 1 - slot).start()

                in_copy(i, slot).wait()
                xrows = pl.ds(pl.multiple_of(i * rt, rt), rt)
                for c, words in enumerate(_load_token_tiles(stg_in.at[slot], 0, rt)):
                    xs_ref[xrows, c * LANES:(c + 1) * LANES] = _unpack_lo(words).astype(BF16)
                    xs_ref[xrows, PACK_WORDS + c * LANES:PACK_WORDS + (c + 1) * LANES] = (
                        _unpack_hi(words).astype(BF16))
                r, part = partial_out(i)
                acc_ref[pl.ds(r, rt), :] = part
                return c

            lax.fori_loop(0, nt, tile, 0)

        @pl.when((j > 0) & (j < N_DE_CHUNKS - 1))
        def _mid():
            def tile(i, c):
                r, part = partial_out(i)
                acc_ref[pl.ds(r, rt), :] += part
                return c

            lax.fori_loop(0, nt, tile, 0)

        @pl.when(j == N_DE_CHUNKS - 1)
        def _last():
            def tile(i, c):
                slot = i % 2

                @pl.when(i >= 2)
                def _():
                    out_copy(i - 2, slot).wait()

                r, part = partial_out(i)
                _store_token_tiles(stg_out.at[slot], 0, rt, _pack_rows(acc_ref[pl.ds(r, rt), :] + part))
                out_copy(i, slot).start()
                return c

            lax.fori_loop(0, nt, tile, 0)

            @pl.when(nt >= 2)
            def _():
                out_copy(nt - 2, nt % 2).wait()

            out_copy(nt - 1, (nt - 1) % 2).wait()


def _experts(xb, se, row, ntl, nzl, w_gate, w_up, w_down, layer):
    p = xb.shape[0] // PACK_SUB
    d = D_MODEL
    nsuper = se.shape[0]
    last = N_DE_CHUNKS - 1

    def jj(s, j, nt_ref):
        return jnp.where(nt_ref[s] > 0, j, last)

    grid_spec = pltpu.PrefetchScalarGridSpec(
        num_scalar_prefetch=4,
        grid=(nsuper, N_DE_CHUNKS),
        in_specs=[
            pl.BlockSpec(memory_space=pl.ANY),
            pl.BlockSpec((None, None, d, DE_CHUNK),
                         lambda s, j, se_r, row_r, nt_r, nz_r: (layer, se_r[s], 0, jj(s, j, nt_r))),
            pl.BlockSpec((None, None, d, DE_CHUNK),
                         lambda s, j, se_r, row_r, nt_r, nz_r: (layer, se_r[s], 0, jj(s, j, nt_r))),
            pl.BlockSpec((None, None, DE_CHUNK, d),
                         lambda s, j, se_r, row_r, nt_r, nz_r: (layer, se_r[s], jj(s, j, nt_r), 0)),
        ],
        out_specs=pl.BlockSpec(memory_space=pl.ANY),
        scratch_shapes=[
            pltpu.VMEM((SUPER, d), BF16),
            pltpu.VMEM((SUPER, d), F32),
            pltpu.VMEM((d, DE_CHUNK), BF16),
            pltpu.VMEM((d, DE_CHUNK), BF16),
            pltpu.VMEM((DE_CHUNK, d), BF16),
            pltpu.VMEM((2, ROW_TILE * PACK_SUB, LANES), U32),
            pltpu.VMEM((2, ROW_TILE * PACK_SUB, LANES), U32),
            pltpu.SemaphoreType.DMA((2,)),
            pltpu.SemaphoreType.DMA((2,)),
        ],
    )
    return pl.pallas_call(
        _experts_kernel,
        grid_spec=grid_spec,
        out_shape=jax.ShapeDtypeStruct((p * PACK_SUB, LANES), U32),
        compiler_params=_cparams(("arbitrary", "arbitrary")),
        name="experts",
    )(se, row, ntl, nzl, xb, w_gate, w_up, w_down)


def _combine_kernel(d0_ref, d1_ref, x1_ref, w_ref, lng_ref, lnb_ref, yb_hbm, o_ref, ybuf_ref, sem):
    tm = TM_COMB

    def issue(tb, c):
        for u in range(ISSUE_UNROLL):
            t = tb * ISSUE_UNROLL + u
            dst = pl.ds(pl.multiple_of(t * PACK_SUB, PACK_SUB), PACK_SUB)
            tile = lambda r: pl.ds(pl.multiple_of(r, PACK_SUB), PACK_SUB)
            pltpu.make_async_copy(yb_hbm.at[tile(d0_ref[0, 0, t])], ybuf_ref.at[0, dst],
                                  sem.at[0]).start(priority=0)
            pltpu.make_async_copy(yb_hbm.at[tile(d1_ref[0, 0, t])], ybuf_ref.at[1, dst],
                                  sem.at[0]).start(priority=1)
        return c

    lax.fori_loop(0, tm // ISSUE_UNROLL, issue, 0)
    wpad = jnp.concatenate([w_ref[...], jnp.zeros((LANES - TOP_K, tm), F32)], axis=0)
    wt = wpad.T
    w0 = wt[:, 0:1]
    w1 = wt[:, 1:2]
    x1 = x1_ref[...]
    for k in range(TOP_K):
        pltpu.make_async_copy(ybuf_ref.at[k], ybuf_ref.at[k], sem.at[0]).wait()
    y0 = _load_token_tiles(ybuf_ref.at[0], 0, tm)
    y1 = _load_token_tiles(ybuf_ref.at[1], 0, tm)
    lo = [_unpack_lo(a) * w0 + _unpack_lo(b) * w1 for a, b in zip(y0, y1)]
    hi = [_unpack_hi(a) * w0 + _unpack_hi(b) * w1 for a, b in zip(y0, y1)]
    moe = jnp.concatenate(lo + hi, axis=1)
    o_ref[...] = _layer_norm_rows(DEEPNORM_ALPHA * x1 + moe, lng_ref[...], lnb_ref[...])


def _combine(x1, dest_tile, wts, lng, lnb, yb):
    t, d = x1.shape
    tm = TM_COMB
    nstep = t // tm
    d0 = dest_tile[0].reshape(nstep, 1, tm)
    d1 = dest_tile[1].reshape(nstep, 1, tm)
    smem_blk = pl.BlockSpec((1, 1, tm), lambda i: (i, 0, 0), memory_space=pltpu.SMEM)
    const = lambda i: (0, 0)
    return pl.pallas_call(
        _combine_kernel,
        grid=(nstep,),
        in_specs=[smem_blk, smem_blk,
                  pl.BlockSpec((tm, d), lambda i: (i, 0)),
                  pl.BlockSpec((TOP_K, tm), lambda i: (0, i)),
                  pl.BlockSpec((1, d), const),
                  pl.BlockSpec((1, d), const),
                  pl.BlockSpec(memory_space=pl.ANY)],
        out_specs=pl.BlockSpec((tm, d), lambda i: (i, 0)),
        out_shape=jax.ShapeDtypeStruct((t, d), F32),
        scratch_shapes=[pltpu.VMEM((TOP_K, tm * PACK_SUB, LANES), U32), pltpu.SemaphoreType.DMA((1,))],
        compiler_params=_cparams(("arbitrary",)),
        name="combine",
    )(d0, d1, x1, wts, lng, lnb, yb)


def _rotary_tables(seq):
    inv = 1.0 / (ROPE_BASE ** (jnp.arange(0, RET_HEAD_DIM, 2, dtype=F32) / RET_HEAD_DIM))
    ang = jnp.arange(seq, dtype=F32)[:, None] * inv[None, :]
    return jnp.cos(ang), jnp.sin(ang)


def _in_proj_weights(w):
    d = w.shape[0]
    half = RET_HEAD_DIM // 2
    qk = w[:, :2 * RET_WIDTH].astype(BF16).reshape(d, 2 * RET_HEADS, half, 2)
    qk = jnp.swapaxes(qk, 2, 3).reshape(d, 2 * RET_WIDTH)
    return jnp.concatenate([qk, w[:, 2 * RET_WIDTH:].astype(BF16)], axis=1)


def _routing_tables(counts, n_rows):
    padded = (counts + ROW_TILE - 1) // ROW_TILE * ROW_TILE
    pend = jnp.cumsum(padded)
    pstart = pend - padded
    ntile = padded // ROW_TILE
    nsup = (ntile + TILES_PER_SUPER - 1) // TILES_PER_SUPER
    send = jnp.cumsum(nsup)
    sstart = send - nsup
    nsuper = n_rows // SUPER + N_EXPERTS
    s = jnp.arange(nsuper, dtype=I32)
    total = send[-1]
    which = lambda q: jnp.clip(jnp.searchsorted(send, q, side='right'), 0, N_EXPERTS - 1).astype(I32)
    se = which(s)
    valid = s < total
    k = s - sstart[se]
    row = pstart[se] + k * SUPER
    nt = jnp.clip(ntile[se] - k * TILES_PER_SUPER, 0, TILES_PER_SUPER)
    idle = s - total
    tail_row = pend[-1] + idle * SUPER
    nz = jnp.clip((n_rows - tail_row) // ROW_TILE, 0, TILES_PER_SUPER)
    se = jnp.where(valid, se, which(total - 1))
    row = jnp.where(valid, row, jnp.minimum(tail_row, n_rows - ROW_TILE))
    nt = jnp.where(valid, nt, 0)
    nz = jnp.where(valid, 0, nz)
    seg_last = jnp.where(padded > 0, pend - ROW_TILE, -1)
    tail = pend[-1] + jnp.arange(N_EXPERTS, dtype=I32) * ROW_TILE
    zrow = jnp.concatenate([seg_last, jnp.where(tail < n_rows, tail, -1)])
    return (pstart.astype(I32), se.astype(I32), row.astype(I32), nt.astype(I32), nz.astype(I32),
            zrow.astype(I32))


def kernel(x, w_in, w_out, ssm_lambda_re, ssm_lambda_im, ssm_b_re, ssm_b_im, ssm_c_re, ssm_c_im,
           ssm_d, ssm_log_dt, w_glu, ln1_g, ln1_b, ln2_g, ln2_b, router_w, router_b,
           w_gate, w_up, w_down):
    batch, seq, d = x.shape
    t = batch * seq
    n_assign = t * TOP_K
    n_rows = n_assign + N_EXPERTS * ROW_TILE
    assert seq % TM_PROJ == 0 and seq % RET_CHUNK == 0 and seq % S5_TL == 0
    assert t % TM_POST == 0 and t % TM_COMB == 0 and t % TM_DISP == 0 and n_rows % SUPER == 0

    cos, sin = _rotary_tables(seq)
    expert_ids = jnp.arange(N_EXPERTS, dtype=I32)
    rw_t = router_w.astype(F32).T
    rhi = rw_t.astype(BF16)
    rlo = (rw_t - rhi.astype(F32)).astype(BF16)
    rcat = jnp.concatenate([rhi, rlo], axis=0)
    rb = router_b.astype(F32).reshape(N_EXPERTS, 1)

    x2 = x.reshape(t, d)
    for l in range(DEPTH):
        proj = _in_proj(x2, _in_proj_weights(w_in[l]), cos, sin, seq)
        ret = _retention(proj, batch, seq)
        tables = _s5_tables(ssm_lambda_re[l], ssm_lambda_im[l], ssm_b_re[l], ssm_b_im[l],
                            ssm_c_re[l], ssm_c_im[l], ssm_d[l], ssm_log_dt[l])
        y = _s5(proj, tables, batch, seq).reshape(t, SSM_WIDTH)
        x1, x1p, e, wts, rank, cnt = _post_mix(
            x2, ret, y, w_glu[l].astype(BF16), w_out[l].astype(BF16),
            ln1_g[l].reshape(1, d), ln1_b[l].reshape(1, d), rcat, rhi, rb)
        pstart, se, row, ntl, nzl, zrow = _routing_tables(cnt[:, 0], n_rows)
        dest = rank + jnp.sum(jnp.where(e[..., None] == expert_ids, pstart, 0), axis=-1)
        dest_tile = dest * PACK_SUB
        xb = _dispatch(x1p, dest_tile, zrow, n_rows)
        yb = _experts(xb, se, row, ntl, nzl, w_gate, w_up, w_down, l)
        x2 = _combine(x1, dest_tile, wts, ln2_g[l].reshape(1, d), ln2_b[l].reshape(1, d), yb)
    return x2.reshape(batch, seq, d)
```

```python
import functools
import math

import numpy as np
import jax
import jax.numpy as jnp
from jax import lax
from jax.experimental import pallas as pl
from jax.experimental.pallas import tpu as pltpu

F32 = jnp.float32
BF16 = jnp.bfloat16
I32 = jnp.int32

D_MODEL = 2048
DEPTH = 2
RET_WIDTH = 1024
SSM_WIDTH = 1024
RET_HEAD_DIM = 256
RET_HEADS = RET_WIDTH // RET_HEAD_DIM
ROPE_BASE = 10000.0
SSM_GROUP = 16
SSM_GROUPS = SSM_WIDTH // SSM_GROUP
SSM_STATE = 64
IN_PROJ_WIDTH = 4 * RET_WIDTH + SSM_WIDTH
N_EXPERTS = 32
N_EXPERT_GROUPS = 4
EXPERTS_PER_GROUP = N_EXPERTS // N_EXPERT_GROUPS
TOP_K = 2
D_EXPERT = D_MODEL // 2
LN_EPS = 1e-5
DEEPNORM_ALPHA = (2.0 * DEPTH) ** 0.25

LANES = 128
SUBLANES = 8
VMEM_LIMIT = 56 * 1024 * 1024

TM_PROJ = 1024
TN_PROJ = 1024
RET_CHUNK = 256
S5_TL = 128
S5_GROUP = 16
S5_LT = SSM_WIDTH // LANES
S5_NSTATE = (LANES // SSM_GROUP) * SSM_STATE
TM_POST = 512
TSUB_POST = 256
TM_COMB = 256
ROW_TILE = 256
TILES_PER_SUPER = 8
SUPER = ROW_TILE * TILES_PER_SUPER
DE_CHUNK = 256
N_DE_CHUNKS = D_EXPERT // DE_CHUNK
TM_DISP = 512
ISSUE_UNROLL = 8


def _sigmoid(x):
    return 1.0 / (1.0 + jnp.exp(-x))


def _cparams(sem, vmem=VMEM_LIMIT):
    return pltpu.CompilerParams(dimension_semantics=sem, vmem_limit_bytes=vmem)


PACK_WORDS = D_MODEL // 2
PACK_SUB = PACK_WORDS // LANES
U32 = jnp.uint32


def _pack_rows(x):
    bits = lambda v: lax.bitcast_convert_type(v.astype(BF16).astype(F32), U32)
    return (bits(x[:, :PACK_WORDS]) >> 16) | (bits(x[:, PACK_WORDS:]) & jnp.uint32(0xFFFF0000))


def _unpack_lo(w):
    return lax.bitcast_convert_type(w << 16, F32)


def _unpack_hi(w):
    return lax.bitcast_convert_type(w & jnp.uint32(0xFFFF0000), F32)


def _store_token_tiles(ref, row0, n, packed):
    for c in range(PACK_SUB):
        ref[pl.ds(row0 * PACK_SUB + c, n, stride=PACK_SUB), :] = packed[:, c * LANES:(c + 1) * LANES]


def _load_token_tiles(ref, row0, n):
    return [ref[pl.ds(row0 * PACK_SUB + c, n, stride=PACK_SUB), :] for c in range(PACK_SUB)]


def _in_proj_kernel(x_ref, w_ref, cos_ref, sin_ref, o_ref):
    j = pl.program_id(1)
    acc = jnp.dot(x_ref[...].astype(BF16), w_ref[...], preferred_element_type=F32)
    is_rot = j < 2
    is_gate = j == 3
    scale = jnp.where(j == 1, RET_HEAD_DIM ** -0.5, 1.0).astype(F32)
    c = jnp.where(is_rot, cos_ref[...] * scale, 1.0)
    s = jnp.where(is_rot, sin_ref[...] * scale, 0.0)
    half = RET_HEAD_DIM // 2
    for h in range(RET_HEADS):
        lo = h * RET_HEAD_DIM
        t1 = acc[:, lo:lo + half]
        t2 = acc[:, lo + half:lo + RET_HEAD_DIM]
        r1 = t1 * c - t2 * s
        r2 = t1 * s + t2 * c
        o_ref[:, lo:lo + half] = (r1 * jnp.where(is_gate, _sigmoid(r1), 1.0)).astype(BF16)
        o_ref[:, lo + half:lo + RET_HEAD_DIM] = (r2 * jnp.where(is_gate, _sigmoid(r2), 1.0)).astype(BF16)


def _in_proj(x2, w_bf, cos, sin, seq):
    t, d = x2.shape
    n = w_bf.shape[1]
    tiles_per_seq = seq // TM_PROJ
    return pl.pallas_call(
        _in_proj_kernel,
        grid=(t // TM_PROJ, n // TN_PROJ),
        in_specs=[
            pl.BlockSpec((TM_PROJ, d), lambda i, j: (i, 0)),
            pl.BlockSpec((d, TN_PROJ), lambda i, j: (0, j)),
            pl.BlockSpec((TM_PROJ, LANES), lambda i, j: (i % tiles_per_seq, 0)),
            pl.BlockSpec((TM_PROJ, LANES), lambda i, j: (i % tiles_per_seq, 0)),
        ],
        out_specs=pl.BlockSpec((TM_PROJ, TN_PROJ), lambda i, j: (i, j)),
        out_shape=jax.ShapeDtypeStruct((t, n), BF16),
        compiler_params=_cparams(("arbitrary", "arbitrary")),
        name="in_proj",
    )(x2, w_bf, cos, sin)


def _ret_kernel(q_ref, k_ref, v_ref, g_ref, mask_ref, qd_ref, kd_ref, o_ref, r_ref):
    n = pl.program_id(1)

    @pl.when(n == 0)
    def _init():
        r_ref[...] = jnp.zeros_like(r_ref)

    for h in range(RET_HEADS):
        cols = slice(h * RET_HEAD_DIM, (h + 1) * RET_HEAD_DIM)
        q = q_ref[:, cols]
        k = k_ref[:, cols]
        v = v_ref[:, cols]
        s = lax.dot_general(q, k, (((1,), (1,)), ((), ())), preferred_element_type=F32)
        s = s * mask_ref[h]
        inner = jnp.dot(s.astype(BF16), v, preferred_element_type=F32)
        qd = qd_ref[h]
        r_prev = r_ref[h]
        cross = jnp.dot((q.astype(F32) * qd).astype(BF16), r_prev.astype(BF16), preferred_element_type=F32)
        o = inner + cross
        kdec = (k.astype(F32) * kd_ref[h]).astype(BF16)
        kv = lax.dot_general(kdec, v, (((0,), (0,)), ((), ())), preferred_element_type=F32)
        r_ref[h] = r_prev * qd[RET_CHUNK - 1:RET_CHUNK, :] + kv
        mu = jnp.mean(o, axis=-1, keepdims=True)
        oc = o - mu
        var = jnp.mean(oc * oc, axis=-1, keepdims=True)
        o_ref[:, cols] = (g_ref[:, cols].astype(F32) * (oc * lax.rsqrt(var + LN_EPS))).astype(BF16)


def _retention_tables():
    c = RET_CHUNK
    log_gamma = np.log(1.0 - 2.0 ** (-5.0 - np.arange(RET_HEADS, dtype=np.float64)))
    idx = np.arange(c, dtype=np.float64)
    diff = idx[:, None] - idx[None, :]
    mask = np.where(diff >= 0, np.exp(log_gamma[:, None, None] * np.maximum(diff, 0.0)), 0.0)
    qd = np.exp(log_gamma[:, None] * (idx + 1.0)[None, :])
    kd = np.exp(log_gamma[:, None] * (c - 1.0 - idx)[None, :])
    bc = lambda a: np.broadcast_to(a[:, :, None], (RET_HEADS, c, RET_HEAD_DIM)).astype(np.float32)
    return mask.astype(np.float32), bc(qd), bc(kd)


def _retention(proj, batch, seq):
    t = proj.shape[0]
    nch = seq // RET_CHUNK
    mask, qd, kd = _retention_tables()
    blk = (RET_CHUNK, RET_WIDTH)
    seg = lambda c: pl.BlockSpec(blk, lambda b, n: (b * nch + n, c))
    whole = lambda a: pl.BlockSpec(a.shape, lambda b, n: (0, 0, 0))
    return pl.pallas_call(
        _ret_kernel,
        grid=(batch, nch),
        in_specs=[seg(0), seg(1), seg(2), seg(3), whole(mask), whole(qd), whole(kd)],
        out_specs=seg(0),
        out_shape=jax.ShapeDtypeStruct((t, RET_WIDTH), BF16),
        scratch_shapes=[pltpu.VMEM((RET_HEADS, RET_HEAD_DIM, RET_HEAD_DIM), F32)],
        compiler_params=_cparams(("arbitrary", "arbitrary")),
        name="retention",
    )(proj, proj, proj, proj, jnp.asarray(mask), jnp.asarray(qd), jnp.asarray(kd))


def _s5_kernel(u_ref, bm_ref, cm_ref, lam_ref, d_ref, y_ref, us_ref, ut_ref, bu_ref, hb_ref, yt_ref, st_ref):
    n = pl.program_id(1)
    tl = S5_TL
    nb = SUBLANES
    ns = S5_NSTATE

    @pl.when(n == 0)
    def _init():
        st_ref[...] = jnp.zeros_like(st_ref)

    for b in range(nb):
        us_ref[b * tl:(b + 1) * tl, :] = u_ref[b].astype(F32)

    for t in range(tl):
        ut_ref[t * nb:(t + 1) * nb, :] = us_ref[pl.ds(t, nb, stride=tl), :]

    ar = jnp.broadcast_to(lam_ref[0:1, :], (nb, ns))
    ai = jnp.broadcast_to(lam_ref[1:2, :], (nb, ns))
    sr = st_ref[0:nb, :]
    si = st_ref[nb:2 * nb, :]

    grows = S5_GROUP * nb

    def project_in(g):
        rows = slice(g * grows, (g + 1) * grows)
        bu_ref[rows, :] = jnp.dot(ut_ref[rows, :].astype(BF16), bm_ref[...], preferred_element_type=F32)

    project_in(0)
    for g in range(tl // S5_GROUP):
        if g + 1 < tl // S5_GROUP:
            project_in(g + 1)
        for tt in range(0, S5_GROUP, 2):
            r0 = g * grows + tt * nb
            pair_r = []
            pair_i = []
            for r in (r0, r0 + nb):
                br = bu_ref[r:r + nb, 0:ns]
                bi = bu_ref[r:r + nb, ns:2 * ns]
                sr, si = ar * sr - ai * si + br, ar * si + ai * sr + bi
                pair_r.append(sr)
                pair_i.append(si)
            hb_ref[r0:r0 + 2 * nb, 0:ns] = jnp.concatenate(pair_r, axis=0).astype(BF16)
            hb_ref[r0:r0 + 2 * nb, ns:2 * ns] = jnp.concatenate(pair_i, axis=0).astype(BF16)
        rows = slice(g * grows, (g + 1) * grows)
        yt_ref[rows, :] = (jnp.dot(hb_ref[rows, :], cm_ref[...], preferred_element_type=F32)
                           + ut_ref[rows, :] * d_ref[...])
    st_ref[0:nb, :] = sr
    st_ref[nb:2 * nb, :] = si
    for b in range(nb):
        y_ref[b] = yt_ref[pl.ds(b, tl, stride=nb), :].astype(BF16)


def _s5_tables(lam_re, lam_im, b_re, b_im, c_re, c_im, d, log_dt):
    lam = lax.complex(lam_re.astype(F32), lam_im.astype(F32))
    dt = jnp.exp(log_dt.astype(F32))[:, None]
    lam_bar = jnp.exp(lam * dt)
    b_bar = ((lam_bar - 1.0) / lam)[..., None] * lax.complex(b_re.astype(F32), b_im.astype(F32))
    gpt = LANES // SSM_GROUP
    eye = jnp.eye(gpt, dtype=F32)

    def bdiag_in(m):
        m = m.reshape(S5_LT, gpt, SSM_STATE, SSM_GROUP)
        return jnp.einsum('jgpi,gh->jgihp', m, eye).reshape(S5_LT, LANES, gpt * SSM_STATE)

    def bdiag_out(m):
        m = m.reshape(S5_LT, gpt, SSM_GROUP, SSM_STATE)
        return jnp.einsum('jgop,gh->jgpho', m, eye).reshape(S5_LT, gpt * SSM_STATE, LANES)

    bmat = jnp.concatenate([bdiag_in(jnp.real(b_bar)), bdiag_in(jnp.imag(b_bar))], axis=-1).astype(BF16)
    cmat = jnp.concatenate([bdiag_out(c_re.astype(F32)), -bdiag_out(c_im.astype(F32))], axis=1).astype(BF16)
    lam_t = jnp.stack([jnp.real(lam_bar).reshape(S5_LT, S5_NSTATE),
                       jnp.imag(lam_bar).reshape(S5_LT, S5_NSTATE)], axis=1)
    d_t = d.astype(F32).reshape(S5_LT, 1, LANES)
    return bmat, cmat, lam_t, d_t


def _s5(proj, tables, batch, seq):
    bmat, cmat, lam_t, d_t = tables
    assert batch == SUBLANES
    proj3 = proj.reshape(batch, seq, IN_PROJ_WIDTH)
    ucol = (4 * RET_WIDTH) // LANES
    tl = S5_TL
    return pl.pallas_call(
        _s5_kernel,
        grid=(S5_LT, seq // tl),
        in_specs=[
            pl.BlockSpec((batch, tl, LANES), lambda j, n: (0, n, ucol + j)),
            pl.BlockSpec((None, LANES, 2 * S5_NSTATE), lambda j, n: (j, 0, 0)),
            pl.BlockSpec((None, 2 * S5_NSTATE, LANES), lambda j, n: (j, 0, 0)),
            pl.BlockSpec((None, 2, S5_NSTATE), lambda j, n: (j, 0, 0)),
            pl.BlockSpec((None, 1, LANES), lambda j, n: (j, 0, 0)),
        ],
        out_specs=pl.BlockSpec((batch, tl, LANES), lambda j, n: (0, n, j)),
        out_shape=jax.ShapeDtypeStruct((batch, seq, SSM_WIDTH), BF16),
        scratch_shapes=[
            pltpu.VMEM((batch * tl, LANES), F32),
            pltpu.VMEM((batch * tl, LANES), F32),
            pltpu.VMEM((batch * tl, 2 * S5_NSTATE), F32),
            pltpu.VMEM((batch * tl, 2 * S5_NSTATE), BF16),
            pltpu.VMEM((batch * tl, LANES), F32),
            pltpu.VMEM((2 * SUBLANES, S5_NSTATE), F32),
        ],
        compiler_params=_cparams(("arbitrary", "arbitrary")),
        name="s5",
    )(proj3, bmat, cmat, lam_t, d_t)


def _layer_norm_rows(r, g, b):
    mu = jnp.mean(r, axis=-1, keepdims=True)
    rc = r - mu
    var = jnp.mean(rc * rc, axis=-1, keepdims=True)
    return rc * lax.rsqrt(var + LN_EPS) * g + b


def _post_mix_kernel(x_ref, ret_ref, y_ref, wglu_ref, wout_ref, lng_ref, lnb_ref,
                     rcat_ref, rhi_ref, rb_ref, tri_ref,
                     x1_ref, x1p_ref, e_ref, w_ref, rank_ref, cnt_ref, carry_ref):
    i = pl.program_id(0)

    @pl.when(i == 0)
    def _init():
        carry_ref[...] = jnp.zeros_like(carry_ref)

    args = (x_ref, ret_ref, y_ref, wglu_ref, wout_ref, lng_ref, lnb_ref, rcat_ref, rhi_ref, rb_ref, tri_ref,
            x1_ref, x1p_ref, e_ref, w_ref, rank_ref, carry_ref)
    a = _post_mix_phases(0, *args)
    b = _post_mix_phases(1, *args)
    nph = len(a)
    a[0]()
    for k in range(1, nph):
        a[k]()
        b[k - 1]()
    b[nph - 1]()
    cnt_ref[...] = carry_ref[...].astype(I32)


def _post_mix_phases(sub, x_ref, ret_ref, y_ref, wglu_ref, wout_ref, lng_ref, lnb_ref,
                     rcat_ref, rhi_ref, rb_ref, tri_ref, x1_ref, x1p_ref, e_ref, w_ref, rank_ref, carry_ref):
    tm = TSUB_POST
    rows = slice(sub * tm, (sub + 1) * tm)
    st = {}

    def gelu():
        st['ya'] = jax.nn.gelu(y_ref[rows, :].astype(F32))

    def glu_matmul():
        st['z'] = jnp.dot(st['ya'].astype(BF16), wglu_ref[...], preferred_element_type=F32)

    def glu_gate():
        st['ssm'] = (st.pop('ya') * _sigmoid(st.pop('z'))).astype(BF16)

    def out_matmul():
        mixed = jnp.concatenate([ret_ref[rows, :], st.pop('ssm')], axis=1)
        st['h'] = jnp.dot(mixed, wout_ref[...], preferred_element_type=F32)

    def norm():
        x1 = _layer_norm_rows(DEEPNORM_ALPHA * x_ref[rows, :] + st.pop('h'), lng_ref[...], lnb_ref[...])
        x1_ref[rows, :] = x1
        _store_token_tiles(x1p_ref, sub * tm, tm, _pack_rows(x1))
        st['xh'] = x1.astype(BF16)
        st['xl'] = (x1 - st['xh'].astype(F32)).astype(BF16)

    def router_matmul():
        nt = (((1,), (1,)), ((), ()))
        l1 = lax.dot_general(rcat_ref[...], st.pop('xh'), nt, preferred_element_type=F32)
        l2 = lax.dot_general(rhi_ref[...], st.pop('xl'), nt, preferred_element_type=F32)
        st['logits'] = l1[0:N_EXPERTS] + l1[N_EXPERTS:] + l2 + rb_ref[...]

    def route():
        _route(st.pop('logits'), rows, tri_ref, e_ref, w_ref, rank_ref, carry_ref)

    return [gelu, glu_matmul, glu_gate, out_matmul, norm, router_matmul, route]


def _route(logits, rows, tri_ref, e_ref, w_ref, rank_ref, carry_ref):
    tm = TSUB_POST
    m = jnp.max(logits, axis=0, keepdims=True)
    ex = jnp.exp(logits - m)
    p = ex / jnp.sum(ex, axis=0, keepdims=True)

    eg = EXPERTS_PER_GROUP
    iota_g = lax.broadcasted_iota(I32, (eg, tm), 0)
    best = None
    for g in range(N_EXPERT_GROUPS):
        pg = p[g * eg:(g + 1) * eg]
        m1 = jnp.max(pg, axis=0, keepdims=True)
        i1 = jnp.min(jnp.where(pg == m1, iota_g, eg), axis=0, keepdims=True)
        pg2 = jnp.where(iota_g == i1, -1.0, pg)
        m2 = jnp.max(pg2, axis=0, keepdims=True)
        i2 = jnp.min(jnp.where(pg2 == m2, iota_g, eg), axis=0, keepdims=True)
        sg = m1 + m2
        if best is None:
            best = (sg, m1, m2, i1, i2)
        else:
            better = sg > best[0]
            cand = (sg, m1, m2, i1 + g * eg, i2 + g * eg)
            best = tuple(jnp.where(better, c, o) for c, o in zip(cand, best))
    _, v1, v2, e1, e2 = best
    tot = v1 + v2
    e_ref[0:1, rows] = e1
    e_ref[1:2, rows] = e2
    w_ref[0:1, rows] = v1 / tot
    w_ref[1:2, rows] = v2 / tot

    iota_e = lax.broadcasted_iota(I32, (N_EXPERTS, tm), 0)
    oh1 = iota_e == e1
    oh2 = iota_e == e2
    oh = jnp.where(oh1, 1.0, jnp.where(oh2, 1.0, 0.0))
    before = jnp.dot(oh.astype(BF16), tri_ref[...], preferred_element_type=F32) + carry_ref[:, 0:1]
    rank_ref[0:1, rows] = jnp.sum(jnp.where(oh1, before, 0.0), axis=0, keepdims=True).astype(I32)
    rank_ref[1:2, rows] = jnp.sum(jnp.where(oh2, before, 0.0), axis=0, keepdims=True).astype(I32)
    carry_ref[...] = carry_ref[...] + jnp.sum(oh, axis=1, keepdims=True)


def _post_mix(x2, ret, y, wglu_bf, wout_bf, lng, lnb, rcat, rhi, rb):
    t, d = x2.shape
    tm = TM_POST
    ts = TSUB_POST
    tri = jnp.asarray(np.triu(np.ones((ts, ts), np.float32), 1), BF16)
    const = lambda i: (0, 0)
    tok = lambda i: (i, 0)
    lane = lambda i: (0, i)
    return pl.pallas_call(
        _post_mix_kernel,
        grid=(t // tm,),
        in_specs=[
            pl.BlockSpec((tm, d), tok),
            pl.BlockSpec((tm, RET_WIDTH), tok),
            pl.BlockSpec((tm, SSM_WIDTH), tok),
            pl.BlockSpec((SSM_WIDTH, SSM_WIDTH), const),
            pl.BlockSpec((RET_WIDTH + SSM_WIDTH, d), const),
            pl.BlockSpec((1, d), const),
            pl.BlockSpec((1, d), const),
            pl.BlockSpec((2 * N_EXPERTS, d), const),
            pl.BlockSpec((N_EXPERTS, d), const),
            pl.BlockSpec((N_EXPERTS, 1), const),
            pl.BlockSpec((ts, ts), const),
        ],
        out_specs=[
            pl.BlockSpec((tm, d), tok),
            pl.BlockSpec((tm * PACK_SUB, LANES), tok),
            pl.BlockSpec((TOP_K, tm), lane),
            pl.BlockSpec((TOP_K, tm), lane),
            pl.BlockSpec((TOP_K, tm), lane),
            pl.BlockSpec((N_EXPERTS, LANES), const),
        ],
        out_shape=[
            jax.ShapeDtypeStruct((t, d), F32),
            jax.ShapeDtypeStruct((t * PACK_SUB, LANES), U32),
            jax.ShapeDtypeStruct((TOP_K, t), I32),
            jax.ShapeDtypeStruct((TOP_K, t), F32),
            jax.ShapeDtypeStruct((TOP_K, t), I32),
            jax.ShapeDtypeStruct((N_EXPERTS, LANES), I32),
        ],
        scratch_shapes=[pltpu.VMEM((N_EXPERTS, LANES), F32)],
        compiler_params=_cparams(("arbitrary",)),
        name="post_mix",
    )(x2, ret, y, wglu_bf, wout_bf, lng, lnb, rcat, rhi, rb, tri)


def _dispatch_kernel(zrow_ref, d0_ref, d1_ref, x_ref, xb_hbm, zbuf_ref, sem, zsem):
    tm = TM_DISP

    @pl.when(pl.program_id(0) == 0)
    def _zero_fill():
        zbuf_ref[...] = jnp.zeros_like(zbuf_ref)

        def zcopy(k):
            n = ROW_TILE * PACK_SUB
            rows = pl.ds(pl.multiple_of(zrow_ref[k] * PACK_SUB, n), n)
            return pltpu.make_async_copy(zbuf_ref, xb_hbm.at[rows], zsem.at[0])

        def start(k, c):
            @pl.when(zrow_ref[k] >= 0)
            def _():
                zcopy(k).start()
            return c

        def wait(k, c):
            @pl.when(zrow_ref[k] >= 0)
            def _():
                zcopy(k).wait()
            return c

        lax.fori_loop(0, 2 * N_EXPERTS, start, 0)
        lax.fori_loop(0, 2 * N_EXPERTS, wait, 0)

    def issue(tb, c):
        for u in range(ISSUE_UNROLL):
            t = tb * ISSUE_UNROLL + u
            src = x_ref.at[pl.ds(t * PACK_SUB, PACK_SUB)]
            tile = lambda r: pl.ds(pl.multiple_of(r, PACK_SUB), PACK_SUB)
            pltpu.make_async_copy(src, xb_hbm.at[tile(d0_ref[0, 0, t])], sem.at[0]).start(priority=0)
            pltpu.make_async_copy(src, xb_hbm.at[tile(d1_ref[0, 0, t])], sem.at[0]).start(priority=1)
        return c

    lax.fori_loop(0, tm // ISSUE_UNROLL, issue, 0)
    for _ in range(TOP_K):
        pltpu.make_async_copy(x_ref, xb_hbm.at[pl.ds(0, tm * PACK_SUB)], sem.at[0]).wait()


def _dispatch(x1p, dest_tile, zrow, n_rows):
    t = x1p.shape[0] // PACK_SUB
    tm = TM_DISP
    nstep = t // tm
    d0 = dest_tile[0].reshape(nstep, 1, tm)
    d1 = dest_tile[1].reshape(nstep, 1, tm)
    smem_blk = pl.BlockSpec((1, 1, tm), lambda i, z: (i, 0, 0), memory_space=pltpu.SMEM)
    grid_spec = pltpu.PrefetchScalarGridSpec(
        num_scalar_prefetch=1,
        grid=(nstep,),
        in_specs=[smem_blk, smem_blk, pl.BlockSpec((tm * PACK_SUB, LANES), lambda i, z: (i, 0))],
        out_specs=pl.BlockSpec(memory_space=pl.ANY),
        scratch_shapes=[pltpu.VMEM((ROW_TILE * PACK_SUB, LANES), U32),
                        pltpu.SemaphoreType.DMA((1,)), pltpu.SemaphoreType.DMA((1,))],
    )
    return pl.pallas_call(
        _dispatch_kernel,
        grid_spec=grid_spec,
        out_shape=jax.ShapeDtypeStruct((n_rows * PACK_SUB, LANES), U32),
        compiler_params=_cparams(("arbitrary",)),
        name="dispatch",
    )(zrow, d0, d1, x1p)


def _experts_kernel(se_ref, row_ref, nt_ref, nz_ref, xb_hbm, wg_ref, wu_ref, wd_ref, yb_hbm,
                    xs_ref, acc_ref, wgb_ref, wub_ref, wdb_ref, stg_in, stg_out, sem_in, sem_out):
    del se_ref
    s = pl.program_id(0)
    j = pl.program_id(1)
    nt = nt_ref[s]
    nz = nz_ref[s]
    row0 = row_ref[s]
    rt = ROW_TILE

    def rows(i):
        n = rt * PACK_SUB
        return pl.ds(pl.multiple_of((row0 + i * rt) * PACK_SUB, n), n)

    def in_copy(i, slot):
        return pltpu.make_async_copy(xb_hbm.at[rows(i)], stg_in.at[slot], sem_in.at[slot])

    def out_copy(i, slot):
        return pltpu.make_async_copy(stg_out.at[slot], yb_hbm.at[rows(i)], sem_out.at[slot])

    def partial_out(i, ntiles):
        r = pl.multiple_of(i * rt, rt)
        xi = xs_ref[pl.ds(r, ntiles * rt), :]
        g = jnp.dot(xi, wgb_ref[...], preferred_element_type=F32)
        u = jnp.dot(xi, wub_ref[...], preferred_element_type=F32)
        hj = (g * _sigmoid(g) * u).astype(BF16)
        return r, jnp.dot(hj, wdb_ref[...], preferred_element_type=F32)

    def load_tile(i):
        slot = i % 2

        @pl.when(i + 1 < nt)
        def _():
            in_copy(i + 1, 1 - slot).start()

        in_copy(i, slot).wait()
        xrows = pl.ds(pl.multiple_of(i * rt, rt), rt)
        for c, words in enumerate(_load_token_tiles(stg_in.at[slot], 0, rt)):
            xs_ref[xrows, c * LANES:(c + 1) * LANES] = _unpack_lo(words).astype(BF16)
            xs_ref[xrows, PACK_WORDS + c * LANES:PACK_WORDS + (c + 1) * LANES] = _unpack_hi(words).astype(BF16)

    def store_tile(i, vals):
        slot = i % 2

        @pl.when(i >= 2)
        def _():
            out_copy(i - 2, slot).wait()

        _store_token_tiles(stg_out.at[slot], 0, rt, _pack_rows(vals))
        out_copy(i, slot).start()

    def for_tiles(body):
        def pair(p, c):
            body(2 * p, 2)
            return c

        lax.fori_loop(0, nt // 2, pair, 0)

        @pl.when(nt % 2 == 1)
        def _():
            body(nt - 1, 1)

    @pl.when((nz > 0) & (j == 0))
    def _zero_tail():
        stg_out[0] = jnp.zeros(stg_out.shape[1:], U32)

        def start(i, c):
            out_copy(i, 0).start()
            return c

        def wait(i, c):
            out_copy(i, 0).wait()
            return c

        lax.fori_loop(0, nz, start, 0)
        lax.fori_loop(0, nz, wait, 0)

    @pl.when(nt > 0)
    def _work():
        wgb_ref[...] = wg_ref[...].astype(BF16)
        wub_ref[...] = wu_ref[...].astype(BF16)
        wdb_ref[...] = wd_ref[...].astype(BF16)

        @pl.when(j == 0)
        def _first():
            in_copy(0, 0).start()

            def body(i, ntiles):
                for k in range(ntiles):
                    load_tile(i + k)
                r, part = partial_out(i, ntiles)
                acc_ref[pl.ds(r, ntiles * rt), :] = part

            for_tiles(body)

        @pl.when((j > 0) & (j < N_DE_CHUNKS - 1))
        def _mid():
            def body(i, ntiles):
                r, part = partial_out(i, ntiles)
                acc_ref[pl.ds(r, ntiles * rt), :] += part

            for_tiles(body)

        @pl.when(j == N_DE_CHUNKS - 1)
        def _last():
            def body(i, ntiles):
                r, part = partial_out(i, ntiles)
                total = acc_ref[pl.ds(r, ntiles * rt), :] + part
                for k in range(ntiles):
                    store_tile(i + k, total[k * rt:(k + 1) * rt, :])

            for_tiles(body)

            @pl.when(nt >= 2)
            def _():
                out_copy(nt - 2, nt % 2).wait()

            out_copy(nt - 1, (nt - 1) % 2).wait()


def _experts(xb, se, row, ntl, nzl, w_gate, w_up, w_down, layer):
    p = xb.shape[0] // PACK_SUB
    d = D_MODEL
    nsuper = se.shape[0]
    last = N_DE_CHUNKS - 1

    def jj(s, j, nt_ref):
        return jnp.where(nt_ref[s] > 0, j, last)

    grid_spec = pltpu.PrefetchScalarGridSpec(
        num_scalar_prefetch=4,
        grid=(nsuper, N_DE_CHUNKS),
        in_specs=[
            pl.BlockSpec(memory_space=pl.ANY),
            pl.BlockSpec((None, None, d, DE_CHUNK),
                         lambda s, j, se_r, row_r, nt_r, nz_r: (layer, se_r[s], 0, jj(s, j, nt_r))),
            pl.BlockSpec((None, None, d, DE_CHUNK),
                         lambda s, j, se_r, row_r, nt_r, nz_r: (layer, se_r[s], 0, jj(s, j, nt_r))),
            pl.BlockSpec((None, None, DE_CHUNK, d),
                         lambda s, j, se_r, row_r, nt_r, nz_r: (layer, se_r[s], jj(s, j, nt_r), 0)),
        ],
        out_specs=pl.BlockSpec(memory_space=pl.ANY),
        scratch_shapes=[
            pltpu.VMEM((SUPER, d), BF16),
            pltpu.VMEM((SUPER, d), F32),
            pltpu.VMEM((d, DE_CHUNK), BF16),
            pltpu.VMEM((d, DE_CHUNK), BF16),
            pltpu.VMEM((DE_CHUNK, d), BF16),
            pltpu.VMEM((2, ROW_TILE * PACK_SUB, LANES), U32),
            pltpu.VMEM((2, ROW_TILE * PACK_SUB, LANES), U32),
            pltpu.SemaphoreType.DMA((2,)),
            pltpu.SemaphoreType.DMA((2,)),
        ],
    )
    return pl.pallas_call(
        _experts_kernel,
        grid_spec=grid_spec,
        out_shape=jax.ShapeDtypeStruct((p * PACK_SUB, LANES), U32),
        compiler_params=_cparams(("arbitrary", "arbitrary")),
        name="experts",
    )(se, row, ntl, nzl, xb, w_gate, w_up, w_down)


def _combine_kernel(d0_ref, d1_ref, d0n_ref, d1n_ref, x1_ref, w_ref, lng_ref, lnb_ref, yb_hbm,
                    o_ref, ybuf_ref, sem):
    tm = TM_COMB
    i = pl.program_id(0)
    slot = i % 2

    def gather(da_ref, db_ref, into):
        def issue(tb, c):
            for u in range(ISSUE_UNROLL):
                t = tb * ISSUE_UNROLL + u
                dst = pl.ds(pl.multiple_of(t * PACK_SUB, PACK_SUB), PACK_SUB)
                tile = lambda r: pl.ds(pl.multiple_of(r, PACK_SUB), PACK_SUB)
                pltpu.make_async_copy(yb_hbm.at[tile(da_ref[0, 0, t])], ybuf_ref.at[into, 0, dst],
                                      sem.at[into]).start(priority=0)
                pltpu.make_async_copy(yb_hbm.at[tile(db_ref[0, 0, t])], ybuf_ref.at[into, 1, dst],
                                      sem.at[into]).start(priority=1)
            return c

        lax.fori_loop(0, tm // ISSUE_UNROLL, issue, 0)

    @pl.when(i == 0)
    def _prime():
        gather(d0_ref, d1_ref, 0)

    for k in range(TOP_K):
        pltpu.make_async_copy(ybuf_ref.at[slot, k], ybuf_ref.at[slot, k], sem.at[slot]).wait()

    @pl.when(i + 1 < pl.num_programs(0))
    def _prefetch():
        gather(d0n_ref, d1n_ref, 1 - slot)

    wpad = jnp.concatenate([w_ref[...], jnp.zeros((LANES - TOP_K, tm), F32)], axis=0)
    wt = wpad.T
    w0 = wt[:, 0:1]
    w1 = wt[:, 1:2]
    x1 = x1_ref[...]
    y0 = _load_token_tiles(ybuf_ref.at[slot, 0], 0, tm)
    y1 = _load_token_tiles(ybuf_ref.at[slot, 1], 0, tm)
    lo = [_unpack_lo(a) * w0 + _unpack_lo(b) * w1 for a, b in zip(y0, y1)]
    hi = [_unpack_hi(a) * w0 + _unpack_hi(b) * w1 for a, b in zip(y0, y1)]
    moe = jnp.concatenate(lo + hi, axis=1)
    o_ref[...] = _layer_norm_rows(DEEPNORM_ALPHA * x1 + moe, lng_ref[...], lnb_ref[...])


def _combine(x1, dest_tile, wts, lng, lnb, yb):
    t, d = x1.shape
    tm = TM_COMB
    nstep = t // tm
    d0 = dest_tile[0].reshape(nstep, 1, tm)
    d1 = dest_tile[1].reshape(nstep, 1, tm)
    smem_blk = pl.BlockSpec((1, 1, tm), lambda i: (i, 0, 0), memory_space=pltpu.SMEM)
    smem_next = pl.BlockSpec((1, 1, tm), lambda i: (jnp.minimum(i + 1, nstep - 1), 0, 0),
                             memory_space=pltpu.SMEM)
    const = lambda i: (0, 0)
    return pl.pallas_call(
        _combine_kernel,
        grid=(nstep,),
        in_specs=[smem_blk, smem_blk, smem_next, smem_next,
                  pl.BlockSpec((tm, d), lambda i: (i, 0)),
                  pl.BlockSpec((TOP_K, tm), lambda i: (0, i)),
                  pl.BlockSpec((1, d), const),
                  pl.BlockSpec((1, d), const),
                  pl.BlockSpec(memory_space=pl.ANY)],
        out_specs=pl.BlockSpec((tm, d), lambda i: (i, 0)),
        out_shape=jax.ShapeDtypeStruct((t, d), F32),
        scratch_shapes=[pltpu.VMEM((2, TOP_K, tm * PACK_SUB, LANES), U32), pltpu.SemaphoreType.DMA((2,))],
        compiler_params=_cparams(("arbitrary",)),
        name="combine",
    )(d0, d1, d0, d1, x1, wts, lng, lnb, yb)


def _rotary_tables(seq):
    inv = 1.0 / (ROPE_BASE ** (jnp.arange(0, RET_HEAD_DIM, 2, dtype=F32) / RET_HEAD_DIM))
    ang = jnp.arange(seq, dtype=F32)[:, None] * inv[None, :]
    return jnp.cos(ang), jnp.sin(ang)


def _in_proj_weights(w):
    d = w.shape[0]
    half = RET_HEAD_DIM // 2
    qk = w[:, :2 * RET_WIDTH].astype(BF16).reshape(d, 2 * RET_HEADS, half, 2)
    qk = jnp.swapaxes(qk, 2, 3).reshape(d, 2 * RET_WIDTH)
    return jnp.concatenate([qk, w[:, 2 * RET_WIDTH:].astype(BF16)], axis=1)


def _routing_tables(counts, n_rows):
    padded = (counts + ROW_TILE - 1) // ROW_TILE * ROW_TILE
    pend = jnp.cumsum(padded)
    pstart = pend - padded
    ntile = padded // ROW_TILE
    nsup = (ntile + TILES_PER_SUPER - 1) // TILES_PER_SUPER
    send = jnp.cumsum(nsup)
    sstart = send - nsup
    nsuper = n_rows // SUPER + N_EXPERTS
    s = jnp.arange(nsuper, dtype=I32)
    total = send[-1]
    which = lambda q: jnp.clip(jnp.searchsorted(send, q, side='right'), 0, N_EXPERTS - 1).astype(I32)
    se = which(s)
    valid = s < total
    k = s - sstart[se]
    row = pstart[se] + k * SUPER
    nt = jnp.clip(ntile[se] - k * TILES_PER_SUPER, 0, TILES_PER_SUPER)
    idle = s - total
    tail_row = pend[-1] + idle * SUPER
    nz = jnp.clip((n_rows - tail_row) // ROW_TILE, 0, TILES_PER_SUPER)
    se = jnp.where(valid, se, which(total - 1))
    row = jnp.where(valid, row, jnp.minimum(tail_row, n_rows - ROW_TILE))
    nt = jnp.where(valid, nt, 0)
    nz = jnp.where(valid, 0, nz)
    seg_last = jnp.where(padded > 0, pend - ROW_TILE, -1)
    tail = pend[-1] + jnp.arange(N_EXPERTS, dtype=I32) * ROW_TILE
    zrow = jnp.concatenate([seg_last, jnp.where(tail < n_rows, tail, -1)])
    return (pstart.astype(I32), se.astype(I32), row.astype(I32), nt.astype(I32), nz.astype(I32),
            zrow.astype(I32))


def kernel(x, w_in, w_out, ssm_lambda_re, ssm_lambda_im, ssm_b_re, ssm_b_im, ssm_c_re, ssm_c_im,
           ssm_d, ssm_log_dt, w_glu, ln1_g, ln1_b, ln2_g, ln2_b, router_w, router_b,
           w_gate, w_up, w_down):
    batch, seq, d = x.shape
    t = batch * seq
    n_assign = t * TOP_K
    n_rows = n_assign + N_EXPERTS * ROW_TILE
    assert seq % TM_PROJ == 0 and seq % RET_CHUNK == 0 and seq % S5_TL == 0
    assert t % TM_POST == 0 and t % TM_COMB == 0 and t % TM_DISP == 0 and n_rows % SUPER == 0

    cos, sin = _rotary_tables(seq)
    expert_ids = jnp.arange(N_EXPERTS, dtype=I32)
    rw_t = router_w.astype(F32).T
    rhi = rw_t.astype(BF16)
    rlo = (rw_t - rhi.astype(F32)).astype(BF16)
    rcat = jnp.concatenate([rhi, rlo], axis=0)
    rb = router_b.astype(F32).reshape(N_EXPERTS, 1)

    x2 = x.reshape(t, d)
    for l in range(DEPTH):
        proj = _in_proj(x2, _in_proj_weights(w_in[l]), cos, sin, seq)
        ret = _retention(proj, batch, seq)
        tables = _s5_tables(ssm_lambda_re[l], ssm_lambda_im[l], ssm_b_re[l], ssm_b_im[l],
                            ssm_c_re[l], ssm_c_im[l], ssm_d[l], ssm_log_dt[l])
        y = _s5(proj, tables, batch, seq).reshape(t, SSM_WIDTH)
        x1, x1p, e, wts, rank, cnt = _post_mix(
            x2, ret, y, w_glu[l].astype(BF16), w_out[l].astype(BF16),
            ln1_g[l].reshape(1, d), ln1_b[l].reshape(1, d), rcat, rhi, rb)
        pstart, se, row, ntl, nzl, zrow = _routing_tables(cnt[:, 0], n_rows)
        dest = rank + jnp.sum(jnp.where(e[..., None] == expert_ids, pstart, 0), axis=-1)
        dest_tile = dest * PACK_SUB
        xb = _dispatch(x1p, dest_tile, zrow, n_rows)
        yb = _experts(xb, se, row, ntl, nzl, w_gate, w_up, w_down, l)
        x2 = _combine(x1, dest_tile, wts, ln2_g[l].reshape(1, d), ln2_b[l].reshape(1, d), yb)
    return x2.reshape(batch, seq, d)
```

```python
import functools
import math

import numpy as np
import jax
import jax.numpy as jnp
from jax import lax
from jax.experimental import pallas as pl
from jax.experimental.pallas import tpu as pltpu

F32 = jnp.float32
BF16 = jnp.bfloat16
I32 = jnp.int32

D_MODEL = 2048
DEPTH = 2
RET_WIDTH = 1024
SSM_WIDTH = 1024
RET_HEAD_DIM = 256
RET_HEADS = RET_WIDTH // RET_HEAD_DIM
ROPE_BASE = 10000.0
SSM_GROUP = 16
SSM_GROUPS = SSM_WIDTH // SSM_GROUP
SSM_STATE = 64
IN_PROJ_WIDTH = 4 * RET_WIDTH + SSM_WIDTH
N_EXPERTS = 32
N_EXPERT_GROUPS = 4
EXPERTS_PER_GROUP = N_EXPERTS // N_EXPERT_GROUPS
TOP_K = 2
D_EXPERT = D_MODEL // 2
LN_EPS = 1e-5
DEEPNORM_ALPHA = (2.0 * DEPTH) ** 0.25

LANES = 128
SUBLANES = 8
VMEM_LIMIT = 56 * 1024 * 1024

TM_PROJ = 1024
TN_PROJ = 1024
RET_CHUNK = 256
S5_TL = 128
S5_GROUP = 16
S5_LT = SSM_WIDTH // LANES
S5_NSTATE = (LANES // SSM_GROUP) * SSM_STATE
TM_POST = 512
TSUB_POST = 256
TM_COMB = 256
ROW_TILE = 256
TILES_PER_SUPER = 8
SUPER = ROW_TILE * TILES_PER_SUPER
DE_CHUNK = 256
N_DE_CHUNKS = D_EXPERT // DE_CHUNK
TM_DISP = 512
ISSUE_UNROLL = 8


def _sigmoid(x):
    return 1.0 / (1.0 + jnp.exp(-x))


def _cparams(sem, vmem=VMEM_LIMIT):
    return pltpu.CompilerParams(dimension_semantics=sem, vmem_limit_bytes=vmem)


PACK_WORDS = D_MODEL // 2
PACK_SUB = PACK_WORDS // LANES
U32 = jnp.uint32


def _pack_rows(x):
    bits = lambda v: lax.bitcast_convert_type(v.astype(BF16).astype(F32), U32)
    return (bits(x[:, :PACK_WORDS]) >> 16) | (bits(x[:, PACK_WORDS:]) & jnp.uint32(0xFFFF0000))


def _unpack_lo(w):
    return lax.bitcast_convert_type(w << 16, F32)


def _unpack_hi(w):
    return lax.bitcast_convert_type(w & jnp.uint32(0xFFFF0000), F32)


def _store_token_tiles(ref, row0, n, packed):
    for c in range(PACK_SUB):
        ref[pl.ds(row0 * PACK_SUB + c, n, stride=PACK_SUB), :] = packed[:, c * LANES:(c + 1) * LANES]


def _load_token_tiles(ref, row0, n):
    return [ref[pl.ds(row0 * PACK_SUB + c, n, stride=PACK_SUB), :] for c in range(PACK_SUB)]


def _in_proj_kernel(x_ref, w_ref, cos_ref, sin_ref, o_ref):
    j = pl.program_id(1)
    acc = jnp.dot(x_ref[...].astype(BF16), w_ref[...], preferred_element_type=F32)
    is_rot = j < 2
    is_gate = j == 3
    scale = jnp.where(j == 1, RET_HEAD_DIM ** -0.5, 1.0).astype(F32)
    c = jnp.where(is_rot, cos_ref[...] * scale, 1.0)
    s = jnp.where(is_rot, sin_ref[...] * scale, 0.0)
    half = RET_HEAD_DIM // 2
    for h in range(RET_HEADS):
        lo = h * RET_HEAD_DIM
        t1 = acc[:, lo:lo + half]
        t2 = acc[:, lo + half:lo + RET_HEAD_DIM]
        r1 = t1 * c - t2 * s
        r2 = t1 * s + t2 * c
        o_ref[:, lo:lo + half] = (r1 * jnp.where(is_gate, _sigmoid(r1), 1.0)).astype(BF16)
        o_ref[:, lo + half:lo + RET_HEAD_DIM] = (r2 * jnp.where(is_gate, _sigmoid(r2), 1.0)).astype(BF16)


def _prep_kernel(w_ref, p_ref, o_ref):
    o_ref[...] = jnp.dot(w_ref[...].astype(BF16), p_ref[...], preferred_element_type=F32).astype(BF16)


def _prep_in_proj_weights(w_in):
    depth, d, n = w_in.shape
    hd = RET_HEAD_DIM
    half = hd // 2
    perm = np.zeros((2, hd, hd), np.float32)
    for i in range(half):
        perm[0, 2 * i, i] = 1.0
        perm[0, 2 * i + 1, half + i] = 1.0
    perm[1] = np.eye(hd, dtype=np.float32)
    n_qk_blocks = 2 * RET_WIDTH // hd
    return pl.pallas_call(
        _prep_kernel,
        grid=(depth, n // hd),
        in_specs=[
            pl.BlockSpec((None, d, hd), lambda l, c: (l, 0, c)),
            pl.BlockSpec((None, hd, hd), lambda l, c: (jnp.where(c < n_qk_blocks, 0, 1), 0, 0)),
        ],
        out_specs=pl.BlockSpec((None, d, hd), lambda l, c: (l, 0, c)),
        out_shape=jax.ShapeDtypeStruct((depth, d, n), BF16),
        compiler_params=_cparams(("arbitrary", "arbitrary")),
        name="prep_w_in",
    )(w_in, jnp.asarray(perm, BF16))


def _in_proj(x2, w_bf, layer, cos, sin, seq):
    t, d = x2.shape
    n = w_bf.shape[2]
    tiles_per_seq = seq // TM_PROJ
    return pl.pallas_call(
        _in_proj_kernel,
        grid=(t // TM_PROJ, n // TN_PROJ),
        in_specs=[
            pl.BlockSpec((TM_PROJ, d), lambda i, j: (i, 0)),
            pl.BlockSpec((None, d, TN_PROJ), lambda i, j: (layer, 0, j)),
            pl.BlockSpec((TM_PROJ, LANES), lambda i, j: (i % tiles_per_seq, 0)),
            pl.BlockSpec((TM_PROJ, LANES), lambda i, j: (i % tiles_per_seq, 0)),
        ],
        out_specs=pl.BlockSpec((TM_PROJ, TN_PROJ), lambda i, j: (i, j)),
        out_shape=jax.ShapeDtypeStruct((t, n), BF16),
        compiler_params=_cparams(("arbitrary", "arbitrary")),
        name="in_proj",
    )(x2, w_bf, cos, sin)


def _ret_kernel(q_ref, k_ref, v_ref, g_ref, mask_ref, qd_ref, kd_ref, o_ref, r_ref):
    n = pl.program_id(1)

    @pl.when(n == 0)
    def _init():
        r_ref[...] = jnp.zeros_like(r_ref)

    for h in range(RET_HEADS):
        cols = slice(h * RET_HEAD_DIM, (h + 1) * RET_HEAD_DIM)
        q = q_ref[:, cols]
        k = k_ref[:, cols]
        v = v_ref[:, cols]
        s = lax.dot_general(q, k, (((1,), (1,)), ((), ())), preferred_element_type=F32)
        s = s * mask_ref[h]
        inner = jnp.dot(s.astype(BF16), v, preferred_element_type=F32)
        qd = qd_ref[h]
        r_prev = r_ref[h]
        cross = jnp.dot((q.astype(F32) * qd).astype(BF16), r_prev.astype(BF16), preferred_element_type=F32)
        o = inner + cross
        kdec = (k.astype(F32) * kd_ref[h]).astype(BF16)
        kv = lax.dot_general(kdec, v, (((0,), (0,)), ((), ())), preferred_element_type=F32)
        r_ref[h] = r_prev * qd[RET_CHUNK - 1:RET_CHUNK, :] + kv
        mu = jnp.mean(o, axis=-1, keepdims=True)
        oc = o - mu
        var = jnp.mean(oc * oc, axis=-1, keepdims=True)
        o_ref[:, cols] = (g_ref[:, cols].astype(F32) * (oc * lax.rsqrt(var + LN_EPS))).astype(BF16)


def _retention_tables():
    c = RET_CHUNK
    log_gamma = np.log(1.0 - 2.0 ** (-5.0 - np.arange(RET_HEADS, dtype=np.float64)))
    idx = np.arange(c, dtype=np.float64)
    diff = idx[:, None] - idx[None, :]
    mask = np.where(diff >= 0, np.exp(log_gamma[:, None, None] * np.maximum(diff, 0.0)), 0.0)
    qd = np.exp(log_gamma[:, None] * (idx + 1.0)[None, :])
    kd = np.exp(log_gamma[:, None] * (c - 1.0 - idx)[None, :])
    bc = lambda a: np.broadcast_to(a[:, :, None], (RET_HEADS, c, RET_HEAD_DIM)).astype(np.float32)
    return mask.astype(np.float32), bc(qd), bc(kd)


def _retention(proj, batch, seq):
    t = proj.shape[0]
    nch = seq // RET_CHUNK
    mask, qd, kd = _retention_tables()
    blk = (RET_CHUNK, RET_WIDTH)
    seg = lambda c: pl.BlockSpec(blk, lambda b, n: (b * nch + n, c))
    whole = lambda a: pl.BlockSpec(a.shape, lambda b, n: (0, 0, 0))
    return pl.pallas_call(
        _ret_kernel,
        grid=(batch, nch),
        in_specs=[seg(0), seg(1), seg(2), seg(3), whole(mask), whole(qd), whole(kd)],
        out_specs=seg(0),
        out_shape=jax.ShapeDtypeStruct((t, RET_WIDTH), BF16),
        scratch_shapes=[pltpu.VMEM((RET_HEADS, RET_HEAD_DIM, RET_HEAD_DIM), F32)],
        compiler_params=_cparams(("arbitrary", "arbitrary")),
        name="retention",
    )(proj, proj, proj, proj, jnp.asarray(mask), jnp.asarray(qd), jnp.asarray(kd))


def _s5_kernel(u_ref, bm_ref, cm_ref, lam_ref, d_ref, y_ref, us_ref, ut_ref, bu_ref, hb_ref, yt_ref, st_ref):
    n = pl.program_id(1)
    tl = S5_TL
    nb = SUBLANES
    ns = S5_NSTATE

    @pl.when(n == 0)
    def _init():
        st_ref[...] = jnp.zeros_like(st_ref)

    for b in range(nb):
        us_ref[b * tl:(b + 1) * tl, :] = u_ref[b].astype(F32)

    for t in range(tl):
        ut_ref[t * nb:(t + 1) * nb, :] = us_ref[pl.ds(t, nb, stride=tl), :]

    ar = jnp.broadcast_to(lam_ref[0:1, :], (nb, ns))
    ai = jnp.broadcast_to(lam_ref[1:2, :], (nb, ns))
    sr = st_ref[0:nb, :]
    si = st_ref[nb:2 * nb, :]

    grows = S5_GROUP * nb

    def project_in(g):
        rows = slice(g * grows, (g + 1) * grows)
        bu_ref[rows, :] = jnp.dot(ut_ref[rows, :].astype(BF16), bm_ref[...], preferred_element_type=F32)

    project_in(0)
    for g in range(tl // S5_GROUP):
        if g + 1 < tl // S5_GROUP:
            project_in(g + 1)
        for tt in range(0, S5_GROUP, 2):
            r0 = g * grows + tt * nb
            pair_r = []
            pair_i = []
            for r in (r0, r0 + nb):
                br = bu_ref[r:r + nb, 0:ns]
                bi = bu_ref[r:r + nb, ns:2 * ns]
                sr, si = ar * sr - ai * si + br, ar * si + ai * sr + bi
                pair_r.append(sr)
                pair_i.append(si)
            hb_ref[r0:r0 + 2 * nb, 0:ns] = jnp.concatenate(pair_r, axis=0).astype(BF16)
            hb_ref[r0:r0 + 2 * nb, ns:2 * ns] = jnp.concatenate(pair_i, axis=0).astype(BF16)
        rows = slice(g * grows, (g + 1) * grows)
        yt_ref[rows, :] = (jnp.dot(hb_ref[rows, :], cm_ref[...], preferred_element_type=F32)
                           + ut_ref[rows, :] * d_ref[...])
    st_ref[0:nb, :] = sr
    st_ref[nb:2 * nb, :] = si
    for b in range(nb):
        y_ref[b] = yt_ref[pl.ds(b, tl, stride=nb), :].astype(BF16)


def _s5_tables(lam_re, lam_im, b_re, b_im, c_re, c_im, d, log_dt):
    lam = lax.complex(lam_re.astype(F32), lam_im.astype(F32))
    dt = jnp.exp(log_dt.astype(F32))[:, None]
    lam_bar = jnp.exp(lam * dt)
    b_bar = ((lam_bar - 1.0) / lam)[..., None] * lax.complex(b_re.astype(F32), b_im.astype(F32))
    gpt = LANES // SSM_GROUP
    eye = jnp.eye(gpt, dtype=F32)

    def bdiag_in(m):
        m = m.reshape(S5_LT, gpt, SSM_STATE, SSM_GROUP)
        return jnp.einsum('jgpi,gh->jgihp', m, eye).reshape(S5_LT, LANES, gpt * SSM_STATE)

    def bdiag_out(m):
        m = m.reshape(S5_LT, gpt, SSM_GROUP, SSM_STATE)
        return jnp.einsum('jgop,gh->jgpho', m, eye).reshape(S5_LT, gpt * SSM_STATE, LANES)

    bmat = jnp.concatenate([bdiag_in(jnp.real(b_bar)), bdiag_in(jnp.imag(b_bar))], axis=-1).astype(BF16)
    cmat = jnp.concatenate([bdiag_out(c_re.astype(F32)), -bdiag_out(c_im.astype(F32))], axis=1).astype(BF16)
    lam_t = jnp.stack([jnp.real(lam_bar).reshape(S5_LT, S5_NSTATE),
                       jnp.imag(lam_bar).reshape(S5_LT, S5_NSTATE)], axis=1)
    d_t = d.astype(F32).reshape(S5_LT, 1, LANES)
    return bmat, cmat, lam_t, d_t


def _s5(proj, tables, layer, batch, seq):
    bmat, cmat, lam_t, d_t = tables
    assert batch == SUBLANES
    proj3 = proj.reshape(batch, seq, IN_PROJ_WIDTH)
    ucol = (4 * RET_WIDTH) // LANES
    tl = S5_TL
    tile = lambda j, n: (layer, j, 0, 0)
    return pl.pallas_call(
        _s5_kernel,
        grid=(S5_LT, seq // tl),
        in_specs=[
            pl.BlockSpec((batch, tl, LANES), lambda j, n: (0, n, ucol + j)),
            pl.BlockSpec((None, None, LANES, 2 * S5_NSTATE), tile),
            pl.BlockSpec((None, None, 2 * S5_NSTATE, LANES), tile),
            pl.BlockSpec((None, None, 2, S5_NSTATE), tile),
            pl.BlockSpec((None, None, 1, LANES), tile),
        ],
        out_specs=pl.BlockSpec((batch, tl, LANES), lambda j, n: (0, n, j)),
        out_shape=jax.ShapeDtypeStruct((batch, seq, SSM_WIDTH), BF16),
        scratch_shapes=[
            pltpu.VMEM((batch * tl, LANES), F32),
            pltpu.VMEM((batch * tl, LANES), F32),
            pltpu.VMEM((batch * tl, 2 * S5_NSTATE), F32),
            pltpu.VMEM((batch * tl, 2 * S5_NSTATE), BF16),
            pltpu.VMEM((batch * tl, LANES), F32),
            pltpu.VMEM((2 * SUBLANES, S5_NSTATE), F32),
        ],
        compiler_params=_cparams(("arbitrary", "arbitrary")),
        name="s5",
    )(proj3, bmat, cmat, lam_t, d_t)


def _layer_norm_rows(r, g, b):
    mu = jnp.mean(r, axis=-1, keepdims=True)
    rc = r - mu
    var = jnp.mean(rc * rc, axis=-1, keepdims=True)
    return rc * lax.rsqrt(var + LN_EPS) * g + b


def _post_mix_kernel(x_ref, ret_ref, y_ref, wglu_ref, wout_ref, lng_ref, lnb_ref,
                     rcat_ref, rhi_ref, rb_ref, tri_ref,
                     x1_ref, x1p_ref, e_ref, w_ref, rank_ref, cnt_ref, carry_ref):
    i = pl.program_id(0)

    @pl.when(i == 0)
    def _init():
        carry_ref[...] = jnp.zeros_like(carry_ref)

    args = (x_ref, ret_ref, y_ref, wglu_ref, wout_ref, lng_ref, lnb_ref, rcat_ref, rhi_ref, rb_ref, tri_ref,
            x1_ref, x1p_ref, e_ref, w_ref, rank_ref, carry_ref)
    a = _post_mix_phases(0, *args)
    b = _post_mix_phases(1, *args)
    nph = len(a)
    a[0]()
    for k in range(1, nph):
        a[k]()
        b[k - 1]()
    b[nph - 1]()
    cnt_ref[...] = carry_ref[...].astype(I32)


def _post_mix_phases(sub, x_ref, ret_ref, y_ref, wglu_ref, wout_ref, lng_ref, lnb_ref,
                     rcat_ref, rhi_ref, rb_ref, tri_ref, x1_ref, x1p_ref, e_ref, w_ref, rank_ref, carry_ref):
    tm = TSUB_POST
    rows = slice(sub * tm, (sub + 1) * tm)
    st = {}

    def gelu():
        st['ya'] = jax.nn.gelu(y_ref[rows, :].astype(F32))

    def glu_matmul():
        st['z'] = jnp.dot(st['ya'].astype(BF16), wglu_ref[...], preferred_element_type=F32)

    def glu_gate():
        st['ssm'] = (st.pop('ya') * _sigmoid(st.pop('z'))).astype(BF16)

    def out_matmul():
        mixed = jnp.concatenate([ret_ref[rows, :], st.pop('ssm')], axis=1)
        st['h'] = jnp.dot(mixed, wout_ref[...], preferred_element_type=F32)

    def norm():
        x1 = _layer_norm_rows(DEEPNORM_ALPHA * x_ref[rows, :] + st.pop('h'), lng_ref[...], lnb_ref[...])
        x1_ref[rows, :] = x1
        _store_token_tiles(x1p_ref, sub * tm, tm, _pack_rows(x1))
        st['xh'] = x1.astype(BF16)
        st['xl'] = (x1 - st['xh'].astype(F32)).astype(BF16)

    def router_matmul():
        nt = (((1,), (1,)), ((), ()))
        l1 = lax.dot_general(rcat_ref[...], st.pop('xh'), nt, preferred_element_type=F32)
        l2 = lax.dot_general(rhi_ref[...], st.pop('xl'), nt, preferred_element_type=F32)
        st['logits'] = l1[0:N_EXPERTS] + l1[N_EXPERTS:] + l2 + rb_ref[...]

    def route():
        _route(st.pop('logits'), rows, tri_ref, e_ref, w_ref, rank_ref, carry_ref)

    return [gelu, glu_matmul, glu_gate, out_matmul, norm, router_matmul, route]


def _route(logits, rows, tri_ref, e_ref, w_ref, rank_ref, carry_ref):
    tm = TSUB_POST
    m = jnp.max(logits, axis=0, keepdims=True)
    ex = jnp.exp(logits - m)
    p = ex / jnp.sum(ex, axis=0, keepdims=True)

    eg = EXPERTS_PER_GROUP
    iota_g = lax.broadcasted_iota(I32, (eg, tm), 0)
    best = None
    for g in range(N_EXPERT_GROUPS):
        pg = p[g * eg:(g + 1) * eg]
        m1 = jnp.max(pg, axis=0, keepdims=True)
        i1 = jnp.min(jnp.where(pg == m1, iota_g, eg), axis=0, keepdims=True)
        pg2 = jnp.where(iota_g == i1, -1.0, pg)
        m2 = jnp.max(pg2, axis=0, keepdims=True)
        i2 = jnp.min(jnp.where(pg2 == m2, iota_g, eg), axis=0, keepdims=True)
        sg = m1 + m2
        if best is None:
            best = (sg, m1, m2, i1, i2)
        else:
            better = sg > best[0]
            cand = (sg, m1, m2, i1 + g * eg, i2 + g * eg)
            best = tuple(jnp.where(better, c, o) for c, o in zip(cand, best))
    _, v1, v2, e1, e2 = best
    tot = v1 + v2
    e_ref[0:1, rows] = e1
    e_ref[1:2, rows] = e2
    w_ref[0:1, rows] = v1 / tot
    w_ref[1:2, rows] = v2 / tot

    iota_e = lax.broadcasted_iota(I32, (N_EXPERTS, tm), 0)
    oh1 = iota_e == e1
    oh2 = iota_e == e2
    oh = jnp.where(oh1, 1.0, jnp.where(oh2, 1.0, 0.0))
    before = jnp.dot(oh.astype(BF16), tri_ref[...], preferred_element_type=F32) + carry_ref[:, 0:1]
    rank_ref[0:1, rows] = jnp.sum(jnp.where(oh1, before, 0.0), axis=0, keepdims=True).astype(I32)
    rank_ref[1:2, rows] = jnp.sum(jnp.where(oh2, before, 0.0), axis=0, keepdims=True).astype(I32)
    carry_ref[...] = carry_ref[...] + jnp.sum(oh, axis=1, keepdims=True)


def _post_mix(x2, ret, y, wglu_bf, wout_bf, lng, lnb, layer, rcat, rhi, rb):
    t, d = x2.shape
    tm = TM_POST
    ts = TSUB_POST
    tri = jnp.asarray(np.triu(np.ones((ts, ts), np.float32), 1), BF16)
    const = lambda i: (0, 0)
    lyr = lambda i: (layer, 0, 0)
    tok = lambda i: (i, 0)
    lane = lambda i: (0, i)
    return pl.pallas_call(
        _post_mix_kernel,
        grid=(t // tm,),
        in_specs=[
            pl.BlockSpec((tm, d), tok),
            pl.BlockSpec((tm, RET_WIDTH), tok),
            pl.BlockSpec((tm, SSM_WIDTH), tok),
            pl.BlockSpec((None, SSM_WIDTH, SSM_WIDTH), lyr),
            pl.BlockSpec((None, RET_WIDTH + SSM_WIDTH, d), lyr),
            pl.BlockSpec((None, 1, d), lyr),
            pl.BlockSpec((None, 1, d), lyr),
            pl.BlockSpec((2 * N_EXPERTS, d), const),
            pl.BlockSpec((N_EXPERTS, d), const),
            pl.BlockSpec((N_EXPERTS, 1), const),
            pl.BlockSpec((ts, ts), const),
        ],
        out_specs=[
            pl.BlockSpec((tm, d), tok),
            pl.BlockSpec((tm * PACK_SUB, LANES), tok),
            pl.BlockSpec((TOP_K, tm), lane),
            pl.BlockSpec((TOP_K, tm), lane),
            pl.BlockSpec((TOP_K, tm), lane),
            pl.BlockSpec((N_EXPERTS, LANES), const),
        ],
        out_shape=[
            jax.ShapeDtypeStruct((t, d), F32),
            jax.ShapeDtypeStruct((t * PACK_SUB, LANES), U32),
            jax.ShapeDtypeStruct((TOP_K, t), I32),
            jax.ShapeDtypeStruct((TOP_K, t), F32),
            jax.ShapeDtypeStruct((TOP_K, t), I32),
            jax.ShapeDtypeStruct((N_EXPERTS, LANES), I32),
        ],
        scratch_shapes=[pltpu.VMEM((N_EXPERTS, LANES), F32)],
        compiler_params=_cparams(("arbitrary",)),
        name="post_mix",
    )(x2, ret, y, wglu_bf, wout_bf, lng, lnb, rcat, rhi, rb, tri)


def _dispatch_kernel(zrow_ref, d0_ref, d1_ref, x_ref, xb_hbm, zbuf_ref, sem, zsem):
    tm = TM_DISP

    @pl.when(pl.program_id(0) == 0)
    def _zero_fill():
        zbuf_ref[...] = jnp.zeros_like(zbuf_ref)

        def zcopy(k):
            n = ROW_TILE * PACK_SUB
            rows = pl.ds(pl.multiple_of(zrow_ref[k] * PACK_SUB, n), n)
            return pltpu.make_async_copy(zbuf_ref, xb_hbm.at[rows], zsem.at[0])

        def start(k, c):
            @pl.when(zrow_ref[k] >= 0)
            def _():
                zcopy(k).start()
            return c

        def wait(k, c):
            @pl.when(zrow_ref[k] >= 0)
            def _():
                zcopy(k).wait()
            return c

        lax.fori_loop(0, 2 * N_EXPERTS, start, 0)
        lax.fori_loop(0, 2 * N_EXPERTS, wait, 0)

    def issue(tb, c):
        for u in range(ISSUE_UNROLL):
            t = tb * ISSUE_UNROLL + u
            src = x_ref.at[pl.ds(t * PACK_SUB, PACK_SUB)]
            tile = lambda r: pl.ds(pl.multiple_of(r, PACK_SUB), PACK_SUB)
            pltpu.make_async_copy(src, xb_hbm.at[tile(d0_ref[0, 0, t])], sem.at[0]).start(priority=0)
            pltpu.make_async_copy(src, xb_hbm.at[tile(d1_ref[0, 0, t])], sem.at[0]).start(priority=1)
        return c

    lax.fori_loop(0, tm // ISSUE_UNROLL, issue, 0)
    for _ in range(TOP_K):
        pltpu.make_async_copy(x_ref, xb_hbm.at[pl.ds(0, tm * PACK_SUB)], sem.at[0]).wait()


def _dispatch(x1p, dest_tile, zrow, n_rows):
    t = x1p.shape[0] // PACK_SUB
    tm = TM_DISP
    nstep = t // tm
    d0 = dest_tile[0].reshape(nstep, 1, tm)
    d1 = dest_tile[1].reshape(nstep, 1, tm)
    smem_blk = pl.BlockSpec((1, 1, tm), lambda i, z: (i, 0, 0), memory_space=pltpu.SMEM)
    grid_spec = pltpu.PrefetchScalarGridSpec(
        num_scalar_prefetch=1,
        grid=(nstep,),
        in_specs=[smem_blk, smem_blk, pl.BlockSpec((tm * PACK_SUB, LANES), lambda i, z: (i, 0))],
        out_specs=pl.BlockSpec(memory_space=pl.ANY),
        scratch_shapes=[pltpu.VMEM((ROW_TILE * PACK_SUB, LANES), U32),
                        pltpu.SemaphoreType.DMA((1,)), pltpu.SemaphoreType.DMA((1,))],
    )
    return pl.pallas_call(
        _dispatch_kernel,
        grid_spec=grid_spec,
        out_shape=jax.ShapeDtypeStruct((n_rows * PACK_SUB, LANES), U32),
        compiler_params=_cparams(("arbitrary",)),
        name="dispatch",
    )(zrow, d0, d1, x1p)


def _experts_kernel(se_ref, row_ref, nt_ref, nz_ref, xb_hbm, wg_ref, wu_ref, wd_ref, yb_hbm,
                    xs_ref, acc_ref, wgb_ref, wub_ref, wdb_ref, stg_in, stg_out, sem_in, sem_out):
    del se_ref
    s = pl.program_id(0)
    j = pl.program_id(1)
    nt = nt_ref[s]
    nz = nz_ref[s]
    row0 = row_ref[s]
    rt = ROW_TILE

    def rows(i):
        n = rt * PACK_SUB
        return pl.ds(pl.multiple_of((row0 + i * rt) * PACK_SUB, n), n)

    def in_copy(i, slot):
        return pltpu.make_async_copy(xb_hbm.at[rows(i)], stg_in.at[slot], sem_in.at[slot])

    def out_copy(i, slot):
        return pltpu.make_async_copy(stg_out.at[slot], yb_hbm.at[rows(i)], sem_out.at[slot])

    def partial_out(i, ntiles):
        r = pl.multiple_of(i * rt, rt)
        xi = xs_ref[pl.ds(r, ntiles * rt), :]
        g = jnp.dot(xi, wgb_ref[...], preferred_element_type=F32)
        u = jnp.dot(xi, wub_ref[...], preferred_element_type=F32)
        hj = (g * _sigmoid(g) * u).astype(BF16)
        return r, jnp.dot(hj, wdb_ref[...], preferred_element_type=F32)

    def load_tile(i):
        slot = i % 2

        @pl.when(i + 1 < nt)
        def _():
            in_copy(i + 1, 1 - slot).start()

        in_copy(i, slot).wait()
        xrows = pl.ds(pl.multiple_of(i * rt, rt), rt)
        for c, words in enumerate(_load_token_tiles(stg_in.at[slot], 0, rt)):
            xs_ref[xrows, c * LANES:(c + 1) * LANES] = _unpack_lo(words).astype(BF16)
            xs_ref[xrows, PACK_WORDS + c * LANES:PACK_WORDS + (c + 1) * LANES] = _unpack_hi(words).astype(BF16)

    def store_tile(i, vals):
        slot = i % 2

        @pl.when(i >= 2)
        def _():
            out_copy(i - 2, slot).wait()

        _store_token_tiles(stg_out.at[slot], 0, rt, _pack_rows(vals))
        out_copy(i, slot).start()

    def for_tiles(body):
        def pair(p, c):
            body(2 * p, 2)
            return c

        lax.fori_loop(0, nt // 2, pair, 0)

        @pl.when(nt % 2 == 1)
        def _():
            body(nt - 1, 1)

    @pl.when((nz > 0) & (j == 0))
    def _zero_tail():
        stg_out[0] = jnp.zeros(stg_out.shape[1:], U32)

        def start(i, c):
            out_copy(i, 0).start()
            return c

        def wait(i, c):
            out_copy(i, 0).wait()
            return c

        lax.fori_loop(0, nz, start, 0)
        lax.fori_loop(0, nz, wait, 0)

    @pl.when(nt > 0)
    def _work():
        wgb_ref[...] = wg_ref[...].astype(BF16)
        wub_ref[...] = wu_ref[...].astype(BF16)
        wdb_ref[...] = wd_ref[...].astype(BF16)

        @pl.when(j == 0)
        def _first():
            in_copy(0, 0).start()

            def body(i, ntiles):
                for k in range(ntiles):
                    load_tile(i + k)
                r, part = partial_out(i, ntiles)
                acc_ref[pl.ds(r, ntiles * rt), :] = part

            for_tiles(body)

        @pl.when((j > 0) & (j < N_DE_CHUNKS - 1))
        def _mid():
            def body(i, ntiles):
                r, part = partial_out(i, ntiles)
                acc_ref[pl.ds(r, ntiles * rt), :] += part

            for_tiles(body)

        @pl.when(j == N_DE_CHUNKS - 1)
        def _last():
            def body(i, ntiles):
                r, part = partial_out(i, ntiles)
                total = acc_ref[pl.ds(r, ntiles * rt), :] + part
                for k in range(ntiles):
                    store_tile(i + k, total[k * rt:(k + 1) * rt, :])

            for_tiles(body)

            @pl.when(nt >= 2)
            def _():
                out_copy(nt - 2, nt % 2).wait()

            out_copy(nt - 1, (nt - 1) % 2).wait()


def _experts(xb, se, row, ntl, nzl, w_gate, w_up, w_down, layer):
    p = xb.shape[0] // PACK_SUB
    d = D_MODEL
    nsuper = se.shape[0]
    last = N_DE_CHUNKS - 1

    def jj(s, j, nt_ref):
        return jnp.where(nt_ref[s] > 0, j, last)

    grid_spec = pltpu.PrefetchScalarGridSpec(
        num_scalar_prefetch=4,
        grid=(nsuper, N_DE_CHUNKS),
        in_specs=[
            pl.BlockSpec(memory_space=pl.ANY),
            pl.BlockSpec((None, None, d, DE_CHUNK),
                         lambda s, j, se_r, row_r, nt_r, nz_r: (layer, se_r[s], 0, jj(s, j, nt_r))),
            pl.BlockSpec((None, None, d, DE_CHUNK),
                         lambda s, j, se_r, row_r, nt_r, nz_r: (layer, se_r[s], 0, jj(s, j, nt_r))),
            pl.BlockSpec((None, None, DE_CHUNK, d),
                         lambda s, j, se_r, row_r, nt_r, nz_r: (layer, se_r[s], jj(s, j, nt_r), 0)),
        ],
        out_specs=pl.BlockSpec(memory_space=pl.ANY),
        scratch_shapes=[
            pltpu.VMEM((SUPER, d), BF16),
            pltpu.VMEM((SUPER, d), F32),
            pltpu.VMEM((d, DE_CHUNK), BF16),
            pltpu.VMEM((d, DE_CHUNK), BF16),
            pltpu.VMEM((DE_CHUNK, d), BF16),
            pltpu.VMEM((2, ROW_TILE * PACK_SUB, LANES), U32),
            pltpu.VMEM((2, ROW_TILE * PACK_SUB, LANES), U32),
            pltpu.SemaphoreType.DMA((2,)),
            pltpu.SemaphoreType.DMA((2,)),
        ],
    )
    return pl.pallas_call(
        _experts_kernel,
        grid_spec=grid_spec,
        out_shape=jax.ShapeDtypeStruct((p * PACK_SUB, LANES), U32),
        compiler_params=_cparams(("arbitrary", "arbitrary")),
        name="experts",
    )(se, row, ntl, nzl, xb, w_gate, w_up, w_down)


def _combine_kernel(d0_ref, d1_ref, d0n_ref, d1n_ref, x1_ref, w_ref, lng_ref, lnb_ref, yb_hbm,
                    o_ref, ybuf_ref, sem):
    tm = TM_COMB
    i = pl.program_id(0)
    slot = i % 2

    def gather(da_ref, db_ref, into):
        def issue(tb, c):
            for u in range(ISSUE_UNROLL):
                t = tb * ISSUE_UNROLL + u
                dst = pl.ds(pl.multiple_of(t * PACK_SUB, PACK_SUB), PACK_SUB)
                tile = lambda r: pl.ds(pl.multiple_of(r, PACK_SUB), PACK_SUB)
                pltpu.make_async_copy(yb_hbm.at[tile(da_ref[0, 0, t])], ybuf_ref.at[into, 0, dst],
                                      sem.at[into]).start(priority=0)
                pltpu.make_async_copy(yb_hbm.at[tile(db_ref[0, 0, t])], ybuf_ref.at[into, 1, dst],
                                      sem.at[into]).start(priority=1)
            return c

        lax.fori_loop(0, tm // ISSUE_UNROLL, issue, 0)

    @pl.when(i == 0)
    def _prime():
        gather(d0_ref, d1_ref, 0)

    for k in range(TOP_K):
        pltpu.make_async_copy(ybuf_ref.at[slot, k], ybuf_ref.at[slot, k], sem.at[slot]).wait()

    @pl.when(i + 1 < pl.num_programs(0))
    def _prefetch():
        gather(d0n_ref, d1n_ref, 1 - slot)

    wpad = jnp.concatenate([w_ref[...], jnp.zeros((LANES - TOP_K, tm), F32)], axis=0)
    wt = wpad.T
    w0 = wt[:, 0:1]
    w1 = wt[:, 1:2]
    x1 = x1_ref[...]
    y0 = _load_token_tiles(ybuf_ref.at[slot, 0], 0, tm)
    y1 = _load_token_tiles(ybuf_ref.at[slot, 1], 0, tm)
    lo = [_unpack_lo(a) * w0 + _unpack_lo(b) * w1 for a, b in zip(y0, y1)]
    hi = [_unpack_hi(a) * w0 + _unpack_hi(b) * w1 for a, b in zip(y0, y1)]
    moe = jnp.concatenate(lo + hi, axis=1)
    o_ref[...] = _layer_norm_rows(DEEPNORM_ALPHA * x1 + moe, lng_ref[...], lnb_ref[...])


def _combine(x1, dest_tile, wts, lng, lnb, layer, yb):
    t, d = x1.shape
    tm = TM_COMB
    nstep = t // tm
    d0 = dest_tile[0].reshape(nstep, 1, tm)
    d1 = dest_tile[1].reshape(nstep, 1, tm)
    smem_blk = pl.BlockSpec((1, 1, tm), lambda i: (i, 0, 0), memory_space=pltpu.SMEM)
    smem_next = pl.BlockSpec((1, 1, tm), lambda i: (jnp.minimum(i + 1, nstep - 1), 0, 0),
                             memory_space=pltpu.SMEM)
    lyr = lambda i: (layer, 0, 0)
    return pl.pallas_call(
        _combine_kernel,
        grid=(nstep,),
        in_specs=[smem_blk, smem_blk, smem_next, smem_next,
                  pl.BlockSpec((tm, d), lambda i: (i, 0)),
                  pl.BlockSpec((TOP_K, tm), lambda i: (0, i)),
                  pl.BlockSpec((None, 1, d), lyr),
                  pl.BlockSpec((None, 1, d), lyr),
                  pl.BlockSpec(memory_space=pl.ANY)],
        out_specs=pl.BlockSpec((tm, d), lambda i: (i, 0)),
        out_shape=jax.ShapeDtypeStruct((t, d), F32),
        scratch_shapes=[pltpu.VMEM((2, TOP_K, tm * PACK_SUB, LANES), U32), pltpu.SemaphoreType.DMA((2,))],
        compiler_params=_cparams(("arbitrary",)),
        name="combine",
    )(d0, d1, d0, d1, x1, wts, lng, lnb, yb)


def _rotary_tables(seq):
    inv = 1.0 / (ROPE_BASE ** (jnp.arange(0, RET_HEAD_DIM, 2, dtype=F32) / RET_HEAD_DIM))
    ang = jnp.arange(seq, dtype=F32)[:, None] * inv[None, :]
    return jnp.cos(ang), jnp.sin(ang)


def _routing_tables(counts, n_rows):
    ids = np.arange(N_EXPERTS)
    upto = jnp.asarray(ids[None, :] <= ids[:, None])
    csum = lambda v: jnp.sum(jnp.where(upto, v[None, :], 0), axis=1)
    padded = (counts + ROW_TILE - 1) // ROW_TILE * ROW_TILE
    pend = csum(padded)
    pstart = pend - padded
    ntile = padded // ROW_TILE
    nsup = (ntile + TILES_PER_SUPER - 1) // TILES_PER_SUPER
    send = csum(nsup)
    sstart = send - nsup
    nsuper = n_rows // SUPER + N_EXPERTS
    s = jnp.arange(nsuper, dtype=I32)
    total = send[-1]
    which = lambda q: jnp.minimum(jnp.sum((send[None, :] <= q[:, None]).astype(I32), axis=1), N_EXPERTS - 1)
    pick = lambda v, e: jnp.sum(jnp.where(e[:, None] == ids[None, :], v[None, :], 0), axis=1)
    se = which(s)
    valid = s < total
    k = s - pick(sstart, se)
    row = pick(pstart, se) + k * SUPER
    nt = jnp.clip(pick(ntile, se) - k * TILES_PER_SUPER, 0, TILES_PER_SUPER)
    idle = s - total
    tail_row = pend[-1] + idle * SUPER
    nz = jnp.clip((n_rows - tail_row) // ROW_TILE, 0, TILES_PER_SUPER)
    se = jnp.where(valid, se, which((total - 1)[None])[0])
    row = jnp.where(valid, row, jnp.minimum(tail_row, n_rows - ROW_TILE))
    nt = jnp.where(valid, nt, 0)
    nz = jnp.where(valid, 0, nz)
    seg_last = jnp.where(padded > 0, pend - ROW_TILE, -1)
    tail = pend[-1] + jnp.arange(N_EXPERTS, dtype=I32) * ROW_TILE
    zrow = jnp.concatenate([seg_last, jnp.where(tail < n_rows, tail, -1)])
    return (pstart.astype(I32), se.astype(I32), row.astype(I32), nt.astype(I32), nz.astype(I32),
            zrow.astype(I32))


def kernel(x, w_in, w_out, ssm_lambda_re, ssm_lambda_im, ssm_b_re, ssm_b_im, ssm_c_re, ssm_c_im,
           ssm_d, ssm_log_dt, w_glu, ln1_g, ln1_b, ln2_g, ln2_b, router_w, router_b,
           w_gate, w_up, w_down):
    batch, seq, d = x.shape
    t = batch * seq
    n_assign = t * TOP_K
    n_rows = n_assign + N_EXPERTS * ROW_TILE
    assert seq % TM_PROJ == 0 and seq % RET_CHUNK == 0 and seq % S5_TL == 0
    assert t % TM_POST == 0 and t % TM_COMB == 0 and t % TM_DISP == 0 and n_rows % SUPER == 0

    cos, sin = _rotary_tables(seq)
    expert_ids = jnp.arange(N_EXPERTS, dtype=I32)
    rw_t = router_w.astype(F32).T
    rhi = rw_t.astype(BF16)
    rlo = (rw_t - rhi.astype(F32)).astype(BF16)
    rcat = jnp.concatenate([rhi, rlo], axis=0)
    rb = router_b.astype(F32).reshape(N_EXPERTS, 1)

    w_in_bf = _prep_in_proj_weights(w_in)
    s5_tables = jax.vmap(_s5_tables)(ssm_lambda_re, ssm_lambda_im, ssm_b_re, ssm_b_im,
                                     ssm_c_re, ssm_c_im, ssm_d, ssm_log_dt)
    w_glu_bf = w_glu.astype(BF16)
    w_out_bf = w_out.astype(BF16)
    row3 = lambda p: p.astype(F32).reshape(DEPTH, 1, d)
    ln1_g, ln1_b, ln2_g, ln2_b = row3(ln1_g), row3(ln1_b), row3(ln2_g), row3(ln2_b)

    x2 = x.reshape(t, d)
    for l in range(DEPTH):
        proj = _in_proj(x2, w_in_bf, l, cos, sin, seq)
        ret = _retention(proj, batch, seq)
        y = _s5(proj, s5_tables, l, batch, seq).reshape(t, SSM_WIDTH)
        x1, x1p, e, wts, rank, cnt = _post_mix(
            x2, ret, y, w_glu_bf, w_out_bf, ln1_g, ln1_b, l, rcat, rhi, rb)
        pstart, se, row, ntl, nzl, zrow = _routing_tables(cnt[:, 0], n_rows)
        dest = rank + jnp.sum(jnp.where(e[..., None] == expert_ids, pstart, 0), axis=-1)
        dest_tile = dest * PACK_SUB
        xb = _dispatch(x1p, dest_tile, zrow, n_rows)
        yb = _experts(xb, se, row, ntl, nzl, w_gate, w_up, w_down, l)
        x2 = _combine(x1, dest_tile, wts, ln2_g, ln2_b, l, yb)
    return x2.reshape(batch, seq, d)
```

```python
import functools
import math

import numpy as np
import jax
import jax.numpy as jnp
from jax import lax
from jax.experimental import pallas as pl
from jax.experimental.pallas import tpu as pltpu

F32 = jnp.float32
BF16 = jnp.bfloat16
I32 = jnp.int32

D_MODEL = 2048
DEPTH = 2
RET_WIDTH = 1024
SSM_WIDTH = 1024
RET_HEAD_DIM = 256
RET_HEADS = RET_WIDTH // RET_HEAD_DIM
ROPE_BASE = 10000.0
SSM_GROUP = 16
SSM_GROUPS = SSM_WIDTH // SSM_GROUP
SSM_STATE = 64
IN_PROJ_WIDTH = 4 * RET_WIDTH + SSM_WIDTH
N_EXPERTS = 32
N_EXPERT_GROUPS = 4
EXPERTS_PER_GROUP = N_EXPERTS // N_EXPERT_GROUPS
TOP_K = 2
D_EXPERT = D_MODEL // 2
LN_EPS = 1e-5
DEEPNORM_ALPHA = (2.0 * DEPTH) ** 0.25

LANES = 128
SUBLANES = 8
VMEM_LIMIT = 56 * 1024 * 1024
EXPERTS_VMEM_LIMIT = 60 * 1024 * 1024

TM_PROJ = 1024
TN_PROJ = 1024
RET_CHUNK = 256
S5_TL = 128
S5_GROUP = 16
S5_LT = SSM_WIDTH // LANES
S5_NSTATE = (LANES // SSM_GROUP) * SSM_STATE
TM_POST = 512
TSUB_POST = 256
TM_COMB = 256
ROW_TILE = 256
TILES_PER_SUPER = 6
SUPER = ROW_TILE * TILES_PER_SUPER
DE_CHUNK = 512
N_DE_CHUNKS = D_EXPERT // DE_CHUNK
TM_DISP = 512
ISSUE_UNROLL = 8


def _sigmoid(x):
    return 1.0 / (1.0 + jnp.exp(-x))


def _cparams(sem, vmem=VMEM_LIMIT):
    return pltpu.CompilerParams(dimension_semantics=sem, vmem_limit_bytes=vmem)


PACK_WORDS = D_MODEL // 2
PACK_SUB = PACK_WORDS // LANES
U32 = jnp.uint32


def _pack_rows(x):
    bits = lambda v: lax.bitcast_convert_type(v.astype(BF16).astype(F32), U32)
    return (bits(x[:, :PACK_WORDS]) >> 16) | (bits(x[:, PACK_WORDS:]) & jnp.uint32(0xFFFF0000))


def _unpack_lo(w):
    return lax.bitcast_convert_type(w << 16, F32)


def _unpack_hi(w):
    return lax.bitcast_convert_type(w & jnp.uint32(0xFFFF0000), F32)


def _store_token_tiles(ref, row0, n, packed):
    for c in range(PACK_SUB):
        ref[pl.ds(row0 * PACK_SUB + c, n, stride=PACK_SUB), :] = packed[:, c * LANES:(c + 1) * LANES]


def _load_token_tiles(ref, row0, n):
    return [ref[pl.ds(row0 * PACK_SUB + c, n, stride=PACK_SUB), :] for c in range(PACK_SUB)]


def _in_proj_kernel(x_ref, w_ref, cos_ref, sin_ref, o_ref):
    j = pl.program_id(1)
    acc = jnp.dot(x_ref[...].astype(BF16), w_ref[...], preferred_element_type=F32)
    is_rot = j < 2
    is_gate = j == 3
    scale = jnp.where(j == 1, RET_HEAD_DIM ** -0.5, 1.0).astype(F32)
    c = jnp.where(is_rot, cos_ref[...] * scale, 1.0)
    s = jnp.where(is_rot, sin_ref[...] * scale, 0.0)
    half = RET_HEAD_DIM // 2
    for h in range(RET_HEADS):
        lo = h * RET_HEAD_DIM
        t1 = acc[:, lo:lo + half]
        t2 = acc[:, lo + half:lo + RET_HEAD_DIM]
        r1 = t1 * c - t2 * s
        r2 = t1 * s + t2 * c
        o_ref[:, lo:lo + half] = (r1 * jnp.where(is_gate, _sigmoid(r1), 1.0)).astype(BF16)
        o_ref[:, lo + half:lo + RET_HEAD_DIM] = (r2 * jnp.where(is_gate, _sigmoid(r2), 1.0)).astype(BF16)


def _prep_kernel(w_ref, p_ref, o_ref):
    o_ref[...] = jnp.dot(w_ref[...].astype(BF16), p_ref[...], preferred_element_type=F32).astype(BF16)


def _prep_in_proj_weights(w_in):
    depth, d, n = w_in.shape
    hd = RET_HEAD_DIM
    half = hd // 2
    perm = np.zeros((2, hd, hd), np.float32)
    for i in range(half):
        perm[0, 2 * i, i] = 1.0
        perm[0, 2 * i + 1, half + i] = 1.0
    perm[1] = np.eye(hd, dtype=np.float32)
    n_qk_blocks = 2 * RET_WIDTH // hd
    return pl.pallas_call(
        _prep_kernel,
        grid=(depth, n // hd),
        in_specs=[
            pl.BlockSpec((None, d, hd), lambda l, c: (l, 0, c)),
            pl.BlockSpec((None, hd, hd), lambda l, c: (jnp.where(c < n_qk_blocks, 0, 1), 0, 0)),
        ],
        out_specs=pl.BlockSpec((None, d, hd), lambda l, c: (l, 0, c)),
        out_shape=jax.ShapeDtypeStruct((depth, d, n), BF16),
        compiler_params=_cparams(("arbitrary", "arbitrary")),
        name="prep_w_in",
    )(w_in, jnp.asarray(perm, BF16))


def _in_proj(x2, w_bf, layer, cos, sin, seq):
    t, d = x2.shape
    n = w_bf.shape[2]
    tiles_per_seq = seq // TM_PROJ
    return pl.pallas_call(
        _in_proj_kernel,
        grid=(t // TM_PROJ, n // TN_PROJ),
        in_specs=[
            pl.BlockSpec((TM_PROJ, d), lambda i, j: (i, 0)),
            pl.BlockSpec((None, d, TN_PROJ), lambda i, j: (layer, 0, j)),
            pl.BlockSpec((TM_PROJ, LANES), lambda i, j: (i % tiles_per_seq, 0)),
            pl.BlockSpec((TM_PROJ, LANES), lambda i, j: (i % tiles_per_seq, 0)),
        ],
        out_specs=pl.BlockSpec((TM_PROJ, TN_PROJ), lambda i, j: (i, j)),
        out_shape=jax.ShapeDtypeStruct((t, n), BF16),
        compiler_params=_cparams(("arbitrary", "arbitrary")),
        name="in_proj",
    )(x2, w_bf, cos, sin)


def _ret_kernel(q_ref, k_ref, v_ref, g_ref, mask_ref, qd_ref, kd_ref, o_ref, r_ref):
    n = pl.program_id(1)

    @pl.when(n == 0)
    def _init():
        r_ref[...] = jnp.zeros_like(r_ref)

    for h in range(RET_HEADS):
        cols = slice(h * RET_HEAD_DIM, (h + 1) * RET_HEAD_DIM)
        q = q_ref[:, cols]
        k = k_ref[:, cols]
        v = v_ref[:, cols]
        s = lax.dot_general(q, k, (((1,), (1,)), ((), ())), preferred_element_type=F32)
        s = s * mask_ref[h]
        inner = jnp.dot(s.astype(BF16), v, preferred_element_type=F32)
        qd = qd_ref[h]
        r_prev = r_ref[h]
        cross = jnp.dot((q.astype(F32) * qd).astype(BF16), r_prev.astype(BF16), preferred_element_type=F32)
        o = inner + cross
        kdec = (k.astype(F32) * kd_ref[h]).astype(BF16)
        kv = lax.dot_general(kdec, v, (((0,), (0,)), ((), ())), preferred_element_type=F32)
        r_ref[h] = r_prev * qd[RET_CHUNK - 1:RET_CHUNK, :] + kv
        mu = jnp.mean(o, axis=-1, keepdims=True)
        oc = o - mu
        var = jnp.mean(oc * oc, axis=-1, keepdims=True)
        o_ref[:, cols] = (g_ref[:, cols].astype(F32) * (oc * lax.rsqrt(var + LN_EPS))).astype(BF16)


def _retention_tables():
    c = RET_CHUNK
    log_gamma = np.log(1.0 - 2.0 ** (-5.0 - np.arange(RET_HEADS, dtype=np.float64)))
    idx = np.arange(c, dtype=np.float64)
    diff = idx[:, None] - idx[None, :]
    mask = np.where(diff >= 0, np.exp(log_gamma[:, None, None] * np.maximum(diff, 0.0)), 0.0)
    qd = np.exp(log_gamma[:, None] * (idx + 1.0)[None, :])
    kd = np.exp(log_gamma[:, None] * (c - 1.0 - idx)[None, :])
    bc = lambda a: np.broadcast_to(a[:, :, None], (RET_HEADS, c, RET_HEAD_DIM)).astype(np.float32)
    return mask.astype(np.float32), bc(qd), bc(kd)


def _retention(proj, batch, seq):
    t = proj.shape[0]
    nch = seq // RET_CHUNK
    mask, qd, kd = _retention_tables()
    blk = (RET_CHUNK, RET_WIDTH)
    seg = lambda c: pl.BlockSpec(blk, lambda b, n: (b * nch + n, c))
    whole = lambda a: pl.BlockSpec(a.shape, lambda b, n: (0, 0, 0))
    return pl.pallas_call(
        _ret_kernel,
        grid=(batch, nch),
        in_specs=[seg(0), seg(1), seg(2), seg(3), whole(mask), whole(qd), whole(kd)],
        out_specs=seg(0),
        out_shape=jax.ShapeDtypeStruct((t, RET_WIDTH), BF16),
        scratch_shapes=[pltpu.VMEM((RET_HEADS, RET_HEAD_DIM, RET_HEAD_DIM), F32)],
        compiler_params=_cparams(("arbitrary", "arbitrary")),
        name="retention",
    )(proj, proj, proj, proj, jnp.asarray(mask), jnp.asarray(qd), jnp.asarray(kd))


def _s5_kernel(u_ref, bm_ref, cm_ref, lam_ref, d_ref, y_ref, us_ref, ut_ref, bu_ref, hb_ref, yt_ref, st_ref):
    n = pl.program_id(1)
    tl = S5_TL
    nb = SUBLANES
    ns = S5_NSTATE

    @pl.when(n == 0)
    def _init():
        st_ref[...] = jnp.zeros_like(st_ref)

    for b in range(nb):
        us_ref[b * tl:(b + 1) * tl, :] = u_ref[b].astype(F32)

    for t in range(tl):
        ut_ref[t * nb:(t + 1) * nb, :] = us_ref[pl.ds(t, nb, stride=tl), :]

    ar = jnp.broadcast_to(lam_ref[0:1, :], (nb, ns))
    ai = jnp.broadcast_to(lam_ref[1:2, :], (nb, ns))
    sr = st_ref[0:nb, :]
    si = st_ref[nb:2 * nb, :]

    grows = S5_GROUP * nb

    def project_in(g):
        rows = slice(g * grows, (g + 1) * grows)
        bu_ref[rows, :] = jnp.dot(ut_ref[rows, :].astype(BF16), bm_ref[...], preferred_element_type=F32)

    project_in(0)
    for g in range(tl // S5_GROUP):
        if g + 1 < tl // S5_GROUP:
            project_in(g + 1)
        for tt in range(0, S5_GROUP, 2):
            r0 = g * grows + tt * nb
            pair_r = []
            pair_i = []
            for r in (r0, r0 + nb):
                br = bu_ref[r:r + nb, 0:ns]
                bi = bu_ref[r:r + nb, ns:2 * ns]
                sr, si = ar * sr - ai * si + br, ar * si + ai * sr + bi
                pair_r.append(sr)
                pair_i.append(si)
            hb_ref[r0:r0 + 2 * nb, 0:ns] = jnp.concatenate(pair_r, axis=0).astype(BF16)
            hb_ref[r0:r0 + 2 * nb, ns:2 * ns] = jnp.concatenate(pair_i, axis=0).astype(BF16)
        rows = slice(g * grows, (g + 1) * grows)
        yt_ref[rows, :] = (jnp.dot(hb_ref[rows, :], cm_ref[...], preferred_element_type=F32)
                           + ut_ref[rows, :] * d_ref[...])
    st_ref[0:nb, :] = sr
    st_ref[nb:2 * nb, :] = si
    for b in range(nb):
        y_ref[b] = yt_ref[pl.ds(b, tl, stride=nb), :].astype(BF16)


def _s5_tables(lam_re, lam_im, b_re, b_im, c_re, c_im, d, log_dt):
    lam = lax.complex(lam_re.astype(F32), lam_im.astype(F32))
    dt = jnp.exp(log_dt.astype(F32))[:, None]
    lam_bar = jnp.exp(lam * dt)
    b_bar = ((lam_bar - 1.0) / lam)[..., None] * lax.complex(b_re.astype(F32), b_im.astype(F32))
    gpt = LANES // SSM_GROUP
    eye = jnp.eye(gpt, dtype=F32)

    def bdiag_in(m):
        m = m.reshape(S5_LT, gpt, SSM_STATE, SSM_GROUP)
        return jnp.einsum('jgpi,gh->jgihp', m, eye).reshape(S5_LT, LANES, gpt * SSM_STATE)

    def bdiag_out(m):
        m = m.reshape(S5_LT, gpt, SSM_GROUP, SSM_STATE)
        return jnp.einsum('jgop,gh->jgpho', m, eye).reshape(S5_LT, gpt * SSM_STATE, LANES)

    bmat = jnp.concatenate([bdiag_in(jnp.real(b_bar)), bdiag_in(jnp.imag(b_bar))], axis=-1).astype(BF16)
    cmat = jnp.concatenate([bdiag_out(c_re.astype(F32)), -bdiag_out(c_im.astype(F32))], axis=1).astype(BF16)
    lam_t = jnp.stack([jnp.real(lam_bar).reshape(S5_LT, S5_NSTATE),
                       jnp.imag(lam_bar).reshape(S5_LT, S5_NSTATE)], axis=1)
    d_t = d.astype(F32).reshape(S5_LT, 1, LANES)
    return bmat, cmat, lam_t, d_t


def _s5(proj, tables, layer, batch, seq):
    bmat, cmat, lam_t, d_t = tables
    assert batch == SUBLANES
    proj3 = proj.reshape(batch, seq, IN_PROJ_WIDTH)
    ucol = (4 * RET_WIDTH) // LANES
    tl = S5_TL
    tile = lambda j, n: (layer, j, 0, 0)
    return pl.pallas_call(
        _s5_kernel,
        grid=(S5_LT, seq // tl),
        in_specs=[
            pl.BlockSpec((batch, tl, LANES), lambda j, n: (0, n, ucol + j)),
            pl.BlockSpec((None, None, LANES, 2 * S5_NSTATE), tile),
            pl.BlockSpec((None, None, 2 * S5_NSTATE, LANES), tile),
            pl.BlockSpec((None, None, 2, S5_NSTATE), tile),
            pl.BlockSpec((None, None, 1, LANES), tile),
        ],
        out_specs=pl.BlockSpec((batch, tl, LANES), lambda j, n: (0, n, j)),
        out_shape=jax.ShapeDtypeStruct((batch, seq, SSM_WIDTH), BF16),
        scratch_shapes=[
            pltpu.VMEM((batch * tl, LANES), F32),
            pltpu.VMEM((batch * tl, LANES), F32),
            pltpu.VMEM((batch * tl, 2 * S5_NSTATE), F32),
            pltpu.VMEM((batch * tl, 2 * S5_NSTATE), BF16),
            pltpu.VMEM((batch * tl, LANES), F32),
            pltpu.VMEM((2 * SUBLANES, S5_NSTATE), F32),
        ],
        compiler_params=_cparams(("arbitrary", "arbitrary")),
        name="s5",
    )(proj3, bmat, cmat, lam_t, d_t)


def _layer_norm_rows(r, g, b):
    mu = jnp.mean(r, axis=-1, keepdims=True)
    rc = r - mu
    var = jnp.mean(rc * rc, axis=-1, keepdims=True)
    return rc * lax.rsqrt(var + LN_EPS) * g + b


def _post_mix_kernel(x_ref, ret_ref, y_ref, wglu_ref, wout_ref, lng_ref, lnb_ref,
                     rcat_ref, rhi_ref, rb_ref, tri_ref,
                     x1_ref, x1p_ref, e_ref, w_ref, rank_ref, cnt_ref, carry_ref):
    i = pl.program_id(0)

    @pl.when(i == 0)
    def _init():
        carry_ref[...] = jnp.zeros_like(carry_ref)

    args = (x_ref, ret_ref, y_ref, wglu_ref, wout_ref, lng_ref, lnb_ref, rcat_ref, rhi_ref, rb_ref, tri_ref,
            x1_ref, x1p_ref, e_ref, w_ref, rank_ref, carry_ref)
    a = _post_mix_phases(0, *args)
    b = _post_mix_phases(1, *args)
    nph = len(a)
    a[0]()
    for k in range(1, nph):
        a[k]()
        b[k - 1]()
    b[nph - 1]()
    cnt_ref[...] = carry_ref[...].astype(I32)


def _post_mix_phases(sub, x_ref, ret_ref, y_ref, wglu_ref, wout_ref, lng_ref, lnb_ref,
                     rcat_ref, rhi_ref, rb_ref, tri_ref, x1_ref, x1p_ref, e_ref, w_ref, rank_ref, carry_ref):
    tm = TSUB_POST
    rows = slice(sub * tm, (sub + 1) * tm)
    st = {}

    def gelu():
        st['ya'] = jax.nn.gelu(y_ref[rows, :].astype(F32))

    def glu_matmul():
        st['z'] = jnp.dot(st['ya'].astype(BF16), wglu_ref[...], preferred_element_type=F32)

    def glu_gate():
        st['ssm'] = (st.pop('ya') * _sigmoid(st.pop('z'))).astype(BF16)

    def out_matmul():
        mixed = jnp.concatenate([ret_ref[rows, :], st.pop('ssm')], axis=1)
        st['h'] = jnp.dot(mixed, wout_ref[...], preferred_element_type=F32)

    def norm():
        x1 = _layer_norm_rows(DEEPNORM_ALPHA * x_ref[rows, :] + st.pop('h'), lng_ref[...], lnb_ref[...])
        x1_ref[rows, :] = x1
        _store_token_tiles(x1p_ref, sub * tm, tm, _pack_rows(x1))
        st['xh'] = x1.astype(BF16)
        st['xl'] = (x1 - st['xh'].astype(F32)).astype(BF16)

    def router_matmul():
        nt = (((1,), (1,)), ((), ()))
        l1 = lax.dot_general(rcat_ref[...], st.pop('xh'), nt, preferred_element_type=F32)
        l2 = lax.dot_general(rhi_ref[...], st.pop('xl'), nt, preferred_element_type=F32)
        st['logits'] = l1[0:N_EXPERTS] + l1[N_EXPERTS:] + l2 + rb_ref[...]

    def route():
        _route(st.pop('logits'), rows, tri_ref, e_ref, w_ref, rank_ref, carry_ref)

    return [gelu, glu_matmul, glu_gate, out_matmul, norm, router_matmul, route]


def _route(logits, rows, tri_ref, e_ref, w_ref, rank_ref, carry_ref):
    tm = TSUB_POST
    m = jnp.max(logits, axis=0, keepdims=True)
    ex = jnp.exp(logits - m)
    p = ex / jnp.sum(ex, axis=0, keepdims=True)

    eg = EXPERTS_PER_GROUP
    iota_g = lax.broadcasted_iota(I32, (eg, tm), 0)
    best = None
    for g in range(N_EXPERT_GROUPS):
        pg = p[g * eg:(g + 1) * eg]
        m1 = jnp.max(pg, axis=0, keepdims=True)
        i1 = jnp.min(jnp.where(pg == m1, iota_g, eg), axis=0, keepdims=True)
        pg2 = jnp.where(iota_g == i1, -1.0, pg)
        m2 = jnp.max(pg2, axis=0, keepdims=True)
        i2 = jnp.min(jnp.where(pg2 == m2, iota_g, eg), axis=0, keepdims=True)
        sg = m1 + m2
        if best is None:
            best = (sg, m1, m2, i1, i2)
        else:
            better = sg > best[0]
            cand = (sg, m1, m2, i1 + g * eg, i2 + g * eg)
            best = tuple(jnp.where(better, c, o) for c, o in zip(cand, best))
    _, v1, v2, e1, e2 = best
    tot = v1 + v2
    e_ref[0:1, rows] = e1
    e_ref[1:2, rows] = e2
    w_ref[0:1, rows] = v1 / tot
    w_ref[1:2, rows] = v2 / tot

    iota_e = lax.broadcasted_iota(I32, (N_EXPERTS, tm), 0)
    oh1 = iota_e == e1
    oh2 = iota_e == e2
    oh = jnp.where(oh1, 1.0, jnp.where(oh2, 1.0, 0.0))
    before = jnp.dot(oh.astype(BF16), tri_ref[...], preferred_element_type=F32) + carry_ref[:, 0:1]
    rank_ref[0:1, rows] = jnp.sum(jnp.where(oh1, before, 0.0), axis=0, keepdims=True).astype(I32)
    rank_ref[1:2, rows] = jnp.sum(jnp.where(oh2, before, 0.0), axis=0, keepdims=True).astype(I32)
    carry_ref[...] = carry_ref[...] + jnp.sum(oh, axis=1, keepdims=True)


def _post_mix(x2, ret, y, wglu_bf, wout_bf, lng, lnb, layer, rcat, rhi, rb):
    t, d = x2.shape
    tm = TM_POST
    ts = TSUB_POST
    tri = jnp.asarray(np.triu(np.ones((ts, ts), np.float32), 1), BF16)
    const = lambda i: (0, 0)
    lyr = lambda i: (layer, 0, 0)
    tok = lambda i: (i, 0)
    lane = lambda i: (0, i)
    return pl.pallas_call(
        _post_mix_kernel,
        grid=(t // tm,),
        in_specs=[
            pl.BlockSpec((tm, d), tok),
            pl.BlockSpec((tm, RET_WIDTH), tok),
            pl.BlockSpec((tm, SSM_WIDTH), tok),
            pl.BlockSpec((None, SSM_WIDTH, SSM_WIDTH), lyr),
            pl.BlockSpec((None, RET_WIDTH + SSM_WIDTH, d), lyr),
            pl.BlockSpec((None, 1, d), lyr),
            pl.BlockSpec((None, 1, d), lyr),
            pl.BlockSpec((2 * N_EXPERTS, d), const),
            pl.BlockSpec((N_EXPERTS, d), const),
            pl.BlockSpec((N_EXPERTS, 1), const),
            pl.BlockSpec((ts, ts), const),
        ],
        out_specs=[
            pl.BlockSpec((tm, d), tok),
            pl.BlockSpec((tm * PACK_SUB, LANES), tok),
            pl.BlockSpec((TOP_K, tm), lane),
            pl.BlockSpec((TOP_K, tm), lane),
            pl.BlockSpec((TOP_K, tm), lane),
            pl.BlockSpec((N_EXPERTS, LANES), const),
        ],
        out_shape=[
            jax.ShapeDtypeStruct((t, d), F32),
            jax.ShapeDtypeStruct((t * PACK_SUB, LANES), U32),
            jax.ShapeDtypeStruct((TOP_K, t), I32),
            jax.ShapeDtypeStruct((TOP_K, t), F32),
            jax.ShapeDtypeStruct((TOP_K, t), I32),
            jax.ShapeDtypeStruct((N_EXPERTS, LANES), I32),
        ],
        scratch_shapes=[pltpu.VMEM((N_EXPERTS, LANES), F32)],
        compiler_params=_cparams(("arbitrary",)),
        name="post_mix",
    )(x2, ret, y, wglu_bf, wout_bf, lng, lnb, rcat, rhi, rb, tri)


def _dispatch_kernel(zrow_ref, d0_ref, d1_ref, x_ref, xb_hbm, zbuf_ref, sem, zsem):
    tm = TM_DISP

    @pl.when(pl.program_id(0) == 0)
    def _zero_fill():
        zbuf_ref[...] = jnp.zeros_like(zbuf_ref)

        def zcopy(k):
            n = ROW_TILE * PACK_SUB
            rows = pl.ds(pl.multiple_of(zrow_ref[k] * PACK_SUB, n), n)
            return pltpu.make_async_copy(zbuf_ref, xb_hbm.at[rows], zsem.at[0])

        def start(k, c):
            @pl.when(zrow_ref[k] >= 0)
            def _():
                zcopy(k).start()
            return c

        def wait(k, c):
            @pl.when(zrow_ref[k] >= 0)
            def _():
                zcopy(k).wait()
            return c

        lax.fori_loop(0, 2 * N_EXPERTS, start, 0)
        lax.fori_loop(0, 2 * N_EXPERTS, wait, 0)

    def issue(tb, c):
        for u in range(ISSUE_UNROLL):
            t = tb * ISSUE_UNROLL + u
            src = x_ref.at[pl.ds(t * PACK_SUB, PACK_SUB)]
            tile = lambda r: pl.ds(pl.multiple_of(r, PACK_SUB), PACK_SUB)
            pltpu.make_async_copy(src, xb_hbm.at[tile(d0_ref[0, 0, t])], sem.at[0]).start(priority=0)
            pltpu.make_async_copy(src, xb_hbm.at[tile(d1_ref[0, 0, t])], sem.at[0]).start(priority=1)
        return c

    lax.fori_loop(0, tm // ISSUE_UNROLL, issue, 0)
    for _ in range(TOP_K):
        pltpu.make_async_copy(x_ref, xb_hbm.at[pl.ds(0, tm * PACK_SUB)], sem.at[0]).wait()


def _dispatch(x1p, dest_tile, zrow, n_rows):
    t = x1p.shape[0] // PACK_SUB
    tm = TM_DISP
    nstep = t // tm
    d0 = dest_tile[0].reshape(nstep, 1, tm)
    d1 = dest_tile[1].reshape(nstep, 1, tm)
    smem_blk = pl.BlockSpec((1, 1, tm), lambda i, z: (i, 0, 0), memory_space=pltpu.SMEM)
    grid_spec = pltpu.PrefetchScalarGridSpec(
        num_scalar_prefetch=1,
        grid=(nstep,),
        in_specs=[smem_blk, smem_blk, pl.BlockSpec((tm * PACK_SUB, LANES), lambda i, z: (i, 0))],
        out_specs=pl.BlockSpec(memory_space=pl.ANY),
        scratch_shapes=[pltpu.VMEM((ROW_TILE * PACK_SUB, LANES), U32),
                        pltpu.SemaphoreType.DMA((1,)), pltpu.SemaphoreType.DMA((1,))],
    )
    return pl.pallas_call(
        _dispatch_kernel,
        grid_spec=grid_spec,
        out_shape=jax.ShapeDtypeStruct((n_rows * PACK_SUB, LANES), U32),
        compiler_params=_cparams(("arbitrary",)),
        name="dispatch",
    )(zrow, d0, d1, x1p)


def _experts_kernel(se_ref, row_ref, nt_ref, nz_ref, xb_hbm, wg_ref, wu_ref, wd_ref, yb_hbm,
                    xs_ref, acc_ref, wgb_ref, wub_ref, wdb_ref, stg_in, stg_out, sem_in, sem_out):
    del se_ref
    s = pl.program_id(0)
    j = pl.program_id(1)
    nt = nt_ref[s]
    nz = nz_ref[s]
    row0 = row_ref[s]
    rt = ROW_TILE

    nsuper = pl.num_programs(0)
    s_next = jnp.minimum(s + 1, nsuper - 1)
    prev_nt = jnp.where(s > 0, nt_ref[jnp.maximum(s - 1, 0)], 0)
    next_nt = jnp.where(s + 1 < nsuper, nt_ref[s_next], 0)

    def rows(i, base=None):
        n = rt * PACK_SUB
        base = row0 if base is None else base
        return pl.ds(pl.multiple_of((base + i * rt) * PACK_SUB, n), n)

    def in_copy(i, slot):
        return pltpu.make_async_copy(xb_hbm.at[rows(i)], stg_in.at[slot], sem_in.at[slot])

    def out_copy(i, slot):
        return pltpu.make_async_copy(stg_out.at[slot], yb_hbm.at[rows(i)], sem_out.at[slot])

    def partial_out(i, ntiles):
        r = pl.multiple_of(i * rt, rt)
        xi = xs_ref[pl.ds(r, ntiles * rt), :]
        g = jnp.dot(xi, wgb_ref[...], preferred_element_type=F32)
        u = jnp.dot(xi, wub_ref[...], preferred_element_type=F32)
        hj = (g * _sigmoid(g) * u).astype(BF16)
        return r, jnp.dot(hj, wdb_ref[...], preferred_element_type=F32)

    def load_tile(i):
        slot = i % 2

        @pl.when(i + 1 < nt)
        def _():
            in_copy(i + 1, 1 - slot).start()

        in_copy(i, slot).wait()
        xrows = pl.ds(pl.multiple_of(i * rt, rt), rt)
        for c, words in enumerate(_load_token_tiles(stg_in.at[slot], 0, rt)):
            xs_ref[xrows, c * LANES:(c + 1) * LANES] = _unpack_lo(words).astype(BF16)
            xs_ref[xrows, PACK_WORDS + c * LANES:PACK_WORDS + (c + 1) * LANES] = _unpack_hi(words).astype(BF16)

    def store_tile(i, vals):
        slot = i % 2

        @pl.when(i >= 2)
        def _():
            out_copy(i - 2, slot).wait()

        _store_token_tiles(stg_out.at[slot], 0, rt, _pack_rows(vals))
        out_copy(i, slot).start()

    def for_tiles(body):
        def pair(p, c):
            body(2 * p, 2)
            return c

        lax.fori_loop(0, nt // 2, pair, 0)

        @pl.when(nt % 2 == 1)
        def _():
            body(nt - 1, 1)

    @pl.when((nz > 0) & (j == 0))
    def _zero_tail():
        stg_out[0] = jnp.zeros(stg_out.shape[1:], U32)

        def start(i, c):
            out_copy(i, 0).start()
            return c

        def wait(i, c):
            out_copy(i, 0).wait()
            return c

        lax.fori_loop(0, nz, start, 0)
        lax.fori_loop(0, nz, wait, 0)

    @pl.when(nt > 0)
    def _work():
        wgb_ref[...] = wg_ref[...].astype(BF16)
        wub_ref[...] = wu_ref[...].astype(BF16)
        wdb_ref[...] = wd_ref[...].astype(BF16)

        @pl.when(j == 0)
        def _first():
            @pl.when(prev_nt == 0)
            def _():
                in_copy(0, 0).start()

            @pl.when(prev_nt >= 2)
            def _():
                out_copy(0, prev_nt % 2).wait()

            @pl.when(prev_nt >= 1)
            def _():
                out_copy(0, (prev_nt - 1) % 2).wait()

            def body(i, ntiles):
                for k in range(ntiles):
                    load_tile(i + k)
                r, part = partial_out(i, ntiles)
                acc_ref[pl.ds(r, ntiles * rt), :] = part

            for_tiles(body)

        @pl.when((j > 0) & (j < N_DE_CHUNKS - 1))
        def _mid():
            def body(i, ntiles):
                r, part = partial_out(i, ntiles)
                acc_ref[pl.ds(r, ntiles * rt), :] += part

            for_tiles(body)

        @pl.when(j == N_DE_CHUNKS - 1)
        def _last():
            @pl.when(next_nt > 0)
            def _():
                pltpu.make_async_copy(xb_hbm.at[rows(0, row_ref[s_next])], stg_in.at[0], sem_in.at[0]).start()

            def body(i, ntiles):
                r, part = partial_out(i, ntiles)
                total = acc_ref[pl.ds(r, ntiles * rt), :] + part
                for k in range(ntiles):
                    store_tile(i + k, total[k * rt:(k + 1) * rt, :])

            for_tiles(body)

            @pl.when(next_nt == 0)
            def _drain():
                @pl.when(nt >= 2)
                def _():
                    out_copy(nt - 2, nt % 2).wait()

                out_copy(nt - 1, (nt - 1) % 2).wait()


def _experts(xb, se, row, ntl, nzl, w_gate, w_up, w_down, layer):
    p = xb.shape[0] // PACK_SUB
    d = D_MODEL
    nsuper = se.shape[0]
    last = N_DE_CHUNKS - 1
    assert last >= 1

    def jj(s, j, nt_ref):
        return jnp.where(nt_ref[s] > 0, j, last)

    grid_spec = pltpu.PrefetchScalarGridSpec(
        num_scalar_prefetch=4,
        grid=(nsuper, N_DE_CHUNKS),
        in_specs=[
            pl.BlockSpec(memory_space=pl.ANY),
            pl.BlockSpec((None, None, d, DE_CHUNK),
                         lambda s, j, se_r, row_r, nt_r, nz_r: (layer, se_r[s], 0, jj(s, j, nt_r))),
            pl.BlockSpec((None, None, d, DE_CHUNK),
                         lambda s, j, se_r, row_r, nt_r, nz_r: (layer, se_r[s], 0, jj(s, j, nt_r))),
            pl.BlockSpec((None, None, DE_CHUNK, d),
                         lambda s, j, se_r, row_r, nt_r, nz_r: (layer, se_r[s], jj(s, j, nt_r), 0)),
        ],
        out_specs=pl.BlockSpec(memory_space=pl.ANY),
        scratch_shapes=[
            pltpu.VMEM((SUPER, d), BF16),
            pltpu.VMEM((SUPER, d), F32),
            pltpu.VMEM((d, DE_CHUNK), BF16),
            pltpu.VMEM((d, DE_CHUNK), BF16),
            pltpu.VMEM((DE_CHUNK, d), BF16),
            pltpu.VMEM((2, ROW_TILE * PACK_SUB, LANES), U32),
            pltpu.VMEM((2, ROW_TILE * PACK_SUB, LANES), U32),
            pltpu.SemaphoreType.DMA((2,)),
            pltpu.SemaphoreType.DMA((2,)),
        ],
    )
    return pl.pallas_call(
        _experts_kernel,
        grid_spec=grid_spec,
        out_shape=jax.ShapeDtypeStruct((p * PACK_SUB, LANES), U32),
        compiler_params=_cparams(("arbitrary", "arbitrary"), vmem=EXPERTS_VMEM_LIMIT),
        name="experts",
    )(se, row, ntl, nzl, xb, w_gate, w_up, w_down)


def _combine_kernel(d0_ref, d1_ref, d0n_ref, d1n_ref, x1_ref, w_ref, lng_ref, lnb_ref, yb_hbm,
                    o_ref, ybuf_ref, sem):
    tm = TM_COMB
    i = pl.program_id(0)
    slot = i % 2

    def gather(da_ref, db_ref, into):
        def issue(tb, c):
            for u in range(ISSUE_UNROLL):
                t = tb * ISSUE_UNROLL + u
                dst = pl.ds(pl.multiple_of(t * PACK_SUB, PACK_SUB), PACK_SUB)
                tile = lambda r: pl.ds(pl.multiple_of(r, PACK_SUB), PACK_SUB)
                pltpu.make_async_copy(yb_hbm.at[tile(da_ref[0, 0, t])], ybuf_ref.at[into, 0, dst],
                                      sem.at[into]).start(priority=0)
                pltpu.make_async_copy(yb_hbm.at[tile(db_ref[0, 0, t])], ybuf_ref.at[into, 1, dst],
                                      sem.at[into]).start(priority=1)
            return c

        lax.fori_loop(0, tm // ISSUE_UNROLL, issue, 0)

    @pl.when(i == 0)
    def _prime():
        gather(d0_ref, d1_ref, 0)

    for k in range(TOP_K):
        pltpu.make_async_copy(ybuf_ref.at[slot, k], ybuf_ref.at[slot, k], sem.at[slot]).wait()

    @pl.when(i + 1 < pl.num_programs(0))
    def _prefetch():
        gather(d0n_ref, d1n_ref, 1 - slot)

    wpad = jnp.concatenate([w_ref[...], jnp.zeros((LANES - TOP_K, tm), F32)], axis=0)
    wt = wpad.T
    w0 = wt[:, 0:1]
    w1 = wt[:, 1:2]
    x1 = x1_ref[...]
    y0 = _load_token_tiles(ybuf_ref.at[slot, 0], 0, tm)
    y1 = _load_token_tiles(ybuf_ref.at[slot, 1], 0, tm)
    lo = [_unpack_lo(a) * w0 + _unpack_lo(b) * w1 for a, b in zip(y0, y1)]
    hi = [_unpack_hi(a) * w0 + _unpack_hi(b) * w1 for a, b in zip(y0, y1)]
    moe = jnp.concatenate(lo + hi, axis=1)
    o_ref[...] = _layer_norm_rows(DEEPNORM_ALPHA * x1 + moe, lng_ref[...], lnb_ref[...])


def _combine(x1, dest_tile, wts, lng, lnb, layer, yb):
    t, d = x1.shape
    tm = TM_COMB
    nstep = t // tm
    d0 = dest_tile[0].reshape(nstep, 1, tm)
    d1 = dest_tile[1].reshape(nstep, 1, tm)
    smem_blk = pl.BlockSpec((1, 1, tm), lambda i: (i, 0, 0), memory_space=pltpu.SMEM)
    smem_next = pl.BlockSpec((1, 1, tm), lambda i: (jnp.minimum(i + 1, nstep - 1), 0, 0),
                             memory_space=pltpu.SMEM)
    lyr = lambda i: (layer, 0, 0)
    return pl.pallas_call(
        _combine_kernel,
        grid=(nstep,),
        in_specs=[smem_blk, smem_blk, smem_next, smem_next,
                  pl.BlockSpec((tm, d), lambda i: (i, 0)),
                  pl.BlockSpec((TOP_K, tm), lambda i: (0, i)),
                  pl.BlockSpec((None, 1, d), lyr),
                  pl.BlockSpec((None, 1, d), lyr),
                  pl.BlockSpec(memory_space=pl.ANY)],
        out_specs=pl.BlockSpec((tm, d), lambda i: (i, 0)),
        out_shape=jax.ShapeDtypeStruct((t, d), F32),
        scratch_shapes=[pltpu.VMEM((2, TOP_K, tm * PACK_SUB, LANES), U32), pltpu.SemaphoreType.DMA((2,))],
        compiler_params=_cparams(("arbitrary",)),
        name="combine",
    )(d0, d1, d0, d1, x1, wts, lng, lnb, yb)


def _rotary_tables(seq):
    inv = 1.0 / (ROPE_BASE ** (jnp.arange(0, RET_HEAD_DIM, 2, dtype=F32) / RET_HEAD_DIM))
    ang = jnp.arange(seq, dtype=F32)[:, None] * inv[None, :]
    return jnp.cos(ang), jnp.sin(ang)


def _routing_tables(counts, n_rows):
    ids = np.arange(N_EXPERTS)
    upto = jnp.asarray(ids[None, :] <= ids[:, None])
    csum = lambda v: jnp.sum(jnp.where(upto, v[None, :], 0), axis=1)
    padded = (counts + ROW_TILE - 1) // ROW_TILE * ROW_TILE
    pend = csum(padded)
    pstart = pend - padded
    ntile = padded // ROW_TILE
    nsup = (ntile + TILES_PER_SUPER - 1) // TILES_PER_SUPER
    send = csum(nsup)
    sstart = send - nsup
    nsuper = -(-n_rows // SUPER) + N_EXPERTS
    s = jnp.arange(nsuper, dtype=I32)
    total = send[-1]
    which = lambda q: jnp.minimum(jnp.sum((send[None, :] <= q[:, None]).astype(I32), axis=1), N_EXPERTS - 1)
    pick = lambda v, e: jnp.sum(jnp.where(e[:, None] == ids[None, :], v[None, :], 0), axis=1)
    se = which(s)
    valid = s < total
    k = s - pick(sstart, se)
    row = pick(pstart, se) + k * SUPER
    nt = jnp.clip(pick(ntile, se) - k * TILES_PER_SUPER, 0, TILES_PER_SUPER)
    idle = s - total
    tail_row = pend[-1] + idle * SUPER
    nz = jnp.clip((n_rows - tail_row) // ROW_TILE, 0, TILES_PER_SUPER)
    se = jnp.where(valid, se, which((total - 1)[None])[0])
    row = jnp.where(valid, row, jnp.minimum(tail_row, n_rows - ROW_TILE))
    nt = jnp.where(valid, nt, 0)
    nz = jnp.where(valid, 0, nz)
    seg_last = jnp.where(padded > 0, pend - ROW_TILE, -1)
    tail = pend[-1] + jnp.arange(N_EXPERTS, dtype=I32) * ROW_TILE
    zrow = jnp.concatenate([seg_last, jnp.where(tail < n_rows, tail, -1)])
    return (pstart.astype(I32), se.astype(I32), row.astype(I32), nt.astype(I32), nz.astype(I32),
            zrow.astype(I32))


def kernel(x, w_in, w_out, ssm_lambda_re, ssm_lambda_im, ssm_b_re, ssm_b_im, ssm_c_re, ssm_c_im,
           ssm_d, ssm_log_dt, w_glu, ln1_g, ln1_b, ln2_g, ln2_b, router_w, router_b,
           w_gate, w_up, w_down):
    batch, seq, d = x.shape
    t = batch * seq
    n_assign = t * TOP_K
    n_rows = n_assign + N_EXPERTS * ROW_TILE
    assert seq % TM_PROJ == 0 and seq % RET_CHUNK == 0 and seq % S5_TL == 0
    assert t % TM_POST == 0 and t % TM_COMB == 0 and t % TM_DISP == 0 and n_rows % ROW_TILE == 0

    cos, sin = _rotary_tables(seq)
    expert_ids = jnp.arange(N_EXPERTS, dtype=I32)
    rw_t = router_w.astype(F32).T
    rhi = rw_t.astype(BF16)
    rlo = (rw_t - rhi.astype(F32)).astype(BF16)
    rcat = jnp.concatenate([rhi, rlo], axis=0)
    rb = router_b.astype(F32).reshape(N_EXPERTS, 1)

    w_in_bf = _prep_in_proj_weights(w_in)
    s5_tables = jax.vmap(_s5_tables)(ssm_lambda_re, ssm_lambda_im, ssm_b_re, ssm_b_im,
                                     ssm_c_re, ssm_c_im, ssm_d, ssm_log_dt)
    w_glu_bf = w_glu.astype(BF16)
    w_out_bf = w_out.astype(BF16)
    row3 = lambda p: p.astype(F32).reshape(DEPTH, 1, d)
    ln1_g, ln1_b, ln2_g, ln2_b = row3(ln1_g), row3(ln1_b), row3(ln2_g), row3(ln2_b)

    x2 = x.reshape(t, d)
    for l in range(DEPTH):
        proj = _in_proj(x2, w_in_bf, l, cos, sin, seq)
        ret = _retention(proj, batch, seq)
        y = _s5(proj, s5_tables, l, batch, seq).reshape(t, SSM_WIDTH)
        x1, x1p, e, wts, rank, cnt = _post_mix(
            x2, ret, y, w_glu_bf, w_out_bf, ln1_g, ln1_b, l, rcat, rhi, rb)
        pstart, se, row, ntl, nzl, zrow = _routing_tables(cnt[:, 0], n_rows)
        dest = rank + jnp.sum(jnp.where(e[..., None] == expert_ids, pstart, 0), axis=-1)
        dest_tile = dest * PACK_SUB
        xb = _dispatch(x1p, dest_tile, zrow, n_rows)
        yb = _experts(xb, se, row, ntl, nzl, w_gate, w_up, w_down, l)
        x2 = _combine(x1, dest_tile, wts, ln2_g, ln2_b, l, yb)
    return x2.reshape(batch, seq, d)
```

```python
import functools
import math

import numpy as np
import jax
import jax.numpy as jnp
from jax import lax
from jax.experimental import pallas as pl
from jax.experimental.pallas import tpu as pltpu

F32 = jnp.float32
BF16 = jnp.bfloat16
I32 = jnp.int32

D_MODEL = 2048
DEPTH = 2
RET_WIDTH = 1024
SSM_WIDTH = 1024
RET_HEAD_DIM = 256
RET_HEADS = RET_WIDTH // RET_HEAD_DIM
ROPE_BASE = 10000.0
SSM_GROUP = 16
SSM_GROUPS = SSM_WIDTH // SSM_GROUP
SSM_STATE = 64
IN_PROJ_WIDTH = 4 * RET_WIDTH + SSM_WIDTH
N_EXPERTS = 32
N_EXPERT_GROUPS = 4
EXPERTS_PER_GROUP = N_EXPERTS // N_EXPERT_GROUPS
TOP_K = 2
D_EXPERT = D_MODEL // 2
LN_EPS = 1e-5
DEEPNORM_ALPHA = (2.0 * DEPTH) ** 0.25

LANES = 128
SUBLANES = 8
VMEM_LIMIT = 56 * 1024 * 1024
EXPERTS_VMEM_LIMIT = 60 * 1024 * 1024

TM_PROJ = 1024
TN_PROJ = 1024
RET_CHUNK = 256
S5_TL = 256
S5_GROUP = 16
S5_LT = SSM_WIDTH // LANES
S5_NSTATE = (LANES // SSM_GROUP) * SSM_STATE
TM_POST = 512
TSUB_POST = 256
TM_COMB = 512
ROW_TILE = 256
TILES_PER_SUPER = 6
SUPER = ROW_TILE * TILES_PER_SUPER
DE_CHUNK = 512
N_DE_CHUNKS = D_EXPERT // DE_CHUNK
TM_DISP = 1024
ISSUE_UNROLL = 8


def _sigmoid(x):
    return 1.0 / (1.0 + jnp.exp(-x))


def _cparams(sem, vmem=VMEM_LIMIT):
    return pltpu.CompilerParams(dimension_semantics=sem, vmem_limit_bytes=vmem)


PACK_WORDS = D_MODEL // 2
PACK_SUB = PACK_WORDS // LANES
U32 = jnp.uint32


def _pack_rows(x):
    bits = lambda v: lax.bitcast_convert_type(v.astype(BF16).astype(F32), U32)
    return (bits(x[:, :PACK_WORDS]) >> 16) | (bits(x[:, PACK_WORDS:]) & jnp.uint32(0xFFFF0000))


def _unpack_lo(w):
    return lax.bitcast_convert_type(w << 16, F32)


def _unpack_hi(w):
    return lax.bitcast_convert_type(w & jnp.uint32(0xFFFF0000), F32)


def _store_token_tiles(ref, row0, n, packed):
    for c in range(PACK_SUB):
        ref[pl.ds(row0 * PACK_SUB + c, n, stride=PACK_SUB), :] = packed[:, c * LANES:(c + 1) * LANES]


def _load_token_tiles(ref, row0, n):
    return [ref[pl.ds(row0 * PACK_SUB + c, n, stride=PACK_SUB), :] for c in range(PACK_SUB)]


def _in_proj_kernel(x_ref, w_ref, cos_ref, sin_ref, o_ref):
    j = pl.program_id(1)
    acc = jnp.dot(x_ref[...].astype(BF16), w_ref[...], preferred_element_type=F32)
    is_rot = j < 2
    is_gate = j == 3
    scale = jnp.where(j == 1, RET_HEAD_DIM ** -0.5, 1.0).astype(F32)
    c = jnp.where(is_rot, cos_ref[...] * scale, 1.0)
    s = jnp.where(is_rot, sin_ref[...] * scale, 0.0)
    half = RET_HEAD_DIM // 2
    for h in range(RET_HEADS):
        lo = h * RET_HEAD_DIM
        t1 = acc[:, lo:lo + half]
        t2 = acc[:, lo + half:lo + RET_HEAD_DIM]
        r1 = t1 * c - t2 * s
        r2 = t1 * s + t2 * c
        o_ref[:, lo:lo + half] = (r1 * jnp.where(is_gate, _sigmoid(r1), 1.0)).astype(BF16)
        o_ref[:, lo + half:lo + RET_HEAD_DIM] = (r2 * jnp.where(is_gate, _sigmoid(r2), 1.0)).astype(BF16)


def _prep_kernel(w_ref, p_ref, o_ref):
    o_ref[...] = jnp.dot(w_ref[...].astype(BF16), p_ref[...], preferred_element_type=F32).astype(BF16)


def _prep_in_proj_weights(w_in):
    depth, d, n = w_in.shape
    hd = RET_HEAD_DIM
    half = hd // 2
    perm = np.zeros((2, hd, hd), np.float32)
    for i in range(half):
        perm[0, 2 * i, i] = 1.0
        perm[0, 2 * i + 1, half + i] = 1.0
    perm[1] = np.eye(hd, dtype=np.float32)
    n_qk_blocks = 2 * RET_WIDTH // hd
    return pl.pallas_call(
        _prep_kernel,
        grid=(depth, n // hd),
        in_specs=[
            pl.BlockSpec((None, d, hd), lambda l, c: (l, 0, c)),
            pl.BlockSpec((None, hd, hd), lambda l, c: (jnp.where(c < n_qk_blocks, 0, 1), 0, 0)),
        ],
        out_specs=pl.BlockSpec((None, d, hd), lambda l, c: (l, 0, c)),
        out_shape=jax.ShapeDtypeStruct((depth, d, n), BF16),
        compiler_params=_cparams(("arbitrary", "arbitrary")),
        name="prep_w_in",
    )(w_in, jnp.asarray(perm, BF16))


def _in_proj(x2, w_bf, layer, cos, sin, seq):
    t, d = x2.shape
    n = w_bf.shape[2]
    tiles_per_seq = seq // TM_PROJ
    return pl.pallas_call(
        _in_proj_kernel,
        grid=(t // TM_PROJ, n // TN_PROJ),
        in_specs=[
            pl.BlockSpec((TM_PROJ, d), lambda i, j: (i, 0)),
            pl.BlockSpec((None, d, TN_PROJ), lambda i, j: (layer, 0, j)),
            pl.BlockSpec((TM_PROJ, LANES), lambda i, j: (i % tiles_per_seq, 0)),
            pl.BlockSpec((TM_PROJ, LANES), lambda i, j: (i % tiles_per_seq, 0)),
        ],
        out_specs=pl.BlockSpec((TM_PROJ, TN_PROJ), lambda i, j: (i, j)),
        out_shape=jax.ShapeDtypeStruct((t, n), BF16),
        compiler_params=_cparams(("arbitrary", "arbitrary")),
        name="in_proj",
    )(x2, w_bf, cos, sin)


def _ret_kernel(q_ref, k_ref, v_ref, g_ref, mask_ref, qd_ref, kd_ref, o_ref, r_ref):
    n = pl.program_id(1)

    @pl.when(n == 0)
    def _init():
        r_ref[...] = jnp.zeros_like(r_ref)

    for h in range(RET_HEADS):
        cols = slice(h * RET_HEAD_DIM, (h + 1) * RET_HEAD_DIM)
        q = q_ref[:, cols]
        k = k_ref[:, cols]
        v = v_ref[:, cols]
        s = lax.dot_general(q, k, (((1,), (1,)), ((), ())), preferred_element_type=F32)
        s = s * mask_ref[h]
        inner = jnp.dot(s.astype(BF16), v, preferred_element_type=F32)
        qd = qd_ref[h]
        r_prev = r_ref[h]
        cross = jnp.dot((q.astype(F32) * qd).astype(BF16), r_prev.astype(BF16), preferred_element_type=F32)
        o = inner + cross
        kdec = (k.astype(F32) * kd_ref[h]).astype(BF16)
        kv = lax.dot_general(kdec, v, (((0,), (0,)), ((), ())), preferred_element_type=F32)
        r_ref[h] = r_prev * qd[RET_CHUNK - 1:RET_CHUNK, :] + kv
        mu = jnp.mean(o, axis=-1, keepdims=True)
        oc = o - mu
        var = jnp.mean(oc * oc, axis=-1, keepdims=True)
        o_ref[:, cols] = (g_ref[:, cols].astype(F32) * (oc * lax.rsqrt(var + LN_EPS))).astype(BF16)


def _retention_tables():
    c = RET_CHUNK
    log_gamma = np.log(1.0 - 2.0 ** (-5.0 - np.arange(RET_HEADS, dtype=np.float64)))
    idx = np.arange(c, dtype=np.float64)
    diff = idx[:, None] - idx[None, :]
    mask = np.where(diff >= 0, np.exp(log_gamma[:, None, None] * np.maximum(diff, 0.0)), 0.0)
    qd = np.exp(log_gamma[:, None] * (idx + 1.0)[None, :])
    kd = np.exp(log_gamma[:, None] * (c - 1.0 - idx)[None, :])
    bc = lambda a: np.broadcast_to(a[:, :, None], (RET_HEADS, c, RET_HEAD_DIM)).astype(np.float32)
    return mask.astype(np.float32), bc(qd), bc(kd)


def _retention(proj, batch, seq):
    t = proj.shape[0]
    nch = seq // RET_CHUNK
    mask, qd, kd = _retention_tables()
    blk = (RET_CHUNK, RET_WIDTH)
    seg = lambda c: pl.BlockSpec(blk, lambda b, n: (b * nch + n, c))
    whole = lambda a: pl.BlockSpec(a.shape, lambda b, n: (0, 0, 0))
    return pl.pallas_call(
        _ret_kernel,
        grid=(batch, nch),
        in_specs=[seg(0), seg(1), seg(2), seg(3), whole(mask), whole(qd), whole(kd)],
        out_specs=seg(0),
        out_shape=jax.ShapeDtypeStruct((t, RET_WIDTH), BF16),
        scratch_shapes=[pltpu.VMEM((RET_HEADS, RET_HEAD_DIM, RET_HEAD_DIM), F32)],
        compiler_params=_cparams(("arbitrary", "arbitrary")),
        name="retention",
    )(proj, proj, proj, proj, jnp.asarray(mask), jnp.asarray(qd), jnp.asarray(kd))


def _s5_kernel(u_ref, bm_ref, cm_ref, lam_ref, d_ref, y_ref, us_ref, ut_ref, bu_ref, hb_ref, yt_ref, st_ref):
    n = pl.program_id(1)
    tl = S5_TL
    nb = SUBLANES
    ns = S5_NSTATE

    @pl.when(n == 0)
    def _init():
        st_ref[...] = jnp.zeros_like(st_ref)

    for b in range(nb):
        us_ref[b * tl:(b + 1) * tl, :] = u_ref[b].astype(F32)

    for t in range(tl):
        ut_ref[t * nb:(t + 1) * nb, :] = us_ref[pl.ds(t, nb, stride=tl), :]

    ar = jnp.broadcast_to(lam_ref[0:1, :], (nb, ns))
    ai = jnp.broadcast_to(lam_ref[1:2, :], (nb, ns))
    sr = st_ref[0:nb, :]
    si = st_ref[nb:2 * nb, :]

    grows = S5_GROUP * nb

    def project_in(g):
        rows = slice(g * grows, (g + 1) * grows)
        bu_ref[rows, :] = jnp.dot(ut_ref[rows, :].astype(BF16), bm_ref[...], preferred_element_type=F32)

    project_in(0)
    for g in range(tl // S5_GROUP):
        if g + 1 < tl // S5_GROUP:
            project_in(g + 1)
        for tt in range(0, S5_GROUP, 2):
            r0 = g * grows + tt * nb
            pair_r = []
            pair_i = []
            for r in (r0, r0 + nb):
                br = bu_ref[r:r + nb, 0:ns]
                bi = bu_ref[r:r + nb, ns:2 * ns]
                sr, si = ar * sr - ai * si + br, ar * si + ai * sr + bi
                pair_r.append(sr)
                pair_i.append(si)
            hb_ref[r0:r0 + 2 * nb, 0:ns] = jnp.concatenate(pair_r, axis=0).astype(BF16)
            hb_ref[r0:r0 + 2 * nb, ns:2 * ns] = jnp.concatenate(pair_i, axis=0).astype(BF16)
        rows = slice(g * grows, (g + 1) * grows)
        yt_ref[rows, :] = (jnp.dot(hb_ref[rows, :], cm_ref[...], preferred_element_type=F32)
                           + ut_ref[rows, :] * d_ref[...])
    st_ref[0:nb, :] = sr
    st_ref[nb:2 * nb, :] = si
    for b in range(nb):
        y_ref[b] = yt_ref[pl.ds(b, tl, stride=nb), :].astype(BF16)


def _s5_tables(lam_re, lam_im, b_re, b_im, c_re, c_im, d, log_dt):
    lam = lax.complex(lam_re.astype(F32), lam_im.astype(F32))
    dt = jnp.exp(log_dt.astype(F32))[:, None]
    lam_bar = jnp.exp(lam * dt)
    b_bar = ((lam_bar - 1.0) / lam)[..., None] * lax.complex(b_re.astype(F32), b_im.astype(F32))
    gpt = LANES // SSM_GROUP
    eye = jnp.eye(gpt, dtype=F32)

    def bdiag_in(m):
        m = m.reshape(S5_LT, gpt, SSM_STATE, SSM_GROUP)
        return jnp.einsum('jgpi,gh->jgihp', m, eye).reshape(S5_LT, LANES, gpt * SSM_STATE)

    def bdiag_out(m):
        m = m.reshape(S5_LT, gpt, SSM_GROUP, SSM_STATE)
        return jnp.einsum('jgop,gh->jgpho', m, eye).reshape(S5_LT, gpt * SSM_STATE, LANES)

    bmat = jnp.concatenate([bdiag_in(jnp.real(b_bar)), bdiag_in(jnp.imag(b_bar))], axis=-1).astype(BF16)
    cmat = jnp.concatenate([bdiag_out(c_re.astype(F32)), -bdiag_out(c_im.astype(F32))], axis=1).astype(BF16)
    lam_t = jnp.stack([jnp.real(lam_bar).reshape(S5_LT, S5_NSTATE),
                       jnp.imag(lam_bar).reshape(S5_LT, S5_NSTATE)], axis=1)
    d_t = d.astype(F32).reshape(S5_LT, 1, LANES)
    return bmat, cmat, lam_t, d_t


def _s5(proj, tables, layer, batch, seq):
    bmat, cmat, lam_t, d_t = tables
    assert batch == SUBLANES
    proj3 = proj.reshape(batch, seq, IN_PROJ_WIDTH)
    ucol = (4 * RET_WIDTH) // LANES
    tl = S5_TL
    tile = lambda j, n: (layer, j, 0, 0)
    return pl.pallas_call(
        _s5_kernel,
        grid=(S5_LT, seq // tl),
        in_specs=[
            pl.BlockSpec((batch, tl, LANES), lambda j, n: (0, n, ucol + j)),
            pl.BlockSpec((None, None, LANES, 2 * S5_NSTATE), tile),
            pl.BlockSpec((None, None, 2 * S5_NSTATE, LANES), tile),
            pl.BlockSpec((None, None, 2, S5_NSTATE), tile),
            pl.BlockSpec((None, None, 1, LANES), tile),
        ],
        out_specs=pl.BlockSpec((batch, tl, LANES), lambda j, n: (0, n, j)),
        out_shape=jax.ShapeDtypeStruct((batch, seq, SSM_WIDTH), BF16),
        scratch_shapes=[
            pltpu.VMEM((batch * tl, LANES), F32),
            pltpu.VMEM((batch * tl, LANES), F32),
            pltpu.VMEM((batch * tl, 2 * S5_NSTATE), F32),
            pltpu.VMEM((batch * tl, 2 * S5_NSTATE), BF16),
            pltpu.VMEM((batch * tl, LANES), F32),
            pltpu.VMEM((2 * SUBLANES, S5_NSTATE), F32),
        ],
        compiler_params=_cparams(("arbitrary", "arbitrary")),
        name="s5",
    )(proj3, bmat, cmat, lam_t, d_t)


def _layer_norm_rows(r, g, b):
    mu = jnp.mean(r, axis=-1, keepdims=True)
    rc = r - mu
    var = jnp.mean(rc * rc, axis=-1, keepdims=True)
    return rc * lax.rsqrt(var + LN_EPS) * g + b


def _post_mix_kernel(x_ref, ret_ref, y_ref, wglu_ref, wout_ref, lng_ref, lnb_ref,
                     rcat_ref, rhi_ref, rb_ref, tri_ref,
                     x1_ref, x1p_ref, e_ref, w_ref, rank_ref, cnt_ref, carry_ref):
    i = pl.program_id(0)

    @pl.when(i == 0)
    def _init():
        carry_ref[...] = jnp.zeros_like(carry_ref)

    args = (x_ref, ret_ref, y_ref, wglu_ref, wout_ref, lng_ref, lnb_ref, rcat_ref, rhi_ref, rb_ref, tri_ref,
            x1_ref, x1p_ref, e_ref, w_ref, rank_ref, carry_ref)
    a = _post_mix_phases(0, *args)
    b = _post_mix_phases(1, *args)
    nph = len(a)
    a[0]()
    for k in range(1, nph):
        a[k]()
        b[k - 1]()
    b[nph - 1]()
    cnt_ref[...] = carry_ref[...].astype(I32)


def _post_mix_phases(sub, x_ref, ret_ref, y_ref, wglu_ref, wout_ref, lng_ref, lnb_ref,
                     rcat_ref, rhi_ref, rb_ref, tri_ref, x1_ref, x1p_ref, e_ref, w_ref, rank_ref, carry_ref):
    tm = TSUB_POST
    rows = slice(sub * tm, (sub + 1) * tm)
    st = {}

    def gelu():
        st['ya'] = jax.nn.gelu(y_ref[rows, :].astype(F32))

    def glu_matmul():
        st['z'] = jnp.dot(st['ya'].astype(BF16), wglu_ref[...], preferred_element_type=F32)

    def glu_gate():
        st['ssm'] = (st.pop('ya') * _sigmoid(st.pop('z'))).astype(BF16)

    def out_matmul():
        mixed = jnp.concatenate([ret_ref[rows, :], st.pop('ssm')], axis=1)
        st['h'] = jnp.dot(mixed, wout_ref[...], preferred_element_type=F32)

    def norm():
        x1 = _layer_norm_rows(DEEPNORM_ALPHA * x_ref[rows, :] + st.pop('h'), lng_ref[...], lnb_ref[...])
        x1_ref[rows, :] = x1
        _store_token_tiles(x1p_ref, sub * tm, tm, _pack_rows(x1))
        st['xh'] = x1.astype(BF16)
        st['xl'] = (x1 - st['xh'].astype(F32)).astype(BF16)

    def router_matmul():
        nt = (((1,), (1,)), ((), ()))
        l1 = lax.dot_general(rcat_ref[...], st.pop('xh'), nt, preferred_element_type=F32)
        l2 = lax.dot_general(rhi_ref[...], st.pop('xl'), nt, preferred_element_type=F32)
        st['logits'] = l1[0:N_EXPERTS] + l1[N_EXPERTS:] + l2 + rb_ref[...]

    def route():
        _route(st.pop('logits'), rows, tri_ref, e_ref, w_ref, rank_ref, carry_ref)

    return [gelu, glu_matmul, glu_gate, out_matmul, norm, router_matmul, route]


def _route(logits, rows, tri_ref, e_ref, w_ref, rank_ref, carry_ref):
    tm = TSUB_POST
    m = jnp.max(logits, axis=0, keepdims=True)
    ex = jnp.exp(logits - m)
    p = ex / jnp.sum(ex, axis=0, keepdims=True)

    eg = EXPERTS_PER_GROUP
    iota_g = lax.broadcasted_iota(I32, (eg, tm), 0)
    best = None
    for g in range(N_EXPERT_GROUPS):
        pg = p[g * eg:(g + 1) * eg]
        m1 = jnp.max(pg, axis=0, keepdims=True)
        i1 = jnp.min(jnp.where(pg == m1, iota_g, eg), axis=0, keepdims=True)
        pg2 = jnp.where(iota_g == i1, -1.0, pg)
        m2 = jnp.max(pg2, axis=0, keepdims=True)
        i2 = jnp.min(jnp.where(pg2 == m2, iota_g, eg), axis=0, keepdims=True)
        sg = m1 + m2
        if best is None:
            best = (sg, m1, m2, i1, i2)
        else:
            better = sg > best[0]
            cand = (sg, m1, m2, i1 + g * eg, i2 + g * eg)
            best = tuple(jnp.where(better, c, o) for c, o in zip(cand, best))
    _, v1, v2, e1, e2 = best
    tot = v1 + v2
    e_ref[0:1, rows] = e1
    e_ref[1:2, rows] = e2
    w_ref[0:1, rows] = v1 / tot
    w_ref[1:2, rows] = v2 / tot

    iota_e = lax.broadcasted_iota(I32, (N_EXPERTS, tm), 0)
    oh1 = iota_e == e1
    oh2 = iota_e == e2
    oh = jnp.where(oh1, 1.0, jnp.where(oh2, 1.0, 0.0))
    before = jnp.dot(oh.astype(BF16), tri_ref[...], preferred_element_type=F32) + carry_ref[:, 0:1]
    rank_ref[0:1, rows] = jnp.sum(jnp.where(oh1, before, 0.0), axis=0, keepdims=True).astype(I32)
    rank_ref[1:2, rows] = jnp.sum(jnp.where(oh2, before, 0.0), axis=0, keepdims=True).astype(I32)
    carry_ref[...] = carry_ref[...] + jnp.sum(oh, axis=1, keepdims=True)


def _post_mix(x2, ret, y, wglu_bf, wout_bf, lng, lnb, layer, rcat, rhi, rb):
    t, d = x2.shape
    tm = TM_POST
    ts = TSUB_POST
    tri = jnp.asarray(np.triu(np.ones((ts, ts), np.float32), 1), BF16)
    const = lambda i: (0, 0)
    lyr = lambda i: (layer, 0, 0)
    tok = lambda i: (i, 0)
    lane = lambda i: (0, i)
    return pl.pallas_call(
        _post_mix_kernel,
        grid=(t // tm,),
        in_specs=[
            pl.BlockSpec((tm, d), tok),
            pl.BlockSpec((tm, RET_WIDTH), tok),
            pl.BlockSpec((tm, SSM_WIDTH), tok),
            pl.BlockSpec((None, SSM_WIDTH, SSM_WIDTH), lyr),
            pl.BlockSpec((None, RET_WIDTH + SSM_WIDTH, d), lyr),
            pl.BlockSpec((None, 1, d), lyr),
            pl.BlockSpec((None, 1, d), lyr),
            pl.BlockSpec((2 * N_EXPERTS, d), const),
            pl.BlockSpec((N_EXPERTS, d), const),
            pl.BlockSpec((N_EXPERTS, 1), const),
            pl.BlockSpec((ts, ts), const),
        ],
        out_specs=[
            pl.BlockSpec((tm, d), tok),
            pl.BlockSpec((tm * PACK_SUB, LANES), tok),
            pl.BlockSpec((TOP_K, tm), lane),
            pl.BlockSpec((TOP_K, tm), lane),
            pl.BlockSpec((TOP_K, tm), lane),
            pl.BlockSpec((N_EXPERTS, LANES), const),
        ],
        out_shape=[
            jax.ShapeDtypeStruct((t, d), F32),
            jax.ShapeDtypeStruct((t * PACK_SUB, LANES), U32),
            jax.ShapeDtypeStruct((TOP_K, t), I32),
            jax.ShapeDtypeStruct((TOP_K, t), F32),
            jax.ShapeDtypeStruct((TOP_K, t), I32),
            jax.ShapeDtypeStruct((N_EXPERTS, LANES), I32),
        ],
        scratch_shapes=[pltpu.VMEM((N_EXPERTS, LANES), F32)],
        compiler_params=_cparams(("arbitrary",)),
        name="post_mix",
    )(x2, ret, y, wglu_bf, wout_bf, lng, lnb, rcat, rhi, rb, tri)


def _dispatch_kernel(zrow_ref, d0_ref, d1_ref, x_ref, xb_hbm, zbuf_ref, sem, zsem):
    tm = TM_DISP

    @pl.when(pl.program_id(0) == 0)
    def _zero_fill():
        zbuf_ref[...] = jnp.zeros_like(zbuf_ref)

        def zcopy(k):
            n = ROW_TILE * PACK_SUB
            rows = pl.ds(pl.multiple_of(zrow_ref[k] * PACK_SUB, n), n)
            return pltpu.make_async_copy(zbuf_ref, xb_hbm.at[rows], zsem.at[0])

        def start(k, c):
            @pl.when(zrow_ref[k] >= 0)
            def _():
                zcopy(k).start()
            return c

        def wait(k, c):
            @pl.when(zrow_ref[k] >= 0)
            def _():
                zcopy(k).wait()
            return c

        lax.fori_loop(0, 2 * N_EXPERTS, start, 0)
        lax.fori_loop(0, 2 * N_EXPERTS, wait, 0)

    def issue(tb, c):
        for u in range(ISSUE_UNROLL):
            t = tb * ISSUE_UNROLL + u
            src = x_ref.at[pl.ds(t * PACK_SUB, PACK_SUB)]
            tile = lambda r: pl.ds(pl.multiple_of(r, PACK_SUB), PACK_SUB)
            pltpu.make_async_copy(src, xb_hbm.at[tile(d0_ref[0, 0, t])], sem.at[0]).start(priority=0)
            pltpu.make_async_copy(src, xb_hbm.at[tile(d1_ref[0, 0, t])], sem.at[0]).start(priority=1)
        return c

    lax.fori_loop(0, tm // ISSUE_UNROLL, issue, 0)
    for _ in range(TOP_K):
        pltpu.make_async_copy(x_ref, xb_hbm.at[pl.ds(0, tm * PACK_SUB)], sem.at[0]).wait()


def _dispatch(x1p, dest_tile, zrow, n_rows):
    t = x1p.shape[0] // PACK_SUB
    tm = TM_DISP
    nstep = t // tm
    d0 = dest_tile[0].reshape(nstep, 1, tm)
    d1 = dest_tile[1].reshape(nstep, 1, tm)
    smem_blk = pl.BlockSpec((1, 1, tm), lambda i, z: (i, 0, 0), memory_space=pltpu.SMEM)
    grid_spec = pltpu.PrefetchScalarGridSpec(
        num_scalar_prefetch=1,
        grid=(nstep,),
        in_specs=[smem_blk, smem_blk, pl.BlockSpec((tm * PACK_SUB, LANES), lambda i, z: (i, 0))],
        out_specs=pl.BlockSpec(memory_space=pl.ANY),
        scratch_shapes=[pltpu.VMEM((ROW_TILE * PACK_SUB, LANES), U32),
                        pltpu.SemaphoreType.DMA((1,)), pltpu.SemaphoreType.DMA((1,))],
    )
    return pl.pallas_call(
        _dispatch_kernel,
        grid_spec=grid_spec,
        out_shape=jax.ShapeDtypeStruct((n_rows * PACK_SUB, LANES), U32),
        compiler_params=_cparams(("arbitrary",)),
        name="dispatch",
    )(zrow, d0, d1, x1p)


def _experts_kernel(se_ref, row_ref, nt_ref, nz_ref, xb_hbm, wg_ref, wu_ref, wd_ref, yb_hbm,
                    xs_ref, acc_ref, wgb_ref, wub_ref, wdb_ref, stg_in, stg_out, sem_in, sem_out):
    del se_ref
    s = pl.program_id(0)
    j = pl.program_id(1)
    nt = nt_ref[s]
    nz = nz_ref[s]
    row0 = row_ref[s]
    rt = ROW_TILE

    nsuper = pl.num_programs(0)
    s_next = jnp.minimum(s + 1, nsuper - 1)
    prev_nt = jnp.where(s > 0, nt_ref[jnp.maximum(s - 1, 0)], 0)
    next_nt = jnp.where(s + 1 < nsuper, nt_ref[s_next], 0)

    def rows(i, base=None):
        n = rt * PACK_SUB
        base = row0 if base is None else base
        return pl.ds(pl.multiple_of((base + i * rt) * PACK_SUB, n), n)

    def in_copy(i, slot):
        return pltpu.make_async_copy(xb_hbm.at[rows(i)], stg_in.at[slot], sem_in.at[slot])

    def out_copy(i, slot):
        return pltpu.make_async_copy(stg_out.at[slot], yb_hbm.at[rows(i)], sem_out.at[slot])

    def partial_out(i, ntiles):
        r = pl.multiple_of(i * rt, rt)
        xi = xs_ref[pl.ds(r, ntiles * rt), :]
        g = jnp.dot(xi, wgb_ref[...], preferred_element_type=F32)
        u = jnp.dot(xi, wub_ref[...], preferred_element_type=F32)
        hj = (g * _sigmoid(g) * u).astype(BF16)
        return r, jnp.dot(hj, wdb_ref[...], preferred_element_type=F32)

    def load_tile(i):
        slot = i % 2

        @pl.when(i + 1 < nt)
        def _():
            in_copy(i + 1, 1 - slot).start()

        in_copy(i, slot).wait()
        xrows = pl.ds(pl.multiple_of(i * rt, rt), rt)
        for c, words in enumerate(_load_token_tiles(stg_in.at[slot], 0, rt)):
            xs_ref[xrows, c * LANES:(c + 1) * LANES] = _unpack_lo(words).astype(BF16)
            xs_ref[xrows, PACK_WORDS + c * LANES:PACK_WORDS + (c + 1) * LANES] = _unpack_hi(words).astype(BF16)

    def store_tile(i, vals):
        slot = i % 2

        @pl.when(i >= 2)
        def _():
            out_copy(i - 2, slot).wait()

        _store_token_tiles(stg_out.at[slot], 0, rt, _pack_rows(vals))
        out_copy(i, slot).start()

    def for_tiles(body):
        def pair(p, c):
            body(2 * p, 2)
            return c

        lax.fori_loop(0, nt // 2, pair, 0)

        @pl.when(nt % 2 == 1)
        def _():
            body(nt - 1, 1)

    @pl.when((nz > 0) & (j == 0))
    def _zero_tail():
        stg_out[0] = jnp.zeros(stg_out.shape[1:], U32)

        def start(i, c):
            out_copy(i, 0).start()
            return c

        def wait(i, c):
            out_copy(i, 0).wait()
            return c

        lax.fori_loop(0, nz, start, 0)
        lax.fori_loop(0, nz, wait, 0)

    @pl.when(nt > 0)
    def _work():
        wgb_ref[...] = wg_ref[...].astype(BF16)
        wub_ref[...] = wu_ref[...].astype(BF16)
        wdb_ref[...] = wd_ref[...].astype(BF16)

        @pl.when(j == 0)
        def _first():
            @pl.when(prev_nt == 0)
            def _():
                in_copy(0, 0).start()

            @pl.when(prev_nt >= 2)
            def _():
                out_copy(0, prev_nt % 2).wait()

            @pl.when(prev_nt >= 1)
            def _():
                out_copy(0, (prev_nt - 1) % 2).wait()

            def body(i, ntiles):
                for k in range(ntiles):
                    load_tile(i + k)
                r, part = partial_out(i, ntiles)
                acc_ref[pl.ds(r, ntiles * rt), :] = part

            for_tiles(body)

        @pl.when((j > 0) & (j < N_DE_CHUNKS - 1))
        def _mid():
            def body(i, ntiles):
                r, part = partial_out(i, ntiles)
                acc_ref[pl.ds(r, ntiles * rt), :] += part

            for_tiles(body)

        @pl.when(j == N_DE_CHUNKS - 1)
        def _last():
            @pl.when(next_nt > 0)
            def _():
                pltpu.make_async_copy(xb_hbm.at[rows(0, row_ref[s_next])], stg_in.at[0], sem_in.at[0]).start()

            def body(i, ntiles):
                r, part = partial_out(i, ntiles)
                total = acc_ref[pl.ds(r, ntiles * rt), :] + part
                for k in range(ntiles):
                    store_tile(i + k, total[k * rt:(k + 1) * rt, :])

            for_tiles(body)

            @pl.when(next_nt == 0)
            def _drain():
                @pl.when(nt >= 2)
                def _():
                    out_copy(nt - 2, nt % 2).wait()

                out_copy(nt - 1, (nt - 1) % 2).wait()


def _experts(xb, se, row, ntl, nzl, w_gate, w_up, w_down, layer):
    p = xb.shape[0] // PACK_SUB
    d = D_MODEL
    nsuper = se.shape[0]
    last = N_DE_CHUNKS - 1
    assert last >= 1

    def jj(s, j, nt_ref):
        return jnp.where(nt_ref[s] > 0, j, last)

    grid_spec = pltpu.PrefetchScalarGridSpec(
        num_scalar_prefetch=4,
        grid=(nsuper, N_DE_CHUNKS),
        in_specs=[
            pl.BlockSpec(memory_space=pl.ANY),
            pl.BlockSpec((None, None, d, DE_CHUNK),
                         lambda s, j, se_r, row_r, nt_r, nz_r: (layer, se_r[s], 0, jj(s, j, nt_r))),
            pl.BlockSpec((None, None, d, DE_CHUNK),
                         lambda s, j, se_r, row_r, nt_r, nz_r: (layer, se_r[s], 0, jj(s, j, nt_r))),
            pl.BlockSpec((None, None, DE_CHUNK, d),
                         lambda s, j, se_r, row_r, nt_r, nz_r: (layer, se_r[s], jj(s, j, nt_r), 0)),
        ],
        out_specs=pl.BlockSpec(memory_space=pl.ANY),
        scratch_shapes=[
            pltpu.VMEM((SUPER, d), BF16),
            pltpu.VMEM((SUPER, d), F32),
            pltpu.VMEM((d, DE_CHUNK), BF16),
            pltpu.VMEM((d, DE_CHUNK), BF16),
            pltpu.VMEM((DE_CHUNK, d), BF16),
            pltpu.VMEM((2, ROW_TILE * PACK_SUB, LANES), U32),
            pltpu.VMEM((2, ROW_TILE * PACK_SUB, LANES), U32),
            pltpu.SemaphoreType.DMA((2,)),
            pltpu.SemaphoreType.DMA((2,)),
        ],
    )
    return pl.pallas_call(
        _experts_kernel,
        grid_spec=grid_spec,
        out_shape=jax.ShapeDtypeStruct((p * PACK_SUB, LANES), U32),
        compiler_params=_cparams(("arbitrary", "arbitrary"), vmem=EXPERTS_VMEM_LIMIT),
        name="experts",
    )(se, row, ntl, nzl, xb, w_gate, w_up, w_down)


def _combine_kernel(d0_ref, d1_ref, d0n_ref, d1n_ref, x1_ref, w_ref, lng_ref, lnb_ref, yb_hbm,
                    o_ref, ybuf_ref, sem):
    tm = TM_COMB
    i = pl.program_id(0)
    slot = i % 2

    def gather(da_ref, db_ref, into):
        def issue(tb, c):
            for u in range(ISSUE_UNROLL):
                t = tb * ISSUE_UNROLL + u
                dst = pl.ds(pl.multiple_of(t * PACK_SUB, PACK_SUB), PACK_SUB)
                tile = lambda r: pl.ds(pl.multiple_of(r, PACK_SUB), PACK_SUB)
                pltpu.make_async_copy(yb_hbm.at[tile(da_ref[0, 0, t])], ybuf_ref.at[into, 0, dst],
                                      sem.at[into]).start(priority=0)
                pltpu.make_async_copy(yb_hbm.at[tile(db_ref[0, 0, t])], ybuf_ref.at[into, 1, dst],
                                      sem.at[into]).start(priority=1)
            return c

        lax.fori_loop(0, tm // ISSUE_UNROLL, issue, 0)

    @pl.when(i == 0)
    def _prime():
        gather(d0_ref, d1_ref, 0)

    for k in range(TOP_K):
        pltpu.make_async_copy(ybuf_ref.at[slot, k], ybuf_ref.at[slot, k], sem.at[slot]).wait()

    @pl.when(i + 1 < pl.num_programs(0))
    def _prefetch():
        gather(d0n_ref, d1n_ref, 1 - slot)

    wpad = jnp.concatenate([w_ref[...], jnp.zeros((LANES - TOP_K, tm), F32)], axis=0)
    wt = wpad.T
    w0 = wt[:, 0:1]
    w1 = wt[:, 1:2]
    x1 = x1_ref[...]
    y0 = _load_token_tiles(ybuf_ref.at[slot, 0], 0, tm)
    y1 = _load_token_tiles(ybuf_ref.at[slot, 1], 0, tm)
    lo = [_unpack_lo(a) * w0 + _unpack_lo(b) * w1 for a, b in zip(y0, y1)]
    hi = [_unpack_hi(a) * w0 + _unpack_hi(b) * w1 for a, b in zip(y0, y1)]
    moe = jnp.concatenate(lo + hi, axis=1)
    o_ref[...] = _layer_norm_rows(DEEPNORM_ALPHA * x1 + moe, lng_ref[...], lnb_ref[...])


def _combine(x1, dest_tile, wts, lng, lnb, layer, yb):
    t, d = x1.shape
    tm = TM_COMB
    nstep = t // tm
    d0 = dest_tile[0].reshape(nstep, 1, tm)
    d1 = dest_tile[1].reshape(nstep, 1, tm)
    smem_blk = pl.BlockSpec((1, 1, tm), lambda i: (i, 0, 0), memory_space=pltpu.SMEM)
    smem_next = pl.BlockSpec((1, 1, tm), lambda i: (jnp.minimum(i + 1, nstep - 1), 0, 0),
                             memory_space=pltpu.SMEM)
    lyr = lambda i: (layer, 0, 0)
    return pl.pallas_call(
        _combine_kernel,
        grid=(nstep,),
        in_specs=[smem_blk, smem_blk, smem_next, smem_next,
                  pl.BlockSpec((tm, d), lambda i: (i, 0)),
                  pl.BlockSpec((TOP_K, tm), lambda i: (0, i)),
                  pl.BlockSpec((None, 1, d), lyr),
                  pl.BlockSpec((None, 1, d), lyr),
                  pl.BlockSpec(memory_space=pl.ANY)],
        out_specs=pl.BlockSpec((tm, d), lambda i: (i, 0)),
        out_shape=jax.ShapeDtypeStruct((t, d), F32),
        scratch_shapes=[pltpu.VMEM((2, TOP_K, tm * PACK_SUB, LANES), U32), pltpu.SemaphoreType.DMA((2,))],
        compiler_params=_cparams(("arbitrary",)),
        name="combine",
    )(d0, d1, d0, d1, x1, wts, lng, lnb, yb)


def _rotary_tables(seq):
    inv = 1.0 / (ROPE_BASE ** (jnp.arange(0, RET_HEAD_DIM, 2, dtype=F32) / RET_HEAD_DIM))
    ang = jnp.arange(seq, dtype=F32)[:, None] * inv[None, :]
    return jnp.cos(ang), jnp.sin(ang)


def _routing_tables(counts, n_rows):
    ids = np.arange(N_EXPERTS)
    upto = jnp.asarray(ids[None, :] <= ids[:, None])
    csum = lambda v: jnp.sum(jnp.where(upto, v[None, :], 0), axis=1)
    padded = (counts + ROW_TILE - 1) // ROW_TILE * ROW_TILE
    pend = csum(padded)
    pstart = pend - padded
    ntile = padded // ROW_TILE
    nsup = (ntile + TILES_PER_SUPER - 1) // TILES_PER_SUPER
    send = csum(nsup)
    sstart = send - nsup
    nsuper = -(-n_rows // SUPER) + N_EXPERTS
    s = jnp.arange(nsuper, dtype=I32)
    total = send[-1]
    which = lambda q: jnp.minimum(jnp.sum((send[None, :] <= q[:, None]).astype(I32), axis=1), N_EXPERTS - 1)
    pick = lambda v, e: jnp.sum(jnp.where(e[:, None] == ids[None, :], v[None, :], 0), axis=1)
    se = which(s)
    valid = s < total
    k = s - pick(sstart, se)
    row = pick(pstart, se) + k * SUPER
    nt = jnp.clip(pick(ntile, se) - k * TILES_PER_SUPER, 0, TILES_PER_SUPER)
    idle = s - total
    tail_row = pend[-1] + idle * SUPER
    nz = jnp.clip((n_rows - tail_row) // ROW_TILE, 0, TILES_PER_SUPER)
    se = jnp.where(valid, se, which((total - 1)[None])[0])
    row = jnp.where(valid, row, jnp.minimum(tail_row, n_rows - ROW_TILE))
    nt = jnp.where(valid, nt, 0)
    nz = jnp.where(valid, 0, nz)
    seg_last = jnp.where(padded > 0, pend - ROW_TILE, -1)
    tail = pend[-1] + jnp.arange(N_EXPERTS, dtype=I32) * ROW_TILE
    zrow = jnp.concatenate([seg_last, jnp.where(tail < n_rows, tail, -1)])
    return (pstart.astype(I32), se.astype(I32), row.astype(I32), nt.astype(I32), nz.astype(I32),
            zrow.astype(I32))


def kernel(x, w_in, w_out, ssm_lambda_re, ssm_lambda_im, ssm_b_re, ssm_b_im, ssm_c_re, ssm_c_im,
           ssm_d, ssm_log_dt, w_glu, ln1_g, ln1_b, ln2_g, ln2_b, router_w, router_b,
           w_gate, w_up, w_down):
    batch, seq, d = x.shape
    t = batch * seq
    n_assign = t * TOP_K
    n_rows = n_assign + N_EXPERTS * ROW_TILE
    assert seq % TM_PROJ == 0 and seq % RET_CHUNK == 0 and seq % S5_TL == 0
    assert t % TM_POST == 0 and t % TM_COMB == 0 and t % TM_DISP == 0 and n_rows % ROW_TILE == 0

    cos, sin = _rotary_tables(seq)
    expert_ids = jnp.arange(N_EXPERTS, dtype=I32)
    rw_t = router_w.astype(F32).T
    rhi = rw_t.astype(BF16)
    rlo = (rw_t - rhi.astype(F32)).astype(BF16)
    rcat = jnp.concatenate([rhi, rlo], axis=0)
    rb = router_b.astype(F32).reshape(N_EXPERTS, 1)

    w_in_bf = _prep_in_proj_weights(w_in)
    s5_tables = jax.vmap(_s5_tables)(ssm_lambda_re, ssm_lambda_im, ssm_b_re, ssm_b_im,
                                     ssm_c_re, ssm_c_im, ssm_d, ssm_log_dt)
    w_glu_bf = w_glu.astype(BF16)
    w_out_bf = w_out.astype(BF16)
    row3 = lambda p: p.astype(F32).reshape(DEPTH, 1, d)
    ln1_g, ln1_b, ln2_g, ln2_b = row3(ln1_g), row3(ln1_b), row3(ln2_g), row3(ln2_b)

    x2 = x.reshape(t, d)
    for l in range(DEPTH):
        proj = _in_proj(x2, w_in_bf, l, cos, sin, seq)
        ret = _retention(proj, batch, seq)
        y = _s5(proj, s5_tables, l, batch, seq).reshape(t, SSM_WIDTH)
        x1, x1p, e, wts, rank, cnt = _post_mix(
            x2, ret, y, w_glu_bf, w_out_bf, ln1_g, ln1_b, l, rcat, rhi, rb)
        pstart, se, row, ntl, nzl, zrow = _routing_tables(cnt[:, 0], n_rows)
        dest = rank + jnp.sum(jnp.where(e[..., None] == expert_ids, pstart, 0), axis=-1)
        dest_tile = dest * PACK_SUB
        xb = _dispatch(x1p, dest_tile, zrow, n_rows)
        yb = _experts(xb, se, row, ntl, nzl, w_gate, w_up, w_down, l)
        x2 = _combine(x1, dest_tile, wts, ln2_g, ln2_b, l, yb)
    return x2.reshape(batch, seq, d)
```

```python
import functools
import math

import numpy as np
import jax
import jax.numpy as jnp
from jax import lax
from jax.experimental import pallas as pl
from jax.experimental.pallas import tpu as pltpu

F32 = jnp.float32
BF16 = jnp.bfloat16
I32 = jnp.int32

D_MODEL = 2048
DEPTH = 2
RET_WIDTH = 1024
SSM_WIDTH = 1024
RET_HEAD_DIM = 256
RET_HEADS = RET_WIDTH // RET_HEAD_DIM
ROPE_BASE = 10000.0
SSM_GROUP = 16
SSM_GROUPS = SSM_WIDTH // SSM_GROUP
SSM_STATE = 64
IN_PROJ_WIDTH = 4 * RET_WIDTH + SSM_WIDTH
N_EXPERTS = 32
N_EXPERT_GROUPS = 4
EXPERTS_PER_GROUP = N_EXPERTS // N_EXPERT_GROUPS
TOP_K = 2
D_EXPERT = D_MODEL // 2
LN_EPS = 1e-5
DEEPNORM_ALPHA = (2.0 * DEPTH) ** 0.25

LANES = 128
SUBLANES = 8
VMEM_LIMIT = 56 * 1024 * 1024
EXPERTS_VMEM_LIMIT = 60 * 1024 * 1024

TM_PROJ = 1024
TN_PROJ = 1024
RET_CHUNK = 256
S5_TL = 256
S5_GROUP = 16
S5_LT = SSM_WIDTH // LANES
S5_NSTATE = (LANES // SSM_GROUP) * SSM_STATE
TM_POST = 512
TSUB_POST = 256
TM_COMB = 256
ROW_TILE = 128
TILES_PER_SUPER = 12
SUPER = ROW_TILE * TILES_PER_SUPER
DE_CHUNK = 512
N_DE_CHUNKS = D_EXPERT // DE_CHUNK
TM_DISP = 1024
ISSUE_UNROLL = 8


def _sigmoid(x):
    return 1.0 / (1.0 + jnp.exp(-x))


def _cparams(sem, vmem=VMEM_LIMIT):
    return pltpu.CompilerParams(dimension_semantics=sem, vmem_limit_bytes=vmem)


PACK_WORDS = D_MODEL // 2
PACK_SUB = PACK_WORDS // LANES
U32 = jnp.uint32


def _pack_rows(x):
    bits = lambda v: lax.bitcast_convert_type(v.astype(BF16).astype(F32), U32)
    return (bits(x[:, :PACK_WORDS]) >> 16) | (bits(x[:, PACK_WORDS:]) & jnp.uint32(0xFFFF0000))


def _unpack_lo(w):
    return lax.bitcast_convert_type(w << 16, F32)


def _unpack_hi(w):
    return lax.bitcast_convert_type(w & jnp.uint32(0xFFFF0000), F32)


def _store_token_tiles(ref, row0, n, packed):
    for c in range(PACK_SUB):
        ref[pl.ds(row0 * PACK_SUB + c, n, stride=PACK_SUB), :] = packed[:, c * LANES:(c + 1) * LANES]


def _load_token_tiles(ref, row0, n):
    return [ref[pl.ds(row0 * PACK_SUB + c, n, stride=PACK_SUB), :] for c in range(PACK_SUB)]


def _in_proj_kernel(x_ref, w_ref, cos_ref, sin_ref, o_ref):
    j = pl.program_id(1)
    acc = jnp.dot(x_ref[...].astype(BF16), w_ref[...], preferred_element_type=F32)
    is_rot = j < 2
    is_gate = j == 3
    scale = jnp.where(j == 1, RET_HEAD_DIM ** -0.5, 1.0).astype(F32)
    c = jnp.where(is_rot, cos_ref[...] * scale, 1.0)
    s = jnp.where(is_rot, sin_ref[...] * scale, 0.0)
    half = RET_HEAD_DIM // 2
    for h in range(RET_HEADS):
        lo = h * RET_HEAD_DIM
        t1 = acc[:, lo:lo + half]
        t2 = acc[:, lo + half:lo + RET_HEAD_DIM]
        r1 = t1 * c - t2 * s
        r2 = t1 * s + t2 * c
        o_ref[:, lo:lo + half] = (r1 * jnp.where(is_gate, _sigmoid(r1), 1.0)).astype(BF16)
        o_ref[:, lo + half:lo + RET_HEAD_DIM] = (r2 * jnp.where(is_gate, _sigmoid(r2), 1.0)).astype(BF16)


def _prep_kernel(w_ref, p_ref, o_ref):
    o_ref[...] = jnp.dot(w_ref[...].astype(BF16), p_ref[...], preferred_element_type=F32).astype(BF16)


def _prep_in_proj_weights(w_in):
    depth, d, n = w_in.shape
    hd = RET_HEAD_DIM
    half = hd // 2
    perm = np.zeros((2, hd, hd), np.float32)
    for i in range(half):
        perm[0, 2 * i, i] = 1.0
        perm[0, 2 * i + 1, half + i] = 1.0
    perm[1] = np.eye(hd, dtype=np.float32)
    n_qk_blocks = 2 * RET_WIDTH // hd
    return pl.pallas_call(
        _prep_kernel,
        grid=(depth, n // hd),
        in_specs=[
            pl.BlockSpec((None, d, hd), lambda l, c: (l, 0, c)),
            pl.BlockSpec((None, hd, hd), lambda l, c: (jnp.where(c < n_qk_blocks, 0, 1), 0, 0)),
        ],
        out_specs=pl.BlockSpec((None, d, hd), lambda l, c: (l, 0, c)),
        out_shape=jax.ShapeDtypeStruct((depth, d, n), BF16),
        compiler_params=_cparams(("arbitrary", "arbitrary")),
        name="prep_w_in",
    )(w_in, jnp.asarray(perm, BF16))


def _in_proj(x2, w_bf, layer, cos, sin, seq):
    t, d = x2.shape
    n = w_bf.shape[2]
    tiles_per_seq = seq // TM_PROJ
    return pl.pallas_call(
        _in_proj_kernel,
        grid=(t // TM_PROJ, n // TN_PROJ),
        in_specs=[
            pl.BlockSpec((TM_PROJ, d), lambda i, j: (i, 0)),
            pl.BlockSpec((None, d, TN_PROJ), lambda i, j: (layer, 0, j)),
            pl.BlockSpec((TM_PROJ, LANES), lambda i, j: (i % tiles_per_seq, 0)),
            pl.BlockSpec((TM_PROJ, LANES), lambda i, j: (i % tiles_per_seq, 0)),
        ],
        out_specs=pl.BlockSpec((TM_PROJ, TN_PROJ), lambda i, j: (i, j)),
        out_shape=jax.ShapeDtypeStruct((t, n), BF16),
        compiler_params=_cparams(("arbitrary", "arbitrary")),
        name="in_proj",
    )(x2, w_bf, cos, sin)


def _ret_kernel(q_ref, k_ref, v_ref, g_ref, mask_ref, qd_ref, kd_ref, o_ref, r_ref):
    n = pl.program_id(1)

    @pl.when(n == 0)
    def _init():
        r_ref[...] = jnp.zeros_like(r_ref)

    for h in range(RET_HEADS):
        cols = slice(h * RET_HEAD_DIM, (h + 1) * RET_HEAD_DIM)
        q = q_ref[:, cols]
        k = k_ref[:, cols]
        v = v_ref[:, cols]
        s = lax.dot_general(q, k, (((1,), (1,)), ((), ())), preferred_element_type=F32)
        s = s * mask_ref[h]
        inner = jnp.dot(s.astype(BF16), v, preferred_element_type=F32)
        qd = qd_ref[h]
        r_prev = r_ref[h]
        cross = jnp.dot((q.astype(F32) * qd).astype(BF16), r_prev.astype(BF16), preferred_element_type=F32)
        o = inner + cross
        kdec = (k.astype(F32) * kd_ref[h]).astype(BF16)
        kv = lax.dot_general(kdec, v, (((0,), (0,)), ((), ())), preferred_element_type=F32)
        r_ref[h] = r_prev * qd[RET_CHUNK - 1:RET_CHUNK, :] + kv
        mu = jnp.mean(o, axis=-1, keepdims=True)
        oc = o - mu
        var = jnp.mean(oc * oc, axis=-1, keepdims=True)
        o_ref[:, cols] = (g_ref[:, cols].astype(F32) * (oc * lax.rsqrt(var + LN_EPS))).astype(BF16)


def _retention_tables():
    c = RET_CHUNK
    log_gamma = np.log(1.0 - 2.0 ** (-5.0 - np.arange(RET_HEADS, dtype=np.float64)))
    idx = np.arange(c, dtype=np.float64)
    diff = idx[:, None] - idx[None, :]
    mask = np.where(diff >= 0, np.exp(log_gamma[:, None, None] * np.maximum(diff, 0.0)), 0.0)
    qd = np.exp(log_gamma[:, None] * (idx + 1.0)[None, :])
    kd = np.exp(log_gamma[:, None] * (c - 1.0 - idx)[None, :])
    bc = lambda a: np.broadcast_to(a[:, :, None], (RET_HEADS, c, RET_HEAD_DIM)).astype(np.float32)
    return mask.astype(np.float32), bc(qd), bc(kd)


def _retention(proj, batch, seq):
    t = proj.shape[0]
    nch = seq // RET_CHUNK
    mask, qd, kd = _retention_tables()
    blk = (RET_CHUNK, RET_WIDTH)
    seg = lambda c: pl.BlockSpec(blk, lambda b, n: (b * nch + n, c))
    whole = lambda a: pl.BlockSpec(a.shape, lambda b, n: (0, 0, 0))
    return pl.pallas_call(
        _ret_kernel,
        grid=(batch, nch),
        in_specs=[seg(0), seg(1), seg(2), seg(3), whole(mask), whole(qd), whole(kd)],
        out_specs=seg(0),
        out_shape=jax.ShapeDtypeStruct((t, RET_WIDTH), BF16),
        scratch_shapes=[pltpu.VMEM((RET_HEADS, RET_HEAD_DIM, RET_HEAD_DIM), F32)],
        compiler_params=_cparams(("arbitrary", "arbitrary")),
        name="retention",
    )(proj, proj, proj, proj, jnp.asarray(mask), jnp.asarray(qd), jnp.asarray(kd))


def _s5_kernel(u_ref, bm_ref, cm_ref, lam_ref, d_ref, y_ref, us_ref, ut_ref, bu_ref, hb_ref, yt_ref, st_ref):
    n = pl.program_id(1)
    tl = S5_TL
    nb = SUBLANES
    ns = S5_NSTATE

    @pl.when(n == 0)
    def _init():
        st_ref[...] = jnp.zeros_like(st_ref)

    for b in range(nb):
        us_ref[b * tl:(b + 1) * tl, :] = u_ref[b].astype(F32)

    for t in range(tl):
        ut_ref[t * nb:(t + 1) * nb, :] = us_ref[pl.ds(t, nb, stride=tl), :]

    ar = jnp.broadcast_to(lam_ref[0:1, :], (nb, ns))
    ai = jnp.broadcast_to(lam_ref[1:2, :], (nb, ns))
    sr = st_ref[0:nb, :]
    si = st_ref[nb:2 * nb, :]

    grows = S5_GROUP * nb

    def project_in(g):
        rows = slice(g * grows, (g + 1) * grows)
        bu_ref[rows, :] = jnp.dot(ut_ref[rows, :].astype(BF16), bm_ref[...], preferred_element_type=F32)

    project_in(0)
    for g in range(tl // S5_GROUP):
        if g + 1 < tl // S5_GROUP:
            project_in(g + 1)
        for tt in range(0, S5_GROUP, 2):
            r0 = g * grows + tt * nb
            pair_r = []
            pair_i = []
            for r in (r0, r0 + nb):
                br = bu_ref[r:r + nb, 0:ns]
                bi = bu_ref[r:r + nb, ns:2 * ns]
                sr, si = ar * sr - ai * si + br, ar * si + ai * sr + bi
                pair_r.append(sr)
                pair_i.append(si)
            hb_ref[r0:r0 + 2 * nb, 0:ns] = jnp.concatenate(pair_r, axis=0).astype(BF16)
            hb_ref[r0:r0 + 2 * nb, ns:2 * ns] = jnp.concatenate(pair_i, axis=0).astype(BF16)
        rows = slice(g * grows, (g + 1) * grows)
        yt_ref[rows, :] = (jnp.dot(hb_ref[rows, :], cm_ref[...], preferred_element_type=F32)
                           + ut_ref[rows, :] * d_ref[...])
    st_ref[0:nb, :] = sr
    st_ref[nb:2 * nb, :] = si
    for b in range(nb):
        y_ref[b] = yt_ref[pl.ds(b, tl, stride=nb), :].astype(BF16)


def _s5_tables(lam_re, lam_im, b_re, b_im, c_re, c_im, d, log_dt):
    lam = lax.complex(lam_re.astype(F32), lam_im.astype(F32))
    dt = jnp.exp(log_dt.astype(F32))[:, None]
    lam_bar = jnp.exp(lam * dt)
    b_bar = ((lam_bar - 1.0) / lam)[..., None] * lax.complex(b_re.astype(F32), b_im.astype(F32))
    gpt = LANES // SSM_GROUP
    eye = jnp.eye(gpt, dtype=F32)

    def bdiag_in(m):
        m = m.reshape(S5_LT, gpt, SSM_STATE, SSM_GROUP)
        return jnp.einsum('jgpi,gh->jgihp', m, eye).reshape(S5_LT, LANES, gpt * SSM_STATE)

    def bdiag_out(m):
        m = m.reshape(S5_LT, gpt, SSM_GROUP, SSM_STATE)
        return jnp.einsum('jgop,gh->jgpho', m, eye).reshape(S5_LT, gpt * SSM_STATE, LANES)

    bmat = jnp.concatenate([bdiag_in(jnp.real(b_bar)), bdiag_in(jnp.imag(b_bar))], axis=-1).astype(BF16)
    cmat = jnp.concatenate([bdiag_out(c_re.astype(F32)), -bdiag_out(c_im.astype(F32))], axis=1).astype(BF16)
    lam_t = jnp.stack([jnp.real(lam_bar).reshape(S5_LT, S5_NSTATE),
                       jnp.imag(lam_bar).reshape(S5_LT, S5_NSTATE)], axis=1)
    d_t = d.astype(F32).reshape(S5_LT, 1, LANES)
    return bmat, cmat, lam_t, d_t


def _s5(proj, tables, layer, batch, seq):
    bmat, cmat, lam_t, d_t = tables
    assert batch == SUBLANES
    proj3 = proj.reshape(batch, seq, IN_PROJ_WIDTH)
    ucol = (4 * RET_WIDTH) // LANES
    tl = S5_TL
    tile = lambda j, n: (layer, j, 0, 0)
    return pl.pallas_call(
        _s5_kernel,
        grid=(S5_LT, seq // tl),
        in_specs=[
            pl.BlockSpec((batch, tl, LANES), lambda j, n: (0, n, ucol + j)),
            pl.BlockSpec((None, None, LANES, 2 * S5_NSTATE), tile),
            pl.BlockSpec((None, None, 2 * S5_NSTATE, LANES), tile),
            pl.BlockSpec((None, None, 2, S5_NSTATE), tile),
            pl.BlockSpec((None, None, 1, LANES), tile),
        ],
        out_specs=pl.BlockSpec((batch, tl, LANES), lambda j, n: (0, n, j)),
        out_shape=jax.ShapeDtypeStruct((batch, seq, SSM_WIDTH), BF16),
        scratch_shapes=[
            pltpu.VMEM((batch * tl, LANES), F32),
            pltpu.VMEM((batch * tl, LANES), F32),
            pltpu.VMEM((batch * tl, 2 * S5_NSTATE), F32),
            pltpu.VMEM((batch * tl, 2 * S5_NSTATE), BF16),
            pltpu.VMEM((batch * tl, LANES), F32),
            pltpu.VMEM((2 * SUBLANES, S5_NSTATE), F32),
        ],
        compiler_params=_cparams(("arbitrary", "arbitrary")),
        name="s5",
    )(proj3, bmat, cmat, lam_t, d_t)


def _layer_norm_rows(r, g, b):
    mu = jnp.mean(r, axis=-1, keepdims=True)
    rc = r - mu
    var = jnp.mean(rc * rc, axis=-1, keepdims=True)
    return rc * lax.rsqrt(var + LN_EPS) * g + b


def _post_mix_kernel(x_ref, ret_ref, y_ref, wglu_ref, wout_ref, lng_ref, lnb_ref,
                     rcat_ref, rhi_ref, rb_ref, tri_ref,
                     x1_ref, x1p_ref, e_ref, w_ref, rank_ref, cnt_ref, carry_ref):
    i = pl.program_id(0)

    @pl.when(i == 0)
    def _init():
        carry_ref[...] = jnp.zeros_like(carry_ref)

    args = (x_ref, ret_ref, y_ref, wglu_ref, wout_ref, lng_ref, lnb_ref, rcat_ref, rhi_ref, rb_ref, tri_ref,
            x1_ref, x1p_ref, e_ref, w_ref, rank_ref, carry_ref)
    a = _post_mix_phases(0, *args)
    b = _post_mix_phases(1, *args)
    nph = len(a)
    a[0]()
    for k in range(1, nph):
        a[k]()
        b[k - 1]()
    b[nph - 1]()
    cnt_ref[...] = carry_ref[...].astype(I32)


def _post_mix_phases(sub, x_ref, ret_ref, y_ref, wglu_ref, wout_ref, lng_ref, lnb_ref,
                     rcat_ref, rhi_ref, rb_ref, tri_ref, x1_ref, x1p_ref, e_ref, w_ref, rank_ref, carry_ref):
    tm = TSUB_POST
    rows = slice(sub * tm, (sub + 1) * tm)
    st = {}

    def gelu():
        st['ya'] = jax.nn.gelu(y_ref[rows, :].astype(F32))

    def glu_matmul():
        st['z'] = jnp.dot(st['ya'].astype(BF16), wglu_ref[...], preferred_element_type=F32)

    def glu_gate():
        st['ssm'] = (st.pop('ya') * _sigmoid(st.pop('z'))).astype(BF16)

    def out_matmul():
        mixed = jnp.concatenate([ret_ref[rows, :], st.pop('ssm')], axis=1)
        st['h'] = jnp.dot(mixed, wout_ref[...], preferred_element_type=F32)

    def norm():
        x1 = _layer_norm_rows(DEEPNORM_ALPHA * x_ref[rows, :] + st.pop('h'), lng_ref[...], lnb_ref[...])
        x1_ref[rows, :] = x1
        _store_token_tiles(x1p_ref, sub * tm, tm, _pack_rows(x1))
        st['xh'] = x1.astype(BF16)
        st['xl'] = (x1 - st['xh'].astype(F32)).astype(BF16)

    def router_matmul():
        nt = (((1,), (1,)), ((), ()))
        l1 = lax.dot_general(rcat_ref[...], st.pop('xh'), nt, preferred_element_type=F32)
        l2 = lax.dot_general(rhi_ref[...], st.pop('xl'), nt, preferred_element_type=F32)
        st['logits'] = l1[0:N_EXPERTS] + l1[N_EXPERTS:] + l2 + rb_ref[...]

    def route():
        _route(st.pop('logits'), rows, tri_ref, e_ref, w_ref, rank_ref, carry_ref)

    return [gelu, glu_matmul, glu_gate, out_matmul, norm, router_matmul, route]


def _route(logits, rows, tri_ref, e_ref, w_ref, rank_ref, carry_ref):
    tm = TSUB_POST
    m = jnp.max(logits, axis=0, keepdims=True)
    ex = jnp.exp(logits - m)
    p = ex / jnp.sum(ex, axis=0, keepdims=True)

    eg = EXPERTS_PER_GROUP
    iota_g = lax.broadcasted_iota(I32, (eg, tm), 0)
    best = None
    for g in range(N_EXPERT_GROUPS):
        pg = p[g * eg:(g + 1) * eg]
        m1 = jnp.max(pg, axis=0, keepdims=True)
        i1 = jnp.min(jnp.where(pg == m1, iota_g, eg), axis=0, keepdims=True)
        pg2 = jnp.where(iota_g == i1, -1.0, pg)
        m2 = jnp.max(pg2, axis=0, keepdims=True)
        i2 = jnp.min(jnp.where(pg2 == m2, iota_g, eg), axis=0, keepdims=True)
        sg = m1 + m2
        if best is None:
            best = (sg, m1, m2, i1, i2)
        else:
            better = sg > best[0]
            cand = (sg, m1, m2, i1 + g * eg, i2 + g * eg)
            best = tuple(jnp.where(better, c, o) for c, o in zip(cand, best))
    _, v1, v2, e1, e2 = best
    tot = v1 + v2
    e_ref[0:1, rows] = e1
    e_ref[1:2, rows] = e2
    w_ref[0:1, rows] = v1 / tot
    w_ref[1:2, rows] = v2 / tot

    iota_e = lax.broadcasted_iota(I32, (N_EXPERTS, tm), 0)
    oh1 = iota_e == e1
    oh2 = iota_e == e2
    oh = jnp.where(oh1, 1.0, jnp.where(oh2, 1.0, 0.0))
    before = jnp.dot(oh.astype(BF16), tri_ref[...], preferred_element_type=F32) + carry_ref[:, 0:1]
    rank_ref[0:1, rows] = jnp.sum(jnp.where(oh1, before, 0.0), axis=0, keepdims=True).astype(I32)
    rank_ref[1:2, rows] = jnp.sum(jnp.where(oh2, before, 0.0), axis=0, keepdims=True).astype(I32)
    carry_ref[...] = carry_ref[...] + jnp.sum(oh, axis=1, keepdims=True)


def _post_mix(x2, ret, y, wglu_bf, wout_bf, lng, lnb, layer, rcat, rhi, rb):
    t, d = x2.shape
    tm = TM_POST
    ts = TSUB_POST
    tri = jnp.asarray(np.triu(np.ones((ts, ts), np.float32), 1), BF16)
    const = lambda i: (0, 0)
    lyr = lambda i: (layer, 0, 0)
    tok = lambda i: (i, 0)
    lane = lambda i: (0, i)
    return pl.pallas_call(
        _post_mix_kernel,
        grid=(t // tm,),
        in_specs=[
            pl.BlockSpec((tm, d), tok),
            pl.BlockSpec((tm, RET_WIDTH), tok),
            pl.BlockSpec((tm, SSM_WIDTH), tok),
            pl.BlockSpec((None, SSM_WIDTH, SSM_WIDTH), lyr),
            pl.BlockSpec((None, RET_WIDTH + SSM_WIDTH, d), lyr),
            pl.BlockSpec((None, 1, d), lyr),
            pl.BlockSpec((None, 1, d), lyr),
            pl.BlockSpec((2 * N_EXPERTS, d), const),
            pl.BlockSpec((N_EXPERTS, d), const),
            pl.BlockSpec((N_EXPERTS, 1), const),
            pl.BlockSpec((ts, ts), const),
        ],
        out_specs=[
            pl.BlockSpec((tm, d), tok),
            pl.BlockSpec((tm * PACK_SUB, LANES), tok),
            pl.BlockSpec((TOP_K, tm), lane),
            pl.BlockSpec((TOP_K, tm), lane),
            pl.BlockSpec((TOP_K, tm), lane),
            pl.BlockSpec((N_EXPERTS, LANES), const),
        ],
        out_shape=[
            jax.ShapeDtypeStruct((t, d), F32),
            jax.ShapeDtypeStruct((t * PACK_SUB, LANES), U32),
            jax.ShapeDtypeStruct((TOP_K, t), I32),
            jax.ShapeDtypeStruct((TOP_K, t), F32),
            jax.ShapeDtypeStruct((TOP_K, t), I32),
            jax.ShapeDtypeStruct((N_EXPERTS, LANES), I32),
        ],
        scratch_shapes=[pltpu.VMEM((N_EXPERTS, LANES), F32)],
        compiler_params=_cparams(("arbitrary",)),
        name="post_mix",
    )(x2, ret, y, wglu_bf, wout_bf, lng, lnb, rcat, rhi, rb, tri)


def _dispatch_kernel(zrow_ref, d0_ref, d1_ref, x_ref, xb_hbm, zbuf_ref, sem, zsem):
    tm = TM_DISP

    @pl.when(pl.program_id(0) == 0)
    def _zero_fill():
        zbuf_ref[...] = jnp.zeros_like(zbuf_ref)

        def zcopy(k):
            n = ROW_TILE * PACK_SUB
            rows = pl.ds(pl.multiple_of(zrow_ref[k] * PACK_SUB, n), n)
            return pltpu.make_async_copy(zbuf_ref, xb_hbm.at[rows], zsem.at[0])

        def start(k, c):
            @pl.when(zrow_ref[k] >= 0)
            def _():
                zcopy(k).start()
            return c

        def wait(k, c):
            @pl.when(zrow_ref[k] >= 0)
            def _():
                zcopy(k).wait()
            return c

        lax.fori_loop(0, 2 * N_EXPERTS, start, 0)
        lax.fori_loop(0, 2 * N_EXPERTS, wait, 0)

    def issue(tb, c):
        for u in range(ISSUE_UNROLL):
            t = tb * ISSUE_UNROLL + u
            src = x_ref.at[pl.ds(t * PACK_SUB, PACK_SUB)]
            tile = lambda r: pl.ds(pl.multiple_of(r, PACK_SUB), PACK_SUB)
            pltpu.make_async_copy(src, xb_hbm.at[tile(d0_ref[0, 0, t])], sem.at[0]).start(priority=0)
            pltpu.make_async_copy(src, xb_hbm.at[tile(d1_ref[0, 0, t])], sem.at[0]).start(priority=1)
        return c

    lax.fori_loop(0, tm // ISSUE_UNROLL, issue, 0)
    for _ in range(TOP_K):
        pltpu.make_async_copy(x_ref, xb_hbm.at[pl.ds(0, tm * PACK_SUB)], sem.at[0]).wait()


def _dispatch(x1p, dest_tile, zrow, n_rows):
    t = x1p.shape[0] // PACK_SUB
    tm = TM_DISP
    nstep = t // tm
    d0 = dest_tile[0].reshape(nstep, 1, tm)
    d1 = dest_tile[1].reshape(nstep, 1, tm)
    smem_blk = pl.BlockSpec((1, 1, tm), lambda i, z: (i, 0, 0), memory_space=pltpu.SMEM)
    grid_spec = pltpu.PrefetchScalarGridSpec(
        num_scalar_prefetch=1,
        grid=(nstep,),
        in_specs=[smem_blk, smem_blk, pl.BlockSpec((tm * PACK_SUB, LANES), lambda i, z: (i, 0))],
        out_specs=pl.BlockSpec(memory_space=pl.ANY),
        scratch_shapes=[pltpu.VMEM((ROW_TILE * PACK_SUB, LANES), U32),
                        pltpu.SemaphoreType.DMA((1,)), pltpu.SemaphoreType.DMA((1,))],
    )
    return pl.pallas_call(
        _dispatch_kernel,
        grid_spec=grid_spec,
        out_shape=jax.ShapeDtypeStruct((n_rows * PACK_SUB, LANES), U32),
        compiler_params=_cparams(("arbitrary",)),
        name="dispatch",
    )(zrow, d0, d1, x1p)


def _experts_kernel(se_ref, row_ref, nt_ref, nz_ref, xb_hbm, wg_ref, wu_ref, wd_ref, yb_hbm,
                    xs_ref, acc_ref, wgb_ref, wub_ref, wdb_ref, stg_in, stg_out, sem_in, sem_out):
    del se_ref
    s = pl.program_id(0)
    j = pl.program_id(1)
    nt = nt_ref[s]
    nz = nz_ref[s]
    row0 = row_ref[s]
    rt = ROW_TILE

    nsuper = pl.num_programs(0)
    s_next = jnp.minimum(s + 1, nsuper - 1)
    prev_nt = jnp.where(s > 0, nt_ref[jnp.maximum(s - 1, 0)], 0)
    next_nt = jnp.where(s + 1 < nsuper, nt_ref[s_next], 0)

    def rows(i, base=None):
        n = rt * PACK_SUB
        base = row0 if base is None else base
        return pl.ds(pl.multiple_of((base + i * rt) * PACK_SUB, n), n)

    def in_copy(i, slot):
        return pltpu.make_async_copy(xb_hbm.at[rows(i)], stg_in.at[slot], sem_in.at[slot])

    def out_copy(i, slot):
        return pltpu.make_async_copy(stg_out.at[slot], yb_hbm.at[rows(i)], sem_out.at[slot])

    def partial_out(i, ntiles):
        r = pl.multiple_of(i * rt, rt)
        xi = xs_ref[pl.ds(r, ntiles * rt), :]
        g = jnp.dot(xi, wgb_ref[...], preferred_element_type=F32)
        u = jnp.dot(xi, wub_ref[...], preferred_element_type=F32)
        hj = (g * _sigmoid(g) * u).astype(BF16)
        return r, jnp.dot(hj, wdb_ref[...], preferred_element_type=F32)

    def load_tile(i):
        slot = i % 2

        @pl.when(i + 1 < nt)
        def _():
            in_copy(i + 1, 1 - slot).start()

        in_copy(i, slot).wait()
        xrows = pl.ds(pl.multiple_of(i * rt, rt), rt)
        for c, words in enumerate(_load_token_tiles(stg_in.at[slot], 0, rt)):
            xs_ref[xrows, c * LANES:(c + 1) * LANES] = _unpack_lo(words).astype(BF16)
            xs_ref[xrows, PACK_WORDS + c * LANES:PACK_WORDS + (c + 1) * LANES] = _unpack_hi(words).astype(BF16)

    def store_tile(i, vals):
        slot = i % 2

        @pl.when(i >= 2)
        def _():
            out_copy(i - 2, slot).wait()

        _store_token_tiles(stg_out.at[slot], 0, rt, _pack_rows(vals))
        out_copy(i, slot).start()

    def for_tiles(body):
        def quad(q, c):
            body(4 * q, 4)
            return c

        lax.fori_loop(0, nt // 4, quad, 0)

        @pl.when(nt % 4 >= 2)
        def _():
            body((nt // 4) * 4, 2)

        @pl.when(nt % 2 == 1)
        def _():
            body(nt - 1, 1)

    @pl.when((nz > 0) & (j == 0))
    def _zero_tail():
        stg_out[0] = jnp.zeros(stg_out.shape[1:], U32)

        def start(i, c):
            out_copy(i, 0).start()
            return c

        def wait(i, c):
            out_copy(i, 0).wait()
            return c

        lax.fori_loop(0, nz, start, 0)
        lax.fori_loop(0, nz, wait, 0)

    @pl.when(nt > 0)
    def _work():
        wgb_ref[...] = wg_ref[...].astype(BF16)
        wub_ref[...] = wu_ref[...].astype(BF16)
        wdb_ref[...] = wd_ref[...].astype(BF16)

        @pl.when(j == 0)
        def _first():
            @pl.when(prev_nt == 0)
            def _():
                in_copy(0, 0).start()

            @pl.when(prev_nt >= 2)
            def _():
                out_copy(0, prev_nt % 2).wait()

            @pl.when(prev_nt >= 1)
            def _():
                out_copy(0, (prev_nt - 1) % 2).wait()

            def body(i, ntiles):
                for k in range(ntiles):
                    load_tile(i + k)
                r, part = partial_out(i, ntiles)
                acc_ref[pl.ds(r, ntiles * rt), :] = part

            for_tiles(body)

        @pl.when((j > 0) & (j < N_DE_CHUNKS - 1))
        def _mid():
            def body(i, ntiles):
                r, part = partial_out(i, ntiles)
                acc_ref[pl.ds(r, ntiles * rt), :] += part

            for_tiles(body)

        @pl.when(j == N_DE_CHUNKS - 1)
        def _last():
            @pl.when(next_nt > 0)
            def _():
                pltpu.make_async_copy(xb_hbm.at[rows(0, row_ref[s_next])], stg_in.at[0], sem_in.at[0]).start()

            def body(i, ntiles):
                r, part = partial_out(i, ntiles)
                total = acc_ref[pl.ds(r, ntiles * rt), :] + part
                for k in range(ntiles):
                    store_tile(i + k, total[k * rt:(k + 1) * rt, :])

            for_tiles(body)

            @pl.when(next_nt == 0)
            def _drain():
                @pl.when(nt >= 2)
                def _():
                    out_copy(nt - 2, nt % 2).wait()

                out_copy(nt - 1, (nt - 1) % 2).wait()


def _experts(xb, se, row, ntl, nzl, w_gate, w_up, w_down, layer):
    p = xb.shape[0] // PACK_SUB
    d = D_MODEL
    nsuper = se.shape[0]
    last = N_DE_CHUNKS - 1
    assert last >= 1

    def jj(s, j, nt_ref):
        return jnp.where(nt_ref[s] > 0, j, last)

    grid_spec = pltpu.PrefetchScalarGridSpec(
        num_scalar_prefetch=4,
        grid=(nsuper, N_DE_CHUNKS),
        in_specs=[
            pl.BlockSpec(memory_space=pl.ANY),
            pl.BlockSpec((None, None, d, DE_CHUNK),
                         lambda s, j, se_r, row_r, nt_r, nz_r: (layer, se_r[s], 0, jj(s, j, nt_r))),
            pl.BlockSpec((None, None, d, DE_CHUNK),
                         lambda s, j, se_r, row_r, nt_r, nz_r: (layer, se_r[s], 0, jj(s, j, nt_r))),
            pl.BlockSpec((None, None, DE_CHUNK, d),
                         lambda s, j, se_r, row_r, nt_r, nz_r: (layer, se_r[s], jj(s, j, nt_r), 0)),
        ],
        out_specs=pl.BlockSpec(memory_space=pl.ANY),
        scratch_shapes=[
            pltpu.VMEM((SUPER, d), BF16),
            pltpu.VMEM((SUPER, d), F32),
            pltpu.VMEM((d, DE_CHUNK), BF16),
            pltpu.VMEM((d, DE_CHUNK), BF16),
            pltpu.VMEM((DE_CHUNK, d), BF16),
            pltpu.VMEM((2, ROW_TILE * PACK_SUB, LANES), U32),
            pltpu.VMEM((2, ROW_TILE * PACK_SUB, LANES), U32),
            pltpu.SemaphoreType.DMA((2,)),
            pltpu.SemaphoreType.DMA((2,)),
        ],
    )
    return pl.pallas_call(
        _experts_kernel,
        grid_spec=grid_spec,
        out_shape=jax.ShapeDtypeStruct((p * PACK_SUB, LANES), U32),
        compiler_params=_cparams(("arbitrary", "arbitrary"), vmem=EXPERTS_VMEM_LIMIT),
        name="experts",
    )(se, row, ntl, nzl, xb, w_gate, w_up, w_down)


def _combine_kernel(d0_ref, d1_ref, d0n_ref, d1n_ref, x1_ref, w_ref, lng_ref, lnb_ref, yb_hbm,
                    o_ref, ybuf_ref, sem):
    tm = TM_COMB
    i = pl.program_id(0)
    slot = i % 2

    def gather(da_ref, db_ref, into):
        def issue(tb, c):
            for u in range(ISSUE_UNROLL):
                t = tb * ISSUE_UNROLL + u
                dst = pl.ds(pl.multiple_of(t * PACK_SUB, PACK_SUB), PACK_SUB)
                tile = lambda r: pl.ds(pl.multiple_of(r, PACK_SUB), PACK_SUB)
                pltpu.make_async_copy(yb_hbm.at[tile(da_ref[0, 0, t])], ybuf_ref.at[into, 0, dst],
                                      sem.at[into]).start(priority=0)
                pltpu.make_async_copy(yb_hbm.at[tile(db_ref[0, 0, t])], ybuf_ref.at[into, 1, dst],
                                      sem.at[into]).start(priority=1)
            return c

        lax.fori_loop(0, tm // ISSUE_UNROLL, issue, 0)

    @pl.when(i == 0)
    def _prime():
        gather(d0_ref, d1_ref, 0)

    for k in range(TOP_K):
        pltpu.make_async_copy(ybuf_ref.at[slot, k], ybuf_ref.at[slot, k], sem.at[slot]).wait()

    @pl.when(i + 1 < pl.num_programs(0))
    def _prefetch():
        gather(d0n_ref, d1n_ref, 1 - slot)

    wpad = jnp.concatenate([w_ref[...], jnp.zeros((LANES - TOP_K, tm), F32)], axis=0)
    wt = wpad.T
    w0 = wt[:, 0:1]
    w1 = wt[:, 1:2]
    x1 = x1_ref[...]
    y0 = _load_token_tiles(ybuf_ref.at[slot, 0], 0, tm)
    y1 = _load_token_tiles(ybuf_ref.at[slot, 1], 0, tm)
    lo = [_unpack_lo(a) * w0 + _unpack_lo(b) * w1 for a, b in zip(y0, y1)]
    hi = [_unpack_hi(a) * w0 + _unpack_hi(b) * w1 for a, b in zip(y0, y1)]
    moe = jnp.concatenate(lo + hi, axis=1)
    o_ref[...] = _layer_norm_rows(DEEPNORM_ALPHA * x1 + moe, lng_ref[...], lnb_ref[...])


def _combine(x1, dest_tile, wts, lng, lnb, layer, yb):
    t, d = x1.shape
    tm = TM_COMB
    nstep = t // tm
    d0 = dest_tile[0].reshape(nstep, 1, tm)
    d1 = dest_tile[1].reshape(nstep, 1, tm)
    smem_blk = pl.BlockSpec((1, 1, tm), lambda i: (i, 0, 0), memory_space=pltpu.SMEM)
    smem_next = pl.BlockSpec((1, 1, tm), lambda i: (jnp.minimum(i + 1, nstep - 1), 0, 0),
                             memory_space=pltpu.SMEM)
    lyr = lambda i: (layer, 0, 0)
    return pl.pallas_call(
        _combine_kernel,
        grid=(nstep,),
        in_specs=[smem_blk, smem_blk, smem_next, smem_next,
                  pl.BlockSpec((tm, d), lambda i: (i, 0)),
                  pl.BlockSpec((TOP_K, tm), lambda i: (0, i)),
                  pl.BlockSpec((None, 1, d), lyr),
                  pl.BlockSpec((None, 1, d), lyr),
                  pl.BlockSpec(memory_space=pl.ANY)],
        out_specs=pl.BlockSpec((tm, d), lambda i: (i, 0)),
        out_shape=jax.ShapeDtypeStruct((t, d), F32),
        scratch_shapes=[pltpu.VMEM((2, TOP_K, tm * PACK_SUB, LANES), U32), pltpu.SemaphoreType.DMA((2,))],
        compiler_params=_cparams(("arbitrary",)),
        name="combine",
    )(d0, d1, d0, d1, x1, wts, lng, lnb, yb)


def _rotary_tables(seq):
    inv = 1.0 / (ROPE_BASE ** (jnp.arange(0, RET_HEAD_DIM, 2, dtype=F32) / RET_HEAD_DIM))
    ang = jnp.arange(seq, dtype=F32)[:, None] * inv[None, :]
    return jnp.cos(ang), jnp.sin(ang)


def _routing_tables(counts, n_rows):
    ids = np.arange(N_EXPERTS)
    upto = jnp.asarray(ids[None, :] <= ids[:, None])
    csum = lambda v: jnp.sum(jnp.where(upto, v[None, :], 0), axis=1)
    padded = (counts + ROW_TILE - 1) // ROW_TILE * ROW_TILE
    pend = csum(padded)
    pstart = pend - padded
    ntile = padded // ROW_TILE
    nsup = (ntile + TILES_PER_SUPER - 1) // TILES_PER_SUPER
    send = csum(nsup)
    sstart = send - nsup
    nsuper = -(-n_rows // SUPER) + N_EXPERTS
    s = jnp.arange(nsuper, dtype=I32)
    total = send[-1]
    which = lambda q: jnp.minimum(jnp.sum((send[None, :] <= q[:, None]).astype(I32), axis=1), N_EXPERTS - 1)
    pick = lambda v, e: jnp.sum(jnp.where(e[:, None] == ids[None, :], v[None, :], 0), axis=1)
    se = which(s)
    valid = s < total
    k = s - pick(sstart, se)
    row = pick(pstart, se) + k * SUPER
    nt = jnp.clip(pick(ntile, se) - k * TILES_PER_SUPER, 0, TILES_PER_SUPER)
    idle = s - total
    tail_row = pend[-1] + idle * SUPER
    nz = jnp.clip((n_rows - tail_row) // ROW_TILE, 0, TILES_PER_SUPER)
    se = jnp.where(valid, se, which((total - 1)[None])[0])
    row = jnp.where(valid, row, jnp.minimum(tail_row, n_rows - ROW_TILE))
    nt = jnp.where(valid, nt, 0)
    nz = jnp.where(valid, 0, nz)
    seg_last = jnp.where(padded > 0, pend - ROW_TILE, -1)
    tail = pend[-1] + jnp.arange(N_EXPERTS, dtype=I32) * ROW_TILE
    zrow = jnp.concatenate([seg_last, jnp.where(tail < n_rows, tail, -1)])
    return (pstart.astype(I32), se.astype(I32), row.astype(I32), nt.astype(I32), nz.astype(I32),
            zrow.astype(I32))


def kernel(x, w_in, w_out, ssm_lambda_re, ssm_lambda_im, ssm_b_re, ssm_b_im, ssm_c_re, ssm_c_im,
           ssm_d, ssm_log_dt, w_glu, ln1_g, ln1_b, ln2_g, ln2_b, router_w, router_b,
           w_gate, w_up, w_down):
    batch, seq, d = x.shape
    t = batch * seq
    n_assign = t * TOP_K
    n_rows = n_assign + N_EXPERTS * ROW_TILE
    assert seq % TM_PROJ == 0 and seq % RET_CHUNK == 0 and seq % S5_TL == 0
    assert t % TM_POST == 0 and t % TM_COMB == 0 and t % TM_DISP == 0 and n_rows % ROW_TILE == 0

    cos, sin = _rotary_tables(seq)
    expert_ids = jnp.arange(N_EXPERTS, dtype=I32)
    rw_t = router_w.astype(F32).T
    rhi = rw_t.astype(BF16)
    rlo = (rw_t - rhi.astype(F32)).astype(BF16)
    rcat = jnp.concatenate([rhi, rlo], axis=0)
    rb = router_b.astype(F32).reshape(N_EXPERTS, 1)

    w_in_bf = _prep_in_proj_weights(w_in)
    s5_tables = jax.vmap(_s5_tables)(ssm_lambda_re, ssm_lambda_im, ssm_b_re, ssm_b_im,
                                     ssm_c_re, ssm_c_im, ssm_d, ssm_log_dt)
    w_glu_bf = w_glu.astype(BF16)
    w_out_bf = w_out.astype(BF16)
    row3 = lambda p: p.astype(F32).reshape(DEPTH, 1, d)
    ln1_g, ln1_b, ln2_g, ln2_b = row3(ln1_g), row3(ln1_b), row3(ln2_g), row3(ln2_b)

    x2 = x.reshape(t, d)
    for l in range(DEPTH):
        proj = _in_proj(x2, w_in_bf, l, cos, sin, seq)
        ret = _retention(proj, batch, seq)
        y = _s5(proj, s5_tables, l, batch, seq).reshape(t, SSM_WIDTH)
        x1, x1p, e, wts, rank, cnt = _post_mix(
            x2, ret, y, w_glu_bf, w_out_bf, ln1_g, ln1_b, l, rcat, rhi, rb)
        pstart, se, row, ntl, nzl, zrow = _routing_tables(cnt[:, 0], n_rows)
        dest = rank + jnp.sum(jnp.where(e[..., None] == expert_ids, pstart, 0), axis=-1)
        dest_tile = dest * PACK_SUB
        xb = _dispatch(x1p, dest_tile, zrow, n_rows)
        yb = _experts(xb, se, row, ntl, nzl, w_gate, w_up, w_down, l)
        x2 = _combine(x1, dest_tile, wts, ln2_g, ln2_b, l, yb)
    return x2.reshape(batch, seq, d)
```

```python
import functools
import math

import numpy as np
import jax
import jax.numpy as jnp
from jax import lax
from jax.experimental import pallas as pl
from jax.experimental.pallas import tpu as pltpu

F32 = jnp.float32
BF16 = jnp.bfloat16
I32 = jnp.int32

D_MODEL = 2048
DEPTH = 2
RET_WIDTH = 1024
SSM_WIDTH = 1024
RET_HEAD_DIM = 256
RET_HEADS = RET_WIDTH // RET_HEAD_DIM
ROPE_BASE = 10000.0
SSM_GROUP = 16
SSM_GROUPS = SSM_WIDTH // SSM_GROUP
SSM_STATE = 64
IN_PROJ_WIDTH = 4 * RET_WIDTH + SSM_WIDTH
N_EXPERTS = 32
N_EXPERT_GROUPS = 4
EXPERTS_PER_GROUP = N_EXPERTS // N_EXPERT_GROUPS
TOP_K = 2
D_EXPERT = D_MODEL // 2
LN_EPS = 1e-5
DEEPNORM_ALPHA = (2.0 * DEPTH) ** 0.25

LANES = 128
SUBLANES = 8
VMEM_LIMIT = 56 * 1024 * 1024
EXPERTS_VMEM_LIMIT = 60 * 1024 * 1024

TM_PROJ = 1024
TN_PROJ = 1024
RET_CHUNK = 256
S5_TL = 256
S5_GROUP = 16
S5_LT = SSM_WIDTH // LANES
S5_NSTATE = (LANES // SSM_GROUP) * SSM_STATE
TM_POST = 512
TSUB_POST = 256
TM_COMB = 256
ROW_TILE = 256
TILES_PER_SUPER = 6
SUPER = ROW_TILE * TILES_PER_SUPER
DE_CHUNK = 512
N_DE_CHUNKS = D_EXPERT // DE_CHUNK
TM_DISP = 1024
ISSUE_UNROLL = 8


def _sigmoid(x):
    return 1.0 / (1.0 + jnp.exp(-x))


def _cparams(sem, vmem=VMEM_LIMIT):
    return pltpu.CompilerParams(dimension_semantics=sem, vmem_limit_bytes=vmem)


PACK_WORDS = D_MODEL // 2
PACK_SUB = PACK_WORDS // LANES
U32 = jnp.uint32


def _pack_rows(x):
    bits = lambda v: lax.bitcast_convert_type(v.astype(BF16).astype(F32), U32)
    return (bits(x[:, :PACK_WORDS]) >> 16) | (bits(x[:, PACK_WORDS:]) & jnp.uint32(0xFFFF0000))


def _unpack_lo(w):
    return lax.bitcast_convert_type(w << 16, F32)


def _unpack_hi(w):
    return lax.bitcast_convert_type(w & jnp.uint32(0xFFFF0000), F32)


def _store_token_tiles(ref, row0, n, packed):
    for c in range(PACK_SUB):
        ref[pl.ds(row0 * PACK_SUB + c, n, stride=PACK_SUB), :] = packed[:, c * LANES:(c + 1) * LANES]


def _load_token_tiles(ref, row0, n):
    return [ref[pl.ds(row0 * PACK_SUB + c, n, stride=PACK_SUB), :] for c in range(PACK_SUB)]


def _in_proj_kernel(x_ref, w_ref, cos_ref, sin_ref, o_ref):
    j = pl.program_id(1)
    acc = jnp.dot(x_ref[...].astype(BF16), w_ref[...], preferred_element_type=F32)
    is_rot = j < 2
    is_gate = j == 3
    scale = jnp.where(j == 1, RET_HEAD_DIM ** -0.5, 1.0).astype(F32)
    c = jnp.where(is_rot, cos_ref[...] * scale, 1.0)
    s = jnp.where(is_rot, sin_ref[...] * scale, 0.0)
    half = RET_HEAD_DIM // 2
    for h in range(RET_HEADS):
        lo = h * RET_HEAD_DIM
        t1 = acc[:, lo:lo + half]
        t2 = acc[:, lo + half:lo + RET_HEAD_DIM]
        r1 = t1 * c - t2 * s
        r2 = t1 * s + t2 * c
        o_ref[:, lo:lo + half] = (r1 * jnp.where(is_gate, _sigmoid(r1), 1.0)).astype(BF16)
        o_ref[:, lo + half:lo + RET_HEAD_DIM] = (r2 * jnp.where(is_gate, _sigmoid(r2), 1.0)).astype(BF16)


def _prep_kernel(w_ref, p_ref, o_ref):
    o_ref[...] = jnp.dot(w_ref[...].astype(BF16), p_ref[...], preferred_element_type=F32).astype(BF16)


def _prep_in_proj_weights(w_in):
    depth, d, n = w_in.shape
    hd = RET_HEAD_DIM
    half = hd // 2
    perm = np.zeros((2, hd, hd), np.float32)
    for i in range(half):
        perm[0, 2 * i, i] = 1.0
        perm[0, 2 * i + 1, half + i] = 1.0
    perm[1] = np.eye(hd, dtype=np.float32)
    n_qk_blocks = 2 * RET_WIDTH // hd
    return pl.pallas_call(
        _prep_kernel,
        grid=(depth, n // hd),
        in_specs=[
            pl.BlockSpec((None, d, hd), lambda l, c: (l, 0, c)),
            pl.BlockSpec((None, hd, hd), lambda l, c: (jnp.where(c < n_qk_blocks, 0, 1), 0, 0)),
        ],
        out_specs=pl.BlockSpec((None, d, hd), lambda l, c: (l, 0, c)),
        out_shape=jax.ShapeDtypeStruct((depth, d, n), BF16),
        compiler_params=_cparams(("arbitrary", "arbitrary")),
        name="prep_w_in",
    )(w_in, jnp.asarray(perm, BF16))


def _in_proj(x2, w_bf, layer, cos, sin, seq):
    t, d = x2.shape
    n = w_bf.shape[2]
    tiles_per_seq = seq // TM_PROJ
    return pl.pallas_call(
        _in_proj_kernel,
        grid=(t // TM_PROJ, n // TN_PROJ),
        in_specs=[
            pl.BlockSpec((TM_PROJ, d), lambda i, j: (i, 0)),
            pl.BlockSpec((None, d, TN_PROJ), lambda i, j: (layer, 0, j)),
            pl.BlockSpec((TM_PROJ, LANES), lambda i, j: (i % tiles_per_seq, 0)),
            pl.BlockSpec((TM_PROJ, LANES), lambda i, j: (i % tiles_per_seq, 0)),
        ],
        out_specs=pl.BlockSpec((TM_PROJ, TN_PROJ), lambda i, j: (i, j)),
        out_shape=jax.ShapeDtypeStruct((t, n), BF16),
        compiler_params=_cparams(("arbitrary", "arbitrary")),
        name="in_proj",
    )(x2, w_bf, cos, sin)


def _ret_kernel(q_ref, k_ref, v_ref, g_ref, mask_ref, qd_ref, kd_ref, o_ref, r_ref):
    n = pl.program_id(1)

    @pl.when(n == 0)
    def _init():
        r_ref[...] = jnp.zeros_like(r_ref)

    for h in range(RET_HEADS):
        cols = slice(h * RET_HEAD_DIM, (h + 1) * RET_HEAD_DIM)
        q = q_ref[:, cols]
        k = k_ref[:, cols]
        v = v_ref[:, cols]
        s = lax.dot_general(q, k, (((1,), (1,)), ((), ())), preferred_element_type=F32)
        s = s * mask_ref[h]
        inner = jnp.dot(s.astype(BF16), v, preferred_element_type=F32)
        qd = qd_ref[h]
        r_prev = r_ref[h]
        cross = jnp.dot((q.astype(F32) * qd).astype(BF16), r_prev.astype(BF16), preferred_element_type=F32)
        o = inner + cross
        kdec = (k.astype(F32) * kd_ref[h]).astype(BF16)
        kv = lax.dot_general(kdec, v, (((0,), (0,)), ((), ())), preferred_element_type=F32)
        r_ref[h] = r_prev * qd[RET_CHUNK - 1:RET_CHUNK, :] + kv
        mu = jnp.mean(o, axis=-1, keepdims=True)
        oc = o - mu
        var = jnp.mean(oc * oc, axis=-1, keepdims=True)
        o_ref[:, cols] = (g_ref[:, cols].astype(F32) * (oc * lax.rsqrt(var + LN_EPS))).astype(BF16)


def _retention_tables():
    c = RET_CHUNK
    log_gamma = np.log(1.0 - 2.0 ** (-5.0 - np.arange(RET_HEADS, dtype=np.float64)))
    idx = np.arange(c, dtype=np.float64)
    diff = idx[:, None] - idx[None, :]
    mask = np.where(diff >= 0, np.exp(log_gamma[:, None, None] * np.maximum(diff, 0.0)), 0.0)
    qd = np.exp(log_gamma[:, None] * (idx + 1.0)[None, :])
    kd = np.exp(log_gamma[:, None] * (c - 1.0 - idx)[None, :])
    bc = lambda a: np.broadcast_to(a[:, :, None], (RET_HEADS, c, RET_HEAD_DIM)).astype(np.float32)
    return mask.astype(np.float32), bc(qd), bc(kd)


def _retention(proj, batch, seq):
    t = proj.shape[0]
    nch = seq // RET_CHUNK
    mask, qd, kd = _retention_tables()
    blk = (RET_CHUNK, RET_WIDTH)
    seg = lambda c: pl.BlockSpec(blk, lambda b, n: (b * nch + n, c))
    whole = lambda a: pl.BlockSpec(a.shape, lambda b, n: (0, 0, 0))
    return pl.pallas_call(
        _ret_kernel,
        grid=(batch, nch),
        in_specs=[seg(0), seg(1), seg(2), seg(3), whole(mask), whole(qd), whole(kd)],
        out_specs=seg(0),
        out_shape=jax.ShapeDtypeStruct((t, RET_WIDTH), BF16),
        scratch_shapes=[pltpu.VMEM((RET_HEADS, RET_HEAD_DIM, RET_HEAD_DIM), F32)],
        compiler_params=_cparams(("arbitrary", "arbitrary")),
        name="retention",
    )(proj, proj, proj, proj, jnp.asarray(mask), jnp.asarray(qd), jnp.asarray(kd))


def _s5_kernel(u_ref, bm_ref, cm_ref, lam_ref, d_ref, y_ref, us_ref, ut_ref, bu_ref, hb_ref, yt_ref, st_ref):
    n = pl.program_id(1)
    tl = S5_TL
    nb = SUBLANES
    ns = S5_NSTATE

    @pl.when(n == 0)
    def _init():
        st_ref[...] = jnp.zeros_like(st_ref)

    for b in range(nb):
        us_ref[b * tl:(b + 1) * tl, :] = u_ref[b].astype(F32)

    for t in range(tl):
        ut_ref[t * nb:(t + 1) * nb, :] = us_ref[pl.ds(t, nb, stride=tl), :]

    ar = jnp.broadcast_to(lam_ref[0:1, :], (nb, ns))
    ai = jnp.broadcast_to(lam_ref[1:2, :], (nb, ns))
    sr = st_ref[0:nb, :]
    si = st_ref[nb:2 * nb, :]

    grows = S5_GROUP * nb

    def project_in(g):
        rows = slice(g * grows, (g + 1) * grows)
        bu_ref[rows, :] = jnp.dot(ut_ref[rows, :].astype(BF16), bm_ref[...], preferred_element_type=F32)

    project_in(0)
    for g in range(tl // S5_GROUP):
        if g + 1 < tl // S5_GROUP:
            project_in(g + 1)
        for tt in range(0, S5_GROUP, 2):
            r0 = g * grows + tt * nb
            pair_r = []
            pair_i = []
            for r in (r0, r0 + nb):
                br = bu_ref[r:r + nb, 0:ns]
                bi = bu_ref[r:r + nb, ns:2 * ns]
                sr, si = ar * sr - ai * si + br, ar * si + ai * sr + bi
                pair_r.append(sr)
                pair_i.append(si)
            hb_ref[r0:r0 + 2 * nb, 0:ns] = jnp.concatenate(pair_r, axis=0).astype(BF16)
            hb_ref[r0:r0 + 2 * nb, ns:2 * ns] = jnp.concatenate(pair_i, axis=0).astype(BF16)
        rows = slice(g * grows, (g + 1) * grows)
        yt_ref[rows, :] = (jnp.dot(hb_ref[rows, :], cm_ref[...], preferred_element_type=F32)
                           + ut_ref[rows, :] * d_ref[...])
    st_ref[0:nb, :] = sr
    st_ref[nb:2 * nb, :] = si
    for b in range(nb):
        y_ref[b] = yt_ref[pl.ds(b, tl, stride=nb), :].astype(BF16)


def _s5_tables(lam_re, lam_im, b_re, b_im, c_re, c_im, d, log_dt):
    lam = lax.complex(lam_re.astype(F32), lam_im.astype(F32))
    dt = jnp.exp(log_dt.astype(F32))[:, None]
    lam_bar = jnp.exp(lam * dt)
    b_bar = ((lam_bar - 1.0) / lam)[..., None] * lax.complex(b_re.astype(F32), b_im.astype(F32))
    gpt = LANES // SSM_GROUP
    eye = jnp.eye(gpt, dtype=F32)

    def bdiag_in(m):
        m = m.reshape(S5_LT, gpt, SSM_STATE, SSM_GROUP)
        return jnp.einsum('jgpi,gh->jgihp', m, eye).reshape(S5_LT, LANES, gpt * SSM_STATE)

    def bdiag_out(m):
        m = m.reshape(S5_LT, gpt, SSM_GROUP, SSM_STATE)
        return jnp.einsum('jgop,gh->jgpho', m, eye).reshape(S5_LT, gpt * SSM_STATE, LANES)

    bmat = jnp.concatenate([bdiag_in(jnp.real(b_bar)), bdiag_in(jnp.imag(b_bar))], axis=-1).astype(BF16)
    cmat = jnp.concatenate([bdiag_out(c_re.astype(F32)), -bdiag_out(c_im.astype(F32))], axis=1).astype(BF16)
    lam_t = jnp.stack([jnp.real(lam_bar).reshape(S5_LT, S5_NSTATE),
                       jnp.imag(lam_bar).reshape(S5_LT, S5_NSTATE)], axis=1)
    d_t = d.astype(F32).reshape(S5_LT, 1, LANES)
    return bmat, cmat, lam_t, d_t


def _s5(proj, tables, layer, batch, seq):
    bmat, cmat, lam_t, d_t = tables
    assert batch == SUBLANES
    proj3 = proj.reshape(batch, seq, IN_PROJ_WIDTH)
    ucol = (4 * RET_WIDTH) // LANES
    tl = S5_TL
    tile = lambda j, n: (layer, j, 0, 0)
    return pl.pallas_call(
        _s5_kernel,
        grid=(S5_LT, seq // tl),
        in_specs=[
            pl.BlockSpec((batch, tl, LANES), lambda j, n: (0, n, ucol + j)),
            pl.BlockSpec((None, None, LANES, 2 * S5_NSTATE), tile),
            pl.BlockSpec((None, None, 2 * S5_NSTATE, LANES), tile),
            pl.BlockSpec((None, None, 2, S5_NSTATE), tile),
            pl.BlockSpec((None, None, 1, LANES), tile),
        ],
        out_specs=pl.BlockSpec((batch, tl, LANES), lambda j, n: (0, n, j)),
        out_shape=jax.ShapeDtypeStruct((batch, seq, SSM_WIDTH), BF16),
        scratch_shapes=[
            pltpu.VMEM((batch * tl, LANES), F32),
            pltpu.VMEM((batch * tl, LANES), F32),
            pltpu.VMEM((batch * tl, 2 * S5_NSTATE), F32),
            pltpu.VMEM((batch * tl, 2 * S5_NSTATE), BF16),
            pltpu.VMEM((batch * tl, LANES), F32),
            pltpu.VMEM((2 * SUBLANES, S5_NSTATE), F32),
        ],
        compiler_params=_cparams(("arbitrary", "arbitrary")),
        name="s5",
    )(proj3, bmat, cmat, lam_t, d_t)


def _layer_norm_rows(r, g, b):
    mu = jnp.mean(r, axis=-1, keepdims=True)
    rc = r - mu
    var = jnp.mean(rc * rc, axis=-1, keepdims=True)
    return rc * lax.rsqrt(var + LN_EPS) * g + b


def _post_mix_kernel(x_ref, ret_ref, y_ref, wglu_ref, wout_ref, lng_ref, lnb_ref,
                     rcat_ref, rhi_ref, rb_ref, tri_ref,
                     x1_ref, x1p_ref, e_ref, w_ref, rank_ref, cnt_ref, carry_ref):
    i = pl.program_id(0)

    @pl.when(i == 0)
    def _init():
        carry_ref[...] = jnp.zeros_like(carry_ref)

    args = (x_ref, ret_ref, y_ref, wglu_ref, wout_ref, lng_ref, lnb_ref, rcat_ref, rhi_ref, rb_ref, tri_ref,
            x1_ref, x1p_ref, e_ref, w_ref, rank_ref, carry_ref)
    a = _post_mix_phases(0, *args)
    b = _post_mix_phases(1, *args)
    nph = len(a)
    a[0]()
    for k in range(1, nph):
        a[k]()
        b[k - 1]()
    b[nph - 1]()
    cnt_ref[...] = carry_ref[...].astype(I32)


def _post_mix_phases(sub, x_ref, ret_ref, y_ref, wglu_ref, wout_ref, lng_ref, lnb_ref,
                     rcat_ref, rhi_ref, rb_ref, tri_ref, x1_ref, x1p_ref, e_ref, w_ref, rank_ref, carry_ref):
    tm = TSUB_POST
    rows = slice(sub * tm, (sub + 1) * tm)
    st = {}

    def gelu():
        st['ya'] = jax.nn.gelu(y_ref[rows, :].astype(F32))

    def glu_matmul():
        st['z'] = jnp.dot(st['ya'].astype(BF16), wglu_ref[...], preferred_element_type=F32)

    def glu_gate():
        st['ssm'] = (st.pop('ya') * _sigmoid(st.pop('z'))).astype(BF16)

    def out_matmul():
        mixed = jnp.concatenate([ret_ref[rows, :], st.pop('ssm')], axis=1)
        st['h'] = jnp.dot(mixed, wout_ref[...], preferred_element_type=F32)

    def norm():
        x1 = _layer_norm_rows(DEEPNORM_ALPHA * x_ref[rows, :] + st.pop('h'), lng_ref[...], lnb_ref[...])
        x1_ref[rows, :] = x1
        _store_token_tiles(x1p_ref, sub * tm, tm, _pack_rows(x1))
        st['xh'] = x1.astype(BF16)
        st['xl'] = (x1 - st['xh'].astype(F32)).astype(BF16)

    def router_matmul():
        nt = (((1,), (1,)), ((), ()))
        l1 = lax.dot_general(rcat_ref[...], st.pop('xh'), nt, preferred_element_type=F32)
        l2 = lax.dot_general(rhi_ref[...], st.pop('xl'), nt, preferred_element_type=F32)
        st['logits'] = l1[0:N_EXPERTS] + l1[N_EXPERTS:] + l2 + rb_ref[...]

    def route():
        _route(st.pop('logits'), rows, tri_ref, e_ref, w_ref, rank_ref, carry_ref)

    return [gelu, glu_matmul, glu_gate, out_matmul, norm, router_matmul, route]


def _route(logits, rows, tri_ref, e_ref, w_ref, rank_ref, carry_ref):
    tm = TSUB_POST
    m = jnp.max(logits, axis=0, keepdims=True)
    ex = jnp.exp(logits - m)
    p = ex / jnp.sum(ex, axis=0, keepdims=True)

    eg = EXPERTS_PER_GROUP
    iota_g = lax.broadcasted_iota(I32, (eg, tm), 0)
    best = None
    for g in range(N_EXPERT_GROUPS):
        pg = p[g * eg:(g + 1) * eg]
        m1 = jnp.max(pg, axis=0, keepdims=True)
        i1 = jnp.min(jnp.where(pg == m1, iota_g, eg), axis=0, keepdims=True)
        pg2 = jnp.where(iota_g == i1, -1.0, pg)
        m2 = jnp.max(pg2, axis=0, keepdims=True)
        i2 = jnp.min(jnp.where(pg2 == m2, iota_g, eg), axis=0, keepdims=True)
        sg = m1 + m2
        if best is None:
            best = (sg, m1, m2, i1, i2)
        else:
            better = sg > best[0]
            cand = (sg, m1, m2, i1 + g * eg, i2 + g * eg)
            best = tuple(jnp.where(better, c, o) for c, o in zip(cand, best))
    _, v1, v2, e1, e2 = best
    tot = v1 + v2
    e_ref[0:1, rows] = e1
    e_ref[1:2, rows] = e2
    w_ref[0:1, rows] = v1 / tot
    w_ref[1:2, rows] = v2 / tot

    iota_e = lax.broadcasted_iota(I32, (N_EXPERTS, tm), 0)
    oh1 = iota_e == e1
    oh2 = iota_e == e2
    oh = jnp.where(oh1, 1.0, jnp.where(oh2, 1.0, 0.0))
    before = jnp.dot(oh.astype(BF16), tri_ref[...], preferred_element_type=F32) + carry_ref[:, 0:1]
    rank_ref[0:1, rows] = jnp.sum(jnp.where(oh1, before, 0.0), axis=0, keepdims=True).astype(I32)
    rank_ref[1:2, rows] = jnp.sum(jnp.where(oh2, before, 0.0), axis=0, keepdims=True).astype(I32)
    carry_ref[...] = carry_ref[...] + jnp.sum(oh, axis=1, keepdims=True)


def _post_mix(x2, ret, y, wglu_bf, wout_bf, lng, lnb, layer, rcat, rhi, rb):
    t, d = x2.shape
    tm = TM_POST
    ts = TSUB_POST
    tri = jnp.asarray(np.triu(np.ones((ts, ts), np.float32), 1), BF16)
    const = lambda i: (0, 0)
    lyr = lambda i: (layer, 0, 0)
    tok = lambda i: (i, 0)
    lane = lambda i: (0, i)
    return pl.pallas_call(
        _post_mix_kernel,
        grid=(t // tm,),
        in_specs=[
            pl.BlockSpec((tm, d), tok),
            pl.BlockSpec((tm, RET_WIDTH), tok),
            pl.BlockSpec((tm, SSM_WIDTH), tok),
            pl.BlockSpec((None, SSM_WIDTH, SSM_WIDTH), lyr),
            pl.BlockSpec((None, RET_WIDTH + SSM_WIDTH, d), lyr),
            pl.BlockSpec((None, 1, d), lyr),
            pl.BlockSpec((None, 1, d), lyr),
            pl.BlockSpec((2 * N_EXPERTS, d), const),
            pl.BlockSpec((N_EXPERTS, d), const),
            pl.BlockSpec((N_EXPERTS, 1), const),
            pl.BlockSpec((ts, ts), const),
        ],
        out_specs=[
            pl.BlockSpec((tm, d), tok),
            pl.BlockSpec((tm * PACK_SUB, LANES), tok),
            pl.BlockSpec((TOP_K, tm), lane),
            pl.BlockSpec((TOP_K, tm), lane),
            pl.BlockSpec((TOP_K, tm), lane),
            pl.BlockSpec((N_EXPERTS, LANES), const),
        ],
        out_shape=[
            jax.ShapeDtypeStruct((t, d), F32),
            jax.ShapeDtypeStruct((t * PACK_SUB, LANES), U32),
            jax.ShapeDtypeStruct((TOP_K, t), I32),
            jax.ShapeDtypeStruct((TOP_K, t), F32),
            jax.ShapeDtypeStruct((TOP_K, t), I32),
            jax.ShapeDtypeStruct((N_EXPERTS, LANES), I32),
        ],
        scratch_shapes=[pltpu.VMEM((N_EXPERTS, LANES), F32)],
        compiler_params=_cparams(("arbitrary",)),
        name="post_mix",
    )(x2, ret, y, wglu_bf, wout_bf, lng, lnb, rcat, rhi, rb, tri)


def _dispatch_kernel(zrow_ref, d0_ref, d1_ref, x_ref, xb_hbm, zbuf_ref, sem, zsem):
    tm = TM_DISP

    @pl.when(pl.program_id(0) == 0)
    def _zero_fill():
        zbuf_ref[...] = jnp.zeros_like(zbuf_ref)

        def zcopy(k):
            n = ROW_TILE * PACK_SUB
            rows = pl.ds(pl.multiple_of(zrow_ref[k] * PACK_SUB, n), n)
            return pltpu.make_async_copy(zbuf_ref, xb_hbm.at[rows], zsem.at[0])

        def start(k, c):
            @pl.when(zrow_ref[k] >= 0)
            def _():
                zcopy(k).start()
            return c

        def wait(k, c):
            @pl.when(zrow_ref[k] >= 0)
            def _():
                zcopy(k).wait()
            return c

        lax.fori_loop(0, 2 * N_EXPERTS, start, 0)
        lax.fori_loop(0, 2 * N_EXPERTS, wait, 0)

    def issue(tb, c):
        for u in range(ISSUE_UNROLL):
            t = tb * ISSUE_UNROLL + u
            src = x_ref.at[pl.ds(t * PACK_SUB, PACK_SUB)]
            tile = lambda r: pl.ds(pl.multiple_of(r, PACK_SUB), PACK_SUB)
            pltpu.make_async_copy(src, xb_hbm.at[tile(d0_ref[0, 0, t])], sem.at[0]).start(priority=0)
            pltpu.make_async_copy(src, xb_hbm.at[tile(d1_ref[0, 0, t])], sem.at[0]).start(priority=1)
        return c

    lax.fori_loop(0, tm // ISSUE_UNROLL, issue, 0)
    for _ in range(TOP_K):
        pltpu.make_async_copy(x_ref, xb_hbm.at[pl.ds(0, tm * PACK_SUB)], sem.at[0]).wait()


def _dispatch(x1p, dest_tile, zrow, n_rows):
    t = x1p.shape[0] // PACK_SUB
    tm = TM_DISP
    nstep = t // tm
    d0 = dest_tile[0].reshape(nstep, 1, tm)
    d1 = dest_tile[1].reshape(nstep, 1, tm)
    smem_blk = pl.BlockSpec((1, 1, tm), lambda i, z: (i, 0, 0), memory_space=pltpu.SMEM)
    grid_spec = pltpu.PrefetchScalarGridSpec(
        num_scalar_prefetch=1,
        grid=(nstep,),
        in_specs=[smem_blk, smem_blk, pl.BlockSpec((tm * PACK_SUB, LANES), lambda i, z: (i, 0))],
        out_specs=pl.BlockSpec(memory_space=pl.ANY),
        scratch_shapes=[pltpu.VMEM((ROW_TILE * PACK_SUB, LANES), U32),
                        pltpu.SemaphoreType.DMA((1,)), pltpu.SemaphoreType.DMA((1,))],
    )
    return pl.pallas_call(
        _dispatch_kernel,
        grid_spec=grid_spec,
        out_shape=jax.ShapeDtypeStruct((n_rows * PACK_SUB, LANES), U32),
        compiler_params=_cparams(("arbitrary",)),
        name="dispatch",
    )(zrow, d0, d1, x1p)


def _experts_kernel(se_ref, row_ref, nt_ref, nz_ref, xb_hbm, wg_ref, wu_ref, wd_ref, yb_hbm,
                    xs_ref, acc_ref, stg_in, stg_out, sem_in, sem_out):
    del se_ref
    s = pl.program_id(0)
    j = pl.program_id(1)
    nt = nt_ref[s]
    nz = nz_ref[s]
    row0 = row_ref[s]
    rt = ROW_TILE

    nsuper = pl.num_programs(0)
    s_next = jnp.minimum(s + 1, nsuper - 1)
    prev_nt = jnp.where(s > 0, nt_ref[jnp.maximum(s - 1, 0)], 0)
    next_nt = jnp.where(s + 1 < nsuper, nt_ref[s_next], 0)

    def rows(i, base=None):
        n = rt * PACK_SUB
        base = row0 if base is None else base
        return pl.ds(pl.multiple_of((base + i * rt) * PACK_SUB, n), n)

    def in_copy(i, slot):
        return pltpu.make_async_copy(xb_hbm.at[rows(i)], stg_in.at[slot], sem_in.at[slot])

    def out_copy(i, slot):
        return pltpu.make_async_copy(stg_out.at[slot], yb_hbm.at[rows(i)], sem_out.at[slot])

    def partial_out(i, ntiles):
        r = pl.multiple_of(i * rt, rt)
        xi = xs_ref[pl.ds(r, ntiles * rt), :]
        g = jnp.dot(xi, wg_ref[...].astype(BF16), preferred_element_type=F32)
        u = jnp.dot(xi, wu_ref[...].astype(BF16), preferred_element_type=F32)
        hj = (g * _sigmoid(g) * u).astype(BF16)
        return r, jnp.dot(hj, wd_ref[...].astype(BF16), preferred_element_type=F32)

    def load_tile(i):
        slot = i % 2

        @pl.when(i + 1 < nt)
        def _():
            in_copy(i + 1, 1 - slot).start()

        in_copy(i, slot).wait()
        xrows = pl.ds(pl.multiple_of(i * rt, rt), rt)
        for c, words in enumerate(_load_token_tiles(stg_in.at[slot], 0, rt)):
            xs_ref[xrows, c * LANES:(c + 1) * LANES] = _unpack_lo(words).astype(BF16)
            xs_ref[xrows, PACK_WORDS + c * LANES:PACK_WORDS + (c + 1) * LANES] = _unpack_hi(words).astype(BF16)

    def store_tile(i, vals):
        slot = i % 2

        @pl.when(i >= 2)
        def _():
            out_copy(i - 2, slot).wait()

        _store_token_tiles(stg_out.at[slot], 0, rt, _pack_rows(vals))
        out_copy(i, slot).start()

    def for_tiles(body):
        def pair(p, c):
            body(2 * p, 2)
            return c

        lax.fori_loop(0, nt // 2, pair, 0)

        @pl.when(nt % 2 == 1)
        def _():
            body(nt - 1, 1)

    @pl.when((nz > 0) & (j == 0))
    def _zero_tail():
        stg_out[0] = jnp.zeros(stg_out.shape[1:], U32)

        def start(i, c):
            out_copy(i, 0).start()
            return c

        def wait(i, c):
            out_copy(i, 0).wait()
            return c

        lax.fori_loop(0, nz, start, 0)
        lax.fori_loop(0, nz, wait, 0)

    @pl.when(nt > 0)
    def _work():
        @pl.when(j == 0)
        def _first():
            @pl.when(prev_nt == 0)
            def _():
                in_copy(0, 0).start()

            @pl.when(prev_nt >= 2)
            def _():
                out_copy(0, prev_nt % 2).wait()

            @pl.when(prev_nt >= 1)
            def _():
                out_copy(0, (prev_nt - 1) % 2).wait()

            def body(i, ntiles):
                for k in range(ntiles):
                    load_tile(i + k)
                r, part = partial_out(i, ntiles)
                acc_ref[pl.ds(r, ntiles * rt), :] = part

            for_tiles(body)

        @pl.when((j > 0) & (j < N_DE_CHUNKS - 1))
        def _mid():
            def body(i, ntiles):
                r, part = partial_out(i, ntiles)
                acc_ref[pl.ds(r, ntiles * rt), :] += part

            for_tiles(body)

        @pl.when(j == N_DE_CHUNKS - 1)
        def _last():
            @pl.when(next_nt > 0)
            def _():
                pltpu.make_async_copy(xb_hbm.at[rows(0, row_ref[s_next])], stg_in.at[0], sem_in.at[0]).start()

            def body(i, ntiles):
                r, part = partial_out(i, ntiles)
                total = acc_ref[pl.ds(r, ntiles * rt), :] + part
                for k in range(ntiles):
                    store_tile(i + k, total[k * rt:(k + 1) * rt, :])

            for_tiles(body)

            @pl.when(next_nt == 0)
            def _drain():
                @pl.when(nt >= 2)
                def _():
                    out_copy(nt - 2, nt % 2).wait()

                out_copy(nt - 1, (nt - 1) % 2).wait()


def _experts(xb, se, row, ntl, nzl, w_gate, w_up, w_down, layer):
    p = xb.shape[0] // PACK_SUB
    d = D_MODEL
    nsuper = se.shape[0]
    last = N_DE_CHUNKS - 1
    assert last >= 1

    def jj(s, j, nt_ref):
        return jnp.where(nt_ref[s] > 0, j, last)

    grid_spec = pltpu.PrefetchScalarGridSpec(
        num_scalar_prefetch=4,
        grid=(nsuper, N_DE_CHUNKS),
        in_specs=[
            pl.BlockSpec(memory_space=pl.ANY),
            pl.BlockSpec((None, None, d, DE_CHUNK),
                         lambda s, j, se_r, row_r, nt_r, nz_r: (layer, se_r[s], 0, jj(s, j, nt_r))),
            pl.BlockSpec((None, None, d, DE_CHUNK),
                         lambda s, j, se_r, row_r, nt_r, nz_r: (layer, se_r[s], 0, jj(s, j, nt_r))),
            pl.BlockSpec((None, None, DE_CHUNK, d),
                         lambda s, j, se_r, row_r, nt_r, nz_r: (layer, se_r[s], jj(s, j, nt_r), 0)),
        ],
        out_specs=pl.BlockSpec(memory_space=pl.ANY),
        scratch_shapes=[
            pltpu.VMEM((SUPER, d), BF16),
            pltpu.VMEM((SUPER, d), F32),
            pltpu.VMEM((2, ROW_TILE * PACK_SUB, LANES), U32),
            pltpu.VMEM((2, ROW_TILE * PACK_SUB, LANES), U32),
            pltpu.SemaphoreType.DMA((2,)),
            pltpu.SemaphoreType.DMA((2,)),
        ],
    )
    return pl.pallas_call(
        _experts_kernel,
        grid_spec=grid_spec,
        out_shape=jax.ShapeDtypeStruct((p * PACK_SUB, LANES), U32),
        compiler_params=_cparams(("arbitrary", "arbitrary"), vmem=EXPERTS_VMEM_LIMIT),
        name="experts",
    )(se, row, ntl, nzl, xb, w_gate, w_up, w_down)


def _combine_kernel(d0_ref, d1_ref, d0n_ref, d1n_ref, x1_ref, w_ref, lng_ref, lnb_ref, yb_hbm,
                    o_ref, ybuf_ref, sem):
    tm = TM_COMB
    i = pl.program_id(0)
    slot = i % 2

    def gather(da_ref, db_ref, into):
        def issue(tb, c):
            for u in range(ISSUE_UNROLL):
                t = tb * ISSUE_UNROLL + u
                dst = pl.ds(pl.multiple_of(t * PACK_SUB, PACK_SUB), PACK_SUB)
                tile = lambda r: pl.ds(pl.multiple_of(r, PACK_SUB), PACK_SUB)
                pltpu.make_async_copy(yb_hbm.at[tile(da_ref[0, 0, t])], ybuf_ref.at[into, 0, dst],
                                      sem.at[into]).start(priority=0)
                pltpu.make_async_copy(yb_hbm.at[tile(db_ref[0, 0, t])], ybuf_ref.at[into, 1, dst],
                                      sem.at[into]).start(priority=1)
            return c

        lax.fori_loop(0, tm // ISSUE_UNROLL, issue, 0)

    @pl.when(i == 0)
    def _prime():
        gather(d0_ref, d1_ref, 0)

    for k in range(TOP_K):
        pltpu.make_async_copy(ybuf_ref.at[slot, k], ybuf_ref.at[slot, k], sem.at[slot]).wait()

    @pl.when(i + 1 < pl.num_programs(0))
    def _prefetch():
        gather(d0n_ref, d1n_ref, 1 - slot)

    wpad = jnp.concatenate([w_ref[...], jnp.zeros((LANES - TOP_K, tm), F32)], axis=0)
    wt = wpad.T
    w0 = wt[:, 0:1]
    w1 = wt[:, 1:2]
    x1 = x1_ref[...]
    y0 = _load_token_tiles(ybuf_ref.at[slot, 0], 0, tm)
    y1 = _load_token_tiles(ybuf_ref.at[slot, 1], 0, tm)
    lo = [_unpack_lo(a) * w0 + _unpack_lo(b) * w1 for a, b in zip(y0, y1)]
    hi = [_unpack_hi(a) * w0 + _unpack_hi(b) * w1 for a, b in zip(y0, y1)]
    moe = jnp.concatenate(lo + hi, axis=1)
    o_ref[...] = _layer_norm_rows(DEEPNORM_ALPHA * x1 + moe, lng_ref[...], lnb_ref[...])


def _combine(x1, dest_tile, wts, lng, lnb, layer, yb):
    t, d = x1.shape
    tm = TM_COMB
    nstep = t // tm
    d0 = dest_tile[0].reshape(nstep, 1, tm)
    d1 = dest_tile[1].reshape(nstep, 1, tm)
    smem_blk = pl.BlockSpec((1, 1, tm), lambda i: (i, 0, 0), memory_space=pltpu.SMEM)
    smem_next = pl.BlockSpec((1, 1, tm), lambda i: (jnp.minimum(i + 1, nstep - 1), 0, 0),
                             memory_space=pltpu.SMEM)
    lyr = lambda i: (layer, 0, 0)
    return pl.pallas_call(
        _combine_kernel,
        grid=(nstep,),
        in_specs=[smem_blk, smem_blk, smem_next, smem_next,
                  pl.BlockSpec((tm, d), lambda i: (i, 0)),
                  pl.BlockSpec((TOP_K, tm), lambda i: (0, i)),
                  pl.BlockSpec((None, 1, d), lyr),
                  pl.BlockSpec((None, 1, d), lyr),
                  pl.BlockSpec(memory_space=pl.ANY)],
        out_specs=pl.BlockSpec((tm, d), lambda i: (i, 0)),
        out_shape=jax.ShapeDtypeStruct((t, d), F32),
        scratch_shapes=[pltpu.VMEM((2, TOP_K, tm * PACK_SUB, LANES), U32), pltpu.SemaphoreType.DMA((2,))],
        compiler_params=_cparams(("arbitrary",)),
        name="combine",
    )(d0, d1, d0, d1, x1, wts, lng, lnb, yb)


def _rotary_tables(seq):
    inv = 1.0 / (ROPE_BASE ** (jnp.arange(0, RET_HEAD_DIM, 2, dtype=F32) / RET_HEAD_DIM))
    ang = jnp.arange(seq, dtype=F32)[:, None] * inv[None, :]
    return jnp.cos(ang), jnp.sin(ang)


def _routing_tables(counts, n_rows):
    ids = np.arange(N_EXPERTS)
    upto = jnp.asarray(ids[None, :] <= ids[:, None])
    csum = lambda v: jnp.sum(jnp.where(upto, v[None, :], 0), axis=1)
    padded = (counts + ROW_TILE - 1) // ROW_TILE * ROW_TILE
    pend = csum(padded)
    pstart = pend - padded
    ntile = padded // ROW_TILE
    nsup = (ntile + TILES_PER_SUPER - 1) // TILES_PER_SUPER
    send = csum(nsup)
    sstart = send - nsup
    nsuper = -(-n_rows // SUPER) + N_EXPERTS
    s = jnp.arange(nsuper, dtype=I32)
    total = send[-1]
    which = lambda q: jnp.minimum(jnp.sum((send[None, :] <= q[:, None]).astype(I32), axis=1), N_EXPERTS - 1)
    pick = lambda v, e: jnp.sum(jnp.where(e[:, None] == ids[None, :], v[None, :], 0), axis=1)
    se = which(s)
    valid = s < total
    k = s - pick(sstart, se)
    row = pick(pstart, se) + k * SUPER
    nt = jnp.clip(pick(ntile, se) - k * TILES_PER_SUPER, 0, TILES_PER_SUPER)
    idle = s - total
    tail_row = pend[-1] + idle * SUPER
    nz = jnp.clip((n_rows - tail_row) // ROW_TILE, 0, TILES_PER_SUPER)
    se = jnp.where(valid, se, which((total - 1)[None])[0])
    row = jnp.where(valid, row, jnp.minimum(tail_row, n_rows - ROW_TILE))
    nt = jnp.where(valid, nt, 0)
    nz = jnp.where(valid, 0, nz)
    seg_last = jnp.where(padded > 0, pend - ROW_TILE, -1)
    tail = pend[-1] + jnp.arange(N_EXPERTS, dtype=I32) * ROW_TILE
    zrow = jnp.concatenate([seg_last, jnp.where(tail < n_rows, tail, -1)])
    return (pstart.astype(I32), se.astype(I32), row.astype(I32), nt.astype(I32), nz.astype(I32),
            zrow.astype(I32))


def kernel(x, w_in, w_out, ssm_lambda_re, ssm_lambda_im, ssm_b_re, ssm_b_im, ssm_c_re, ssm_c_im,
           ssm_d, ssm_log_dt, w_glu, ln1_g, ln1_b, ln2_g, ln2_b, router_w, router_b,
           w_gate, w_up, w_down):
    batch, seq, d = x.shape
    t = batch * seq
    n_assign = t * TOP_K
    n_rows = n_assign + N_EXPERTS * ROW_TILE
    assert seq % TM_PROJ == 0 and seq % RET_CHUNK == 0 and seq % S5_TL == 0
    assert t % TM_POST == 0 and t % TM_COMB == 0 and t % TM_DISP == 0 and n_rows % ROW_TILE == 0

    cos, sin = _rotary_tables(seq)
    expert_ids = jnp.arange(N_EXPERTS, dtype=I32)
    rw_t = router_w.astype(F32).T
    rhi = rw_t.astype(BF16)
    rlo = (rw_t - rhi.astype(F32)).astype(BF16)
    rcat = jnp.concatenate([rhi, rlo], axis=0)
    rb = router_b.astype(F32).reshape(N_EXPERTS, 1)

    w_in_bf = _prep_in_proj_weights(w_in)
    s5_tables = jax.vmap(_s5_tables)(ssm_lambda_re, ssm_lambda_im, ssm_b_re, ssm_b_im,
                                     ssm_c_re, ssm_c_im, ssm_d, ssm_log_dt)
    w_glu_bf = w_glu.astype(BF16)
    w_out_bf = w_out.astype(BF16)
    row3 = lambda p: p.astype(F32).reshape(DEPTH, 1, d)
    ln1_g, ln1_b, ln2_g, ln2_b = row3(ln1_g), row3(ln1_b), row3(ln2_g), row3(ln2_b)

    x2 = x.reshape(t, d)
    for l in range(DEPTH):
        proj = _in_proj(x2, w_in_bf, l, cos, sin, seq)
        ret = _retention(proj, batch, seq)
        y = _s5(proj, s5_tables, l, batch, seq).reshape(t, SSM_WIDTH)
        x1, x1p, e, wts, rank, cnt = _post_mix(
            x2, ret, y, w_glu_bf, w_out_bf, ln1_g, ln1_b, l, rcat, rhi, rb)
        pstart, se, row, ntl, nzl, zrow = _routing_tables(cnt[:, 0], n_rows)
        dest = rank + jnp.sum(jnp.where(e[..., None] == expert_ids, pstart, 0), axis=-1)
        dest_tile = dest * PACK_SUB
        xb = _dispatch(x1p, dest_tile, zrow, n_rows)
        yb = _experts(xb, se, row, ntl, nzl, w_gate, w_up, w_down, l)
        x2 = _combine(x1, dest_tile, wts, ln2_g, ln2_b, l, yb)
    return x2.reshape(batch, seq, d)
```

```python
import functools
import math

import numpy as np
import jax
import jax.numpy as jnp
from jax import lax
from jax.experimental import pallas as pl
from jax.experimental.pallas import tpu as pltpu

F32 = jnp.float32
BF16 = jnp.bfloat16
I32 = jnp.int32

D_MODEL = 2048
DEPTH = 2
RET_WIDTH = 1024
SSM_WIDTH = 1024
RET_HEAD_DIM = 256
RET_HEADS = RET_WIDTH // RET_HEAD_DIM
ROPE_BASE = 10000.0
SSM_GROUP = 16
SSM_GROUPS = SSM_WIDTH // SSM_GROUP
SSM_STATE = 64
IN_PROJ_WIDTH = 4 * RET_WIDTH + SSM_WIDTH
N_EXPERTS = 32
N_EXPERT_GROUPS = 4
EXPERTS_PER_GROUP = N_EXPERTS // N_EXPERT_GROUPS
TOP_K = 2
D_EXPERT = D_MODEL // 2
LN_EPS = 1e-5
DEEPNORM_ALPHA = (2.0 * DEPTH) ** 0.25

LANES = 128
SUBLANES = 8
VMEM_LIMIT = 56 * 1024 * 1024
EXPERTS_VMEM_LIMIT = 60 * 1024 * 1024

TM_PROJ = 1024
TN_PROJ = 1024
RET_CHUNK = 256
S5_TL = 256
S5_GROUP = 16
S5_LT = SSM_WIDTH // LANES
S5_NSTATE = (LANES // SSM_GROUP) * SSM_STATE
TM_POST = 512
TSUB_POST = 256
TM_COMB = 256
ROW_TILE = 256
TILES_PER_SUPER = 6
SUPER = ROW_TILE * TILES_PER_SUPER
DE_CHUNK = 512
N_DE_CHUNKS = D_EXPERT // DE_CHUNK
TM_DISP = 1024
ISSUE_UNROLL = 8


def _sigmoid(x):
    return 1.0 / (1.0 + jnp.exp(-x))


def _cparams(sem, vmem=VMEM_LIMIT):
    return pltpu.CompilerParams(dimension_semantics=sem, vmem_limit_bytes=vmem)


PACK_WORDS = D_MODEL // 2
PACK_SUB = PACK_WORDS // LANES
U32 = jnp.uint32


def _pack_rows(x):
    bits = lambda v: lax.bitcast_convert_type(v.astype(BF16).astype(F32), U32)
    return (bits(x[:, :PACK_WORDS]) >> 16) | (bits(x[:, PACK_WORDS:]) & jnp.uint32(0xFFFF0000))


def _unpack_lo(w):
    return lax.bitcast_convert_type(w << 16, F32)


def _unpack_hi(w):
    return lax.bitcast_convert_type(w & jnp.uint32(0xFFFF0000), F32)


def _store_token_tiles(ref, row0, n, packed):
    for c in range(PACK_SUB):
        ref[pl.ds(row0 * PACK_SUB + c, n, stride=PACK_SUB), :] = packed[:, c * LANES:(c + 1) * LANES]


def _load_token_tiles(ref, row0, n):
    return [ref[pl.ds(row0 * PACK_SUB + c, n, stride=PACK_SUB), :] for c in range(PACK_SUB)]


def _in_proj_kernel(x_ref, w_ref, cos_ref, sin_ref, o_ref):
    j = pl.program_id(1)
    acc = jnp.dot(x_ref[...].astype(BF16), w_ref[...], preferred_element_type=F32)
    is_rot = j < 2
    is_gate = j == 3
    scale = jnp.where(j == 1, RET_HEAD_DIM ** -0.5, 1.0).astype(F32)
    c = jnp.where(is_rot, cos_ref[...] * scale, 1.0)
    s = jnp.where(is_rot, sin_ref[...] * scale, 0.0)
    half = RET_HEAD_DIM // 2
    for h in range(RET_HEADS):
        lo = h * RET_HEAD_DIM
        t1 = acc[:, lo:lo + half]
        t2 = acc[:, lo + half:lo + RET_HEAD_DIM]
        r1 = t1 * c - t2 * s
        r2 = t1 * s + t2 * c
        o_ref[:, lo:lo + half] = (r1 * jnp.where(is_gate, _sigmoid(r1), 1.0)).astype(BF16)
        o_ref[:, lo + half:lo + RET_HEAD_DIM] = (r2 * jnp.where(is_gate, _sigmoid(r2), 1.0)).astype(BF16)


def _prep_kernel(w_ref, p_ref, o_ref):
    o_ref[...] = jnp.dot(w_ref[...].astype(BF16), p_ref[...], preferred_element_type=F32).astype(BF16)


def _prep_in_proj_weights(w_in):
    depth, d, n = w_in.shape
    hd = RET_HEAD_DIM
    half = hd // 2
    perm = np.zeros((2, hd, hd), np.float32)
    for i in range(half):
        perm[0, 2 * i, i] = 1.0
        perm[0, 2 * i + 1, half + i] = 1.0
    perm[1] = np.eye(hd, dtype=np.float32)
    n_qk_blocks = 2 * RET_WIDTH // hd
    return pl.pallas_call(
        _prep_kernel,
        grid=(depth, n // hd),
        in_specs=[
            pl.BlockSpec((None, d, hd), lambda l, c: (l, 0, c)),
            pl.BlockSpec((None, hd, hd), lambda l, c: (jnp.where(c < n_qk_blocks, 0, 1), 0, 0)),
        ],
        out_specs=pl.BlockSpec((None, d, hd), lambda l, c: (l, 0, c)),
        out_shape=jax.ShapeDtypeStruct((depth, d, n), BF16),
        compiler_params=_cparams(("arbitrary", "arbitrary")),
        name="prep_w_in",
    )(w_in, jnp.asarray(perm, BF16))


def _in_proj(x2, w_bf, layer, cos, sin, seq):
    t, d = x2.shape
    n = w_bf.shape[2]
    tiles_per_seq = seq // TM_PROJ
    return pl.pallas_call(
        _in_proj_kernel,
        grid=(t // TM_PROJ, n // TN_PROJ),
        in_specs=[
            pl.BlockSpec((TM_PROJ, d), lambda i, j: (i, 0)),
            pl.BlockSpec((None, d, TN_PROJ), lambda i, j: (layer, 0, j)),
            pl.BlockSpec((TM_PROJ, LANES), lambda i, j: (i % tiles_per_seq, 0)),
            pl.BlockSpec((TM_PROJ, LANES), lambda i, j: (i % tiles_per_seq, 0)),
        ],
        out_specs=pl.BlockSpec((TM_PROJ, TN_PROJ), lambda i, j: (i, j)),
        out_shape=jax.ShapeDtypeStruct((t, n), BF16),
        compiler_params=_cparams(("arbitrary", "arbitrary")),
        name="in_proj",
    )(x2, w_bf, cos, sin)


def _ret_kernel(q_ref, k_ref, v_ref, g_ref, mask_ref, qd_ref, kd_ref, o_ref, r_ref):
    n = pl.program_id(1)

    @pl.when(n == 0)
    def _init():
        r_ref[...] = jnp.zeros_like(r_ref)

    for h in range(RET_HEADS):
        cols = slice(h * RET_HEAD_DIM, (h + 1) * RET_HEAD_DIM)
        q = q_ref[:, cols]
        k = k_ref[:, cols]
        v = v_ref[:, cols]
        s = lax.dot_general(q, k, (((1,), (1,)), ((), ())), preferred_element_type=F32)
        s = s * mask_ref[h]
        inner = jnp.dot(s.astype(BF16), v, preferred_element_type=F32)
        qd = qd_ref[h]
        r_prev = r_ref[h]
        cross = jnp.dot((q.astype(F32) * qd).astype(BF16), r_prev.astype(BF16), preferred_element_type=F32)
        o = inner + cross
        kdec = (k.astype(F32) * kd_ref[h]).astype(BF16)
        kv = lax.dot_general(kdec, v, (((0,), (0,)), ((), ())), preferred_element_type=F32)
        r_ref[h] = r_prev * qd[RET_CHUNK - 1:RET_CHUNK, :] + kv
        mu = jnp.mean(o, axis=-1, keepdims=True)
        oc = o - mu
        var = jnp.mean(oc * oc, axis=-1, keepdims=True)
        o_ref[:, cols] = (g_ref[:, cols].astype(F32) * (oc * lax.rsqrt(var + LN_EPS))).astype(BF16)


def _retention_tables():
    c = RET_CHUNK
    log_gamma = np.log(1.0 - 2.0 ** (-5.0 - np.arange(RET_HEADS, dtype=np.float64)))
    idx = np.arange(c, dtype=np.float64)
    diff = idx[:, None] - idx[None, :]
    mask = np.where(diff >= 0, np.exp(log_gamma[:, None, None] * np.maximum(diff, 0.0)), 0.0)
    qd = np.exp(log_gamma[:, None] * (idx + 1.0)[None, :])
    kd = np.exp(log_gamma[:, None] * (c - 1.0 - idx)[None, :])
    bc = lambda a: np.broadcast_to(a[:, :, None], (RET_HEADS, c, RET_HEAD_DIM)).astype(np.float32)
    return mask.astype(np.float32), bc(qd), bc(kd)


def _retention(proj, batch, seq):
    t = proj.shape[0]
    nch = seq // RET_CHUNK
    mask, qd, kd = _retention_tables()
    blk = (RET_CHUNK, RET_WIDTH)
    seg = lambda c: pl.BlockSpec(blk, lambda b, n: (b * nch + n, c))
    whole = lambda a: pl.BlockSpec(a.shape, lambda b, n: (0, 0, 0))
    return pl.pallas_call(
        _ret_kernel,
        grid=(batch, nch),
        in_specs=[seg(0), seg(1), seg(2), seg(3), whole(mask), whole(qd), whole(kd)],
        out_specs=seg(0),
        out_shape=jax.ShapeDtypeStruct((t, RET_WIDTH), BF16),
        scratch_shapes=[pltpu.VMEM((RET_HEADS, RET_HEAD_DIM, RET_HEAD_DIM), F32)],
        compiler_params=_cparams(("arbitrary", "arbitrary")),
        name="retention",
    )(proj, proj, proj, proj, jnp.asarray(mask), jnp.asarray(qd), jnp.asarray(kd))


def _s5_kernel(u_ref, bm_ref, cm_ref, lam_ref, d_ref, y_ref, us_ref, ut_ref, bu_ref, hb_ref, yt_ref, st_ref):
    n = pl.program_id(1)
    tl = S5_TL
    nb = SUBLANES
    ns = S5_NSTATE

    @pl.when(n == 0)
    def _init():
        st_ref[...] = jnp.zeros_like(st_ref)

    for b in range(nb):
        us_ref[b * tl:(b + 1) * tl, :] = u_ref[b].astype(F32)

    for t in range(tl):
        ut_ref[t * nb:(t + 1) * nb, :] = us_ref[pl.ds(t, nb, stride=tl), :]

    ar = jnp.broadcast_to(lam_ref[0:1, :], (nb, ns))
    ai = jnp.broadcast_to(lam_ref[1:2, :], (nb, ns))
    sr = st_ref[0:nb, :]
    si = st_ref[nb:2 * nb, :]

    grows = S5_GROUP * nb

    def project_in(g):
        rows = slice(g * grows, (g + 1) * grows)
        bu_ref[rows, :] = jnp.dot(ut_ref[rows, :].astype(BF16), bm_ref[...], preferred_element_type=F32)

    project_in(0)
    for g in range(tl // S5_GROUP):
        if g + 1 < tl // S5_GROUP:
            project_in(g + 1)
        for tt in range(0, S5_GROUP, 2):
            r0 = g * grows + tt * nb
            pair_r = []
            pair_i = []
            for r in (r0, r0 + nb):
                br = bu_ref[r:r + nb, 0:ns]
                bi = bu_ref[r:r + nb, ns:2 * ns]
                sr, si = ar * sr - ai * si + br, ar * si + ai * sr + bi
                pair_r.append(sr)
                pair_i.append(si)
            hb_ref[r0:r0 + 2 * nb, 0:ns] = jnp.concatenate(pair_r, axis=0).astype(BF16)
            hb_ref[r0:r0 + 2 * nb, ns:2 * ns] = jnp.concatenate(pair_i, axis=0).astype(BF16)
        rows = slice(g * grows, (g + 1) * grows)
        yt_ref[rows, :] = (jnp.dot(hb_ref[rows, :], cm_ref[...], preferred_element_type=F32)
                           + ut_ref[rows, :] * d_ref[...])
    st_ref[0:nb, :] = sr
    st_ref[nb:2 * nb, :] = si
    for b in range(nb):
        y_ref[b] = yt_ref[pl.ds(b, tl, stride=nb), :].astype(BF16)


def _s5_tables(lam_re, lam_im, b_re, b_im, c_re, c_im, d, log_dt):
    lam = lax.complex(lam_re.astype(F32), lam_im.astype(F32))
    dt = jnp.exp(log_dt.astype(F32))[:, None]
    lam_bar = jnp.exp(lam * dt)
    b_bar = ((lam_bar - 1.0) / lam)[..., None] * lax.complex(b_re.astype(F32), b_im.astype(F32))
    gpt = LANES // SSM_GROUP
    eye = jnp.eye(gpt, dtype=F32)

    def bdiag_in(m):
        m = m.reshape(S5_LT, gpt, SSM_STATE, SSM_GROUP)
        return jnp.einsum('jgpi,gh->jgihp', m, eye).reshape(S5_LT, LANES, gpt * SSM_STATE)

    def bdiag_out(m):
        m = m.reshape(S5_LT, gpt, SSM_GROUP, SSM_STATE)
        return jnp.einsum('jgop,gh->jgpho', m, eye).reshape(S5_LT, gpt * SSM_STATE, LANES)

    bmat = jnp.concatenate([bdiag_in(jnp.real(b_bar)), bdiag_in(jnp.imag(b_bar))], axis=-1).astype(BF16)
    cmat = jnp.concatenate([bdiag_out(c_re.astype(F32)), -bdiag_out(c_im.astype(F32))], axis=1).astype(BF16)
    lam_t = jnp.stack([jnp.real(lam_bar).reshape(S5_LT, S5_NSTATE),
                       jnp.imag(lam_bar).reshape(S5_LT, S5_NSTATE)], axis=1)
    d_t = d.astype(F32).reshape(S5_LT, 1, LANES)
    return bmat, cmat, lam_t, d_t


def _s5(proj, tables, layer, batch, seq):
    bmat, cmat, lam_t, d_t = tables
    assert batch == SUBLANES
    proj3 = proj.reshape(batch, seq, IN_PROJ_WIDTH)
    ucol = (4 * RET_WIDTH) // LANES
    tl = S5_TL
    tile = lambda j, n: (layer, j, 0, 0)
    return pl.pallas_call(
        _s5_kernel,
        grid=(S5_LT, seq // tl),
        in_specs=[
            pl.BlockSpec((batch, tl, LANES), lambda j, n: (0, n, ucol + j)),
            pl.BlockSpec((None, None, LANES, 2 * S5_NSTATE), tile),
            pl.BlockSpec((None, None, 2 * S5_NSTATE, LANES), tile),
            pl.BlockSpec((None, None, 2, S5_NSTATE), tile),
            pl.BlockSpec((None, None, 1, LANES), tile),
        ],
        out_specs=pl.BlockSpec((batch, tl, LANES), lambda j, n: (0, n, j)),
        out_shape=jax.ShapeDtypeStruct((batch, seq, SSM_WIDTH), BF16),
        scratch_shapes=[
            pltpu.VMEM((batch * tl, LANES), F32),
            pltpu.VMEM((batch * tl, LANES), F32),
            pltpu.VMEM((batch * tl, 2 * S5_NSTATE), F32),
            pltpu.VMEM((batch * tl, 2 * S5_NSTATE), BF16),
            pltpu.VMEM((batch * tl, LANES), F32),
            pltpu.VMEM((2 * SUBLANES, S5_NSTATE), F32),
        ],
        compiler_params=_cparams(("arbitrary", "arbitrary")),
        name="s5",
    )(proj3, bmat, cmat, lam_t, d_t)


def _layer_norm_rows(r, g, b):
    mu = jnp.mean(r, axis=-1, keepdims=True)
    rc = r - mu
    var = jnp.mean(rc * rc, axis=-1, keepdims=True)
    return rc * lax.rsqrt(var + LN_EPS) * g + b


def _post_mix_kernel(x_ref, ret_ref, y_ref, wglu_ref, wout_ref, lng_ref, lnb_ref,
                     rcat_ref, rhi_ref, rb_ref, tri_ref,
                     x1_ref, x1p_ref, e_ref, w_ref, rank_ref, cnt_ref, carry_ref):
    i = pl.program_id(0)

    @pl.when(i == 0)
    def _init():
        carry_ref[...] = jnp.zeros_like(carry_ref)

    args = (x_ref, ret_ref, y_ref, wglu_ref, wout_ref, lng_ref, lnb_ref, rcat_ref, rhi_ref, rb_ref, tri_ref,
            x1_ref, x1p_ref, e_ref, w_ref, rank_ref, carry_ref)
    a = _post_mix_phases(0, *args)
    b = _post_mix_phases(1, *args)
    nph = len(a)
    a[0]()
    for k in range(1, nph):
        a[k]()
        b[k - 1]()
    b[nph - 1]()
    cnt_ref[...] = carry_ref[...].astype(I32)


def _post_mix_phases(sub, x_ref, ret_ref, y_ref, wglu_ref, wout_ref, lng_ref, lnb_ref,
                     rcat_ref, rhi_ref, rb_ref, tri_ref, x1_ref, x1p_ref, e_ref, w_ref, rank_ref, carry_ref):
    tm = TSUB_POST
    rows = slice(sub * tm, (sub + 1) * tm)
    st = {}

    def gelu():
        st['ya'] = jax.nn.gelu(y_ref[rows, :].astype(F32))

    def glu_matmul():
        st['z'] = jnp.dot(st['ya'].astype(BF16), wglu_ref[...], preferred_element_type=F32)

    def glu_gate():
        st['ssm'] = (st.pop('ya') * _sigmoid(st.pop('z'))).astype(BF16)

    def out_matmul():
        mixed = jnp.concatenate([ret_ref[rows, :], st.pop('ssm')], axis=1)
        st['h'] = jnp.dot(mixed, wout_ref[...], preferred_element_type=F32)

    def norm():
        x1 = _layer_norm_rows(DEEPNORM_ALPHA * x_ref[rows, :] + st.pop('h'), lng_ref[...], lnb_ref[...])
        x1_ref[rows, :] = x1
        _store_token_tiles(x1p_ref, sub * tm, tm, _pack_rows(x1))
        st['xh'] = x1.astype(BF16)
        st['xl'] = (x1 - st['xh'].astype(F32)).astype(BF16)

    def router_matmul():
        nt = (((1,), (1,)), ((), ()))
        l1 = lax.dot_general(rcat_ref[...], st.pop('xh'), nt, preferred_element_type=F32)
        l2 = lax.dot_general(rhi_ref[...], st.pop('xl'), nt, preferred_element_type=F32)
        st['logits'] = l1[0:N_EXPERTS] + l1[N_EXPERTS:] + l2 + rb_ref[...]

    def route():
        _route(st.pop('logits'), rows, tri_ref, e_ref, w_ref, rank_ref, carry_ref)

    return [gelu, glu_matmul, glu_gate, out_matmul, norm, router_matmul, route]


def _route(logits, rows, tri_ref, e_ref, w_ref, rank_ref, carry_ref):
    tm = TSUB_POST
    m = jnp.max(logits, axis=0, keepdims=True)
    ex = jnp.exp(logits - m)
    p = ex / jnp.sum(ex, axis=0, keepdims=True)

    eg = EXPERTS_PER_GROUP
    iota_g = lax.broadcasted_iota(I32, (eg, tm), 0)
    best = None
    for g in range(N_EXPERT_GROUPS):
        pg = p[g * eg:(g + 1) * eg]
        m1 = jnp.max(pg, axis=0, keepdims=True)
        i1 = jnp.min(jnp.where(pg == m1, iota_g, eg), axis=0, keepdims=True)
        pg2 = jnp.where(iota_g == i1, -1.0, pg)
        m2 = jnp.max(pg2, axis=0, keepdims=True)
        i2 = jnp.min(jnp.where(pg2 == m2, iota_g, eg), axis=0, keepdims=True)
        sg = m1 + m2
        if best is None:
            best = (sg, m1, m2, i1, i2)
        else:
            better = sg > best[0]
            cand = (sg, m1, m2, i1 + g * eg, i2 + g * eg)
            best = tuple(jnp.where(better, c, o) for c, o in zip(cand, best))
    _, v1, v2, e1, e2 = best
    tot = v1 + v2
    e_ref[0:1, rows] = e1
    e_ref[1:2, rows] = e2
    w_ref[0:1, rows] = v1 / tot
    w_ref[1:2, rows] = v2 / tot

    iota_e = lax.broadcasted_iota(I32, (N_EXPERTS, tm), 0)
    oh1 = iota_e == e1
    oh2 = iota_e == e2
    oh = jnp.where(oh1, 1.0, jnp.where(oh2, 1.0, 0.0))
    before = jnp.dot(oh.astype(BF16), tri_ref[...], preferred_element_type=F32) + carry_ref[:, 0:1]
    rank_ref[0:1, rows] = jnp.sum(jnp.where(oh1, before, 0.0), axis=0, keepdims=True).astype(I32)
    rank_ref[1:2, rows] = jnp.sum(jnp.where(oh2, before, 0.0), axis=0, keepdims=True).astype(I32)
    carry_ref[...] = carry_ref[...] + jnp.sum(oh, axis=1, keepdims=True)


def _post_mix(x2, ret, y, wglu_bf, wout_bf, lng, lnb, layer, rcat, rhi, rb):
    t, d = x2.shape
    tm = TM_POST
    ts = TSUB_POST
    tri = jnp.asarray(np.triu(np.ones((ts, ts), np.float32), 1), BF16)
    const = lambda i: (0, 0)
    lyr = lambda i: (layer, 0, 0)
    tok = lambda i: (i, 0)
    lane = lambda i: (0, i)
    return pl.pallas_call(
        _post_mix_kernel,
        grid=(t // tm,),
        in_specs=[
            pl.BlockSpec((tm, d), tok),
            pl.BlockSpec((tm, RET_WIDTH), tok),
            pl.BlockSpec((tm, SSM_WIDTH), tok),
            pl.BlockSpec((None, SSM_WIDTH, SSM_WIDTH), lyr),
            pl.BlockSpec((None, RET_WIDTH + SSM_WIDTH, d), lyr),
            pl.BlockSpec((None, 1, d), lyr),
            pl.BlockSpec((None, 1, d), lyr),
            pl.BlockSpec((2 * N_EXPERTS, d), const),
            pl.BlockSpec((N_EXPERTS, d), const),
            pl.BlockSpec((N_EXPERTS, 1), const),
            pl.BlockSpec((ts, ts), const),
        ],
        out_specs=[
            pl.BlockSpec((tm, d), tok),
            pl.BlockSpec((tm * PACK_SUB, LANES), tok),
            pl.BlockSpec((TOP_K, tm), lane),
            pl.BlockSpec((TOP_K, tm), lane),
            pl.BlockSpec((TOP_K, tm), lane),
            pl.BlockSpec((N_EXPERTS, LANES), const),
        ],
        out_shape=[
            jax.ShapeDtypeStruct((t, d), F32),
            jax.ShapeDtypeStruct((t * PACK_SUB, LANES), U32),
            jax.ShapeDtypeStruct((TOP_K, t), I32),
            jax.ShapeDtypeStruct((TOP_K, t), F32),
            jax.ShapeDtypeStruct((TOP_K, t), I32),
            jax.ShapeDtypeStruct((N_EXPERTS, LANES), I32),
        ],
        scratch_shapes=[pltpu.VMEM((N_EXPERTS, LANES), F32)],
        compiler_params=_cparams(("arbitrary",)),
        name="post_mix",
    )(x2, ret, y, wglu_bf, wout_bf, lng, lnb, rcat, rhi, rb, tri)


def _dispatch_kernel(zrow_ref, d0_ref, d1_ref, x_ref, xb_hbm, zbuf_ref, sem, zsem):
    tm = TM_DISP

    @pl.when(pl.program_id(0) == 0)
    def _zero_fill():
        zbuf_ref[...] = jnp.zeros_like(zbuf_ref)

        def zcopy(k):
            n = ROW_TILE * PACK_SUB
            rows = pl.ds(pl.multiple_of(zrow_ref[k] * PACK_SUB, n), n)
            return pltpu.make_async_copy(zbuf_ref, xb_hbm.at[rows], zsem.at[0])

        def start(k, c):
            @pl.when(zrow_ref[k] >= 0)
            def _():
                zcopy(k).start()
            return c

        def wait(k, c):
            @pl.when(zrow_ref[k] >= 0)
            def _():
                zcopy(k).wait()
            return c

        lax.fori_loop(0, 2 * N_EXPERTS, start, 0)
        lax.fori_loop(0, 2 * N_EXPERTS, wait, 0)

    def issue(tb, c):
        for u in range(ISSUE_UNROLL):
            t = tb * ISSUE_UNROLL + u
            src = x_ref.at[pl.ds(t * PACK_SUB, PACK_SUB)]
            tile = lambda r: pl.ds(pl.multiple_of(r, PACK_SUB), PACK_SUB)
            pltpu.make_async_copy(src, xb_hbm.at[tile(d0_ref[0, 0, t])], sem.at[0]).start(priority=0)
            pltpu.make_async_copy(src, xb_hbm.at[tile(d1_ref[0, 0, t])], sem.at[0]).start(priority=1)
        return c

    lax.fori_loop(0, tm // ISSUE_UNROLL, issue, 0)
    for _ in range(TOP_K):
        pltpu.make_async_copy(x_ref, xb_hbm.at[pl.ds(0, tm * PACK_SUB)], sem.at[0]).wait()


def _dispatch(x1p, dest_tile, zrow, n_rows):
    t = x1p.shape[0] // PACK_SUB
    tm = TM_DISP
    nstep = t // tm
    d0 = dest_tile[0].reshape(nstep, 1, tm)
    d1 = dest_tile[1].reshape(nstep, 1, tm)
    smem_blk = pl.BlockSpec((1, 1, tm), lambda i, z: (i, 0, 0), memory_space=pltpu.SMEM)
    grid_spec = pltpu.PrefetchScalarGridSpec(
        num_scalar_prefetch=1,
        grid=(nstep,),
        in_specs=[smem_blk, smem_blk, pl.BlockSpec((tm * PACK_SUB, LANES), lambda i, z: (i, 0))],
        out_specs=pl.BlockSpec(memory_space=pl.ANY),
        scratch_shapes=[pltpu.VMEM((ROW_TILE * PACK_SUB, LANES), U32),
                        pltpu.SemaphoreType.DMA((1,)), pltpu.SemaphoreType.DMA((1,))],
    )
    return pl.pallas_call(
        _dispatch_kernel,
        grid_spec=grid_spec,
        out_shape=jax.ShapeDtypeStruct((n_rows * PACK_SUB, LANES), U32),
        compiler_params=_cparams(("arbitrary",)),
        name="dispatch",
    )(zrow, d0, d1, x1p)


def _experts_kernel(se_ref, row_ref, nt_ref, nz_ref, xb_hbm, wg_ref, wu_ref, wd_ref, yb_hbm,
                    xs_ref, acc_ref, stg_in, stg_out, sem_in, sem_out):
    del se_ref
    s = pl.program_id(0)
    j = pl.program_id(1)
    tiles = lambda nhalf: (nhalf + 1) // 2
    nt = tiles(nt_ref[s])
    nfull = nt_ref[s] // 2
    nz = nz_ref[s]
    row0 = row_ref[s]
    rt = ROW_TILE

    nsuper = pl.num_programs(0)
    s_next = jnp.minimum(s + 1, nsuper - 1)
    prev_nt = jnp.where(s > 0, tiles(nt_ref[jnp.maximum(s - 1, 0)]), 0)
    next_nt = jnp.where(s + 1 < nsuper, nt_ref[s_next], 0)

    def rows(i, base=None):
        n = rt * PACK_SUB
        base = row0 if base is None else base
        return pl.ds(pl.multiple_of((base + i * rt) * PACK_SUB, n), n)

    def in_copy(i, slot):
        return pltpu.make_async_copy(xb_hbm.at[rows(i)], stg_in.at[slot], sem_in.at[slot])

    def out_copy(i, slot):
        return pltpu.make_async_copy(stg_out.at[slot], yb_hbm.at[rows(i)], sem_out.at[slot])

    def partial_out(i, m):
        r = pl.multiple_of(i * rt, rt)
        xi = xs_ref[pl.ds(r, m), :]
        g = jnp.dot(xi, wg_ref[...].astype(BF16), preferred_element_type=F32)
        u = jnp.dot(xi, wu_ref[...].astype(BF16), preferred_element_type=F32)
        hj = (g * _sigmoid(g) * u).astype(BF16)
        return r, jnp.dot(hj, wd_ref[...].astype(BF16), preferred_element_type=F32)

    def load_tile(i):
        slot = i % 2

        @pl.when(i + 1 < nt)
        def _():
            in_copy(i + 1, 1 - slot).start()

        in_copy(i, slot).wait()
        xrows = pl.ds(pl.multiple_of(i * rt, rt), rt)
        for c, words in enumerate(_load_token_tiles(stg_in.at[slot], 0, rt)):
            xs_ref[xrows, c * LANES:(c + 1) * LANES] = _unpack_lo(words).astype(BF16)
            xs_ref[xrows, PACK_WORDS + c * LANES:PACK_WORDS + (c + 1) * LANES] = _unpack_hi(words).astype(BF16)

    def store_tile(i, vals):
        slot = i % 2

        @pl.when(i >= 2)
        def _():
            out_copy(i - 2, slot).wait()

        _store_token_tiles(stg_out.at[slot], 0, rt, _pack_rows(vals))
        out_copy(i, slot).start()

    def for_tiles(body):
        def pair(p, c):
            body(2 * p, 2, 2 * rt)
            return c

        lax.fori_loop(0, nfull // 2, pair, 0)

        @pl.when(nfull % 2 == 1)
        def _():
            body(nfull - 1, 1, rt)

        @pl.when(nt > nfull)
        def _():
            body(nfull, 1, rt // 2)

    @pl.when((nz > 0) & (j == 0))
    def _zero_tail():
        stg_out[0] = jnp.zeros(stg_out.shape[1:], U32)

        def start(i, c):
            out_copy(i, 0).start()
            return c

        def wait(i, c):
            out_copy(i, 0).wait()
            return c

        lax.fori_loop(0, nz, start, 0)
        lax.fori_loop(0, nz, wait, 0)

    @pl.when(nt > 0)
    def _work():
        @pl.when(j == 0)
        def _first():
            @pl.when(prev_nt == 0)
            def _():
                in_copy(0, 0).start()

            @pl.when(prev_nt >= 2)
            def _():
                out_copy(0, prev_nt % 2).wait()

            @pl.when(prev_nt >= 1)
            def _():
                out_copy(0, (prev_nt - 1) % 2).wait()

            def body(i, ntiles, m):
                for k in range(ntiles):
                    load_tile(i + k)
                r, part = partial_out(i, m)
                acc_ref[pl.ds(r, m), :] = part

            for_tiles(body)

        @pl.when((j > 0) & (j < N_DE_CHUNKS - 1))
        def _mid():
            def body(i, ntiles, m):
                r, part = partial_out(i, m)
                acc_ref[pl.ds(r, m), :] += part

            for_tiles(body)

        @pl.when(j == N_DE_CHUNKS - 1)
        def _last():
            @pl.when(next_nt > 0)
            def _():
                pltpu.make_async_copy(xb_hbm.at[rows(0, row_ref[s_next])], stg_in.at[0], sem_in.at[0]).start()

            def body(i, ntiles, m):
                r, part = partial_out(i, m)
                total = acc_ref[pl.ds(r, m), :] + part
                if m < rt:
                    total = jnp.concatenate([total, jnp.zeros((rt - m, total.shape[1]), F32)], axis=0)
                for k in range(ntiles):
                    store_tile(i + k, total[k * rt:(k + 1) * rt, :])

            for_tiles(body)

            @pl.when(next_nt == 0)
            def _drain():
                @pl.when(nt >= 2)
                def _():
                    out_copy(nt - 2, nt % 2).wait()

                out_copy(nt - 1, (nt - 1) % 2).wait()


def _experts(xb, se, row, ntl, nzl, w_gate, w_up, w_down, layer):
    p = xb.shape[0] // PACK_SUB
    d = D_MODEL
    nsuper = se.shape[0]
    last = N_DE_CHUNKS - 1
    assert last >= 1

    def jj(s, j, nt_ref):
        return jnp.where(nt_ref[s] > 0, j, last)

    grid_spec = pltpu.PrefetchScalarGridSpec(
        num_scalar_prefetch=4,
        grid=(nsuper, N_DE_CHUNKS),
        in_specs=[
            pl.BlockSpec(memory_space=pl.ANY),
            pl.BlockSpec((None, None, d, DE_CHUNK),
                         lambda s, j, se_r, row_r, nt_r, nz_r: (layer, se_r[s], 0, jj(s, j, nt_r))),
            pl.BlockSpec((None, None, d, DE_CHUNK),
                         lambda s, j, se_r, row_r, nt_r, nz_r: (layer, se_r[s], 0, jj(s, j, nt_r))),
            pl.BlockSpec((None, None, DE_CHUNK, d),
                         lambda s, j, se_r, row_r, nt_r, nz_r: (layer, se_r[s], jj(s, j, nt_r), 0)),
        ],
        out_specs=pl.BlockSpec(memory_space=pl.ANY),
        scratch_shapes=[
            pltpu.VMEM((SUPER, d), BF16),
            pltpu.VMEM((SUPER, d), F32),
            pltpu.VMEM((2, ROW_TILE * PACK_SUB, LANES), U32),
            pltpu.VMEM((2, ROW_TILE * PACK_SUB, LANES), U32),
            pltpu.SemaphoreType.DMA((2,)),
            pltpu.SemaphoreType.DMA((2,)),
        ],
    )
    return pl.pallas_call(
        _experts_kernel,
        grid_spec=grid_spec,
        out_shape=jax.ShapeDtypeStruct((p * PACK_SUB, LANES), U32),
        compiler_params=_cparams(("arbitrary", "arbitrary"), vmem=EXPERTS_VMEM_LIMIT),
        name="experts",
    )(se, row, ntl, nzl, xb, w_gate, w_up, w_down)


def _combine_kernel(d0_ref, d1_ref, d0n_ref, d1n_ref, x1_ref, w_ref, lng_ref, lnb_ref, yb_hbm,
                    o_ref, ybuf_ref, sem):
    tm = TM_COMB
    i = pl.program_id(0)
    slot = i % 2

    def gather(da_ref, db_ref, into):
        def issue(tb, c):
            for u in range(ISSUE_UNROLL):
                t = tb * ISSUE_UNROLL + u
                dst = pl.ds(pl.multiple_of(t * PACK_SUB, PACK_SUB), PACK_SUB)
                tile = lambda r: pl.ds(pl.multiple_of(r, PACK_SUB), PACK_SUB)
                pltpu.make_async_copy(yb_hbm.at[tile(da_ref[0, 0, t])], ybuf_ref.at[into, 0, dst],
                                      sem.at[into]).start(priority=0)
                pltpu.make_async_copy(yb_hbm.at[tile(db_ref[0, 0, t])], ybuf_ref.at[into, 1, dst],
                                      sem.at[into]).start(priority=1)
            return c

        lax.fori_loop(0, tm // ISSUE_UNROLL, issue, 0)

    @pl.when(i == 0)
    def _prime():
        gather(d0_ref, d1_ref, 0)

    for k in range(TOP_K):
        pltpu.make_async_copy(ybuf_ref.at[slot, k], ybuf_ref.at[slot, k], sem.at[slot]).wait()

    @pl.when(i + 1 < pl.num_programs(0))
    def _prefetch():
        gather(d0n_ref, d1n_ref, 1 - slot)

    wpad = jnp.concatenate([w_ref[...], jnp.zeros((LANES - TOP_K, tm), F32)], axis=0)
    wt = wpad.T
    w0 = wt[:, 0:1]
    w1 = wt[:, 1:2]
    x1 = x1_ref[...]
    y0 = _load_token_tiles(ybuf_ref.at[slot, 0], 0, tm)
    y1 = _load_token_tiles(ybuf_ref.at[slot, 1], 0, tm)
    lo = [_unpack_lo(a) * w0 + _unpack_lo(b) * w1 for a, b in zip(y0, y1)]
    hi = [_unpack_hi(a) * w0 + _unpack_hi(b) * w1 for a, b in zip(y0, y1)]
    moe = jnp.concatenate(lo + hi, axis=1)
    o_ref[...] = _layer_norm_rows(DEEPNORM_ALPHA * x1 + moe, lng_ref[...], lnb_ref[...])


def _combine(x1, dest_tile, wts, lng, lnb, layer, yb):
    t, d = x1.shape
    tm = TM_COMB
    nstep = t // tm
    d0 = dest_tile[0].reshape(nstep, 1, tm)
    d1 = dest_tile[1].reshape(nstep, 1, tm)
    smem_blk = pl.BlockSpec((1, 1, tm), lambda i: (i, 0, 0), memory_space=pltpu.SMEM)
    smem_next = pl.BlockSpec((1, 1, tm), lambda i: (jnp.minimum(i + 1, nstep - 1), 0, 0),
                             memory_space=pltpu.SMEM)
    lyr = lambda i: (layer, 0, 0)
    return pl.pallas_call(
        _combine_kernel,
        grid=(nstep,),
        in_specs=[smem_blk, smem_blk, smem_next, smem_next,
                  pl.BlockSpec((tm, d), lambda i: (i, 0)),
                  pl.BlockSpec((TOP_K, tm), lambda i: (0, i)),
                  pl.BlockSpec((None, 1, d), lyr),
                  pl.BlockSpec((None, 1, d), lyr),
                  pl.BlockSpec(memory_space=pl.ANY)],
        out_specs=pl.BlockSpec((tm, d), lambda i: (i, 0)),
        out_shape=jax.ShapeDtypeStruct((t, d), F32),
        scratch_shapes=[pltpu.VMEM((2, TOP_K, tm * PACK_SUB, LANES), U32), pltpu.SemaphoreType.DMA((2,))],
        compiler_params=_cparams(("arbitrary",)),
        name="combine",
    )(d0, d1, d0, d1, x1, wts, lng, lnb, yb)


def _rotary_tables(seq):
    inv = 1.0 / (ROPE_BASE ** (jnp.arange(0, RET_HEAD_DIM, 2, dtype=F32) / RET_HEAD_DIM))
    ang = jnp.arange(seq, dtype=F32)[:, None] * inv[None, :]
    return jnp.cos(ang), jnp.sin(ang)


def _routing_tables(counts, n_rows):
    ids = np.arange(N_EXPERTS)
    upto = jnp.asarray(ids[None, :] <= ids[:, None])
    csum = lambda v: jnp.sum(jnp.where(upto, v[None, :], 0), axis=1)
    padded = (counts + ROW_TILE - 1) // ROW_TILE * ROW_TILE
    pend = csum(padded)
    pstart = pend - padded
    ntile = padded // ROW_TILE
    nsup = (ntile + TILES_PER_SUPER - 1) // TILES_PER_SUPER
    send = csum(nsup)
    sstart = send - nsup
    nsuper = -(-n_rows // SUPER) + N_EXPERTS
    s = jnp.arange(nsuper, dtype=I32)
    total = send[-1]
    which = lambda q: jnp.minimum(jnp.sum((send[None, :] <= q[:, None]).astype(I32), axis=1), N_EXPERTS - 1)
    pick = lambda v, e: jnp.sum(jnp.where(e[:, None] == ids[None, :], v[None, :], 0), axis=1)
    se = which(s)
    valid = s < total
    k = s - pick(sstart, se)
    row = pick(pstart, se) + k * SUPER
    nt = jnp.clip(pick(ntile, se) - k * TILES_PER_SUPER, 0, TILES_PER_SUPER)
    idle = s - total
    tail_row = pend[-1] + idle * SUPER
    nz = jnp.clip((n_rows - tail_row) // ROW_TILE, 0, TILES_PER_SUPER)
    se = jnp.where(valid, se, which((total - 1)[None])[0])
    row = jnp.where(valid, row, jnp.minimum(tail_row, n_rows - ROW_TILE))
    nt = jnp.where(valid, nt, 0)
    nz = jnp.where(valid, 0, nz)
    last_fill = counts - (ntile - 1) * ROW_TILE
    half_last = ((counts > 0) & (last_fill <= ROW_TILE // 2)).astype(I32)
    ends_expert = (k * TILES_PER_SUPER + nt) == pick(ntile, se)
    nhalf = 2 * nt - jnp.where(valid & ends_expert, pick(half_last, se), 0)
    seg_last = jnp.where(padded > 0, pend - ROW_TILE, -1)
    tail = pend[-1] + jnp.arange(N_EXPERTS, dtype=I32) * ROW_TILE
    zrow = jnp.concatenate([seg_last, jnp.where(tail < n_rows, tail, -1)])
    return (pstart.astype(I32), se.astype(I32), row.astype(I32), nhalf.astype(I32), nz.astype(I32),
            zrow.astype(I32))


def kernel(x, w_in, w_out, ssm_lambda_re, ssm_lambda_im, ssm_b_re, ssm_b_im, ssm_c_re, ssm_c_im,
           ssm_d, ssm_log_dt, w_glu, ln1_g, ln1_b, ln2_g, ln2_b, router_w, router_b,
           w_gate, w_up, w_down):
    batch, seq, d = x.shape
    t = batch * seq
    n_assign = t * TOP_K
    n_rows = n_assign + N_EXPERTS * ROW_TILE
    assert seq % TM_PROJ == 0 and seq % RET_CHUNK == 0 and seq % S5_TL == 0
    assert t % TM_POST == 0 and t % TM_COMB == 0 and t % TM_DISP == 0 and n_rows % ROW_TILE == 0

    cos, sin = _rotary_tables(seq)
    expert_ids = jnp.arange(N_EXPERTS, dtype=I32)
    rw_t = router_w.astype(F32).T
    rhi = rw_t.astype(BF16)
    rlo = (rw_t - rhi.astype(F32)).astype(BF16)
    rcat = jnp.concatenate([rhi, rlo], axis=0)
    rb = router_b.astype(F32).reshape(N_EXPERTS, 1)

    w_in_bf = _prep_in_proj_weights(w_in)
    s5_tables = jax.vmap(_s5_tables)(ssm_lambda_re, ssm_lambda_im, ssm_b_re, ssm_b_im,
                                     ssm_c_re, ssm_c_im, ssm_d, ssm_log_dt)
    w_glu_bf = w_glu.astype(BF16)
    w_out_bf = w_out.astype(BF16)
    row3 = lambda p: p.astype(F32).reshape(DEPTH, 1, d)
    ln1_g, ln1_b, ln2_g, ln2_b = row3(ln1_g), row3(ln1_b), row3(ln2_g), row3(ln2_b)

    x2 = x.reshape(t, d)
    for l in range(DEPTH):
        proj = _in_proj(x2, w_in_bf, l, cos, sin, seq)
        ret = _retention(proj, batch, seq)
        y = _s5(proj, s5_tables, l, batch, seq).reshape(t, SSM_WIDTH)
        x1, x1p, e, wts, rank, cnt = _post_mix(
            x2, ret, y, w_glu_bf, w_out_bf, ln1_g, ln1_b, l, rcat, rhi, rb)
        pstart, se, row, ntl, nzl, zrow = _routing_tables(cnt[:, 0], n_rows)
        dest = rank + jnp.sum(jnp.where(e[..., None] == expert_ids, pstart, 0), axis=-1)
        dest_tile = dest * PACK_SUB
        xb = _dispatch(x1p, dest_tile, zrow, n_rows)
        yb = _experts(xb, se, row, ntl, nzl, w_gate, w_up, w_down, l)
        x2 = _combine(x1, dest_tile, wts, ln2_g, ln2_b, l, yb)
    return x2.reshape(batch, seq, d)
```

```python
import numpy as np
import jax
import jax.numpy as jnp
from jax import lax
from jax.experimental import pallas as pl
from jax.experimental.pallas import tpu as pltpu

F32 = jnp.float32
BF16 = jnp.bfloat16
I32 = jnp.int32

D_MODEL = 2048
DEPTH = 2
RET_WIDTH = 1024
SSM_WIDTH = 1024
RET_HEAD_DIM = 256
RET_HEADS = RET_WIDTH // RET_HEAD_DIM
ROPE_BASE = 10000.0
SSM_GROUP = 16
SSM_GROUPS = SSM_WIDTH // SSM_GROUP
SSM_STATE = 64
IN_PROJ_WIDTH = 4 * RET_WIDTH + SSM_WIDTH
N_EXPERTS = 32
N_EXPERT_GROUPS = 4
EXPERTS_PER_GROUP = N_EXPERTS // N_EXPERT_GROUPS
TOP_K = 2
D_EXPERT = D_MODEL // 2
LN_EPS = 1e-5
DEEPNORM_ALPHA = (2.0 * DEPTH) ** 0.25

LANES = 128
SUBLANES = 8
VMEM_LIMIT = 56 * 1024 * 1024
EXPERTS_VMEM_LIMIT = 60 * 1024 * 1024

TM_PROJ = 1024
TN_PROJ = 1024
RET_CHUNK = 512
S5_TL = 256
S5_GROUP = 16
S5_LT = SSM_WIDTH // LANES
S5_NSTATE = (LANES // SSM_GROUP) * SSM_STATE
TM_POST = 512
TSUB_POST = 512
TM_COMB = 256
ROW_TILE = 256
TILES_PER_SUPER = 6
SUPER = ROW_TILE * TILES_PER_SUPER
DE_CHUNK = 512
N_DE_CHUNKS = D_EXPERT // DE_CHUNK
TM_DISP = 1024
ISSUE_UNROLL = 8


def _sigmoid(x):
    return 1.0 / (1.0 + jnp.exp(-x))


def _cparams(sem, vmem=VMEM_LIMIT):
    return pltpu.CompilerParams(dimension_semantics=sem, vmem_limit_bytes=vmem)


PACK_WORDS = D_MODEL // 2
PACK_SUB = PACK_WORDS // LANES
U32 = jnp.uint32


def _pack_rows(x):
    bits = lambda v: lax.bitcast_convert_type(v.astype(BF16).astype(F32), U32)
    return (bits(x[:, :PACK_WORDS]) >> 16) | (bits(x[:, PACK_WORDS:]) & jnp.uint32(0xFFFF0000))


def _unpack_lo(w):
    return lax.bitcast_convert_type(w << 16, F32)


def _unpack_hi(w):
    return lax.bitcast_convert_type(w & jnp.uint32(0xFFFF0000), F32)


def _store_token_tiles(ref, row0, n, packed):
    for c in range(PACK_SUB):
        ref[pl.ds(row0 * PACK_SUB + c, n, stride=PACK_SUB), :] = packed[:, c * LANES:(c + 1) * LANES]


def _load_token_tiles(ref, row0, n):
    return [ref[pl.ds(row0 * PACK_SUB + c, n, stride=PACK_SUB), :] for c in range(PACK_SUB)]


def _in_proj_kernel(x_ref, w_ref, cos_ref, sin_ref, o_ref):
    j = pl.program_id(1)
    acc = jnp.dot(x_ref[...].astype(BF16), w_ref[...], preferred_element_type=F32)
    is_rot = j < 2
    is_gate = j == 3
    scale = jnp.where(j == 1, RET_HEAD_DIM ** -0.5, 1.0).astype(F32)
    c = jnp.where(is_rot, cos_ref[...] * scale, 1.0)
    s = jnp.where(is_rot, sin_ref[...] * scale, 0.0)
    half = RET_HEAD_DIM // 2
    for h in range(RET_HEADS):
        lo = h * RET_HEAD_DIM
        t1 = acc[:, lo:lo + half]
        t2 = acc[:, lo + half:lo + RET_HEAD_DIM]
        r1 = t1 * c - t2 * s
        r2 = t1 * s + t2 * c
        o_ref[:, lo:lo + half] = (r1 * jnp.where(is_gate, _sigmoid(r1), 1.0)).astype(BF16)
        o_ref[:, lo + half:lo + RET_HEAD_DIM] = (r2 * jnp.where(is_gate, _sigmoid(r2), 1.0)).astype(BF16)


def _prep_kernel(w_ref, p_ref, o_ref):
    o_ref[...] = jnp.dot(w_ref[...].astype(BF16), p_ref[...], preferred_element_type=F32).astype(BF16)


def _prep_in_proj_weights(w_in):
    depth, d, n = w_in.shape
    hd = RET_HEAD_DIM
    half = hd // 2
    perm = np.zeros((2, hd, hd), np.float32)
    for i in range(half):
        perm[0, 2 * i, i] = 1.0
        perm[0, 2 * i + 1, half + i] = 1.0
    perm[1] = np.eye(hd, dtype=np.float32)
    n_qk_blocks = 2 * RET_WIDTH // hd
    return pl.pallas_call(
        _prep_kernel,
        grid=(depth, n // hd),
        in_specs=[
            pl.BlockSpec((None, d, hd), lambda l, c: (l, 0, c)),
            pl.BlockSpec((None, hd, hd), lambda l, c: (jnp.where(c < n_qk_blocks, 0, 1), 0, 0)),
        ],
        out_specs=pl.BlockSpec((None, d, hd), lambda l, c: (l, 0, c)),
        out_shape=jax.ShapeDtypeStruct((depth, d, n), BF16),
        compiler_params=_cparams(("arbitrary", "arbitrary")),
        name="prep_w_in",
    )(w_in, jnp.asarray(perm, BF16))


def _in_proj(x2, w_bf, layer, cos, sin, seq):
    t, d = x2.shape
    n = w_bf.shape[2]
    tiles_per_seq = seq // TM_PROJ
    return pl.pallas_call(
        _in_proj_kernel,
        grid=(t // TM_PROJ, n // TN_PROJ),
        in_specs=[
            pl.BlockSpec((TM_PROJ, d), lambda i, j: (i, 0)),
            pl.BlockSpec((None, d, TN_PROJ), lambda i, j: (layer, 0, j)),
            pl.BlockSpec((TM_PROJ, LANES), lambda i, j: (i % tiles_per_seq, 0)),
            pl.BlockSpec((TM_PROJ, LANES), lambda i, j: (i % tiles_per_seq, 0)),
        ],
        out_specs=pl.BlockSpec((TM_PROJ, TN_PROJ), lambda i, j: (i, j)),
        out_shape=jax.ShapeDtypeStruct((t, n), BF16),
        compiler_params=_cparams(("arbitrary", "arbitrary")),
        name="in_proj",
    )(x2, w_bf, cos, sin)


def _ret_kernel(q_ref, k_ref, v_ref, g_ref, mask_ref, qd_ref, kd_ref, o_ref, r_ref):
    n = pl.program_id(1)

    @pl.when(n == 0)
    def _init():
        r_ref[...] = jnp.zeros_like(r_ref)

    for h in range(RET_HEADS):
        cols = slice(h * RET_HEAD_DIM, (h + 1) * RET_HEAD_DIM)
        q = q_ref[:, cols]
        k = k_ref[:, cols]
        v = v_ref[:, cols]
        s = lax.dot_general(q, k, (((1,), (1,)), ((), ())), preferred_element_type=F32)
        s = s * mask_ref[h]
        inner = jnp.dot(s.astype(BF16), v, preferred_element_type=F32)
        qd = qd_ref[h]
        r_prev = r_ref[h]
        cross = jnp.dot((q.astype(F32) * qd).astype(BF16), r_prev.astype(BF16), preferred_element_type=F32)
        o = inner + cross
        kdec = (k.astype(F32) * kd_ref[h]).astype(BF16)
        kv = lax.dot_general(kdec, v, (((0,), (0,)), ((), ())), preferred_element_type=F32)
        r_ref[h] = r_prev * qd[RET_CHUNK - 1:RET_CHUNK, :] + kv
        mu = jnp.mean(o, axis=-1, keepdims=True)
        oc = o - mu
        var = jnp.mean(oc * oc, axis=-1, keepdims=True)
        o_ref[:, cols] = (g_ref[:, cols].astype(F32) * (oc * lax.rsqrt(var + LN_EPS))).astype(BF16)


def _retention_tables():
    c = RET_CHUNK
    log_gamma = np.log(1.0 - 2.0 ** (-5.0 - np.arange(RET_HEADS, dtype=np.float64)))
    idx = np.arange(c, dtype=np.float64)
    diff = idx[:, None] - idx[None, :]
    mask = np.where(diff >= 0, np.exp(log_gamma[:, None, None] * np.maximum(diff, 0.0)), 0.0)
    qd = np.exp(log_gamma[:, None] * (idx + 1.0)[None, :])
    kd = np.exp(log_gamma[:, None] * (c - 1.0 - idx)[None, :])
    bc = lambda a: np.broadcast_to(a[:, :, None], (RET_HEADS, c, RET_HEAD_DIM)).astype(np.float32)
    return mask.astype(np.float32), bc(qd), bc(kd)


def _retention(proj, batch, seq):
    t = proj.shape[0]
    nch = seq // RET_CHUNK
    mask, qd, kd = _retention_tables()
    blk = (RET_CHUNK, RET_WIDTH)
    seg = lambda c: pl.BlockSpec(blk, lambda b, n: (b * nch + n, c))
    whole = lambda a: pl.BlockSpec(a.shape, lambda b, n: (0, 0, 0))
    return pl.pallas_call(
        _ret_kernel,
        grid=(batch, nch),
        in_specs=[seg(0), seg(1), seg(2), seg(3), whole(mask), whole(qd), whole(kd)],
        out_specs=seg(0),
        out_shape=jax.ShapeDtypeStruct((t, RET_WIDTH), BF16),
        scratch_shapes=[pltpu.VMEM((RET_HEADS, RET_HEAD_DIM, RET_HEAD_DIM), F32)],
        compiler_params=_cparams(("arbitrary", "arbitrary")),
        name="retention",
    )(proj, proj, proj, proj, jnp.asarray(mask), jnp.asarray(qd), jnp.asarray(kd))


def _s5_kernel(u_ref, bm_ref, cm_ref, lam_ref, d_ref, y_ref, us_ref, ut_ref, bu_ref, hb_ref, yt_ref, st_ref):
    n = pl.program_id(1)
    tl = S5_TL
    nb = SUBLANES
    ns = S5_NSTATE

    @pl.when(n == 0)
    def _init():
        st_ref[...] = jnp.zeros_like(st_ref)

    for b in range(nb):
        us_ref[b * tl:(b + 1) * tl, :] = u_ref[b].astype(F32)

    for t in range(tl):
        ut_ref[t * nb:(t + 1) * nb, :] = us_ref[pl.ds(t, nb, stride=tl), :]

    ar = jnp.broadcast_to(lam_ref[0:1, :], (nb, ns))
    ai = jnp.broadcast_to(lam_ref[1:2, :], (nb, ns))
    sr = st_ref[0:nb, :]
    si = st_ref[nb:2 * nb, :]

    grows = S5_GROUP * nb

    def project_in(g):
        rows = slice(g * grows, (g + 1) * grows)
        bu_ref[rows, :] = jnp.dot(ut_ref[rows, :].astype(BF16), bm_ref[...], preferred_element_type=F32)

    project_in(0)
    for g in range(tl // S5_GROUP):
        if g + 1 < tl // S5_GROUP:
            project_in(g + 1)
        for tt in range(0, S5_GROUP, 2):
            r0 = g * grows + tt * nb
            pair_r = []
            pair_i = []
            for r in (r0, r0 + nb):
                br = bu_ref[r:r + nb, 0:ns]
                bi = bu_ref[r:r + nb, ns:2 * ns]
                sr, si = ar * sr - ai * si + br, ar * si + ai * sr + bi
                pair_r.append(sr)
                pair_i.append(si)
            hb_ref[r0:r0 + 2 * nb, 0:ns] = jnp.concatenate(pair_r, axis=0).astype(BF16)
            hb_ref[r0:r0 + 2 * nb, ns:2 * ns] = jnp.concatenate(pair_i, axis=0).astype(BF16)
        rows = slice(g * grows, (g + 1) * grows)
        yt_ref[rows, :] = (jnp.dot(hb_ref[rows, :], cm_ref[...], preferred_element_type=F32)
                           + ut_ref[rows, :] * d_ref[...])
    st_ref[0:nb, :] = sr
    st_ref[nb:2 * nb, :] = si
    for b in range(nb):
        y_ref[b] = yt_ref[pl.ds(b, tl, stride=nb), :].astype(BF16)


def _s5_tables(lam_re, lam_im, b_re, b_im, c_re, c_im, d, log_dt):
    lam = lax.complex(lam_re.astype(F32), lam_im.astype(F32))
    dt = jnp.exp(log_dt.astype(F32))[:, None]
    lam_bar = jnp.exp(lam * dt)
    b_bar = ((lam_bar - 1.0) / lam)[..., None] * lax.complex(b_re.astype(F32), b_im.astype(F32))
    gpt = LANES // SSM_GROUP
    eye = jnp.eye(gpt, dtype=F32)

    def bdiag_in(m):
        m = m.reshape(S5_LT, gpt, SSM_STATE, SSM_GROUP)
        return jnp.einsum('jgpi,gh->jgihp', m, eye).reshape(S5_LT, LANES, gpt * SSM_STATE)

    def bdiag_out(m):
        m = m.reshape(S5_LT, gpt, SSM_GROUP, SSM_STATE)
        return jnp.einsum('jgop,gh->jgpho', m, eye).reshape(S5_LT, gpt * SSM_STATE, LANES)

    bmat = jnp.concatenate([bdiag_in(jnp.real(b_bar)), bdiag_in(jnp.imag(b_bar))], axis=-1).astype(BF16)
    cmat = jnp.concatenate([bdiag_out(c_re.astype(F32)), -bdiag_out(c_im.astype(F32))], axis=1).astype(BF16)
    lam_t = jnp.stack([jnp.real(lam_bar).reshape(S5_LT, S5_NSTATE),
                       jnp.imag(lam_bar).reshape(S5_LT, S5_NSTATE)], axis=1)
    d_t = d.astype(F32).reshape(S5_LT, 1, LANES)
    return bmat, cmat, lam_t, d_t


def _s5(proj, tables, layer, batch, seq):
    bmat, cmat, lam_t, d_t = tables
    assert batch == SUBLANES
    proj3 = proj.reshape(batch, seq, IN_PROJ_WIDTH)
    ucol = (4 * RET_WIDTH) // LANES
    tl = S5_TL
    tile = lambda j, n: (layer, j, 0, 0)
    return pl.pallas_call(
        _s5_kernel,
        grid=(S5_LT, seq // tl),
        in_specs=[
            pl.BlockSpec((batch, tl, LANES), lambda j, n: (0, n, ucol + j)),
            pl.BlockSpec((None, None, LANES, 2 * S5_NSTATE), tile),
            pl.BlockSpec((None, None, 2 * S5_NSTATE, LANES), tile),
            pl.BlockSpec((None, None, 2, S5_NSTATE), tile),
            pl.BlockSpec((None, None, 1, LANES), tile),
        ],
        out_specs=pl.BlockSpec((batch, tl, LANES), lambda j, n: (0, n, j)),
        out_shape=jax.ShapeDtypeStruct((batch, seq, SSM_WIDTH), BF16),
        scratch_shapes=[
            pltpu.VMEM((batch * tl, LANES), F32),
            pltpu.VMEM((batch * tl, LANES), F32),
            pltpu.VMEM((batch * tl, 2 * S5_NSTATE), F32),
            pltpu.VMEM((batch * tl, 2 * S5_NSTATE), BF16),
            pltpu.VMEM((batch * tl, LANES), F32),
            pltpu.VMEM((2 * SUBLANES, S5_NSTATE), F32),
        ],
        compiler_params=_cparams(("arbitrary", "arbitrary")),
        name="s5",
    )(proj3, bmat, cmat, lam_t, d_t)


def _layer_norm_rows(r, g, b):
    mu = jnp.mean(r, axis=-1, keepdims=True)
    rc = r - mu
    var = jnp.mean(rc * rc, axis=-1, keepdims=True)
    return rc * lax.rsqrt(var + LN_EPS) * g + b


def _post_mix_kernel(x_ref, ret_ref, y_ref, wglu_ref, wout_ref, lng_ref, lnb_ref,
                     rcat_ref, rhi_ref, rb_ref, tri_ref,
                     x1_ref, x1p_ref, e_ref, w_ref, rank_ref, cnt_ref, carry_ref):
    i = pl.program_id(0)

    @pl.when(i == 0)
    def _init():
        carry_ref[...] = jnp.zeros_like(carry_ref)

    args = (x_ref, ret_ref, y_ref, wglu_ref, wout_ref, lng_ref, lnb_ref, rcat_ref, rhi_ref, rb_ref, tri_ref,
            x1_ref, x1p_ref, e_ref, w_ref, rank_ref, carry_ref)
    subs = [_post_mix_phases(k, *args) for k in range(TM_POST // TSUB_POST)]
    nph = len(subs[0])
    for step in range(nph + len(subs) - 1):
        for k, phases in enumerate(subs):
            if 0 <= step - k < nph:
                phases[step - k]()
    cnt_ref[...] = carry_ref[...].astype(I32)


def _post_mix_phases(sub, x_ref, ret_ref, y_ref, wglu_ref, wout_ref, lng_ref, lnb_ref,
                     rcat_ref, rhi_ref, rb_ref, tri_ref, x1_ref, x1p_ref, e_ref, w_ref, rank_ref, carry_ref):
    tm = TSUB_POST
    rows = slice(sub * tm, (sub + 1) * tm)
    st = {}

    def gelu():
        st['ya'] = jax.nn.gelu(y_ref[rows, :].astype(F32))

    def glu_matmul():
        st['z'] = jnp.dot(st['ya'].astype(BF16), wglu_ref[...], preferred_element_type=F32)

    def glu_gate():
        st['ssm'] = (st.pop('ya') * _sigmoid(st.pop('z'))).astype(BF16)

    def out_matmul():
        mixed = jnp.concatenate([ret_ref[rows, :], st.pop('ssm')], axis=1)
        st['h'] = jnp.dot(mixed, wout_ref[...], preferred_element_type=F32)

    def norm():
        x1 = _layer_norm_rows(DEEPNORM_ALPHA * x_ref[rows, :] + st.pop('h'), lng_ref[...], lnb_ref[...])
        x1_ref[rows, :] = x1
        _store_token_tiles(x1p_ref, sub * tm, tm, _pack_rows(x1))
        st['xh'] = x1.astype(BF16)
        st['xl'] = (x1 - st['xh'].astype(F32)).astype(BF16)

    def router_matmul():
        nt = (((1,), (1,)), ((), ()))
        l1 = lax.dot_general(rcat_ref[...], st.pop('xh'), nt, preferred_element_type=F32)
        l2 = lax.dot_general(rhi_ref[...], st.pop('xl'), nt, preferred_element_type=F32)
        st['logits'] = l1[0:N_EXPERTS] + l1[N_EXPERTS:] + l2 + rb_ref[...]

    def route():
        _route(st.pop('logits'), rows, tri_ref, e_ref, w_ref, rank_ref, carry_ref)

    return [gelu, glu_matmul, glu_gate, out_matmul, norm, router_matmul, route]


def _route(logits, rows, tri_ref, e_ref, w_ref, rank_ref, carry_ref):
    tm = TSUB_POST
    m = jnp.max(logits, axis=0, keepdims=True)
    ex = jnp.exp(logits - m)
    p = ex / jnp.sum(ex, axis=0, keepdims=True)

    eg = EXPERTS_PER_GROUP
    iota_g = lax.broadcasted_iota(I32, (eg, tm), 0)
    best = None
    for g in range(N_EXPERT_GROUPS):
        pg = p[g * eg:(g + 1) * eg]
        m1 = jnp.max(pg, axis=0, keepdims=True)
        i1 = jnp.min(jnp.where(pg == m1, iota_g, eg), axis=0, keepdims=True)
        pg2 = jnp.where(iota_g == i1, -1.0, pg)
        m2 = jnp.max(pg2, axis=0, keepdims=True)
        i2 = jnp.min(jnp.where(pg2 == m2, iota_g, eg), axis=0, keepdims=True)
        sg = m1 + m2
        if best is None:
            best = (sg, m1, m2, i1, i2)
        else:
            better = sg > best[0]
            cand = (sg, m1, m2, i1 + g * eg, i2 + g * eg)
            best = tuple(jnp.where(better, c, o) for c, o in zip(cand, best))
    _, v1, v2, e1, e2 = best
    tot = v1 + v2
    e_ref[0:1, rows] = e1
    e_ref[1:2, rows] = e2
    w_ref[0:1, rows] = v1 / tot
    w_ref[1:2, rows] = v2 / tot

    iota_e = lax.broadcasted_iota(I32, (N_EXPERTS, tm), 0)
    oh1 = iota_e == e1
    oh2 = iota_e == e2
    oh = jnp.where(oh1, 1.0, jnp.where(oh2, 1.0, 0.0))
    before = jnp.dot(oh.astype(BF16), tri_ref[...], preferred_element_type=F32) + carry_ref[:, 0:1]
    rank_ref[0:1, rows] = jnp.sum(jnp.where(oh1, before, 0.0), axis=0, keepdims=True).astype(I32)
    rank_ref[1:2, rows] = jnp.sum(jnp.where(oh2, before, 0.0), axis=0, keepdims=True).astype(I32)
    carry_ref[...] = carry_ref[...] + jnp.sum(oh, axis=1, keepdims=True)


def _post_mix(x2, ret, y, wglu_bf, wout_bf, lng, lnb, layer, rcat, rhi, rb):
    t, d = x2.shape
    tm = TM_POST
    ts = TSUB_POST
    tri = jnp.asarray(np.triu(np.ones((ts, ts), np.float32), 1), BF16)
    const = lambda i: (0, 0)
    lyr = lambda i: (layer, 0, 0)
    tok = lambda i: (i, 0)
    lane = lambda i: (0, i)
    return pl.pallas_call(
        _post_mix_kernel,
        grid=(t // tm,),
        in_specs=[
            pl.BlockSpec((tm, d), tok),
            pl.BlockSpec((tm, RET_WIDTH), tok),
            pl.BlockSpec((tm, SSM_WIDTH), tok),
            pl.BlockSpec((None, SSM_WIDTH, SSM_WIDTH), lyr),
            pl.BlockSpec((None, RET_WIDTH + SSM_WIDTH, d), lyr),
            pl.BlockSpec((None, 1, d), lyr),
            pl.BlockSpec((None, 1, d), lyr),
            pl.BlockSpec((2 * N_EXPERTS, d), const),
            pl.BlockSpec((N_EXPERTS, d), const),
            pl.BlockSpec((N_EXPERTS, 1), const),
            pl.BlockSpec((ts, ts), const),
        ],
        out_specs=[
            pl.BlockSpec((tm, d), tok),
            pl.BlockSpec((tm * PACK_SUB, LANES), tok),
            pl.BlockSpec((TOP_K, tm), lane),
            pl.BlockSpec((TOP_K, tm), lane),
            pl.BlockSpec((TOP_K, tm), lane),
            pl.BlockSpec((N_EXPERTS, LANES), const),
        ],
        out_shape=[
            jax.ShapeDtypeStruct((t, d), F32),
            jax.ShapeDtypeStruct((t * PACK_SUB, LANES), U32),
            jax.ShapeDtypeStruct((TOP_K, t), I32),
            jax.ShapeDtypeStruct((TOP_K, t), F32),
            jax.ShapeDtypeStruct((TOP_K, t), I32),
            jax.ShapeDtypeStruct((N_EXPERTS, LANES), I32),
        ],
        scratch_shapes=[pltpu.VMEM((N_EXPERTS, LANES), F32)],
        compiler_params=_cparams(("arbitrary",)),
        name="post_mix",
    )(x2, ret, y, wglu_bf, wout_bf, lng, lnb, rcat, rhi, rb, tri)


def _dispatch_kernel(zrow_ref, d0_ref, d1_ref, x_ref, xb_hbm, zbuf_ref, sem, zsem):
    tm = TM_DISP

    @pl.when(pl.program_id(0) == 0)
    def _zero_fill():
        zbuf_ref[...] = jnp.zeros_like(zbuf_ref)

        def zcopy(k):
            n = ROW_TILE * PACK_SUB
            rows = pl.ds(pl.multiple_of(zrow_ref[k] * PACK_SUB, n), n)
            return pltpu.make_async_copy(zbuf_ref, xb_hbm.at[rows], zsem.at[0])

        def start(k, c):
            @pl.when(zrow_ref[k] >= 0)
            def _():
                zcopy(k).start()
            return c

        def wait(k, c):
            @pl.when(zrow_ref[k] >= 0)
            def _():
                zcopy(k).wait()
            return c

        lax.fori_loop(0, 2 * N_EXPERTS, start, 0)
        lax.fori_loop(0, 2 * N_EXPERTS, wait, 0)

    def issue(tb, c):
        for u in range(ISSUE_UNROLL):
            t = tb * ISSUE_UNROLL + u
            src = x_ref.at[pl.ds(t * PACK_SUB, PACK_SUB)]
            tile = lambda r: pl.ds(pl.multiple_of(r, PACK_SUB), PACK_SUB)
            pltpu.make_async_copy(src, xb_hbm.at[tile(d0_ref[0, 0, t])], sem.at[0]).start(priority=0)
            pltpu.make_async_copy(src, xb_hbm.at[tile(d1_ref[0, 0, t])], sem.at[0]).start(priority=1)
        return c

    lax.fori_loop(0, tm // ISSUE_UNROLL, issue, 0)
    for _ in range(TOP_K):
        pltpu.make_async_copy(x_ref, xb_hbm.at[pl.ds(0, tm * PACK_SUB)], sem.at[0]).wait()


def _dispatch(x1p, dest_tile, zrow, n_rows):
    t = x1p.shape[0] // PACK_SUB
    tm = TM_DISP
    nstep = t // tm
    d0 = dest_tile[0].reshape(nstep, 1, tm)
    d1 = dest_tile[1].reshape(nstep, 1, tm)
    smem_blk = pl.BlockSpec((1, 1, tm), lambda i, z: (i, 0, 0), memory_space=pltpu.SMEM)
    grid_spec = pltpu.PrefetchScalarGridSpec(
        num_scalar_prefetch=1,
        grid=(nstep,),
        in_specs=[smem_blk, smem_blk, pl.BlockSpec((tm * PACK_SUB, LANES), lambda i, z: (i, 0))],
        out_specs=pl.BlockSpec(memory_space=pl.ANY),
        scratch_shapes=[pltpu.VMEM((ROW_TILE * PACK_SUB, LANES), U32),
                        pltpu.SemaphoreType.DMA((1,)), pltpu.SemaphoreType.DMA((1,))],
    )
    return pl.pallas_call(
        _dispatch_kernel,
        grid_spec=grid_spec,
        out_shape=jax.ShapeDtypeStruct((n_rows * PACK_SUB, LANES), U32),
        compiler_params=_cparams(("arbitrary",)),
        name="dispatch",
    )(zrow, d0, d1, x1p)


def _experts_kernel(se_ref, row_ref, nt_ref, nz_ref, xb_hbm, wg_ref, wu_ref, wd_ref, yb_hbm,
                    xs_ref, acc_ref, stg_in, stg_out, sem_in, sem_out):
    del se_ref
    s = pl.program_id(0)
    j = pl.program_id(1)
    tiles = lambda nhalf: (nhalf + 1) // 2
    nt = tiles(nt_ref[s])
    nfull = nt_ref[s] // 2
    nz = nz_ref[s]
    row0 = row_ref[s]
    rt = ROW_TILE

    nsuper = pl.num_programs(0)
    s_next = jnp.minimum(s + 1, nsuper - 1)
    prev_nt = jnp.where(s > 0, tiles(nt_ref[jnp.maximum(s - 1, 0)]), 0)
    next_nt = jnp.where(s + 1 < nsuper, nt_ref[s_next], 0)

    def rows(i, base=None):
        n = rt * PACK_SUB
        base = row0 if base is None else base
        return pl.ds(pl.multiple_of((base + i * rt) * PACK_SUB, n), n)

    def in_copy(i, slot):
        return pltpu.make_async_copy(xb_hbm.at[rows(i)], stg_in.at[slot], sem_in.at[slot])

    def out_copy(i, slot):
        return pltpu.make_async_copy(stg_out.at[slot], yb_hbm.at[rows(i)], sem_out.at[slot])

    def partial_out(i, m):
        r = pl.multiple_of(i * rt, rt)
        xi = xs_ref[pl.ds(r, m), :]
        g = jnp.dot(xi, wg_ref[...].astype(BF16), preferred_element_type=F32)
        u = jnp.dot(xi, wu_ref[...].astype(BF16), preferred_element_type=F32)
        hj = (g * _sigmoid(g) * u).astype(BF16)
        return r, jnp.dot(hj, wd_ref[...].astype(BF16), preferred_element_type=F32)

    def load_tile(i):
        slot = i % 2

        @pl.when(i + 1 < nt)
        def _():
            in_copy(i + 1, 1 - slot).start()

        in_copy(i, slot).wait()
        xrows = pl.ds(pl.multiple_of(i * rt, rt), rt)
        for c, words in enumerate(_load_token_tiles(stg_in.at[slot], 0, rt)):
            xs_ref[xrows, c * LANES:(c + 1) * LANES] = _unpack_lo(words).astype(BF16)
            xs_ref[xrows, PACK_WORDS + c * LANES:PACK_WORDS + (c + 1) * LANES] = _unpack_hi(words).astype(BF16)

    def store_tile(i, vals):
        slot = i % 2

        @pl.when(i >= 2)
        def _():
            out_copy(i - 2, slot).wait()

        _store_token_tiles(stg_out.at[slot], 0, rt, _pack_rows(vals))
        out_copy(i, slot).start()

    def for_tiles(body):
        def pair(p, c):
            body(2 * p, 2, 2 * rt)
            return c

        lax.fori_loop(0, nfull // 2, pair, 0)

        @pl.when(nfull % 2 == 1)
        def _():
            body(nfull - 1, 1, rt)

        @pl.when(nt > nfull)
        def _():
            body(nfull, 1, rt // 2)

    @pl.when((nz > 0) & (j == 0))
    def _zero_tail():
        stg_out[0] = jnp.zeros(stg_out.shape[1:], U32)

        def start(i, c):
            out_copy(i, 0).start()
            return c

        def wait(i, c):
            out_copy(i, 0).wait()
            return c

        lax.fori_loop(0, nz, start, 0)
        lax.fori_loop(0, nz, wait, 0)

    @pl.when(nt > 0)
    def _work():
        @pl.when(j == 0)
        def _first():
            @pl.when(prev_nt == 0)
            def _():
                in_copy(0, 0).start()

            @pl.when(prev_nt >= 2)
            def _():
                out_copy(0, prev_nt % 2).wait()

            @pl.when(prev_nt >= 1)
            def _():
                out_copy(0, (prev_nt - 1) % 2).wait()

            def body(i, ntiles, m):
                for k in range(ntiles):
                    load_tile(i + k)
                r, part = partial_out(i, m)
                acc_ref[pl.ds(r, m), :] = part

            for_tiles(body)

        @pl.when((j > 0) & (j < N_DE_CHUNKS - 1))
        def _mid():
            def body(i, ntiles, m):
                r, part = partial_out(i, m)
                acc_ref[pl.ds(r, m), :] += part

            for_tiles(body)

        @pl.when(j == N_DE_CHUNKS - 1)
        def _last():
            @pl.when(next_nt > 0)
            def _():
                pltpu.make_async_copy(xb_hbm.at[rows(0, row_ref[s_next])], stg_in.at[0], sem_in.at[0]).start()

            def body(i, ntiles, m):
                r, part = partial_out(i, m)
                total = acc_ref[pl.ds(r, m), :] + part
                if m < rt:
                    total = jnp.concatenate([total, jnp.zeros((rt - m, total.shape[1]), F32)], axis=0)
                for k in range(ntiles):
                    store_tile(i + k, total[k * rt:(k + 1) * rt, :])

            for_tiles(body)

            @pl.when(next_nt == 0)
            def _drain():
                @pl.when(nt >= 2)
                def _():
                    out_copy(nt - 2, nt % 2).wait()

                out_copy(nt - 1, (nt - 1) % 2).wait()


def _experts(xb, se, row, ntl, nzl, w_gate, w_up, w_down, layer):
    p = xb.shape[0] // PACK_SUB
    d = D_MODEL
    nsuper = se.shape[0]
    last = N_DE_CHUNKS - 1
    assert last >= 1

    def jj(s, j, nt_ref):
        return jnp.where(nt_ref[s] > 0, j, last)

    grid_spec = pltpu.PrefetchScalarGridSpec(
        num_scalar_prefetch=4,
        grid=(nsuper, N_DE_CHUNKS),
        in_specs=[
            pl.BlockSpec(memory_space=pl.ANY),
            pl.BlockSpec((None, None, d, DE_CHUNK),
                         lambda s, j, se_r, row_r, nt_r, nz_r: (layer, se_r[s], 0, jj(s, j, nt_r))),
            pl.BlockSpec((None, None, d, DE_CHUNK),
                         lambda s, j, se_r, row_r, nt_r, nz_r: (layer, se_r[s], 0, jj(s, j, nt_r))),
            pl.BlockSpec((None, None, DE_CHUNK, d),
                         lambda s, j, se_r, row_r, nt_r, nz_r: (layer, se_r[s], jj(s, j, nt_r), 0)),
        ],
        out_specs=pl.BlockSpec(memory_space=pl.ANY),
        scratch_shapes=[
            pltpu.VMEM((SUPER, d), BF16),
            pltpu.VMEM((SUPER, d), F32),
            pltpu.VMEM((2, ROW_TILE * PACK_SUB, LANES), U32),
            pltpu.VMEM((2, ROW_TILE * PACK_SUB, LANES), U32),
            pltpu.SemaphoreType.DMA((2,)),
            pltpu.SemaphoreType.DMA((2,)),
        ],
    )
    return pl.pallas_call(
        _experts_kernel,
        grid_spec=grid_spec,
        out_shape=jax.ShapeDtypeStruct((p * PACK_SUB, LANES), U32),
        compiler_params=_cparams(("arbitrary", "arbitrary"), vmem=EXPERTS_VMEM_LIMIT),
        name="experts",
    )(se, row, ntl, nzl, xb, w_gate, w_up, w_down)


def _combine_kernel(d0_ref, d1_ref, d0n_ref, d1n_ref, x1_ref, w_ref, lng_ref, lnb_ref, yb_hbm,
                    o_ref, ybuf_ref, sem):
    tm = TM_COMB
    i = pl.program_id(0)
    slot = i % 2

    def gather(da_ref, db_ref, into):
        def issue(tb, c):
            for u in range(ISSUE_UNROLL):
                t = tb * ISSUE_UNROLL + u
                dst = pl.ds(pl.multiple_of(t * PACK_SUB, PACK_SUB), PACK_SUB)
                tile = lambda r: pl.ds(pl.multiple_of(r, PACK_SUB), PACK_SUB)
                pltpu.make_async_copy(yb_hbm.at[tile(da_ref[0, 0, t])], ybuf_ref.at[into, 0, dst],
                                      sem.at[into]).start(priority=0)
                pltpu.make_async_copy(yb_hbm.at[tile(db_ref[0, 0, t])], ybuf_ref.at[into, 1, dst],
                                      sem.at[into]).start(priority=1)
            return c

        lax.fori_loop(0, tm // ISSUE_UNROLL, issue, 0)

    @pl.when(i == 0)
    def _prime():
        gather(d0_ref, d1_ref, 0)

    for k in range(TOP_K):
        pltpu.make_async_copy(ybuf_ref.at[slot, k], ybuf_ref.at[slot, k], sem.at[slot]).wait()

    @pl.when(i + 1 < pl.num_programs(0))
    def _prefetch():
        gather(d0n_ref, d1n_ref, 1 - slot)

    wpad = jnp.concatenate([w_ref[...], jnp.zeros((LANES - TOP_K, tm), F32)], axis=0)
    wt = wpad.T
    w0 = wt[:, 0:1]
    w1 = wt[:, 1:2]
    x1 = x1_ref[...]
    y0 = _load_token_tiles(ybuf_ref.at[slot, 0], 0, tm)
    y1 = _load_token_tiles(ybuf_ref.at[slot, 1], 0, tm)
    lo = [_unpack_lo(a) * w0 + _unpack_lo(b) * w1 for a, b in zip(y0, y1)]
    hi = [_unpack_hi(a) * w0 + _unpack_hi(b) * w1 for a, b in zip(y0, y1)]
    moe = jnp.concatenate(lo + hi, axis=1)
    o_ref[...] = _layer_norm_rows(DEEPNORM_ALPHA * x1 + moe, lng_ref[...], lnb_ref[...])


def _combine(x1, dest_tile, wts, lng, lnb, layer, yb):
    t, d = x1.shape
    tm = TM_COMB
    nstep = t // tm
    d0 = dest_tile[0].reshape(nstep, 1, tm)
    d1 = dest_tile[1].reshape(nstep, 1, tm)
    smem_blk = pl.BlockSpec((1, 1, tm), lambda i: (i, 0, 0), memory_space=pltpu.SMEM)
    smem_next = pl.BlockSpec((1, 1, tm), lambda i: (jnp.minimum(i + 1, nstep - 1), 0, 0),
                             memory_space=pltpu.SMEM)
    lyr = lambda i: (layer, 0, 0)
    return pl.pallas_call(
        _combine_kernel,
        grid=(nstep,),
        in_specs=[smem_blk, smem_blk, smem_next, smem_next,
                  pl.BlockSpec((tm, d), lambda i: (i, 0)),
                  pl.BlockSpec((TOP_K, tm), lambda i: (0, i)),
                  pl.BlockSpec((None, 1, d), lyr),
                  pl.BlockSpec((None, 1, d), lyr),
                  pl.BlockSpec(memory_space=pl.ANY)],
        out_specs=pl.BlockSpec((tm, d), lambda i: (i, 0)),
        out_shape=jax.ShapeDtypeStruct((t, d), F32),
        scratch_shapes=[pltpu.VMEM((2, TOP_K, tm * PACK_SUB, LANES), U32), pltpu.SemaphoreType.DMA((2,))],
        compiler_params=_cparams(("arbitrary",)),
        name="combine",
    )(d0, d1, d0, d1, x1, wts, lng, lnb, yb)


def _rotary_tables(seq):
    inv = 1.0 / (ROPE_BASE ** (jnp.arange(0, RET_HEAD_DIM, 2, dtype=F32) / RET_HEAD_DIM))
    ang = jnp.arange(seq, dtype=F32)[:, None] * inv[None, :]
    return jnp.cos(ang), jnp.sin(ang)


def _routing_tables(counts, n_rows):
    ids = np.arange(N_EXPERTS)
    upto = jnp.asarray(ids[None, :] <= ids[:, None])
    csum = lambda v: jnp.sum(jnp.where(upto, v[None, :], 0), axis=1)
    padded = (counts + ROW_TILE - 1) // ROW_TILE * ROW_TILE
    pend = csum(padded)
    pstart = pend - padded
    ntile = padded // ROW_TILE
    nsup = (ntile + TILES_PER_SUPER - 1) // TILES_PER_SUPER
    send = csum(nsup)
    sstart = send - nsup
    nsuper = -(-n_rows // SUPER) + N_EXPERTS
    s = jnp.arange(nsuper, dtype=I32)
    total = send[-1]
    which = lambda q: jnp.minimum(jnp.sum((send[None, :] <= q[:, None]).astype(I32), axis=1), N_EXPERTS - 1)
    pick = lambda v, e: jnp.sum(jnp.where(e[:, None] == ids[None, :], v[None, :], 0), axis=1)
    se = which(s)
    valid = s < total
    k = s - pick(sstart, se)
    row = pick(pstart, se) + k * SUPER
    nt = jnp.clip(pick(ntile, se) - k * TILES_PER_SUPER, 0, TILES_PER_SUPER)
    idle = s - total
    tail_row = pend[-1] + idle * SUPER
    nz = jnp.clip((n_rows - tail_row) // ROW_TILE, 0, TILES_PER_SUPER)
    se = jnp.where(valid, se, which((total - 1)[None])[0])
    row = jnp.where(valid, row, jnp.minimum(tail_row, n_rows - ROW_TILE))
    nt = jnp.where(valid, nt, 0)
    nz = jnp.where(valid, 0, nz)
    last_fill = counts - (ntile - 1) * ROW_TILE
    half_last = ((counts > 0) & (last_fill <= ROW_TILE // 2)).astype(I32)
    ends_expert = (k * TILES_PER_SUPER + nt) == pick(ntile, se)
    nhalf = 2 * nt - jnp.where(valid & ends_expert, pick(half_last, se), 0)
    seg_last = jnp.where(padded > 0, pend - ROW_TILE, -1)
    tail = pend[-1] + jnp.arange(N_EXPERTS, dtype=I32) * ROW_TILE
    zrow = jnp.concatenate([seg_last, jnp.where(tail < n_rows, tail, -1)])
    return (pstart.astype(I32), se.astype(I32), row.astype(I32), nhalf.astype(I32), nz.astype(I32),
            zrow.astype(I32))


def kernel(x, w_in, w_out, ssm_lambda_re, ssm_lambda_im, ssm_b_re, ssm_b_im, ssm_c_re, ssm_c_im,
           ssm_d, ssm_log_dt, w_glu, ln1_g, ln1_b, ln2_g, ln2_b, router_w, router_b,
           w_gate, w_up, w_down):
    batch, seq, d = x.shape
    t = batch * seq
    n_assign = t * TOP_K
    n_rows = n_assign + N_EXPERTS * ROW_TILE
    assert seq % TM_PROJ == 0 and seq % RET_CHUNK == 0 and seq % S5_TL == 0
    assert t % TM_POST == 0 and t % TM_COMB == 0 and t % TM_DISP == 0 and n_rows % ROW_TILE == 0

    cos, sin = _rotary_tables(seq)
    expert_ids = jnp.arange(N_EXPERTS, dtype=I32)
    rw_t = router_w.astype(F32).T
    rhi = rw_t.astype(BF16)
    rlo = (rw_t - rhi.astype(F32)).astype(BF16)
    rcat = jnp.concatenate([rhi, rlo], axis=0)
    rb = router_b.astype(F32).reshape(N_EXPERTS, 1)

    w_in_bf = _prep_in_proj_weights(w_in)
    s5_tables = jax.vmap(_s5_tables)(ssm_lambda_re, ssm_lambda_im, ssm_b_re, ssm_b_im,
                                     ssm_c_re, ssm_c_im, ssm_d, ssm_log_dt)
    w_glu_bf = w_glu.astype(BF16)
    w_out_bf = w_out.astype(BF16)
    row3 = lambda p: p.astype(F32).reshape(DEPTH, 1, d)
    ln1_g, ln1_b, ln2_g, ln2_b = row3(ln1_g), row3(ln1_b), row3(ln2_g), row3(ln2_b)

    x2 = x.reshape(t, d)
    for l in range(DEPTH):
        proj = _in_proj(x2, w_in_bf, l, cos, sin, seq)
        ret = _retention(proj, batch, seq)
        y = _s5(proj, s5_tables, l, batch, seq).reshape(t, SSM_WIDTH)
        x1, x1p, e, wts, rank, cnt = _post_mix(
            x2, ret, y, w_glu_bf, w_out_bf, ln1_g, ln1_b, l, rcat, rhi, rb)
        pstart, se, row, ntl, nzl, zrow = _routing_tables(cnt[:, 0], n_rows)
        dest = rank + jnp.sum(jnp.where(e[..., None] == expert_ids, pstart, 0), axis=-1)
        dest_tile = dest * PACK_SUB
        xb = _dispatch(x1p, dest_tile, zrow, n_rows)
        yb = _experts(xb, se, row, ntl, nzl, w_gate, w_up, w_down, l)
        x2 = _combine(x1, dest_tile, wts, ln2_g, ln2_b, l, yb)
    return x2.reshape(batch, seq, d)
```

```python
import numpy as np
import jax
import jax.numpy as jnp
from jax import lax
from jax.experimental import pallas as pl
from jax.experimental.pallas import tpu as pltpu

F32 = jnp.float32
BF16 = jnp.bfloat16
I32 = jnp.int32

D_MODEL = 2048
DEPTH = 2
RET_WIDTH = 1024
SSM_WIDTH = 1024
RET_HEAD_DIM = 256
RET_HEADS = RET_WIDTH // RET_HEAD_DIM
ROPE_BASE = 10000.0
SSM_GROUP = 16
SSM_GROUPS = SSM_WIDTH // SSM_GROUP
SSM_STATE = 64
IN_PROJ_WIDTH = 4 * RET_WIDTH + SSM_WIDTH
N_EXPERTS = 32
N_EXPERT_GROUPS = 4
EXPERTS_PER_GROUP = N_EXPERTS // N_EXPERT_GROUPS
TOP_K = 2
D_EXPERT = D_MODEL // 2
LN_EPS = 1e-5
DEEPNORM_ALPHA = (2.0 * DEPTH) ** 0.25

LANES = 128
SUBLANES = 8
VMEM_LIMIT = 56 * 1024 * 1024
EXPERTS_VMEM_LIMIT = 60 * 1024 * 1024

TM_PROJ = 1024
TN_PROJ = 1024
RET_CHUNK = 512
S5_TL = 256
S5_GROUP = 16
S5_LT = SSM_WIDTH // LANES
S5_NSTATE = (LANES // SSM_GROUP) * SSM_STATE
TM_POST = 512
TSUB_POST = 512
TM_COMB = 256
ROW_TILE = 256
TILES_PER_SUPER = 6
SUPER = ROW_TILE * TILES_PER_SUPER
DE_CHUNK = 512
N_DE_CHUNKS = D_EXPERT // DE_CHUNK
TM_DISP = 1024
ISSUE_UNROLL = 8


def _sigmoid(x):
    return 1.0 / (1.0 + jnp.exp(-x))


def _cparams(sem, vmem=VMEM_LIMIT):
    return pltpu.CompilerParams(dimension_semantics=sem, vmem_limit_bytes=vmem)


PACK_WORDS = D_MODEL // 2
PACK_SUB = PACK_WORDS // LANES
U32 = jnp.uint32


def _pack_rows(x):
    bits = lambda v: lax.bitcast_convert_type(v.astype(BF16).astype(F32), U32)
    return (bits(x[:, :PACK_WORDS]) >> 16) | (bits(x[:, PACK_WORDS:]) & jnp.uint32(0xFFFF0000))


def _unpack_lo(w):
    return lax.bitcast_convert_type(w << 16, F32)


def _unpack_hi(w):
    return lax.bitcast_convert_type(w & jnp.uint32(0xFFFF0000), F32)


def _store_token_tiles(ref, row0, n, packed):
    for c in range(PACK_SUB):
        ref[pl.ds(row0 * PACK_SUB + c, n, stride=PACK_SUB), :] = packed[:, c * LANES:(c + 1) * LANES]


def _load_token_tiles(ref, row0, n):
    return [ref[pl.ds(row0 * PACK_SUB + c, n, stride=PACK_SUB), :] for c in range(PACK_SUB)]


def _in_proj_kernel(x_ref, w_ref, cos_ref, sin_ref, o_ref):
    j = pl.program_id(1)
    acc = jnp.dot(x_ref[...].astype(BF16), w_ref[...], preferred_element_type=F32)
    is_rot = j < 2
    is_gate = j == 3
    scale = jnp.where(j == 1, RET_HEAD_DIM ** -0.5, 1.0).astype(F32)
    c = jnp.where(is_rot, cos_ref[...] * scale, 1.0)
    s = jnp.where(is_rot, sin_ref[...] * scale, 0.0)
    half = RET_HEAD_DIM // 2
    for h in range(RET_HEADS):
        lo = h * RET_HEAD_DIM
        t1 = acc[:, lo:lo + half]
        t2 = acc[:, lo + half:lo + RET_HEAD_DIM]
        r1 = t1 * c - t2 * s
        r2 = t1 * s + t2 * c
        o_ref[:, lo:lo + half] = (r1 * jnp.where(is_gate, _sigmoid(r1), 1.0)).astype(BF16)
        o_ref[:, lo + half:lo + RET_HEAD_DIM] = (r2 * jnp.where(is_gate, _sigmoid(r2), 1.0)).astype(BF16)


def _prep_kernel(w_ref, p_ref, o_ref):
    o_ref[...] = jnp.dot(w_ref[...].astype(BF16), p_ref[...], preferred_element_type=F32).astype(BF16)


def _prep_in_proj_weights(w_in):
    depth, d, n = w_in.shape
    hd = RET_HEAD_DIM
    half = hd // 2
    perm = np.zeros((2, hd, hd), np.float32)
    for i in range(half):
        perm[0, 2 * i, i] = 1.0
        perm[0, 2 * i + 1, half + i] = 1.0
    perm[1] = np.eye(hd, dtype=np.float32)
    n_qk_blocks = 2 * RET_WIDTH // hd
    return pl.pallas_call(
        _prep_kernel,
        grid=(depth, n // hd),
        in_specs=[
            pl.BlockSpec((None, d, hd), lambda l, c: (l, 0, c)),
            pl.BlockSpec((None, hd, hd), lambda l, c: (jnp.where(c < n_qk_blocks, 0, 1), 0, 0)),
        ],
        out_specs=pl.BlockSpec((None, d, hd), lambda l, c: (l, 0, c)),
        out_shape=jax.ShapeDtypeStruct((depth, d, n), BF16),
        compiler_params=_cparams(("arbitrary", "arbitrary")),
        name="prep_w_in",
    )(w_in, jnp.asarray(perm, BF16))


def _in_proj(x2, w_bf, layer, cos, sin, seq):
    t, d = x2.shape
    n = w_bf.shape[2]
    tiles_per_seq = seq // TM_PROJ
    return pl.pallas_call(
        _in_proj_kernel,
        grid=(t // TM_PROJ, n // TN_PROJ),
        in_specs=[
            pl.BlockSpec((TM_PROJ, d), lambda i, j: (i, 0)),
            pl.BlockSpec((None, d, TN_PROJ), lambda i, j: (layer, 0, j)),
            pl.BlockSpec((TM_PROJ, LANES), lambda i, j: (i % tiles_per_seq, 0)),
            pl.BlockSpec((TM_PROJ, LANES), lambda i, j: (i % tiles_per_seq, 0)),
        ],
        out_specs=pl.BlockSpec((TM_PROJ, TN_PROJ), lambda i, j: (i, j)),
        out_shape=jax.ShapeDtypeStruct((t, n), BF16),
        compiler_params=_cparams(("arbitrary", "arbitrary")),
        name="in_proj",
    )(x2, w_bf, cos, sin)


def _ret_kernel(q_ref, k_ref, v_ref, g_ref, mask_ref, qd_ref, kd_ref, o_ref, r_ref):
    n = pl.program_id(1)

    @pl.when(n == 0)
    def _init():
        r_ref[...] = jnp.zeros_like(r_ref)

    for h in range(RET_HEADS):
        cols = slice(h * RET_HEAD_DIM, (h + 1) * RET_HEAD_DIM)
        q = q_ref[:, cols]
        k = k_ref[:, cols]
        v = v_ref[:, cols]
        s = lax.dot_general(q, k, (((1,), (1,)), ((), ())), preferred_element_type=F32)
        s = s * mask_ref[h]
        inner = jnp.dot(s.astype(BF16), v, preferred_element_type=F32)
        qd = qd_ref[h]
        r_prev = r_ref[h]
        cross = jnp.dot((q.astype(F32) * qd).astype(BF16), r_prev.astype(BF16), preferred_element_type=F32)
        o = inner + cross
        kdec = (k.astype(F32) * kd_ref[h]).astype(BF16)
        kv = lax.dot_general(kdec, v, (((0,), (0,)), ((), ())), preferred_element_type=F32)
        r_ref[h] = r_prev * qd[RET_CHUNK - 1:RET_CHUNK, :] + kv
        mu = jnp.mean(o, axis=-1, keepdims=True)
        oc = o - mu
        var = jnp.mean(oc * oc, axis=-1, keepdims=True)
        o_ref[:, cols] = (g_ref[:, cols].astype(F32) * (oc * lax.rsqrt(var + LN_EPS))).astype(BF16)


def _retention_tables():
    c = RET_CHUNK
    log_gamma = np.log(1.0 - 2.0 ** (-5.0 - np.arange(RET_HEADS, dtype=np.float64)))
    idx = np.arange(c, dtype=np.float64)
    diff = idx[:, None] - idx[None, :]
    mask = np.where(diff >= 0, np.exp(log_gamma[:, None, None] * np.maximum(diff, 0.0)), 0.0)
    qd = np.exp(log_gamma[:, None] * (idx + 1.0)[None, :])
    kd = np.exp(log_gamma[:, None] * (c - 1.0 - idx)[None, :])
    bc = lambda a: np.broadcast_to(a[:, :, None], (RET_HEADS, c, RET_HEAD_DIM)).astype(np.float32)
    return mask.astype(np.float32), bc(qd), bc(kd)


def _retention(proj, batch, seq):
    t = proj.shape[0]
    nch = seq // RET_CHUNK
    mask, qd, kd = _retention_tables()
    blk = (RET_CHUNK, RET_WIDTH)
    seg = lambda c: pl.BlockSpec(blk, lambda b, n: (b * nch + n, c))
    whole = lambda a: pl.BlockSpec(a.shape, lambda b, n: (0, 0, 0))
    return pl.pallas_call(
        _ret_kernel,
        grid=(batch, nch),
        in_specs=[seg(0), seg(1), seg(2), seg(3), whole(mask), whole(qd), whole(kd)],
        out_specs=seg(0),
        out_shape=jax.ShapeDtypeStruct((t, RET_WIDTH), BF16),
        scratch_shapes=[pltpu.VMEM((RET_HEADS, RET_HEAD_DIM, RET_HEAD_DIM), F32)],
        compiler_params=_cparams(("arbitrary", "arbitrary")),
        name="retention",
    )(proj, proj, proj, proj, jnp.asarray(mask), jnp.asarray(qd), jnp.asarray(kd))


def _s5_kernel(u_ref, win_ref, wout_ref, lam2_ref, y_ref, us_ref, up_ref, cb_ref, hb_ref, ynb_ref, yt_ref,
               st_ref):
    n = pl.program_id(1)
    tl = S5_TL
    nb = SUBLANES
    ns = S5_NSTATE
    npair = tl // 2
    e_even = slice(2 * ns, 2 * ns + LANES)
    e_odd = slice(2 * ns + LANES, 2 * ns + 2 * LANES)

    @pl.when(n == 0)
    def _init():
        st_ref[...] = jnp.zeros_like(st_ref)
        ynb_ref[0:nb, :] = jnp.zeros((nb, LANES), F32)

    for b in range(nb):
        us_ref[b * tl:(b + 1) * tl, :] = u_ref[b].astype(F32)
    for k in range(npair):
        up_ref[k * nb:(k + 1) * nb, 0:LANES] = us_ref[pl.ds(2 * k, nb, stride=tl), :]
        up_ref[k * nb:(k + 1) * nb, LANES:2 * LANES] = us_ref[pl.ds(2 * k + 1, nb, stride=tl), :]

    ar = jnp.broadcast_to(lam2_ref[0:1, :], (nb, ns))
    ai = jnp.broadcast_to(lam2_ref[1:2, :], (nb, ns))
    sr = st_ref[0:nb, :]
    si = st_ref[nb:2 * nb, :]

    grows = S5_GROUP * nb

    def project_in(g):
        rows = slice(g * grows, (g + 1) * grows)
        cb_ref[rows, :] = jnp.dot(up_ref[rows, :].astype(BF16), win_ref[...], preferred_element_type=F32)

    project_in(0)
    for g in range(npair // S5_GROUP):
        if g + 1 < npair // S5_GROUP:
            project_in(g + 1)
        for kk in range(0, S5_GROUP, 2):
            r0 = g * grows + kk * nb
            two_r = []
            two_i = []
            for r in (r0, r0 + nb):
                br = cb_ref[r:r + nb, 0:ns]
                bi = cb_ref[r:r + nb, ns:2 * ns]
                sr, si = ar * sr - ai * si + br, ar * si + ai * sr + bi
                two_r.append(sr)
                two_i.append(si)
            hb_ref[r0:r0 + 2 * nb, 0:ns] = jnp.concatenate(two_r, axis=0).astype(BF16)
            hb_ref[r0:r0 + 2 * nb, ns:2 * ns] = jnp.concatenate(two_i, axis=0).astype(BF16)
        rows = slice(g * grows, (g + 1) * grows)
        out = jnp.dot(hb_ref[rows, :], wout_ref[...], preferred_element_type=F32)
        y_odd = out[:, 0:LANES] + cb_ref[rows, e_odd]
        ynb_ref[nb + g * grows:nb + (g + 1) * grows, :] = out[:, LANES:2 * LANES]
        y_even = ynb_ref[rows, :] + cb_ref[rows, e_even]
        for k in range(S5_GROUP):
            t0 = 2 * (g * S5_GROUP + k)
            yt_ref[t0 * nb:(t0 + 1) * nb, :] = y_even[k * nb:(k + 1) * nb, :]
            yt_ref[(t0 + 1) * nb:(t0 + 2) * nb, :] = y_odd[k * nb:(k + 1) * nb, :]
    st_ref[0:nb, :] = sr
    st_ref[nb:2 * nb, :] = si
    ynb_ref[0:nb, :] = ynb_ref[npair * nb:(npair + 1) * nb, :]
    for b in range(nb):
        y_ref[b] = yt_ref[pl.ds(b, tl, stride=nb), :].astype(BF16)


def _s5_tables(lam_re, lam_im, b_re, b_im, c_re, c_im, d, log_dt):
    lam = lax.complex(lam_re.astype(F32), lam_im.astype(F32))
    dt = jnp.exp(log_dt.astype(F32))[:, None]
    lam_bar = jnp.exp(lam * dt)
    b_bar = ((lam_bar - 1.0) / lam)[..., None] * lax.complex(b_re.astype(F32), b_im.astype(F32))
    gpt = LANES // SSM_GROUP
    eye = jnp.eye(gpt, dtype=F32)

    def bdiag_in(m):
        m = m.reshape(S5_LT, gpt, SSM_STATE, SSM_GROUP)
        return jnp.einsum('jgpi,gh->jgihp', m, eye).reshape(S5_LT, LANES, gpt * SSM_STATE)

    def bdiag_out(m):
        m = m.reshape(S5_LT, gpt, SSM_GROUP, SSM_STATE)
        return jnp.einsum('jgop,gh->jgpho', m, eye).reshape(S5_LT, gpt * SSM_STATE, LANES)

    ar = jnp.real(lam_bar).reshape(S5_LT, S5_NSTATE)
    ai = jnp.imag(lam_bar).reshape(S5_LT, S5_NSTATE)
    bre, bim = bdiag_in(jnp.real(b_bar)), bdiag_in(jnp.imag(b_bar))
    ctop, cbot = bdiag_out(c_re.astype(F32)), -bdiag_out(c_im.astype(F32))
    bare = bre * ar[:, None, :] - bim * ai[:, None, :]
    baim = bre * ai[:, None, :] + bim * ar[:, None, :]
    catop = ctop * ar[:, :, None] + cbot * ai[:, :, None]
    cabot = cbot * ar[:, :, None] - ctop * ai[:, :, None]
    hi = lax.Precision.HIGHEST
    cb = (jnp.einsum('jis,jso->jio', bre, ctop, precision=hi)
          + jnp.einsum('jis,jso->jio', bim, cbot, precision=hi))
    dmat = d.astype(F32).reshape(S5_LT, LANES)[:, :, None] * jnp.eye(LANES, dtype=F32)
    zero = jnp.zeros_like(dmat)
    win = jnp.concatenate([jnp.concatenate([bare, baim, cb + dmat, zero], axis=-1),
                           jnp.concatenate([bre, bim, zero, dmat], axis=-1)], axis=1).astype(BF16)
    wout = jnp.concatenate([jnp.concatenate([ctop, catop], axis=-1),
                            jnp.concatenate([cbot, cabot], axis=-1)], axis=1).astype(BF16)
    lam2 = jnp.stack([ar * ar - ai * ai, 2.0 * ar * ai], axis=1)
    return win, wout, lam2


def _s5(proj, tables, layer, batch, seq):
    win, wout, lam2 = tables
    assert batch == SUBLANES
    proj3 = proj.reshape(batch, seq, IN_PROJ_WIDTH)
    ucol = (4 * RET_WIDTH) // LANES
    tl = S5_TL
    npair = tl // 2
    assert npair % S5_GROUP == 0 and S5_GROUP % 2 == 0
    tile = lambda j, n: (layer, j, 0, 0)
    return pl.pallas_call(
        _s5_kernel,
        grid=(S5_LT, seq // tl),
        in_specs=[
            pl.BlockSpec((batch, tl, LANES), lambda j, n: (0, n, ucol + j)),
            pl.BlockSpec((None, None, 2 * LANES, 2 * S5_NSTATE + 2 * LANES), tile),
            pl.BlockSpec((None, None, 2 * S5_NSTATE, 2 * LANES), tile),
            pl.BlockSpec((None, None, 2, S5_NSTATE), tile),
        ],
        out_specs=pl.BlockSpec((batch, tl, LANES), lambda j, n: (0, n, j)),
        out_shape=jax.ShapeDtypeStruct((batch, seq, SSM_WIDTH), BF16),
        scratch_shapes=[
            pltpu.VMEM((batch * tl, LANES), F32),
            pltpu.VMEM((batch * npair, 2 * LANES), F32),
            pltpu.VMEM((batch * npair, 2 * S5_NSTATE + 2 * LANES), F32),
            pltpu.VMEM((batch * npair, 2 * S5_NSTATE), BF16),
            pltpu.VMEM((batch * (npair + 1), LANES), F32),
            pltpu.VMEM((batch * tl, LANES), F32),
            pltpu.VMEM((2 * SUBLANES, S5_NSTATE), F32),
        ],
        compiler_params=_cparams(("arbitrary", "arbitrary")),
        name="s5",
    )(proj3, win, wout, lam2)


def _layer_norm_rows(r, g, b):
    mu = jnp.mean(r, axis=-1, keepdims=True)
    rc = r - mu
    var = jnp.mean(rc * rc, axis=-1, keepdims=True)
    return rc * lax.rsqrt(var + LN_EPS) * g + b


def _post_mix_kernel(x_ref, ret_ref, y_ref, wglu_ref, wout_ref, lng_ref, lnb_ref,
                     rcat_ref, rhi_ref, rb_ref, tri_ref,
                     x1_ref, x1p_ref, e_ref, w_ref, rank_ref, cnt_ref, carry_ref):
    i = pl.program_id(0)

    @pl.when(i == 0)
    def _init():
        carry_ref[...] = jnp.zeros_like(carry_ref)

    args = (x_ref, ret_ref, y_ref, wglu_ref, wout_ref, lng_ref, lnb_ref, rcat_ref, rhi_ref, rb_ref, tri_ref,
            x1_ref, x1p_ref, e_ref, w_ref, rank_ref, carry_ref)
    subs = [_post_mix_phases(k, *args) for k in range(TM_POST // TSUB_POST)]
    nph = len(subs[0])
    for step in range(nph + len(subs) - 1):
        for k, phases in enumerate(subs):
            if 0 <= step - k < nph:
                phases[step - k]()
    cnt_ref[...] = carry_ref[...].astype(I32)


def _post_mix_phases(sub, x_ref, ret_ref, y_ref, wglu_ref, wout_ref, lng_ref, lnb_ref,
                     rcat_ref, rhi_ref, rb_ref, tri_ref, x1_ref, x1p_ref, e_ref, w_ref, rank_ref, carry_ref):
    tm = TSUB_POST
    rows = slice(sub * tm, (sub + 1) * tm)
    st = {}

    def gelu():
        st['ya'] = jax.nn.gelu(y_ref[rows, :].astype(F32))

    def glu_matmul():
        st['z'] = jnp.dot(st['ya'].astype(BF16), wglu_ref[...], preferred_element_type=F32)

    def glu_gate():
        st['ssm'] = (st.pop('ya') * _sigmoid(st.pop('z'))).astype(BF16)

    def out_matmul():
        mixed = jnp.concatenate([ret_ref[rows, :], st.pop('ssm')], axis=1)
        st['h'] = jnp.dot(mixed, wout_ref[...], preferred_element_type=F32)

    def norm():
        x1 = _layer_norm_rows(DEEPNORM_ALPHA * x_ref[rows, :] + st.pop('h'), lng_ref[...], lnb_ref[...])
        x1_ref[rows, :] = x1
        _store_token_tiles(x1p_ref, sub * tm, tm, _pack_rows(x1))
        st['xh'] = x1.astype(BF16)
        st['xl'] = (x1 - st['xh'].astype(F32)).astype(BF16)

    def router_matmul():
        nt = (((1,), (1,)), ((), ()))
        l1 = lax.dot_general(rcat_ref[...], st.pop('xh'), nt, preferred_element_type=F32)
        l2 = lax.dot_general(rhi_ref[...], st.pop('xl'), nt, preferred_element_type=F32)
        st['logits'] = l1[0:N_EXPERTS] + l1[N_EXPERTS:] + l2 + rb_ref[...]

    def route():
        _route(st.pop('logits'), rows, tri_ref, e_ref, w_ref, rank_ref, carry_ref)

    return [gelu, glu_matmul, glu_gate, out_matmul, norm, router_matmul, route]


def _route(logits, rows, tri_ref, e_ref, w_ref, rank_ref, carry_ref):
    tm = TSUB_POST
    m = jnp.max(logits, axis=0, keepdims=True)
    ex = jnp.exp(logits - m)
    p = ex / jnp.sum(ex, axis=0, keepdims=True)

    eg = EXPERTS_PER_GROUP
    iota_g = lax.broadcasted_iota(I32, (eg, tm), 0)
    best = None
    for g in range(N_EXPERT_GROUPS):
        pg = p[g * eg:(g + 1) * eg]
        m1 = jnp.max(pg, axis=0, keepdims=True)
        i1 = jnp.min(jnp.where(pg == m1, iota_g, eg), axis=0, keepdims=True)
        pg2 = jnp.where(iota_g == i1, -1.0, pg)
        m2 = jnp.max(pg2, axis=0, keepdims=True)
        i2 = jnp.min(jnp.where(pg2 == m2, iota_g, eg), axis=0, keepdims=True)
        sg = m1 + m2
        if best is None:
            best = (sg, m1, m2, i1, i2)
        else:
            better = sg > best[0]
            cand = (sg, m1, m2, i1 + g * eg, i2 + g * eg)
            best = tuple(jnp.where(better, c, o) for c, o in zip(cand, best))
    _, v1, v2, e1, e2 = best
    tot = v1 + v2
    e_ref[0:1, rows] = e1
    e_ref[1:2, rows] = e2
    w_ref[0:1, rows] = v1 / tot
    w_ref[1:2, rows] = v2 / tot

    iota_e = lax.broadcasted_iota(I32, (N_EXPERTS, tm), 0)
    oh1 = iota_e == e1
    oh2 = iota_e == e2
    oh = jnp.where(oh1, 1.0, jnp.where(oh2, 1.0, 0.0))
    before = jnp.dot(oh.astype(BF16), tri_ref[...], preferred_element_type=F32) + carry_ref[:, 0:1]
    rank_ref[0:1, rows] = jnp.sum(jnp.where(oh1, before, 0.0), axis=0, keepdims=True).astype(I32)
    rank_ref[1:2, rows] = jnp.sum(jnp.where(oh2, before, 0.0), axis=0, keepdims=True).astype(I32)
    carry_ref[...] = carry_ref[...] + jnp.sum(oh, axis=1, keepdims=True)


def _post_mix(x2, ret, y, wglu_bf, wout_bf, lng, lnb, layer, rcat, rhi, rb):
    t, d = x2.shape
    tm = TM_POST
    ts = TSUB_POST
    tri = jnp.asarray(np.triu(np.ones((ts, ts), np.float32), 1), BF16)
    const = lambda i: (0, 0)
    lyr = lambda i: (layer, 0, 0)
    tok = lambda i: (i, 0)
    lane = lambda i: (0, i)
    return pl.pallas_call(
        _post_mix_kernel,
        grid=(t // tm,),
        in_specs=[
            pl.BlockSpec((tm, d), tok),
            pl.BlockSpec((tm, RET_WIDTH), tok),
            pl.BlockSpec((tm, SSM_WIDTH), tok),
            pl.BlockSpec((None, SSM_WIDTH, SSM_WIDTH), lyr),
            pl.BlockSpec((None, RET_WIDTH + SSM_WIDTH, d), lyr),
            pl.BlockSpec((None, 1, d), lyr),
            pl.BlockSpec((None, 1, d), lyr),
            pl.BlockSpec((2 * N_EXPERTS, d), const),
            pl.BlockSpec((N_EXPERTS, d), const),
            pl.BlockSpec((N_EXPERTS, 1), const),
            pl.BlockSpec((ts, ts), const),
        ],
        out_specs=[
            pl.BlockSpec((tm, d), tok),
            pl.BlockSpec((tm * PACK_SUB, LANES), tok),
            pl.BlockSpec((TOP_K, tm), lane),
            pl.BlockSpec((TOP_K, tm), lane),
            pl.BlockSpec((TOP_K, tm), lane),
            pl.BlockSpec((N_EXPERTS, LANES), const),
        ],
        out_shape=[
            jax.ShapeDtypeStruct((t, d), F32),
            jax.ShapeDtypeStruct((t * PACK_SUB, LANES), U32),
            jax.ShapeDtypeStruct((TOP_K, t), I32),
            jax.ShapeDtypeStruct((TOP_K, t), F32),
            jax.ShapeDtypeStruct((TOP_K, t), I32),
            jax.ShapeDtypeStruct((N_EXPERTS, LANES), I32),
        ],
        scratch_shapes=[pltpu.VMEM((N_EXPERTS, LANES), F32)],
        compiler_params=_cparams(("arbitrary",)),
        name="post_mix",
    )(x2, ret, y, wglu_bf, wout_bf, lng, lnb, rcat, rhi, rb, tri)


def _dispatch_kernel(zrow_ref, d0_ref, d1_ref, x_ref, xb_hbm, zbuf_ref, sem, zsem):
    tm = TM_DISP

    @pl.when(pl.program_id(0) == 0)
    def _zero_fill():
        zbuf_ref[...] = jnp.zeros_like(zbuf_ref)

        def zcopy(k):
            n = ROW_TILE * PACK_SUB
            rows = pl.ds(pl.multiple_of(zrow_ref[k] * PACK_SUB, n), n)
            return pltpu.make_async_copy(zbuf_ref, xb_hbm.at[rows], zsem.at[0])

        def start(k, c):
            @pl.when(zrow_ref[k] >= 0)
            def _():
                zcopy(k).start()
            return c

        def wait(k, c):
            @pl.when(zrow_ref[k] >= 0)
            def _():
                zcopy(k).wait()
            return c

        lax.fori_loop(0, 2 * N_EXPERTS, start, 0)
        lax.fori_loop(0, 2 * N_EXPERTS, wait, 0)

    def issue(tb, c):
        for u in range(ISSUE_UNROLL):
            t = tb * ISSUE_UNROLL + u
            src = x_ref.at[pl.ds(t * PACK_SUB, PACK_SUB)]
            tile = lambda r: pl.ds(pl.multiple_of(r, PACK_SUB), PACK_SUB)
            pltpu.make_async_copy(src, xb_hbm.at[tile(d0_ref[0, 0, t])], sem.at[0]).start(priority=0)
            pltpu.make_async_copy(src, xb_hbm.at[tile(d1_ref[0, 0, t])], sem.at[0]).start(priority=1)
        return c

    lax.fori_loop(0, tm // ISSUE_UNROLL, issue, 0)
    for _ in range(TOP_K):
        pltpu.make_async_copy(x_ref, xb_hbm.at[pl.ds(0, tm * PACK_SUB)], sem.at[0]).wait()


def _dispatch(x1p, dest_tile, zrow, n_rows):
    t = x1p.shape[0] // PACK_SUB
    tm = TM_DISP
    nstep = t // tm
    d0 = dest_tile[0].reshape(nstep, 1, tm)
    d1 = dest_tile[1].reshape(nstep, 1, tm)
    smem_blk = pl.BlockSpec((1, 1, tm), lambda i, z: (i, 0, 0), memory_space=pltpu.SMEM)
    grid_spec = pltpu.PrefetchScalarGridSpec(
        num_scalar_prefetch=1,
        grid=(nstep,),
        in_specs=[smem_blk, smem_blk, pl.BlockSpec((tm * PACK_SUB, LANES), lambda i, z: (i, 0))],
        out_specs=pl.BlockSpec(memory_space=pl.ANY),
        scratch_shapes=[pltpu.VMEM((ROW_TILE * PACK_SUB, LANES), U32),
                        pltpu.SemaphoreType.DMA((1,)), pltpu.SemaphoreType.DMA((1,))],
    )
    return pl.pallas_call(
        _dispatch_kernel,
        grid_spec=grid_spec,
        out_shape=jax.ShapeDtypeStruct((n_rows * PACK_SUB, LANES), U32),
        compiler_params=_cparams(("arbitrary",)),
        name="dispatch",
    )(zrow, d0, d1, x1p)


def _experts_kernel(se_ref, row_ref, nt_ref, nz_ref, xb_hbm, wg_ref, wu_ref, wd_ref, yb_hbm,
                    xs_ref, acc_ref, stg_in, stg_out, sem_in, sem_out):
    del se_ref
    s = pl.program_id(0)
    j = pl.program_id(1)
    tiles = lambda nhalf: (nhalf + 1) // 2
    nt = tiles(nt_ref[s])
    nfull = nt_ref[s] // 2
    nz = nz_ref[s]
    row0 = row_ref[s]
    rt = ROW_TILE

    nsuper = pl.num_programs(0)
    s_next = jnp.minimum(s + 1, nsuper - 1)
    prev_nt = jnp.where(s > 0, tiles(nt_ref[jnp.maximum(s - 1, 0)]), 0)
    next_nt = jnp.where(s + 1 < nsuper, nt_ref[s_next], 0)

    def rows(i, base=None):
        n = rt * PACK_SUB
        base = row0 if base is None else base
        return pl.ds(pl.multiple_of((base + i * rt) * PACK_SUB, n), n)

    def in_copy(i, slot):
        return pltpu.make_async_copy(xb_hbm.at[rows(i)], stg_in.at[slot], sem_in.at[slot])

    def out_copy(i, slot):
        return pltpu.make_async_copy(stg_out.at[slot], yb_hbm.at[rows(i)], sem_out.at[slot])

    def partial_out(i, m):
        r = pl.multiple_of(i * rt, rt)
        xi = xs_ref[pl.ds(r, m), :]
        g = jnp.dot(xi, wg_ref[...].astype(BF16), preferred_element_type=F32)
        u = jnp.dot(xi, wu_ref[...].astype(BF16), preferred_element_type=F32)
        hj = (g * _sigmoid(g) * u).astype(BF16)
        return r, jnp.dot(hj, wd_ref[...].astype(BF16), preferred_element_type=F32)

    def load_tile(i):
        slot = i % 2

        @pl.when(i + 1 < nt)
        def _():
            in_copy(i + 1, 1 - slot).start()

        in_copy(i, slot).wait()
        xrows = pl.ds(pl.multiple_of(i * rt, rt), rt)
        for c, words in enumerate(_load_token_tiles(stg_in.at[slot], 0, rt)):
            xs_ref[xrows, c * LANES:(c + 1) * LANES] = _unpack_lo(words).astype(BF16)
            xs_ref[xrows, PACK_WORDS + c * LANES:PACK_WORDS + (c + 1) * LANES] = _unpack_hi(words).astype(BF16)

    def store_tile(i, vals):
        slot = i % 2

        @pl.when(i >= 2)
        def _():
            out_copy(i - 2, slot).wait()

        _store_token_tiles(stg_out.at[slot], 0, rt, _pack_rows(vals))
        out_copy(i, slot).start()

    def for_tiles(body):
        def pair(p, c):
            body(2 * p, 2, 2 * rt)
            return c

        lax.fori_loop(0, nfull // 2, pair, 0)

        @pl.when(nfull % 2 == 1)
        def _():
            body(nfull - 1, 1, rt)

        @pl.when(nt > nfull)
        def _():
            body(nfull, 1, rt // 2)

    @pl.when((nz > 0) & (j == 0))
    def _zero_tail():
        stg_out[0] = jnp.zeros(stg_out.shape[1:], U32)

        def start(i, c):
            out_copy(i, 0).start()
            return c

        def wait(i, c):
            out_copy(i, 0).wait()
            return c

        lax.fori_loop(0, nz, start, 0)
        lax.fori_loop(0, nz, wait, 0)

    @pl.when(nt > 0)
    def _work():
        @pl.when(j == 0)
        def _first():
            @pl.when(prev_nt == 0)
            def _():
                in_copy(0, 0).start()

            @pl.when(prev_nt >= 2)
            def _():
                out_copy(0, prev_nt % 2).wait()

            @pl.when(prev_nt >= 1)
            def _():
                out_copy(0, (prev_nt - 1) % 2).wait()

            def body(i, ntiles, m):
                for k in range(ntiles):
                    load_tile(i + k)
                r, part = partial_out(i, m)
                acc_ref[pl.ds(r, m), :] = part

            for_tiles(body)

        @pl.when((j > 0) & (j < N_DE_CHUNKS - 1))
        def _mid():
            def body(i, ntiles, m):
                r, part = partial_out(i, m)
                acc_ref[pl.ds(r, m), :] += part

            for_tiles(body)

        @pl.when(j == N_DE_CHUNKS - 1)
        def _last():
            @pl.when(next_nt > 0)
            def _():
                pltpu.make_async_copy(xb_hbm.at[rows(0, row_ref[s_next])], stg_in.at[0], sem_in.at[0]).start()

            def body(i, ntiles, m):
                r, part = partial_out(i, m)
                total = acc_ref[pl.ds(r, m), :] + part
                if m < rt:
                    total = jnp.concatenate([total, jnp.zeros((rt - m, total.shape[1]), F32)], axis=0)
                for k in range(ntiles):
                    store_tile(i + k, total[k * rt:(k + 1) * rt, :])

            for_tiles(body)

            @pl.when(next_nt == 0)
            def _drain():
                @pl.when(nt >= 2)
                def _():
                    out_copy(nt - 2, nt % 2).wait()

                out_copy(nt - 1, (nt - 1) % 2).wait()


def _experts(xb, se, row, ntl, nzl, w_gate, w_up, w_down, layer):
    p = xb.shape[0] // PACK_SUB
    d = D_MODEL
    nsuper = se.shape[0]
    last = N_DE_CHUNKS - 1
    assert last >= 1

    def jj(s, j, nt_ref):
        return jnp.where(nt_ref[s] > 0, j, last)

    grid_spec = pltpu.PrefetchScalarGridSpec(
        num_scalar_prefetch=4,
        grid=(nsuper, N_DE_CHUNKS),
        in_specs=[
            pl.BlockSpec(memory_space=pl.ANY),
            pl.BlockSpec((None, None, d, DE_CHUNK),
                         lambda s, j, se_r, row_r, nt_r, nz_r: (layer, se_r[s], 0, jj(s, j, nt_r))),
            pl.BlockSpec((None, None, d, DE_CHUNK),
                         lambda s, j, se_r, row_r, nt_r, nz_r: (layer, se_r[s], 0, jj(s, j, nt_r))),
            pl.BlockSpec((None, None, DE_CHUNK, d),
                         lambda s, j, se_r, row_r, nt_r, nz_r: (layer, se_r[s], jj(s, j, nt_r), 0)),
        ],
        out_specs=pl.BlockSpec(memory_space=pl.ANY),
        scratch_shapes=[
            pltpu.VMEM((SUPER, d), BF16),
            pltpu.VMEM((SUPER, d), F32),
            pltpu.VMEM((2, ROW_TILE * PACK_SUB, LANES), U32),
            pltpu.VMEM((2, ROW_TILE * PACK_SUB, LANES), U32),
            pltpu.SemaphoreType.DMA((2,)),
            pltpu.SemaphoreType.DMA((2,)),
        ],
    )
    return pl.pallas_call(
        _experts_kernel,
        grid_spec=grid_spec,
        out_shape=jax.ShapeDtypeStruct((p * PACK_SUB, LANES), U32),
        compiler_params=_cparams(("arbitrary", "arbitrary"), vmem=EXPERTS_VMEM_LIMIT),
        name="experts",
    )(se, row, ntl, nzl, xb, w_gate, w_up, w_down)


def _combine_kernel(d0_ref, d1_ref, d0n_ref, d1n_ref, x1_ref, w_ref, lng_ref, lnb_ref, yb_hbm,
                    o_ref, ybuf_ref, sem):
    tm = TM_COMB
    i = pl.program_id(0)
    slot = i % 2

    def gather(da_ref, db_ref, into):
        def issue(tb, c):
            for u in range(ISSUE_UNROLL):
                t = tb * ISSUE_UNROLL + u
                dst = pl.ds(pl.multiple_of(t * PACK_SUB, PACK_SUB), PACK_SUB)
                tile = lambda r: pl.ds(pl.multiple_of(r, PACK_SUB), PACK_SUB)
                pltpu.make_async_copy(yb_hbm.at[tile(da_ref[0, 0, t])], ybuf_ref.at[into, 0, dst],
                                      sem.at[into]).start(priority=0)
                pltpu.make_async_copy(yb_hbm.at[tile(db_ref[0, 0, t])], ybuf_ref.at[into, 1, dst],
                                      sem.at[into]).start(priority=1)
            return c

        lax.fori_loop(0, tm // ISSUE_UNROLL, issue, 0)

    @pl.when(i == 0)
    def _prime():
        gather(d0_ref, d1_ref, 0)

    for k in range(TOP_K):
        pltpu.make_async_copy(ybuf_ref.at[slot, k], ybuf_ref.at[slot, k], sem.at[slot]).wait()

    @pl.when(i + 1 < pl.num_programs(0))
    def _prefetch():
        gather(d0n_ref, d1n_ref, 1 - slot)

    wpad = jnp.concatenate([w_ref[...], jnp.zeros((LANES - TOP_K, tm), F32)], axis=0)
    wt = wpad.T
    w0 = wt[:, 0:1]
    w1 = wt[:, 1:2]
    x1 = x1_ref[...]
    y0 = _load_token_tiles(ybuf_ref.at[slot, 0], 0, tm)
    y1 = _load_token_tiles(ybuf_ref.at[slot, 1], 0, tm)
    lo = [_unpack_lo(a) * w0 + _unpack_lo(b) * w1 for a, b in zip(y0, y1)]
    hi = [_unpack_hi(a) * w0 + _unpack_hi(b) * w1 for a, b in zip(y0, y1)]
    moe = jnp.concatenate(lo + hi, axis=1)
    o_ref[...] = _layer_norm_rows(DEEPNORM_ALPHA * x1 + moe, lng_ref[...], lnb_ref[...])


def _combine(x1, dest_tile, wts, lng, lnb, layer, yb):
    t, d = x1.shape
    tm = TM_COMB
    nstep = t // tm
    d0 = dest_tile[0].reshape(nstep, 1, tm)
    d1 = dest_tile[1].reshape(nstep, 1, tm)
    smem_blk = pl.BlockSpec((1, 1, tm), lambda i: (i, 0, 0), memory_space=pltpu.SMEM)
    smem_next = pl.BlockSpec((1, 1, tm), lambda i: (jnp.minimum(i + 1, nstep - 1), 0, 0),
                             memory_space=pltpu.SMEM)
    lyr = lambda i: (layer, 0, 0)
    return pl.pallas_call(
        _combine_kernel,
        grid=(nstep,),
        in_specs=[smem_blk, smem_blk, smem_next, smem_next,
                  pl.BlockSpec((tm, d), lambda i: (i, 0)),
                  pl.BlockSpec((TOP_K, tm), lambda i: (0, i)),
                  pl.BlockSpec((None, 1, d), lyr),
                  pl.BlockSpec((None, 1, d), lyr),
                  pl.BlockSpec(memory_space=pl.ANY)],
        out_specs=pl.BlockSpec((tm, d), lambda i: (i, 0)),
        out_shape=jax.ShapeDtypeStruct((t, d), F32),
        scratch_shapes=[pltpu.VMEM((2, TOP_K, tm * PACK_SUB, LANES), U32), pltpu.SemaphoreType.DMA((2,))],
        compiler_params=_cparams(("arbitrary",)),
        name="combine",
    )(d0, d1, d0, d1, x1, wts, lng, lnb, yb)


def _rotary_tables(seq):
    inv = 1.0 / (ROPE_BASE ** (jnp.arange(0, RET_HEAD_DIM, 2, dtype=F32) / RET_HEAD_DIM))
    ang = jnp.arange(seq, dtype=F32)[:, None] * inv[None, :]
    return jnp.cos(ang), jnp.sin(ang)


def _routing_tables(counts, n_rows):
    ids = np.arange(N_EXPERTS)
    upto = jnp.asarray(ids[None, :] <= ids[:, None])
    csum = lambda v: jnp.sum(jnp.where(upto, v[None, :], 0), axis=1)
    padded = (counts + ROW_TILE - 1) // ROW_TILE * ROW_TILE
    pend = csum(padded)
    pstart = pend - padded
    ntile = padded // ROW_TILE
    nsup = (ntile + TILES_PER_SUPER - 1) // TILES_PER_SUPER
    send = csum(nsup)
    sstart = send - nsup
    nsuper = -(-n_rows // SUPER) + N_EXPERTS
    s = jnp.arange(nsuper, dtype=I32)
    total = send[-1]
    which = lambda q: jnp.minimum(jnp.sum((send[None, :] <= q[:, None]).astype(I32), axis=1), N_EXPERTS - 1)
    pick = lambda v, e: jnp.sum(jnp.where(e[:, None] == ids[None, :], v[None, :], 0), axis=1)
    se = which(s)
    valid = s < total
    k = s - pick(sstart, se)
    row = pick(pstart, se) + k * SUPER
    nt = jnp.clip(pick(ntile, se) - k * TILES_PER_SUPER, 0, TILES_PER_SUPER)
    idle = s - total
    tail_row = pend[-1] + idle * SUPER
    nz = jnp.clip((n_rows - tail_row) // ROW_TILE, 0, TILES_PER_SUPER)
    se = jnp.where(valid, se, which((total - 1)[None])[0])
    row = jnp.where(valid, row, jnp.minimum(tail_row, n_rows - ROW_TILE))
    nt = jnp.where(valid, nt, 0)
    nz = jnp.where(valid, 0, nz)
    last_fill = counts - (ntile - 1) * ROW_TILE
    half_last = ((counts > 0) & (last_fill <= ROW_TILE // 2)).astype(I32)
    ends_expert = (k * TILES_PER_SUPER + nt) == pick(ntile, se)
    nhalf = 2 * nt - jnp.where(valid & ends_expert, pick(half_last, se), 0)
    seg_last = jnp.where(padded > 0, pend - ROW_TILE, -1)
    tail = pend[-1] + jnp.arange(N_EXPERTS, dtype=I32) * ROW_TILE
    zrow = jnp.concatenate([seg_last, jnp.where(tail < n_rows, tail, -1)])
    return (pstart.astype(I32), se.astype(I32), row.astype(I32), nhalf.astype(I32), nz.astype(I32),
            zrow.astype(I32))


def kernel(x, w_in, w_out, ssm_lambda_re, ssm_lambda_im, ssm_b_re, ssm_b_im, ssm_c_re, ssm_c_im,
           ssm_d, ssm_log_dt, w_glu, ln1_g, ln1_b, ln2_g, ln2_b, router_w, router_b,
           w_gate, w_up, w_down):
    batch, seq, d = x.shape
    t = batch * seq
    n_assign = t * TOP_K
    n_rows = n_assign + N_EXPERTS * ROW_TILE
    assert seq % TM_PROJ == 0 and seq % RET_CHUNK == 0 and seq % S5_TL == 0
    assert t % TM_POST == 0 and t % TM_COMB == 0 and t % TM_DISP == 0 and n_rows % ROW_TILE == 0

    cos, sin = _rotary_tables(seq)
    expert_ids = jnp.arange(N_EXPERTS, dtype=I32)
    rw_t = router_w.astype(F32).T
    rhi = rw_t.astype(BF16)
    rlo = (rw_t - rhi.astype(F32)).astype(BF16)
    rcat = jnp.concatenate([rhi, rlo], axis=0)
    rb = router_b.astype(F32).reshape(N_EXPERTS, 1)

    w_in_bf = _prep_in_proj_weights(w_in)
    s5_tables = jax.vmap(_s5_tables)(ssm_lambda_re, ssm_lambda_im, ssm_b_re, ssm_b_im,
                                     ssm_c_re, ssm_c_im, ssm_d, ssm_log_dt)
    w_glu_bf = w_glu.astype(BF16)
    w_out_bf = w_out.astype(BF16)
    row3 = lambda p: p.astype(F32).reshape(DEPTH, 1, d)
    ln1_g, ln1_b, ln2_g, ln2_b = row3(ln1_g), row3(ln1_b), row3(ln2_g), row3(ln2_b)

    x2 = x.reshape(t, d)
    for l in range(DEPTH):
        proj = _in_proj(x2, w_in_bf, l, cos, sin, seq)
        ret = _retention(proj, batch, seq)
        y = _s5(proj, s5_tables, l, batch, seq).reshape(t, SSM_WIDTH)
        x1, x1p, e, wts, rank, cnt = _post_mix(
            x2, ret, y, w_glu_bf, w_out_bf, ln1_g, ln1_b, l, rcat, rhi, rb)
        pstart, se, row, ntl, nzl, zrow = _routing_tables(cnt[:, 0], n_rows)
        dest = rank + jnp.sum(jnp.where(e[..., None] == expert_ids, pstart, 0), axis=-1)
        dest_tile = dest * PACK_SUB
        xb = _dispatch(x1p, dest_tile, zrow, n_rows)
        yb = _experts(xb, se, row, ntl, nzl, w_gate, w_up, w_down, l)
        x2 = _combine(x1, dest_tile, wts, ln2_g, ln2_b, l, yb)
    return x2.reshape(batch, seq, d)
```

```python
import numpy as np
import jax
import jax.numpy as jnp
from jax import lax
from jax.experimental import pallas as pl
from jax.experimental.pallas import tpu as pltpu

F32 = jnp.float32
BF16 = jnp.bfloat16
I32 = jnp.int32

D_MODEL = 2048
DEPTH = 2
RET_WIDTH = 1024
SSM_WIDTH = 1024
RET_HEAD_DIM = 256
RET_HEADS = RET_WIDTH // RET_HEAD_DIM
ROPE_BASE = 10000.0
SSM_GROUP = 16
SSM_GROUPS = SSM_WIDTH // SSM_GROUP
SSM_STATE = 64
IN_PROJ_WIDTH = 4 * RET_WIDTH + SSM_WIDTH
N_EXPERTS = 32
N_EXPERT_GROUPS = 4
EXPERTS_PER_GROUP = N_EXPERTS // N_EXPERT_GROUPS
TOP_K = 2
D_EXPERT = D_MODEL // 2
LN_EPS = 1e-5
DEEPNORM_ALPHA = (2.0 * DEPTH) ** 0.25

LANES = 128
SUBLANES = 8
VMEM_LIMIT = 56 * 1024 * 1024
EXPERTS_VMEM_LIMIT = 60 * 1024 * 1024

TM_PROJ = 1024
TN_PROJ = 1024
RET_CHUNK = 512
S5_TL = 512
S5_GROUP = 32
S5_LT = SSM_WIDTH // LANES
S5_NSTATE = (LANES // SSM_GROUP) * SSM_STATE
TM_POST = 512
TSUB_POST = 512
TM_COMB = 256
ROW_TILE = 256
TILES_PER_SUPER = 6
SUPER = ROW_TILE * TILES_PER_SUPER
DE_CHUNK = 512
N_DE_CHUNKS = D_EXPERT // DE_CHUNK
TM_DISP = 1024
ISSUE_UNROLL = 8


def _sigmoid(x):
    return 1.0 / (1.0 + jnp.exp(-x))


def _cparams(sem, vmem=VMEM_LIMIT):
    return pltpu.CompilerParams(dimension_semantics=sem, vmem_limit_bytes=vmem)


PACK_WORDS = D_MODEL // 2
PACK_SUB = PACK_WORDS // LANES
U32 = jnp.uint32


def _pack_rows(x):
    bits = lambda v: lax.bitcast_convert_type(v.astype(BF16).astype(F32), U32)
    return (bits(x[:, :PACK_WORDS]) >> 16) | (bits(x[:, PACK_WORDS:]) & jnp.uint32(0xFFFF0000))


def _unpack_lo(w):
    return lax.bitcast_convert_type(w << 16, F32)


def _unpack_hi(w):
    return lax.bitcast_convert_type(w & jnp.uint32(0xFFFF0000), F32)


def _store_token_tiles(ref, row0, n, packed):
    for c in range(PACK_SUB):
        ref[pl.ds(row0 * PACK_SUB + c, n, stride=PACK_SUB), :] = packed[:, c * LANES:(c + 1) * LANES]


def _load_token_tiles(ref, row0, n):
    return [ref[pl.ds(row0 * PACK_SUB + c, n, stride=PACK_SUB), :] for c in range(PACK_SUB)]


def _in_proj_kernel(x_ref, w_ref, cos_ref, sin_ref, o_ref):
    j = pl.program_id(1)
    acc = jnp.dot(x_ref[...].astype(BF16), w_ref[...], preferred_element_type=F32)
    is_rot = j < 2
    is_gate = j == 3
    scale = jnp.where(j == 1, RET_HEAD_DIM ** -0.5, 1.0).astype(F32)
    c = jnp.where(is_rot, cos_ref[...] * scale, 1.0)
    s = jnp.where(is_rot, sin_ref[...] * scale, 0.0)
    half = RET_HEAD_DIM // 2
    for h in range(RET_HEADS):
        lo = h * RET_HEAD_DIM
        t1 = acc[:, lo:lo + half]
        t2 = acc[:, lo + half:lo + RET_HEAD_DIM]
        r1 = t1 * c - t2 * s
        r2 = t1 * s + t2 * c
        o_ref[:, lo:lo + half] = (r1 * jnp.where(is_gate, _sigmoid(r1), 1.0)).astype(BF16)
        o_ref[:, lo + half:lo + RET_HEAD_DIM] = (r2 * jnp.where(is_gate, _sigmoid(r2), 1.0)).astype(BF16)


def _prep_kernel(w_ref, p_ref, o_ref):
    o_ref[...] = jnp.dot(w_ref[...].astype(BF16), p_ref[...], preferred_element_type=F32).astype(BF16)


def _prep_in_proj_weights(w_in):
    depth, d, n = w_in.shape
    hd = RET_HEAD_DIM
    half = hd // 2
    perm = np.zeros((2, hd, hd), np.float32)
    for i in range(half):
        perm[0, 2 * i, i] = 1.0
        perm[0, 2 * i + 1, half + i] = 1.0
    perm[1] = np.eye(hd, dtype=np.float32)
    n_qk_blocks = 2 * RET_WIDTH // hd
    return pl.pallas_call(
        _prep_kernel,
        grid=(depth, n // hd),
        in_specs=[
            pl.BlockSpec((None, d, hd), lambda l, c: (l, 0, c)),
            pl.BlockSpec((None, hd, hd), lambda l, c: (jnp.where(c < n_qk_blocks, 0, 1), 0, 0)),
        ],
        out_specs=pl.BlockSpec((None, d, hd), lambda l, c: (l, 0, c)),
        out_shape=jax.ShapeDtypeStruct((depth, d, n), BF16),
        compiler_params=_cparams(("arbitrary", "arbitrary")),
        name="prep_w_in",
    )(w_in, jnp.asarray(perm, BF16))


def _in_proj(x2, w_bf, layer, cos, sin, seq):
    t, d = x2.shape
    n = w_bf.shape[2]
    tiles_per_seq = seq // TM_PROJ
    return pl.pallas_call(
        _in_proj_kernel,
        grid=(t // TM_PROJ, n // TN_PROJ),
        in_specs=[
            pl.BlockSpec((TM_PROJ, d), lambda i, j: (i, 0)),
            pl.BlockSpec((None, d, TN_PROJ), lambda i, j: (layer, 0, j)),
            pl.BlockSpec((TM_PROJ, LANES), lambda i, j: (i % tiles_per_seq, 0)),
            pl.BlockSpec((TM_PROJ, LANES), lambda i, j: (i % tiles_per_seq, 0)),
        ],
        out_specs=pl.BlockSpec((TM_PROJ, TN_PROJ), lambda i, j: (i, j)),
        out_shape=jax.ShapeDtypeStruct((t, n), BF16),
        compiler_params=_cparams(("arbitrary", "arbitrary")),
        name="in_proj",
    )(x2, w_bf, cos, sin)


def _ret_kernel(q_ref, k_ref, v_ref, g_ref, mask_ref, qd_ref, kd_ref, o_ref, r_ref):
    n = pl.program_id(1)

    @pl.when(n == 0)
    def _init():
        r_ref[...] = jnp.zeros_like(r_ref)

    for h in range(RET_HEADS):
        cols = slice(h * RET_HEAD_DIM, (h + 1) * RET_HEAD_DIM)
        q = q_ref[:, cols]
        k = k_ref[:, cols]
        v = v_ref[:, cols]
        s = lax.dot_general(q, k, (((1,), (1,)), ((), ())), preferred_element_type=F32)
        s = s * mask_ref[h]
        inner = jnp.dot(s.astype(BF16), v, preferred_element_type=F32)
        qd = qd_ref[h]
        r_prev = r_ref[h]
        cross = jnp.dot((q.astype(F32) * qd).astype(BF16), r_prev.astype(BF16), preferred_element_type=F32)
        o = inner + cross
        kdec = (k.astype(F32) * kd_ref[h]).astype(BF16)
        kv = lax.dot_general(kdec, v, (((0,), (0,)), ((), ())), preferred_element_type=F32)
        r_ref[h] = r_prev * qd[RET_CHUNK - 1:RET_CHUNK, :] + kv
        mu = jnp.mean(o, axis=-1, keepdims=True)
        oc = o - mu
        var = jnp.mean(oc * oc, axis=-1, keepdims=True)
        o_ref[:, cols] = (g_ref[:, cols].astype(F32) * (oc * lax.rsqrt(var + LN_EPS))).astype(BF16)


def _retention_tables():
    c = RET_CHUNK
    log_gamma = np.log(1.0 - 2.0 ** (-5.0 - np.arange(RET_HEADS, dtype=np.float64)))
    idx = np.arange(c, dtype=np.float64)
    diff = idx[:, None] - idx[None, :]
    mask = np.where(diff >= 0, np.exp(log_gamma[:, None, None] * np.maximum(diff, 0.0)), 0.0)
    qd = np.exp(log_gamma[:, None] * (idx + 1.0)[None, :])
    kd = np.exp(log_gamma[:, None] * (c - 1.0 - idx)[None, :])
    bc = lambda a: np.broadcast_to(a[:, :, None], (RET_HEADS, c, RET_HEAD_DIM)).astype(np.float32)
    return mask.astype(np.float32), bc(qd), bc(kd)


def _retention(proj, batch, seq):
    t = proj.shape[0]
    nch = seq // RET_CHUNK
    mask, qd, kd = _retention_tables()
    blk = (RET_CHUNK, RET_WIDTH)
    seg = lambda c: pl.BlockSpec(blk, lambda b, n: (b * nch + n, c))
    whole = lambda a: pl.BlockSpec(a.shape, lambda b, n: (0, 0, 0))
    return pl.pallas_call(
        _ret_kernel,
        grid=(batch, nch),
        in_specs=[seg(0), seg(1), seg(2), seg(3), whole(mask), whole(qd), whole(kd)],
        out_specs=seg(0),
        out_shape=jax.ShapeDtypeStruct((t, RET_WIDTH), BF16),
        scratch_shapes=[pltpu.VMEM((RET_HEADS, RET_HEAD_DIM, RET_HEAD_DIM), F32)],
        compiler_params=_cparams(("arbitrary", "arbitrary")),
        name="retention",
    )(proj, proj, proj, proj, jnp.asarray(mask), jnp.asarray(qd), jnp.asarray(kd))


def _s5_kernel(u_ref, win_ref, wout_ref, lam2_ref, y_ref, us_ref, up_ref, cb_ref, hb_ref, ynb_ref, yt_ref,
               st_ref):
    n = pl.program_id(1)
    tl = S5_TL
    nb = SUBLANES
    ns = S5_NSTATE
    npair = tl // 2
    e_even = slice(2 * ns, 2 * ns + LANES)
    e_odd = slice(2 * ns + LANES, 2 * ns + 2 * LANES)

    @pl.when(n == 0)
    def _init():
        st_ref[...] = jnp.zeros_like(st_ref)
        ynb_ref[0:nb, :] = jnp.zeros((nb, LANES), F32)

    for b in range(nb):
        us_ref[b * tl:(b + 1) * tl, :] = u_ref[b].astype(F32)
    for k in range(npair):
        up_ref[k * nb:(k + 1) * nb, 0:LANES] = us_ref[pl.ds(2 * k, nb, stride=tl), :]
        up_ref[k * nb:(k + 1) * nb, LANES:2 * LANES] = us_ref[pl.ds(2 * k + 1, nb, stride=tl), :]

    ar = jnp.broadcast_to(lam2_ref[0:1, :], (nb, ns))
    ai = jnp.broadcast_to(lam2_ref[1:2, :], (nb, ns))
    sr = st_ref[0:nb, :]
    si = st_ref[nb:2 * nb, :]

    grows = S5_GROUP * nb

    def project_in(g):
        rows = slice(g * grows, (g + 1) * grows)
        cb_ref[rows, :] = jnp.dot(up_ref[rows, :].astype(BF16), win_ref[...], preferred_element_type=F32)

    project_in(0)
    for g in range(npair // S5_GROUP):
        if g + 1 < npair // S5_GROUP:
            project_in(g + 1)
        for kk in range(0, S5_GROUP, 2):
            r0 = g * grows + kk * nb
            two_r = []
            two_i = []
            for r in (r0, r0 + nb):
                br = cb_ref[r:r + nb, 0:ns]
                bi = cb_ref[r:r + nb, ns:2 * ns]
                sr, si = ar * sr - ai * si + br, ar * si + ai * sr + bi
                two_r.append(sr)
                two_i.append(si)
            hb_ref[r0:r0 + 2 * nb, 0:ns] = jnp.concatenate(two_r, axis=0).astype(BF16)
            hb_ref[r0:r0 + 2 * nb, ns:2 * ns] = jnp.concatenate(two_i, axis=0).astype(BF16)
        rows = slice(g * grows, (g + 1) * grows)
        out = jnp.dot(hb_ref[rows, :], wout_ref[...], preferred_element_type=F32)
        y_odd = out[:, 0:LANES] + cb_ref[rows, e_odd]
        ynb_ref[nb + g * grows:nb + (g + 1) * grows, :] = out[:, LANES:2 * LANES]
        y_even = ynb_ref[rows, :] + cb_ref[rows, e_even]
        for k in range(S5_GROUP):
            t0 = 2 * (g * S5_GROUP + k)
            yt_ref[t0 * nb:(t0 + 1) * nb, :] = y_even[k * nb:(k + 1) * nb, :]
            yt_ref[(t0 + 1) * nb:(t0 + 2) * nb, :] = y_odd[k * nb:(k + 1) * nb, :]
    st_ref[0:nb, :] = sr
    st_ref[nb:2 * nb, :] = si
    ynb_ref[0:nb, :] = ynb_ref[npair * nb:(npair + 1) * nb, :]
    for b in range(nb):
        y_ref[b] = yt_ref[pl.ds(b, tl, stride=nb), :].astype(BF16)


def _s5_tables(lam_re, lam_im, b_re, b_im, c_re, c_im, d, log_dt):
    lam = lax.complex(lam_re.astype(F32), lam_im.astype(F32))
    dt = jnp.exp(log_dt.astype(F32))[:, None]
    lam_bar = jnp.exp(lam * dt)
    b_bar = ((lam_bar - 1.0) / lam)[..., None] * lax.complex(b_re.astype(F32), b_im.astype(F32))
    gpt = LANES // SSM_GROUP
    eye = jnp.eye(gpt, dtype=F32)

    def bdiag_in(m):
        m = m.reshape(S5_LT, gpt, SSM_STATE, SSM_GROUP)
        return jnp.einsum('jgpi,gh->jgihp', m, eye).reshape(S5_LT, LANES, gpt * SSM_STATE)

    def bdiag_out(m):
        m = m.reshape(S5_LT, gpt, SSM_GROUP, SSM_STATE)
        return jnp.einsum('jgop,gh->jgpho', m, eye).reshape(S5_LT, gpt * SSM_STATE, LANES)

    ar = jnp.real(lam_bar).reshape(S5_LT, S5_NSTATE)
    ai = jnp.imag(lam_bar).reshape(S5_LT, S5_NSTATE)
    bre, bim = bdiag_in(jnp.real(b_bar)), bdiag_in(jnp.imag(b_bar))
    ctop, cbot = bdiag_out(c_re.astype(F32)), -bdiag_out(c_im.astype(F32))
    bare = bre * ar[:, None, :] - bim * ai[:, None, :]
    baim = bre * ai[:, None, :] + bim * ar[:, None, :]
    catop = ctop * ar[:, :, None] + cbot * ai[:, :, None]
    cabot = cbot * ar[:, :, None] - ctop * ai[:, :, None]
    hi = lax.Precision.HIGHEST
    cb = (jnp.einsum('jis,jso->jio', bre, ctop, precision=hi)
          + jnp.einsum('jis,jso->jio', bim, cbot, precision=hi))
    dmat = d.astype(F32).reshape(S5_LT, LANES)[:, :, None] * jnp.eye(LANES, dtype=F32)
    zero = jnp.zeros_like(dmat)
    win = jnp.concatenate([jnp.concatenate([bare, baim, cb + dmat, zero], axis=-1),
                           jnp.concatenate([bre, bim, zero, dmat], axis=-1)], axis=1).astype(BF16)
    wout = jnp.concatenate([jnp.concatenate([ctop, catop], axis=-1),
                            jnp.concatenate([cbot, cabot], axis=-1)], axis=1).astype(BF16)
    lam2 = jnp.stack([ar * ar - ai * ai, 2.0 * ar * ai], axis=1)
    return win, wout, lam2


def _s5(proj, tables, layer, batch, seq):
    win, wout, lam2 = tables
    assert batch == SUBLANES
    proj3 = proj.reshape(batch, seq, IN_PROJ_WIDTH)
    ucol = (4 * RET_WIDTH) // LANES
    tl = S5_TL
    npair = tl // 2
    assert npair % S5_GROUP == 0 and S5_GROUP % 2 == 0
    tile = lambda j, n: (layer, j, 0, 0)
    return pl.pallas_call(
        _s5_kernel,
        grid=(S5_LT, seq // tl),
        in_specs=[
            pl.BlockSpec((batch, tl, LANES), lambda j, n: (0, n, ucol + j)),
            pl.BlockSpec((None, None, 2 * LANES, 2 * S5_NSTATE + 2 * LANES), tile),
            pl.BlockSpec((None, None, 2 * S5_NSTATE, 2 * LANES), tile),
            pl.BlockSpec((None, None, 2, S5_NSTATE), tile),
        ],
        out_specs=pl.BlockSpec((batch, tl, LANES), lambda j, n: (0, n, j)),
        out_shape=jax.ShapeDtypeStruct((batch, seq, SSM_WIDTH), BF16),
        scratch_shapes=[
            pltpu.VMEM((batch * tl, LANES), F32),
            pltpu.VMEM((batch * npair, 2 * LANES), F32),
            pltpu.VMEM((batch * npair, 2 * S5_NSTATE + 2 * LANES), F32),
            pltpu.VMEM((batch * npair, 2 * S5_NSTATE), BF16),
            pltpu.VMEM((batch * (npair + 1), LANES), F32),
            pltpu.VMEM((batch * tl, LANES), F32),
            pltpu.VMEM((2 * SUBLANES, S5_NSTATE), F32),
        ],
        compiler_params=_cparams(("arbitrary", "arbitrary")),
        name="s5",
    )(proj3, win, wout, lam2)


def _layer_norm_rows(r, g, b):
    mu = jnp.mean(r, axis=-1, keepdims=True)
    rc = r - mu
    var = jnp.mean(rc * rc, axis=-1, keepdims=True)
    return rc * lax.rsqrt(var + LN_EPS) * g + b


def _post_mix_kernel(x_ref, ret_ref, y_ref, wglu_ref, wout_ref, lng_ref, lnb_ref,
                     rcat_ref, rhi_ref, rb_ref, tri_ref,
                     x1_ref, x1p_ref, e_ref, w_ref, rank_ref, cnt_ref, carry_ref):
    i = pl.program_id(0)

    @pl.when(i == 0)
    def _init():
        carry_ref[...] = jnp.zeros_like(carry_ref)

    args = (x_ref, ret_ref, y_ref, wglu_ref, wout_ref, lng_ref, lnb_ref, rcat_ref, rhi_ref, rb_ref, tri_ref,
            x1_ref, x1p_ref, e_ref, w_ref, rank_ref, carry_ref)
    subs = [_post_mix_phases(k, *args) for k in range(TM_POST // TSUB_POST)]
    nph = len(subs[0])
    for step in range(nph + len(subs) - 1):
        for k, phases in enumerate(subs):
            if 0 <= step - k < nph:
                phases[step - k]()
    cnt_ref[...] = carry_ref[...].astype(I32)


def _post_mix_phases(sub, x_ref, ret_ref, y_ref, wglu_ref, wout_ref, lng_ref, lnb_ref,
                     rcat_ref, rhi_ref, rb_ref, tri_ref, x1_ref, x1p_ref, e_ref, w_ref, rank_ref, carry_ref):
    tm = TSUB_POST
    rows = slice(sub * tm, (sub + 1) * tm)
    st = {}

    def gelu():
        st['ya'] = jax.nn.gelu(y_ref[rows, :].astype(F32))

    def glu_matmul():
        st['z'] = jnp.dot(st['ya'].astype(BF16), wglu_ref[...], preferred_element_type=F32)

    def glu_gate():
        st['ssm'] = (st.pop('ya') * _sigmoid(st.pop('z'))).astype(BF16)

    def out_matmul():
        mixed = jnp.concatenate([ret_ref[rows, :], st.pop('ssm')], axis=1)
        st['h'] = jnp.dot(mixed, wout_ref[...], preferred_element_type=F32)

    def norm():
        x1 = _layer_norm_rows(DEEPNORM_ALPHA * x_ref[rows, :] + st.pop('h'), lng_ref[...], lnb_ref[...])
        x1_ref[rows, :] = x1
        _store_token_tiles(x1p_ref, sub * tm, tm, _pack_rows(x1))
        st['xh'] = x1.astype(BF16)
        st['xl'] = (x1 - st['xh'].astype(F32)).astype(BF16)

    def router_matmul():
        nt = (((1,), (1,)), ((), ()))
        l1 = lax.dot_general(rcat_ref[...], st.pop('xh'), nt, preferred_element_type=F32)
        l2 = lax.dot_general(rhi_ref[...], st.pop('xl'), nt, preferred_element_type=F32)
        st['logits'] = l1[0:N_EXPERTS] + l1[N_EXPERTS:] + l2 + rb_ref[...]

    def route():
        _route(st.pop('logits'), rows, tri_ref, e_ref, w_ref, rank_ref, carry_ref)

    return [gelu, glu_matmul, glu_gate, out_matmul, norm, router_matmul, route]


def _route(logits, rows, tri_ref, e_ref, w_ref, rank_ref, carry_ref):
    tm = TSUB_POST
    m = jnp.max(logits, axis=0, keepdims=True)
    ex = jnp.exp(logits - m)
    p = ex / jnp.sum(ex, axis=0, keepdims=True)

    eg = EXPERTS_PER_GROUP
    iota_g = lax.broadcasted_iota(I32, (eg, tm), 0)
    best = None
    for g in range(N_EXPERT_GROUPS):
        pg = p[g * eg:(g + 1) * eg]
        m1 = jnp.max(pg, axis=0, keepdims=True)
        i1 = jnp.min(jnp.where(pg == m1, iota_g, eg), axis=0, keepdims=True)
        pg2 = jnp.where(iota_g == i1, -1.0, pg)
        m2 = jnp.max(pg2, axis=0, keepdims=True)
        i2 = jnp.min(jnp.where(pg2 == m2, iota_g, eg), axis=0, keepdims=True)
        sg = m1 + m2
        if best is None:
            best = (sg, m1, m2, i1, i2)
        else:
            better = sg > best[0]
            cand = (sg, m1, m2, i1 + g * eg, i2 + g * eg)
            best = tuple(jnp.where(better, c, o) for c, o in zip(cand, best))
    _, v1, v2, e1, e2 = best
    tot = v1 + v2
    e_ref[0:1, rows] = e1
    e_ref[1:2, rows] = e2
    w_ref[0:1, rows] = v1 / tot
    w_ref[1:2, rows] = v2 / tot

    iota_e = lax.broadcasted_iota(I32, (N_EXPERTS, tm), 0)
    oh1 = iota_e == e1
    oh2 = iota_e == e2
    oh = jnp.where(oh1, 1.0, jnp.where(oh2, 1.0, 0.0))
    before = jnp.dot(oh.astype(BF16), tri_ref[...], preferred_element_type=F32) + carry_ref[:, 0:1]
    rank_ref[0:1, rows] = jnp.sum(jnp.where(oh1, before, 0.0), axis=0, keepdims=True).astype(I32)
    rank_ref[1:2, rows] = jnp.sum(jnp.where(oh2, before, 0.0), axis=0, keepdims=True).astype(I32)
    carry_ref[...] = carry_ref[...] + jnp.sum(oh, axis=1, keepdims=True)


def _post_mix(x2, ret, y, wglu_bf, wout_bf, lng, lnb, layer, rcat, rhi, rb):
    t, d = x2.shape
    tm = TM_POST
    ts = TSUB_POST
    tri = jnp.asarray(np.triu(np.ones((ts, ts), np.float32), 1), BF16)
    const = lambda i: (0, 0)
    lyr = lambda i: (layer, 0, 0)
    tok = lambda i: (i, 0)
    lane = lambda i: (0, i)
    return pl.pallas_call(
        _post_mix_kernel,
        grid=(t // tm,),
        in_specs=[
            pl.BlockSpec((tm, d), tok),
            pl.BlockSpec((tm, RET_WIDTH), tok),
            pl.BlockSpec((tm, SSM_WIDTH), tok),
            pl.BlockSpec((None, SSM_WIDTH, SSM_WIDTH), lyr),
            pl.BlockSpec((None, RET_WIDTH + SSM_WIDTH, d), lyr),
            pl.BlockSpec((None, 1, d), lyr),
            pl.BlockSpec((None, 1, d), lyr),
            pl.BlockSpec((2 * N_EXPERTS, d), const),
            pl.BlockSpec((N_EXPERTS, d), const),
            pl.BlockSpec((N_EXPERTS, 1), const),
            pl.BlockSpec((ts, ts), const),
        ],
        out_specs=[
            pl.BlockSpec((tm, d), tok),
            pl.BlockSpec((tm * PACK_SUB, LANES), tok),
            pl.BlockSpec((TOP_K, tm), lane),
            pl.BlockSpec((TOP_K, tm), lane),
            pl.BlockSpec((TOP_K, tm), lane),
            pl.BlockSpec((N_EXPERTS, LANES), const),
        ],
        out_shape=[
            jax.ShapeDtypeStruct((t, d), F32),
            jax.ShapeDtypeStruct((t * PACK_SUB, LANES), U32),
            jax.ShapeDtypeStruct((TOP_K, t), I32),
            jax.ShapeDtypeStruct((TOP_K, t), F32),
            jax.ShapeDtypeStruct((TOP_K, t), I32),
            jax.ShapeDtypeStruct((N_EXPERTS, LANES), I32),
        ],
        scratch_shapes=[pltpu.VMEM((N_EXPERTS, LANES), F32)],
        compiler_params=_cparams(("arbitrary",)),
        name="post_mix",
    )(x2, ret, y, wglu_bf, wout_bf, lng, lnb, rcat, rhi, rb, tri)


def _dispatch_kernel(zrow_ref, d0_ref, d1_ref, x_ref, xb_hbm, zbuf_ref, sem, zsem):
    tm = TM_DISP

    @pl.when(pl.program_id(0) == 0)
    def _zero_fill():
        zbuf_ref[...] = jnp.zeros_like(zbuf_ref)

        def zcopy(k):
            n = ROW_TILE * PACK_SUB
            rows = pl.ds(pl.multiple_of(zrow_ref[k] * PACK_SUB, n), n)
            return pltpu.make_async_copy(zbuf_ref, xb_hbm.at[rows], zsem.at[0])

        def start(k, c):
            @pl.when(zrow_ref[k] >= 0)
            def _():
                zcopy(k).start()
            return c

        def wait(k, c):
            @pl.when(zrow_ref[k] >= 0)
            def _():
                zcopy(k).wait()
            return c

        lax.fori_loop(0, 2 * N_EXPERTS, start, 0)
        lax.fori_loop(0, 2 * N_EXPERTS, wait, 0)

    def issue(tb, c):
        for u in range(ISSUE_UNROLL):
            t = tb * ISSUE_UNROLL + u
            src = x_ref.at[pl.ds(t * PACK_SUB, PACK_SUB)]
            tile = lambda r: pl.ds(pl.multiple_of(r, PACK_SUB), PACK_SUB)
            pltpu.make_async_copy(src, xb_hbm.at[tile(d0_ref[0, 0, t])], sem.at[0]).start(priority=0)
            pltpu.make_async_copy(src, xb_hbm.at[tile(d1_ref[0, 0, t])], sem.at[0]).start(priority=1)
        return c

    lax.fori_loop(0, tm // ISSUE_UNROLL, issue, 0)
    for _ in range(TOP_K):
        pltpu.make_async_copy(x_ref, xb_hbm.at[pl.ds(0, tm * PACK_SUB)], sem.at[0]).wait()


def _dispatch(x1p, dest_tile, zrow, n_rows):
    t = x1p.shape[0] // PACK_SUB
    tm = TM_DISP
    nstep = t // tm
    d0 = dest_tile[0].reshape(nstep, 1, tm)
    d1 = dest_tile[1].reshape(nstep, 1, tm)
    smem_blk = pl.BlockSpec((1, 1, tm), lambda i, z: (i, 0, 0), memory_space=pltpu.SMEM)
    grid_spec = pltpu.PrefetchScalarGridSpec(
        num_scalar_prefetch=1,
        grid=(nstep,),
        in_specs=[smem_blk, smem_blk, pl.BlockSpec((tm * PACK_SUB, LANES), lambda i, z: (i, 0))],
        out_specs=pl.BlockSpec(memory_space=pl.ANY),
        scratch_shapes=[pltpu.VMEM((ROW_TILE * PACK_SUB, LANES), U32),
                        pltpu.SemaphoreType.DMA((1,)), pltpu.SemaphoreType.DMA((1,))],
    )
    return pl.pallas_call(
        _dispatch_kernel,
        grid_spec=grid_spec,
        out_shape=jax.ShapeDtypeStruct((n_rows * PACK_SUB, LANES), U32),
        compiler_params=_cparams(("arbitrary",)),
        name="dispatch",
    )(zrow, d0, d1, x1p)


def _experts_kernel(se_ref, row_ref, nt_ref, nz_ref, xb_hbm, wg_ref, wu_ref, wd_ref, yb_hbm,
                    xs_ref, acc_ref, stg_in, stg_out, sem_in, sem_out):
    del se_ref
    s = pl.program_id(0)
    j = pl.program_id(1)
    tiles = lambda nhalf: (nhalf + 1) // 2
    nt = tiles(nt_ref[s])
    nfull = nt_ref[s] // 2
    nz = nz_ref[s]
    row0 = row_ref[s]
    rt = ROW_TILE

    nsuper = pl.num_programs(0)
    s_next = jnp.minimum(s + 1, nsuper - 1)
    prev_nt = jnp.where(s > 0, tiles(nt_ref[jnp.maximum(s - 1, 0)]), 0)
    next_nt = jnp.where(s + 1 < nsuper, nt_ref[s_next], 0)

    def rows(i, base=None):
        n = rt * PACK_SUB
        base = row0 if base is None else base
        return pl.ds(pl.multiple_of((base + i * rt) * PACK_SUB, n), n)

    def in_copy(i, slot):
        return pltpu.make_async_copy(xb_hbm.at[rows(i)], stg_in.at[slot], sem_in.at[slot])

    def out_copy(i, slot):
        return pltpu.make_async_copy(stg_out.at[slot], yb_hbm.at[rows(i)], sem_out.at[slot])

    def partial_out(i, m):
        r = pl.multiple_of(i * rt, rt)
        xi = xs_ref[pl.ds(r, m), :]
        g = jnp.dot(xi, wg_ref[...].astype(BF16), preferred_element_type=F32)
        u = jnp.dot(xi, wu_ref[...].astype(BF16), preferred_element_type=F32)
        hj = (g * _sigmoid(g) * u).astype(BF16)
        return r, jnp.dot(hj, wd_ref[...].astype(BF16), preferred_element_type=F32)

    def load_tile(i):
        slot = i % 2

        @pl.when(i + 1 < nt)
        def _():
            in_copy(i + 1, 1 - slot).start()

        in_copy(i, slot).wait()
        xrows = pl.ds(pl.multiple_of(i * rt, rt), rt)
        for c, words in enumerate(_load_token_tiles(stg_in.at[slot], 0, rt)):
            xs_ref[xrows, c * LANES:(c + 1) * LANES] = _unpack_lo(words).astype(BF16)
            xs_ref[xrows, PACK_WORDS + c * LANES:PACK_WORDS + (c + 1) * LANES] = _unpack_hi(words).astype(BF16)

    def store_tile(i, vals):
        slot = i % 2

        @pl.when(i >= 2)
        def _():
            out_copy(i - 2, slot).wait()

        _store_token_tiles(stg_out.at[slot], 0, rt, _pack_rows(vals))
        out_copy(i, slot).start()

    def for_tiles(body):
        def pair(p, c):
            body(2 * p, 2, 2 * rt)
            return c

        lax.fori_loop(0, nfull // 2, pair, 0)

        @pl.when(nfull % 2 == 1)
        def _():
            body(nfull - 1, 1, rt)

        @pl.when(nt > nfull)
        def _():
            body(nfull, 1, rt // 2)

    @pl.when((nz > 0) & (j == 0))
    def _zero_tail():
        stg_out[0] = jnp.zeros(stg_out.shape[1:], U32)

        def start(i, c):
            out_copy(i, 0).start()
            return c

        def wait(i, c):
            out_copy(i, 0).wait()
            return c

        lax.fori_loop(0, nz, start, 0)
        lax.fori_loop(0, nz, wait, 0)

    @pl.when(nt > 0)
    def _work():
        @pl.when(j == 0)
        def _first():
            @pl.when(prev_nt == 0)
            def _():
                in_copy(0, 0).start()

            @pl.when(prev_nt >= 2)
            def _():
                out_copy(0, prev_nt % 2).wait()

            @pl.when(prev_nt >= 1)
            def _():
                out_copy(0, (prev_nt - 1) % 2).wait()

            def body(i, ntiles, m):
                for k in range(ntiles):
                    load_tile(i + k)
                r, part = partial_out(i, m)
                acc_ref[pl.ds(r, m), :] = part

            for_tiles(body)

        @pl.when((j > 0) & (j < N_DE_CHUNKS - 1))
        def _mid():
            def body(i, ntiles, m):
                r, part = partial_out(i, m)
                acc_ref[pl.ds(r, m), :] += part

            for_tiles(body)

        @pl.when(j == N_DE_CHUNKS - 1)
        def _last():
            @pl.when(next_nt > 0)
            def _():
                pltpu.make_async_copy(xb_hbm.at[rows(0, row_ref[s_next])], stg_in.at[0], sem_in.at[0]).start()

            def body(i, ntiles, m):
                r, part = partial_out(i, m)
                total = acc_ref[pl.ds(r, m), :] + part
                if m < rt:
                    total = jnp.concatenate([total, jnp.zeros((rt - m, total.shape[1]), F32)], axis=0)
                for k in range(ntiles):
                    store_tile(i + k, total[k * rt:(k + 1) * rt, :])

            for_tiles(body)

            @pl.when(next_nt == 0)
            def _drain():
                @pl.when(nt >= 2)
                def _():
                    out_copy(nt - 2, nt % 2).wait()

                out_copy(nt - 1, (nt - 1) % 2).wait()


def _experts(xb, se, row, ntl, nzl, w_gate, w_up, w_down, layer):
    p = xb.shape[0] // PACK_SUB
    d = D_MODEL
    nsuper = se.shape[0]
    last = N_DE_CHUNKS - 1
    assert last >= 1

    def jj(s, j, nt_ref):
        return jnp.where(nt_ref[s] > 0, j, last)

    grid_spec = pltpu.PrefetchScalarGridSpec(
        num_scalar_prefetch=4,
        grid=(nsuper, N_DE_CHUNKS),
        in_specs=[
            pl.BlockSpec(memory_space=pl.ANY),
            pl.BlockSpec((None, None, d, DE_CHUNK),
                         lambda s, j, se_r, row_r, nt_r, nz_r: (layer, se_r[s], 0, jj(s, j, nt_r))),
            pl.BlockSpec((None, None, d, DE_CHUNK),
                         lambda s, j, se_r, row_r, nt_r, nz_r: (layer, se_r[s], 0, jj(s, j, nt_r))),
            pl.BlockSpec((None, None, DE_CHUNK, d),
                         lambda s, j, se_r, row_r, nt_r, nz_r: (layer, se_r[s], jj(s, j, nt_r), 0)),
        ],
        out_specs=pl.BlockSpec(memory_space=pl.ANY),
        scratch_shapes=[
            pltpu.VMEM((SUPER, d), BF16),
            pltpu.VMEM((SUPER, d), F32),
            pltpu.VMEM((2, ROW_TILE * PACK_SUB, LANES), U32),
            pltpu.VMEM((2, ROW_TILE * PACK_SUB, LANES), U32),
            pltpu.SemaphoreType.DMA((2,)),
            pltpu.SemaphoreType.DMA((2,)),
        ],
    )
    return pl.pallas_call(
        _experts_kernel,
        grid_spec=grid_spec,
        out_shape=jax.ShapeDtypeStruct((p * PACK_SUB, LANES), U32),
        compiler_params=_cparams(("arbitrary", "arbitrary"), vmem=EXPERTS_VMEM_LIMIT),
        name="experts",
    )(se, row, ntl, nzl, xb, w_gate, w_up, w_down)


def _combine_kernel(d0_ref, d1_ref, d0n_ref, d1n_ref, x1_ref, w_ref, lng_ref, lnb_ref, yb_hbm,
                    o_ref, ybuf_ref, sem):
    tm = TM_COMB
    i = pl.program_id(0)
    slot = i % 2

    def gather(da_ref, db_ref, into):
        def issue(tb, c):
            for u in range(ISSUE_UNROLL):
                t = tb * ISSUE_UNROLL + u
                dst = pl.ds(pl.multiple_of(t * PACK_SUB, PACK_SUB), PACK_SUB)
                tile = lambda r: pl.ds(pl.multiple_of(r, PACK_SUB), PACK_SUB)
                pltpu.make_async_copy(yb_hbm.at[tile(da_ref[0, 0, t])], ybuf_ref.at[into, 0, dst],
                                      sem.at[into]).start(priority=0)
                pltpu.make_async_copy(yb_hbm.at[tile(db_ref[0, 0, t])], ybuf_ref.at[into, 1, dst],
                                      sem.at[into]).start(priority=1)
            return c

        lax.fori_loop(0, tm // ISSUE_UNROLL, issue, 0)

    @pl.when(i == 0)
    def _prime():
        gather(d0_ref, d1_ref, 0)

    for k in range(TOP_K):
        pltpu.make_async_copy(ybuf_ref.at[slot, k], ybuf_ref.at[slot, k], sem.at[slot]).wait()

    @pl.when(i + 1 < pl.num_programs(0))
    def _prefetch():
        gather(d0n_ref, d1n_ref, 1 - slot)

    wpad = jnp.concatenate([w_ref[...], jnp.zeros((LANES - TOP_K, tm), F32)], axis=0)
    wt = wpad.T
    w0 = wt[:, 0:1]
    w1 = wt[:, 1:2]
    x1 = x1_ref[...]
    y0 = _load_token_tiles(ybuf_ref.at[slot, 0], 0, tm)
    y1 = _load_token_tiles(ybuf_ref.at[slot, 1], 0, tm)
    lo = [_unpack_lo(a) * w0 + _unpack_lo(b) * w1 for a, b in zip(y0, y1)]
    hi = [_unpack_hi(a) * w0 + _unpack_hi(b) * w1 for a, b in zip(y0, y1)]
    moe = jnp.concatenate(lo + hi, axis=1)
    o_ref[...] = _layer_norm_rows(DEEPNORM_ALPHA * x1 + moe, lng_ref[...], lnb_ref[...])


def _combine(x1, dest_tile, wts, lng, lnb, layer, yb):
    t, d = x1.shape
    tm = TM_COMB
    nstep = t // tm
    d0 = dest_tile[0].reshape(nstep, 1, tm)
    d1 = dest_tile[1].reshape(nstep, 1, tm)
    smem_blk = pl.BlockSpec((1, 1, tm), lambda i: (i, 0, 0), memory_space=pltpu.SMEM)
    smem_next = pl.BlockSpec((1, 1, tm), lambda i: (jnp.minimum(i + 1, nstep - 1), 0, 0),
                             memory_space=pltpu.SMEM)
    lyr = lambda i: (layer, 0, 0)
    return pl.pallas_call(
        _combine_kernel,
        grid=(nstep,),
        in_specs=[smem_blk, smem_blk, smem_next, smem_next,
                  pl.BlockSpec((tm, d), lambda i: (i, 0)),
                  pl.BlockSpec((TOP_K, tm), lambda i: (0, i)),
                  pl.BlockSpec((None, 1, d), lyr),
                  pl.BlockSpec((None, 1, d), lyr),
                  pl.BlockSpec(memory_space=pl.ANY)],
        out_specs=pl.BlockSpec((tm, d), lambda i: (i, 0)),
        out_shape=jax.ShapeDtypeStruct((t, d), F32),
        scratch_shapes=[pltpu.VMEM((2, TOP_K, tm * PACK_SUB, LANES), U32), pltpu.SemaphoreType.DMA((2,))],
        compiler_params=_cparams(("arbitrary",)),
        name="combine",
    )(d0, d1, d0, d1, x1, wts, lng, lnb, yb)


def _rotary_tables(seq):
    inv = 1.0 / (ROPE_BASE ** (jnp.arange(0, RET_HEAD_DIM, 2, dtype=F32) / RET_HEAD_DIM))
    ang = jnp.arange(seq, dtype=F32)[:, None] * inv[None, :]
    return jnp.cos(ang), jnp.sin(ang)


def _routing_tables(counts, n_rows):
    ids = np.arange(N_EXPERTS)
    upto = jnp.asarray(ids[None, :] <= ids[:, None])
    csum = lambda v: jnp.sum(jnp.where(upto, v[None, :], 0), axis=1)
    padded = (counts + ROW_TILE - 1) // ROW_TILE * ROW_TILE
    pend = csum(padded)
    pstart = pend - padded
    ntile = padded // ROW_TILE
    nsup = (ntile + TILES_PER_SUPER - 1) // TILES_PER_SUPER
    send = csum(nsup)
    sstart = send - nsup
    nsuper = -(-n_rows // SUPER) + N_EXPERTS
    s = jnp.arange(nsuper, dtype=I32)
    total = send[-1]
    which = lambda q: jnp.minimum(jnp.sum((send[None, :] <= q[:, None]).astype(I32), axis=1), N_EXPERTS - 1)
    pick = lambda v, e: jnp.sum(jnp.where(e[:, None] == ids[None, :], v[None, :], 0), axis=1)
    se = which(s)
    valid = s < total
    k = s - pick(sstart, se)
    row = pick(pstart, se) + k * SUPER
    nt = jnp.clip(pick(ntile, se) - k * TILES_PER_SUPER, 0, TILES_PER_SUPER)
    idle = s - total
    tail_row = pend[-1] + idle * SUPER
    nz = jnp.clip((n_rows - tail_row) // ROW_TILE, 0, TILES_PER_SUPER)
    se = jnp.where(valid, se, which((total - 1)[None])[0])
    row = jnp.where(valid, row, jnp.minimum(tail_row, n_rows - ROW_TILE))
    nt = jnp.where(valid, nt, 0)
    nz = jnp.where(valid, 0, nz)
    last_fill = counts - (ntile - 1) * ROW_TILE
    half_last = ((counts > 0) & (last_fill <= ROW_TILE // 2)).astype(I32)
    ends_expert = (k * TILES_PER_SUPER + nt) == pick(ntile, se)
    nhalf = 2 * nt - jnp.where(valid & ends_expert, pick(half_last, se), 0)
    seg_last = jnp.where(padded > 0, pend - ROW_TILE, -1)
    tail = pend[-1] + jnp.arange(N_EXPERTS, dtype=I32) * ROW_TILE
    zrow = jnp.concatenate([seg_last, jnp.where(tail < n_rows, tail, -1)])
    return (pstart.astype(I32), se.astype(I32), row.astype(I32), nhalf.astype(I32), nz.astype(I32),
            zrow.astype(I32))


def kernel(x, w_in, w_out, ssm_lambda_re, ssm_lambda_im, ssm_b_re, ssm_b_im, ssm_c_re, ssm_c_im,
           ssm_d, ssm_log_dt, w_glu, ln1_g, ln1_b, ln2_g, ln2_b, router_w, router_b,
           w_gate, w_up, w_down):
    batch, seq, d = x.shape
    t = batch * seq
    n_assign = t * TOP_K
    n_rows = n_assign + N_EXPERTS * ROW_TILE
    assert seq % TM_PROJ == 0 and seq % RET_CHUNK == 0 and seq % S5_TL == 0
    assert t % TM_POST == 0 and t % TM_COMB == 0 and t % TM_DISP == 0 and n_rows % ROW_TILE == 0

    cos, sin = _rotary_tables(seq)
    expert_ids = jnp.arange(N_EXPERTS, dtype=I32)
    rw_t = router_w.astype(F32).T
    rhi = rw_t.astype(BF16)
    rlo = (rw_t - rhi.astype(F32)).astype(BF16)
    rcat = jnp.concatenate([rhi, rlo], axis=0)
    rb = router_b.astype(F32).reshape(N_EXPERTS, 1)

    w_in_bf = _prep_in_proj_weights(w_in)
    s5_tables = jax.vmap(_s5_tables)(ssm_lambda_re, ssm_lambda_im, ssm_b_re, ssm_b_im,
                                     ssm_c_re, ssm_c_im, ssm_d, ssm_log_dt)
    w_glu_bf = w_glu.astype(BF16)
    w_out_bf = w_out.astype(BF16)
    row3 = lambda p: p.astype(F32).reshape(DEPTH, 1, d)
    ln1_g, ln1_b, ln2_g, ln2_b = row3(ln1_g), row3(ln1_b), row3(ln2_g), row3(ln2_b)

    x2 = x.reshape(t, d)
    for l in range(DEPTH):
        proj = _in_proj(x2, w_in_bf, l, cos, sin, seq)
        ret = _retention(proj, batch, seq)
        y = _s5(proj, s5_tables, l, batch, seq).reshape(t, SSM_WIDTH)
        x1, x1p, e, wts, rank, cnt = _post_mix(
            x2, ret, y, w_glu_bf, w_out_bf, ln1_g, ln1_b, l, rcat, rhi, rb)
        pstart, se, row, ntl, nzl, zrow = _routing_tables(cnt[:, 0], n_rows)
        dest = rank + jnp.sum(jnp.where(e[..., None] == expert_ids, pstart, 0), axis=-1)
        dest_tile = dest * PACK_SUB
        xb = _dispatch(x1p, dest_tile, zrow, n_rows)
        yb = _experts(xb, se, row, ntl, nzl, w_gate, w_up, w_down, l)
        x2 = _combine(x1, dest_tile, wts, ln2_g, ln2_b, l, yb)
    return x2.reshape(batch, seq, d)
```
